```python
import math
import jax, jax.numpy as jnp
from jax import lax
import numpy as np

D_MODEL = 1024
BATCH = 8
SEQ = 4096
DEPTH = 1
DEC_BATCH = 4
DEC_SEQ = 4096
PAST_LEN = 128

GRID_W = 64
HEAD_DIM = 64
NA_HEADS = 8
NA_WIDTH = NA_HEADS * HEAD_DIM
NA_KH_MAX = 8
NA_KW = 16
DIFF_HEADS = 4
DIFF_WIDTH = DIFF_HEADS * 2 * HEAD_DIM
MIX_WIDTH = NA_WIDTH + DIFF_WIDTH
IN_WIDTH = 3 * NA_WIDTH + 3 * DIFF_WIDTH
ROT_DIM = HEAD_DIM // 4
ROPE_THETA = 500000.0
Q_BLOCK = 128
PEER_HEADS = 8
PEER_KEY_DIM = 256
PEER_HALF = PEER_KEY_DIM // 2
N_KEYS = 128
N_EXPERTS = N_KEYS * N_KEYS
PEER_TOPK = 16
PEER_BLOCK = 128
EPS = 1e-6

kernel_name = "hymba_na_diffattn_peer_encoder"


def rmsnorm(x, w):
    xf = x.astype(jnp.float32)
    y = xf * lax.rsqrt(jnp.mean(xf * xf, axis=-1, keepdims=True) + EPS)
    return (y * w.astype(jnp.float32)).astype(x.dtype)


def partial_rope(x):
    S = x.shape[1]
    pos = jnp.arange(S, dtype=jnp.float32)
    inv = ROPE_THETA ** (-jnp.arange(0, ROT_DIM, 2, dtype=jnp.float32) / ROT_DIM)
    ang = pos[:, None] * inv[None, :]
    cos = jnp.concatenate([jnp.cos(ang), jnp.cos(ang)], -1)[:, None, None, :]
    sin = jnp.concatenate([jnp.sin(ang), jnp.sin(ang)], -1)[:, None, None, :]
    xr = x[..., :ROT_DIM].astype(jnp.float32)
    x1, x2 = xr[..., :ROT_DIM // 2], xr[..., ROT_DIM // 2:]
    rot = jnp.concatenate([-x2, x1], -1)
    out = xr * cos + rot * sin
    return jnp.concatenate([out.astype(x.dtype), x[..., ROT_DIM:]], -1)


def neighbourhood_attention(q, k, v, q_norm_w, k_norm_w, rpb):
    B, S = q.shape[0], q.shape[1]
    rows = S // GRID_W
    kh = min(NA_KH_MAX, rows)
    q = rmsnorm(q, q_norm_w)
    k = rmsnorm(k, k_norm_w)

    def to_grid(t):
        return t.transpose(0, 2, 1, 3).reshape(B, NA_HEADS, rows, GRID_W, HEAD_DIM)

    qg, kg, vg = to_grid(q), to_grid(k), to_grid(v)
    col = np.arange(GRID_W)
    cs = np.clip(col - NA_KW // 2, 0, GRID_W - NA_KW)
    cidx = cs[:, None] + np.arange(NA_KW)[None, :]
    rel_c = cidx - col[:, None] + (NA_KW - 1)
    rpb_c = rpb[:, :, rel_c]
    scale = HEAD_DIM ** -0.5

    def one_row(r):
        rs = jnp.clip(r - kh // 2, 0, rows - kh)
        q_r = lax.dynamic_index_in_dim(qg, r, axis=2, keepdims=False)
        k_rows = lax.dynamic_slice_in_dim(kg, rs, kh, axis=2)
        v_rows = lax.dynamic_slice_in_dim(vg, rs, kh, axis=2)
        k_win = k_rows[:, :, :, cidx]
        v_win = v_rows[:, :, :, cidx]
        rel_r = rs + jnp.arange(kh) - r + (NA_KH_MAX - 1)
        bias = jnp.take(rpb_c, rel_r, axis=1).transpose(0, 2, 1, 3)
        s = jnp.einsum('bhqd,bhiqjd->bhqij', q_r, k_win).astype(jnp.float32) * scale
        s = s + bias.astype(jnp.float32)
        p = jax.nn.softmax(s.reshape(B, NA_HEADS, GRID_W, kh * NA_KW), axis=-1)
        p = p.reshape(B, NA_HEADS, GRID_W, kh, NA_KW).astype(v.dtype)
        return jnp.einsum('bhqij,bhiqjd->bhqd', p, v_win)

    out = lax.map(one_row, jnp.arange(rows))
    return out.transpose(1, 0, 3, 2, 4).reshape(B, S, NA_WIDTH)


def diff_attention(q, k, v, q_norm_w, k_norm_w, lq1, lk1, lq2, lk2, subln_w, lambda_init):
    B, S = q.shape[0], q.shape[1]
    nb = S // Q_BLOCK
    q = partial_rope(rmsnorm(q, q_norm_w))
    k = partial_rope(rmsnorm(k, k_norm_w))
    lam = (jnp.exp(jnp.sum(lq1.astype(jnp.float32) * lk1.astype(jnp.float32)))
           - jnp.exp(jnp.sum(lq2.astype(jnp.float32) * lk2.astype(jnp.float32)))
           + lambda_init)
    qb = q.transpose(0, 2, 3, 1, 4).reshape(B, DIFF_HEADS, 2, nb, Q_BLOCK, HEAD_DIM)
    qb = qb.transpose(3, 0, 1, 2, 4, 5)
    kt = k.transpose(0, 2, 3, 1, 4)
    vt = v.transpose(0, 2, 1, 3)
    scale = HEAD_DIM ** -0.5

    def block(qblk):
        s = jnp.einsum('bhcqd,bhckd->bhcqk', qblk, kt).astype(jnp.float32) * scale
        p = jax.nn.softmax(s, axis=-1)
        a = (p[:, :, 0] - lam * p[:, :, 1]).astype(v.dtype)
        return jnp.einsum('bhqk,bhkd->bhqd', a, vt)

    o = lax.map(block, qb)
    o = o.transpose(1, 0, 3, 2, 4).reshape(B, S, DIFF_HEADS, 2 * HEAD_DIM)
    o = rmsnorm(o, subln_w) * (1.0 - lambda_init)
    return o.reshape(B, S, DIFF_WIDTH)


def peer_ffn(h, w_q, sub_keys, u_tab, v_tab):
    B, S, D = h.shape
    xt = h.reshape(-1, PEER_BLOCK, D)

    def block(xb):
        q = (xb @ w_q).reshape(PEER_BLOCK, PEER_HEADS, 2, PEER_HALF)
        sc = jnp.einsum('thpd,hpnd->thpn', q, sub_keys).astype(jnp.float32)
        s, idx = lax.top_k(sc, PEER_TOPK)
        cand_s = (s[:, :, 0, :, None] + s[:, :, 1, None, :]).reshape(PEER_BLOCK, PEER_HEADS, PEER_TOPK * PEER_TOPK)
        cand_i = (idx[:, :, 0, :, None] * N_KEYS + idx[:, :, 1, None, :]).reshape(PEER_BLOCK, PEER_HEADS, PEER_TOPK * PEER_TOPK)
        top_s, pos = lax.top_k(cand_s, PEER_TOPK)
        eid = jnp.take_along_axis(cand_i, pos, axis=-1)
        g = jax.nn.softmax(top_s, axis=-1)
        a = jnp.einsum('td,thkd->thk', xb, u_tab[eid]).astype(jnp.float32)
        w = (g * jax.nn.gelu(a, approximate=False)).astype(xb.dtype)
        return jnp.einsum('thk,thkd->td', w, v_tab[eid])

    return lax.map(block, xt).reshape(B, S, D)


def encoder_layer(x, layer_idx, norm1_w, w_in, na_q_norm, na_k_norm, na_rpb,
                  diff_q_norm, diff_k_norm, diff_lambda_q1, diff_lambda_k1,
                  diff_lambda_q2, diff_lambda_k2, diff_subln_w, w_out, norm2_w,
                  peer_w_q, peer_sub_keys, peer_u, peer_v):
    B, S, _ = x.shape
    lambda_init = 0.8 - 0.6 * math.exp(-0.3 * layer_idx)
    h = rmsnorm(x, norm1_w)
    proj = h @ w_in
    na_q, na_k, na_v, d_q, d_k, d_v = jnp.split(proj, 6, axis=-1)
    na_out = neighbourhood_attention(
        na_q.reshape(B, S, NA_HEADS, HEAD_DIM), na_k.reshape(B, S, NA_HEADS, HEAD_DIM),
        na_v.reshape(B, S, NA_HEADS, HEAD_DIM), na_q_norm, na_k_norm, na_rpb)
    diff_out = diff_attention(
        d_q.reshape(B, S, DIFF_HEADS, 2, HEAD_DIM), d_k.reshape(B, S, DIFF_HEADS, 2, HEAD_DIM),
        d_v.reshape(B, S, DIFF_HEADS, 2 * HEAD_DIM), diff_q_norm, diff_k_norm,
        diff_lambda_q1, diff_lambda_k1, diff_lambda_q2, diff_lambda_k2, diff_subln_w, lambda_init)
    x = x + jnp.concatenate([na_out, diff_out], axis=-1) @ w_out
    x = x + peer_ffn(rmsnorm(x, norm2_w), peer_w_q, peer_sub_keys, peer_u, peer_v)
    return x


def setup_inputs(seed: int = 0) -> dict:
    key = jax.random.key(seed)
    ks = jax.random.split(key, 20)

    def nrm(k, shape, s):
        return s * jax.random.normal(k, shape, jnp.float32)

    L = DEPTH
    return {
        "x_prompt": nrm(ks[0], (BATCH, SEQ, D_MODEL), 1.0),
        "x_sample": nrm(ks[1], (DEC_BATCH, DEC_SEQ, D_MODEL), 1.0),
        "norm1_w": 1.0 + nrm(ks[2], (L, D_MODEL), 0.02),
        "w_in": nrm(ks[3], (L, D_MODEL, IN_WIDTH), D_MODEL ** -0.5),
        "na_q_norm": 1.0 + nrm(ks[4], (L, HEAD_DIM), 0.02),
        "na_k_norm": 1.0 + nrm(ks[5], (L, HEAD_DIM), 0.02),
        "na_rpb": nrm(ks[6], (L, NA_HEADS, 2 * NA_KH_MAX - 1, 2 * NA_KW - 1), 0.1),
        "diff_q_norm": 1.0 + nrm(ks[7], (L, HEAD_DIM), 0.02),
        "diff_k_norm": 1.0 + nrm(ks[8], (L, HEAD_DIM), 0.02),
        "diff_lambda_q1": nrm(ks[9], (L, HEAD_DIM), 0.1),
        "diff_lambda_k1": nrm(ks[10], (L, HEAD_DIM), 0.1),
        "diff_lambda_q2": nrm(ks[11], (L, HEAD_DIM), 0.1),
        "diff_lambda_k2": nrm(ks[12], (L, HEAD_DIM), 0.1),
        "diff_subln_w": 1.0 + nrm(ks[13], (L, 2 * HEAD_DIM), 0.02),
        "w_out": nrm(ks[14], (L, MIX_WIDTH, D_MODEL), MIX_WIDTH ** -0.5),
        "norm2_w": 1.0 + nrm(ks[15], (L, D_MODEL), 0.02),
        "peer_w_q": nrm(ks[16], (L, D_MODEL, PEER_HEADS * PEER_KEY_DIM), D_MODEL ** -0.5),
        "peer_sub_keys": nrm(ks[17], (L, PEER_HEADS, 2, N_KEYS, PEER_HALF), PEER_HALF ** -0.5),
        "peer_u": nrm(ks[18], (L, N_EXPERTS, D_MODEL), D_MODEL ** -0.5),
        "peer_v": nrm(ks[19], (L, N_EXPERTS, D_MODEL), 0.5),
    }


def reference(x_prompt, x_sample, norm1_w, w_in, na_q_norm, na_k_norm, na_rpb,
              diff_q_norm, diff_k_norm, diff_lambda_q1, diff_lambda_k1,
              diff_lambda_q2, diff_lambda_k2, diff_subln_w, w_out, norm2_w,
              peer_w_q, peer_sub_keys, peer_u, peer_v):
    y_prompt = x_prompt
    y_sample = x_sample
    for l in range(DEPTH):
        params = (norm1_w[l], w_in[l], na_q_norm[l], na_k_norm[l], na_rpb[l],
                  diff_q_norm[l], diff_k_norm[l], diff_lambda_q1[l], diff_lambda_k1[l],
                  diff_lambda_q2[l], diff_lambda_k2[l], diff_subln_w[l], w_out[l], norm2_w[l],
                  peer_w_q[l], peer_sub_keys[l], peer_u[l], peer_v[l])
        y_prompt = encoder_layer(y_prompt, l, *params)
        y_sample = encoder_layer(y_sample, l, *params)
    return (y_prompt, y_sample)
```

```python
import functools
import math

import jax
import jax.numpy as jnp
import numpy as np
from jax import lax
from jax.experimental import pallas as pl
from jax.experimental.pallas import tpu as pltpu

F32 = jnp.float32
BF16 = jnp.bfloat16

D_MODEL = 1024
GRID_W = 64
HEAD_DIM = 64
NA_HEADS = 8
NA_KH = 8
NA_KW = 16
DIFF_HEADS = 4
SECTION = 512
ROT_DIM = HEAD_DIM // 4
ROPE_THETA = 500000.0
PEER_HEADS = 8
N_KEYS = 128
PEER_TOPK = 16
N_EXPERTS = N_KEYS * N_KEYS
HK = PEER_HEADS * PEER_TOPK
EPS = 1e-6
NEG_BIG = -1e30
LAMBDA_INIT = 0.8 - 0.6 * math.exp(-0.3 * 0)

LANES = 128
SUBLANES = 8
VMEM_LIMIT = 56 * 1024 * 1024

TM_IN = 512
NA_ROWS_PER_STEP = 8
TQ = 512
TK = 512
TM_OUT = 256
TB = 128


def _cparams(sem):
    return pltpu.CompilerParams(dimension_semantics=sem, vmem_limit_bytes=VMEM_LIMIT)


def _const_spec(shape):
    nd = len(shape)
    return pl.BlockSpec(shape, lambda *_: (0,) * nd)


def _inproj_kernel(x_ref, n1_ref, w_ref, bd_ref, nw_ref, rc_ref, ra_ref, rb_ref,
                   naq_ref, nak_ref, nav_ref, dq_ref, dk_ref, dv_ref):
    x = x_ref[...]
    ms = jnp.mean(x * x, axis=-1, keepdims=True)
    h = (x * lax.rsqrt(ms + EPS) * n1_ref[...]).astype(BF16)
    bd = bd_ref[...]

    def proj(c):
        return jnp.dot(h, w_ref[:, c * SECTION:(c + 1) * SECTION], preferred_element_type=F32)

    def qknorm(y, row):
        sq = y * y
        hi = sq.astype(BF16)
        lo = (sq - hi.astype(F32)).astype(BF16)
        msq = (jnp.dot(hi, bd, preferred_element_type=F32)
               + jnp.dot(lo, bd, preferred_element_type=F32))
        return y * lax.rsqrt(msq + EPS) * nw_ref[row:row + 1, :]

    def rope(y):
        return (y * rc_ref[...]
                + pltpu.roll(y, SECTION - ROT_DIM // 2, axis=1) * ra_ref[...]
                + pltpu.roll(y, ROT_DIM // 2, axis=1) * rb_ref[...])

    scale = HEAD_DIM ** -0.5
    naq_ref[...] = (qknorm(proj(0), 0) * scale).astype(BF16)
    nak_ref[...] = qknorm(proj(1), 1).astype(BF16)
    nav_ref[...] = proj(2).astype(BF16)
    dq_ref[...] = (rope(qknorm(proj(3), 2)) * scale).astype(BF16)
    dk_ref[...] = rope(qknorm(proj(4), 3)).astype(BF16)
    dv_ref[...] = proj(5).astype(BF16)


def _rope_tables(seq):
    pos = jnp.arange(seq, dtype=F32)
    inv = ROPE_THETA ** (-jnp.arange(0, ROT_DIM, 2, dtype=F32) / ROT_DIM)
    ang = pos[:, None] * inv[None, :]
    cos, sin = jnp.cos(ang), jnp.sin(ang)
    half = ROT_DIM // 2
    pad = HEAD_DIM - ROT_DIM
    c_head = jnp.concatenate([cos, cos, jnp.ones((seq, pad), F32)], -1)
    a_head = jnp.concatenate([-sin, jnp.zeros((seq, pad + half), F32)], -1)
    b_head = jnp.concatenate([jnp.zeros((seq, half), F32), sin, jnp.zeros((seq, pad), F32)], -1)
    reps = SECTION // HEAD_DIM
    return (jnp.tile(c_head, (1, reps)), jnp.tile(a_head, (1, reps)), jnp.tile(b_head, (1, reps)))


def _in_projection(x2d, seq, norm1_w, w_in_bf, qk_w, rope_tabs):
    n = x2d.shape[0]
    tm = min(TM_IN, seq)
    steps_per_seq = seq // tm
    gid = np.arange(SECTION) // HEAD_DIM
    bd = jnp.asarray((gid[:, None] == gid[None, :]).astype(np.float32) / HEAD_DIM, BF16)
    tok = pl.BlockSpec((tm, SECTION), lambda i: (i, 0))
    pos = pl.BlockSpec((tm, SECTION), lambda i: (i % steps_per_seq, 0))
    out = jax.ShapeDtypeStruct((n, SECTION), BF16)
    return pl.pallas_call(
        _inproj_kernel,
        grid=(n // tm,),
        in_specs=[pl.BlockSpec((tm, D_MODEL), lambda i: (i, 0)),
                  _const_spec((1, D_MODEL)),
                  _const_spec((D_MODEL, 6 * SECTION)),
                  _const_spec((SECTION, SECTION)),
                  _const_spec((4, SECTION)),
                  pos, pos, pos],
        out_specs=[tok] * 6,
        out_shape=[out] * 6,
        compiler_params=_cparams(("parallel",)),
    )(x2d, norm1_w.reshape(1, D_MODEL), w_in_bf, bd, qk_w, *rope_tabs)


def _na_bias(rpb, rows):
    off = np.arange(NA_KH)[:, None, None, None]
    c = np.arange(GRID_W)[None, :, None, None]
    i = np.arange(NA_KH)[None, None, :, None]
    kc = np.arange(GRID_W)[None, None, None, :]
    rel_r = np.broadcast_to(i - off + (NA_KH - 1), (NA_KH, GRID_W, NA_KH, GRID_W))
    cs = np.clip(c - NA_KW // 2, 0, GRID_W - NA_KW)
    valid = np.broadcast_to((kc >= cs) & (kc < cs + NA_KW), rel_r.shape)
    rel_c = np.broadcast_to(np.clip(kc - c + (NA_KW - 1), 0, 2 * NA_KW - 2), rel_r.shape)
    vals = rpb[:, rel_r, rel_c]
    vals = jnp.where(valid[None], vals, NEG_BIG)
    return vals.transpose(1, 0, 2, 3, 4).reshape(NA_KH, NA_HEADS, GRID_W, NA_KH * GRID_W).astype(F32)


def _na_kernel(q_ref, k_ref, v_ref, b_ref, o_ref, *, rows):
    j = pl.program_id(1)
    win = NA_KH * GRID_W

    def row_body(rr, carry):
        r = j * NA_ROWS_PER_STEP + rr
        rs = jnp.clip(r - NA_KH // 2, 0, rows - NA_KH)
        off = r - rs
        kstart = pl.multiple_of(rs * GRID_W, GRID_W)
        qstart = pl.multiple_of(rr * GRID_W, GRID_W)
        kw = k_ref[pl.ds(kstart, win), :]
        vw = v_ref[pl.ds(kstart, win), :]
        qrow = q_ref[pl.ds(qstart, GRID_W), :]
        outs = []
        for h in range(NA_HEADS):
            sl = slice(h * HEAD_DIM, (h + 1) * HEAD_DIM)
            s = lax.dot_general(qrow[:, sl], kw[:, sl], (((1,), (1,)), ((), ())),
                                preferred_element_type=F32)
            s = s + b_ref[off, h]
            m = jnp.max(s, axis=-1, keepdims=True)
            p = jnp.exp(s - m)
            l = jnp.sum(p, axis=-1, keepdims=True)
            o = jnp.dot(p.astype(BF16), vw[:, sl], preferred_element_type=F32)
            outs.append(o / l)
        o_ref[pl.ds(qstart, GRID_W), :] = jnp.concatenate(outs, axis=-1).astype(BF16)
        return carry

    lax.fori_loop(0, NA_ROWS_PER_STEP, row_body, 0)


def _neighbourhood_attention(q, k, v, bias, batch, seq):
    rows = seq // GRID_W
    assert rows >= NA_KH and rows % NA_ROWS_PER_STEP == 0
    nblk = rows // NA_ROWS_PER_STEP
    tq = NA_ROWS_PER_STEP * GRID_W
    qspec = pl.BlockSpec((tq, SECTION), lambda b, j: (b * nblk + j, 0))
    kvspec = pl.BlockSpec((seq, SECTION), lambda b, j: (b, 0))
    return pl.pallas_call(
        functools.partial(_na_kernel, rows=rows),
        grid=(batch, nblk),
        in_specs=[qspec, kvspec, kvspec, _const_spec(bias.shape)],
        out_specs=qspec,
        out_shape=jax.ShapeDtypeStruct(q.shape, BF16),
        compiler_params=_cparams(("parallel", "arbitrary")),
    )(q, k, v, bias)


def _diff_kernel(q_ref, k_ref, v_ref, lam_ref, sub_ref, o_ref, *, seq, tq, tk):
    q = q_ref[...]
    lane = lax.broadcasted_iota(jnp.int32, q.shape, 1)
    zero = jnp.zeros_like(q)
    q_maps = (jnp.where(lane < HEAD_DIM, q, zero), jnp.where(lane >= HEAD_DIM, q, zero))

    lp = lam_ref[...]
    lam = (jnp.exp(jnp.sum(lp[0:1] * lp[1:2], axis=-1, keepdims=True))
           - jnp.exp(jnp.sum(lp[2:3] * lp[3:4], axis=-1, keepdims=True)) + LAMBDA_INIT)

    def kv_body(c, carry):
        start = pl.multiple_of(c * tk, tk)
        kc = k_ref[pl.ds(start, tk), :]
        vc = v_ref[pl.ds(start, tk), :]
        new = []
        for mp in range(2):
            m, l, acc = carry[mp]
            s = lax.dot_general(q_maps[mp], kc, (((1,), (1,)), ((), ())), preferred_element_type=F32)
            mn = jnp.maximum(m, jnp.max(s, axis=-1, keepdims=True))
            alpha = jnp.exp(m - mn)
            p = jnp.exp(s - mn)
            l = alpha * l + jnp.sum(p, axis=-1, keepdims=True)
            acc = alpha * acc + jnp.dot(p.astype(BF16), vc, preferred_element_type=F32)
            new.append((mn, l, acc))
        return tuple(new)

    init = tuple((jnp.full((tq, 1), -jnp.inf, F32), jnp.zeros((tq, 1), F32),
                  jnp.zeros((tq, 2 * HEAD_DIM), F32)) for _ in range(2))
    (_, l0, a0), (_, l1, a1) = lax.fori_loop(0, seq // tk, kv_body, init)
    o = a0 / l0 - lam * (a1 / l1)
    ms = jnp.mean(o * o, axis=-1, keepdims=True)
    y = o * lax.rsqrt(ms + EPS) * sub_ref[...]
    o_ref[...] = (y * (1.0 - LAMBDA_INIT)).astype(BF16)


def _diff_attention(q, k, v, lam_params, subln_w, batch, seq):
    tq, tk = min(TQ, seq), min(TK, seq)
    nq = seq // tq
    width = 2 * HEAD_DIM
    qspec = pl.BlockSpec((tq, width), lambda b, h, i: (b * nq + i, h))
    kvspec = pl.BlockSpec((seq, width), lambda b, h, i: (b, h))
    return pl.pallas_call(
        functools.partial(_diff_kernel, seq=seq, tq=tq, tk=tk),
        grid=(batch, DIFF_HEADS, nq),
        in_specs=[qspec, kvspec, kvspec, _const_spec((4, HEAD_DIM)), _const_spec((1, width))],
        out_specs=qspec,
        out_shape=jax.ShapeDtypeStruct(q.shape, BF16),
        compiler_params=_cparams(("parallel", "parallel", "arbitrary")),
    )(q, k, v, lam_params, subln_w.reshape(1, width))


def _top16(sc, payload=None):
    iota = lax.broadcasted_iota(jnp.int32, sc.shape, 0)
    big = jnp.int32(sc.shape[0])
    vals, picks = [], []
    for _ in range(PEER_TOPK):
        m = jnp.max(sc, axis=0, keepdims=True)
        ix = jnp.min(jnp.where(sc == m, iota, big), axis=0, keepdims=True)
        sel = iota == ix
        if payload is None:
            picks.append(ix)
        else:
            picks.append(jnp.sum(jnp.where(sel, payload, 0), axis=0, keepdims=True))
        sc = jnp.where(sel, -jnp.inf, sc)
        vals.append(m)
    return jnp.concatenate(vals, axis=0), jnp.concatenate(picks, axis=0)


def _route_kernel(na_ref, df_ref, x_ref, wo_ref, n2_ref, wq_ref, sk_ref,
                  x1_ref, xn_ref, eid_ref, g_ref, q_scr):
    tm = x_ref.shape[0]
    x1 = (x_ref[...]
          + jnp.dot(na_ref[...], wo_ref[0:SECTION, :], preferred_element_type=F32)
          + jnp.dot(df_ref[...], wo_ref[SECTION:2 * SECTION, :], preferred_element_type=F32))
    x1_ref[...] = x1
    ms = jnp.mean(x1 * x1, axis=-1, keepdims=True)
    xn = x1 * lax.rsqrt(ms + EPS) * n2_ref[...]
    xn_ref[...] = xn
    xb = xn.astype(BF16)
    for c in range(2 * PEER_HEADS):
        q_scr[c] = jnp.dot(xb, wq_ref[:, c * LANES:(c + 1) * LANES],
                           preferred_element_type=F32).astype(BF16)

    def head_body(h, carry):
        tops = []
        for p in range(2):
            sc = lax.dot_general(sk_ref[2 * h + p], q_scr[2 * h + p], (((1,), (1,)), ((), ())),
                                 preferred_element_type=F32)
            tops.append(_top16(sc))
        (s0, i0), (s1, i1) = tops
        cand_s = jnp.concatenate([s0[a:a + 1] + s1 for a in range(PEER_TOPK)], axis=0)
        cand_i = jnp.concatenate([i0[a:a + 1] * N_KEYS + i1 for a in range(PEER_TOPK)], axis=0)
        top_s, eid = _top16(cand_s, cand_i)
        e = jnp.exp(top_s - top_s[0:1])
        g = e / jnp.sum(e, axis=0, keepdims=True)
        row = pl.multiple_of(h * PEER_TOPK, PEER_TOPK)
        for blk in range(tm // TB):
            eid_ref[blk, pl.ds(row, PEER_TOPK), :] = eid[:, blk * TB:(blk + 1) * TB]
            g_ref[blk, pl.ds(row, PEER_TOPK), :] = g[:, blk * TB:(blk + 1) * TB]
        return carry

    lax.fori_loop(0, PEER_HEADS, head_body, 0)


def _out_and_route(na, df, x2d, w_out_bf, norm2_w, w_q_bf, sk_bf):
    n = x2d.shape[0]
    tm = TM_OUT
    nb = tm // TB
    tok = lambda w: pl.BlockSpec((tm, w), lambda i: (i, 0))
    slot = pl.BlockSpec((nb, HK, TB), lambda i: (i, 0, 0))
    return pl.pallas_call(
        _route_kernel,
        grid=(n // tm,),
        in_specs=[tok(SECTION), tok(SECTION), tok(D_MODEL),
                  _const_spec((D_MODEL, D_MODEL)), _const_spec((1, D_MODEL)),
                  _const_spec(w_q_bf.shape), _const_spec(sk_bf.shape)],
        out_specs=[tok(D_MODEL), tok(D_MODEL), slot, slot],
        out_shape=[jax.ShapeDtypeStruct((n, D_MODEL), F32), jax.ShapeDtypeStruct((n, D_MODEL), F32),
                   jax.ShapeDtypeStruct((n // TB, HK, TB), jnp.int32),
                   jax.ShapeDtypeStruct((n // TB, HK, TB), F32)],
        scratch_shapes=[pltpu.VMEM((2 * PEER_HEADS, tm, LANES), BF16)],
        compiler_params=_cparams(("parallel",)),
    )(na, df, x2d, w_out_bf, norm2_w.reshape(1, D_MODEL), w_q_bf, sk_bf)


HALF_SUB = SUBLANES // 2
_TREE_SLOT_ROW = (0, 4, 2, 6, 1, 5, 3, 7)


def _pack_table(tab):
    bits = lax.bitcast_convert_type(tab.astype(BF16), jnp.uint16).astype(jnp.uint32)
    half = D_MODEL // 2
    words = bits[:, :half] | (bits[:, half:] << 16)
    return words.reshape(tab.shape[0], HALF_SUB, LANES)


def _unpack(words):
    lo = pltpu.bitcast(words << 16, F32)
    hi = pltpu.bitcast(words & jnp.uint32(0xFFFF0000), F32)
    return lo, hi


def _peer_u_kernel(eid_ref, xn_ref, g_ref, tab_ref, w_ref):
    sub = lax.broadcasted_iota(jnp.int32, (SUBLANES, LANES), 0)
    lane = lax.broadcasted_iota(jnp.int32, (SUBLANES, LANES), 1)
    pair_mask = (sub % 4) < 2
    even_mask = (sub % 2) == 0

    def tok_body(t, accs):
        x = xn_ref[t]
        xl, xh = x[0:HALF_SUB], x[HALF_SUB:SUBLANES]
        out = []
        for s in range(HK // SUBLANES):
            prods = [None] * SUBLANES
            for slot in range(SUBLANES):
                j = s * SUBLANES + _TREE_SLOT_ROW[slot]
                lo, hi = _unpack(tab_ref[eid_ref[0, j, t]])
                prods[slot] = lo * xl + hi * xh
            v = [jnp.concatenate([prods[2 * a], prods[2 * a + 1]], axis=0) for a in range(4)]
            w = [jnp.where(pair_mask,
                           v[2 * a] + pltpu.roll(v[2 * a], SUBLANES - 2, axis=0),
                           v[2 * a + 1] + pltpu.roll(v[2 * a + 1], 2, axis=0)) for a in range(2)]
            y = jnp.where(even_mask,
                          w[0] + pltpu.roll(w[0], SUBLANES - 1, axis=0),
                          w[1] + pltpu.roll(w[1], 1, axis=0))
            tot = jnp.sum(y, axis=1, keepdims=True)
            out.append(jnp.where(lane == t, tot, accs[s]))
        return tuple(out)

    init = tuple(jnp.zeros((SUBLANES, LANES), F32) for _ in range(HK // SUBLANES))
    accs = lax.fori_loop(0, TB, tok_body, init)
    a = jnp.concatenate(accs, axis=0)
    gelu = 0.5 * a * (1.0 + lax.erf(a * (2.0 ** -0.5)))
    w_ref[0] = g_ref[0] * gelu


def _peer_v_kernel(eid_ref, w_ref, x1_ref, tab_ref, o_ref):
    n_acc = 4

    def tok_body(t, carry):
        acc_lo = [jnp.zeros((HALF_SUB, LANES), F32) for _ in range(n_acc)]
        acc_hi = [jnp.zeros((HALF_SUB, LANES), F32) for _ in range(n_acc)]
        for j in range(HK):
            lo, hi = _unpack(tab_ref[eid_ref[0, j, t]])
            wgt = w_ref[0, j, t]
            acc_lo[j % n_acc] = acc_lo[j % n_acc] + wgt * lo
            acc_hi[j % n_acc] = acc_hi[j % n_acc] + wgt * hi
        lo = (acc_lo[0] + acc_lo[1]) + (acc_lo[2] + acc_lo[3])
        hi = (acc_hi[0] + acc_hi[1]) + (acc_hi[2] + acc_hi[3])
        o_ref[t] = x1_ref[t] + jnp.concatenate([lo, hi], axis=0)
        return carry

    lax.fori_loop(0, TB, tok_body, 0)


def _peer_experts(eid, g, xn3, x13, u_pack, v_pack):
    n = xn3.shape[0]
    nblk = n // TB
    smem_slot = pl.BlockSpec((1, HK, TB), lambda i: (i, 0, 0), memory_space=pltpu.SMEM)
    vmem_slot = pl.BlockSpec((1, HK, TB), lambda i: (i, 0, 0))
    tok = pl.BlockSpec((TB, SUBLANES, LANES), lambda i: (i, 0, 0))
    table = pl.BlockSpec(u_pack.shape, lambda i: (0, 0, 0), pipeline_mode=pl.Buffered(1))
    w = pl.pallas_call(
        _peer_u_kernel,
        grid=(nblk,),
        in_specs=[smem_slot, tok, vmem_slot, table],
        out_specs=vmem_slot,
        out_shape=jax.ShapeDtypeStruct((nblk, HK, TB), F32),
        compiler_params=_cparams(("arbitrary",)),
    )(eid, xn3, g, u_pack)
    return pl.pallas_call(
        _peer_v_kernel,
        grid=(nblk,),
        in_specs=[smem_slot, smem_slot, tok, table],
        out_specs=tok,
        out_shape=jax.ShapeDtypeStruct((n, SUBLANES, LANES), F32),
        compiler_params=_cparams(("arbitrary",)),
    )(eid, w, x13, v_pack)


def _encoder_layer(x, p):
    batch, seq, _ = x.shape
    n = batch * seq
    x2d = x.reshape(n, D_MODEL)
    naq, nak, nav, dq, dk, dv = _in_projection(x2d, seq, p["norm1_w"], p["w_in"], p["qk_w"], p["rope"])
    na = _neighbourhood_attention(naq, nak, nav, p["na_bias"], batch, seq)
    df = _diff_attention(dq, dk, dv, p["lam"], p["subln_w"], batch, seq)
    x1, xn, eid, g = _out_and_route(na, df, x2d, p["w_out"], p["norm2_w"], p["w_q"], p["sub_keys"])
    y3 = _peer_experts(eid, g, xn.reshape(n, SUBLANES, LANES), x1.reshape(n, SUBLANES, LANES),
                       p["u_pack"], p["v_pack"])
    return y3.reshape(batch, seq, D_MODEL)


def kernel(x_prompt, x_sample, norm1_w, w_in, na_q_norm, na_k_norm, na_rpb, diff_q_norm, diff_k_norm,
           diff_lambda_q1, diff_lambda_k1, diff_lambda_q2, diff_lambda_k2, diff_subln_w, w_out, norm2_w,
           peer_w_q, peer_sub_keys, peer_u, peer_v):
    assert norm1_w.shape[0] == 1, "single-layer problem"
    reps = SECTION // HEAD_DIM
    seqs = {x_prompt.shape[1], x_sample.shape[1]}
    params = {
        "norm1_w": norm1_w[0],
        "w_in": w_in[0].astype(BF16),
        "qk_w": jnp.stack([jnp.tile(w[0], reps) for w in (na_q_norm, na_k_norm, diff_q_norm, diff_k_norm)]),
        "lam": jnp.stack([diff_lambda_q1[0], diff_lambda_k1[0], diff_lambda_q2[0], diff_lambda_k2[0]]),
        "subln_w": diff_subln_w[0],
        "w_out": w_out[0].astype(BF16),
        "norm2_w": norm2_w[0],
        "w_q": peer_w_q[0].astype(BF16),
        "sub_keys": peer_sub_keys[0].reshape(2 * PEER_HEADS, N_KEYS, N_KEYS).astype(BF16),
        "u_pack": _pack_table(peer_u[0]),
        "v_pack": _pack_table(peer_v[0]),
    }
    outs = []
    for x in (x_prompt, x_sample):
        seq = x.shape[1]
        p = dict(params, rope=_rope_tables(seq), na_bias=_na_bias(na_rpb[0], seq // GRID_W))
        outs.append(_encoder_layer(x, p))
    del seqs
    return tuple(outs)
```

```python
import functools
import math

import jax
import jax.numpy as jnp
import numpy as np
from jax import lax
from jax.experimental import pallas as pl
from jax.experimental.pallas import tpu as pltpu

F32 = jnp.float32
BF16 = jnp.bfloat16

D_MODEL = 1024
GRID_W = 64
HEAD_DIM = 64
NA_HEADS = 8
NA_KH = 8
NA_KW = 16
DIFF_HEADS = 4
SECTION = 512
ROT_DIM = HEAD_DIM // 4
ROPE_THETA = 500000.0
PEER_HEADS = 8
N_KEYS = 128
PEER_TOPK = 16
N_EXPERTS = N_KEYS * N_KEYS
HK = PEER_HEADS * PEER_TOPK
EPS = 1e-6
NEG_BIG = -1e30
LAMBDA_INIT = 0.8 - 0.6 * math.exp(-0.3 * 0)

LANES = 128
SUBLANES = 8
VMEM_LIMIT = 56 * 1024 * 1024

TM_IN = 512
NA_ROWS_PER_STEP = 8
TQ = 512
TK = 512
TM_OUT = 256
TB = 128


def _cparams(sem):
    return pltpu.CompilerParams(dimension_semantics=sem, vmem_limit_bytes=VMEM_LIMIT)


def _const_spec(shape):
    nd = len(shape)
    return pl.BlockSpec(shape, lambda *_: (0,) * nd)


def _inproj_kernel(x_ref, n1_ref, w_ref, bd_ref, nw_ref, rc_ref, ra_ref, rb_ref,
                   naq_ref, nak_ref, nav_ref, dq_ref, dk_ref, dv_ref):
    x = x_ref[...]
    ms = jnp.mean(x * x, axis=-1, keepdims=True)
    h = (x * lax.rsqrt(ms + EPS) * n1_ref[...]).astype(BF16)
    bd = bd_ref[...]

    def proj(c):
        return jnp.dot(h, w_ref[:, c * SECTION:(c + 1) * SECTION], preferred_element_type=F32)

    def qknorm(y, row):
        sq = y * y
        hi = sq.astype(BF16)
        lo = (sq - hi.astype(F32)).astype(BF16)
        msq = (jnp.dot(hi, bd, preferred_element_type=F32)
               + jnp.dot(lo, bd, preferred_element_type=F32))
        return y * lax.rsqrt(msq + EPS) * nw_ref[row:row + 1, :]

    def rope(y):
        return (y * rc_ref[...]
                + pltpu.roll(y, SECTION - ROT_DIM // 2, axis=1) * ra_ref[...]
                + pltpu.roll(y, ROT_DIM // 2, axis=1) * rb_ref[...])

    scale = HEAD_DIM ** -0.5
    naq_ref[...] = (qknorm(proj(0), 0) * scale).astype(BF16)
    nak_ref[...] = qknorm(proj(1), 1).astype(BF16)
    nav_ref[...] = proj(2).astype(BF16)
    dq_ref[...] = (rope(qknorm(proj(3), 2)) * scale).astype(BF16)
    dk_ref[...] = rope(qknorm(proj(4), 3)).astype(BF16)
    dv_ref[...] = proj(5).astype(BF16)


def _rope_tables(seq):
    pos = jnp.arange(seq, dtype=F32)
    inv = ROPE_THETA ** (-jnp.arange(0, ROT_DIM, 2, dtype=F32) / ROT_DIM)
    ang = pos[:, None] * inv[None, :]
    cos, sin = jnp.cos(ang), jnp.sin(ang)
    half = ROT_DIM // 2
    pad = HEAD_DIM - ROT_DIM
    c_head = jnp.concatenate([cos, cos, jnp.ones((seq, pad), F32)], -1)
    a_head = jnp.concatenate([-sin, jnp.zeros((seq, pad + half), F32)], -1)
    b_head = jnp.concatenate([jnp.zeros((seq, half), F32), sin, jnp.zeros((seq, pad), F32)], -1)
    reps = SECTION // HEAD_DIM
    return (jnp.tile(c_head, (1, reps)), jnp.tile(a_head, (1, reps)), jnp.tile(b_head, (1, reps)))


def _in_projection(x2d, seq, norm1_w, w_in_bf, qk_w, rope_tabs):
    n = x2d.shape[0]
    tm = min(TM_IN, seq)
    steps_per_seq = seq // tm
    gid = np.arange(SECTION) // HEAD_DIM
    bd = jnp.asarray((gid[:, None] == gid[None, :]).astype(np.float32) / HEAD_DIM, BF16)
    tok = pl.BlockSpec((tm, SECTION), lambda i: (i, 0))
    pos = pl.BlockSpec((tm, SECTION), lambda i: (i % steps_per_seq, 0))
    out = jax.ShapeDtypeStruct((n, SECTION), BF16)
    return pl.pallas_call(
        _inproj_kernel,
        grid=(n // tm,),
        in_specs=[pl.BlockSpec((tm, D_MODEL), lambda i: (i, 0)),
                  _const_spec((1, D_MODEL)),
                  _const_spec((D_MODEL, 6 * SECTION)),
                  _const_spec((SECTION, SECTION)),
                  _const_spec((4, SECTION)),
                  pos, pos, pos],
        out_specs=[tok] * 6,
        out_shape=[out] * 6,
        compiler_params=_cparams(("parallel",)),
    )(x2d, norm1_w.reshape(1, D_MODEL), w_in_bf, bd, qk_w, *rope_tabs)


def _na_bias(rpb):
    c = np.arange(GRID_W)[:, None]
    kc = np.arange(GRID_W)[None, :]
    cs = np.clip(c - NA_KW // 2, 0, GRID_W - NA_KW)
    valid = (kc >= cs) & (kc < cs + NA_KW)
    rel_c = kc - c + (NA_KW - 1)
    onehot = ((np.arange(2 * NA_KW - 1)[:, None, None] == rel_c[None]) & valid[None]).astype(np.float32)
    col = jnp.einsum("hrx,xck->hrck", rpb.astype(F32), onehot, precision=lax.Precision.HIGHEST)
    col = jnp.where(valid, col, NEG_BIG)
    per_off = [col[:, NA_KH - 1 - o:2 * NA_KH - 1 - o] for o in range(NA_KH)]
    b = jnp.stack(per_off).transpose(0, 1, 3, 2, 4)
    return b.reshape(NA_KH, NA_HEADS, GRID_W, NA_KH * GRID_W)


def _na_kernel(q_ref, k_ref, v_ref, b_ref, o_ref, *, rows):
    j = pl.program_id(1)
    win = NA_KH * GRID_W

    def row_body(rr, carry):
        r = j * NA_ROWS_PER_STEP + rr
        rs = jnp.clip(r - NA_KH // 2, 0, rows - NA_KH)
        off = r - rs
        kstart = pl.multiple_of(rs * GRID_W, GRID_W)
        qstart = pl.multiple_of(rr * GRID_W, GRID_W)
        kw = k_ref[pl.ds(kstart, win), :]
        vw = v_ref[pl.ds(kstart, win), :]
        qrow = q_ref[pl.ds(qstart, GRID_W), :]
        outs = []
        for h in range(NA_HEADS):
            sl = slice(h * HEAD_DIM, (h + 1) * HEAD_DIM)
            s = lax.dot_general(qrow[:, sl], kw[:, sl], (((1,), (1,)), ((), ())),
                                preferred_element_type=F32)
            s = s + b_ref[off, h]
            m = jnp.max(s, axis=-1, keepdims=True)
            p = jnp.exp(s - m)
            l = jnp.sum(p, axis=-1, keepdims=True)
            o = jnp.dot(p.astype(BF16), vw[:, sl], preferred_element_type=F32)
            outs.append(o / l)
        o_ref[pl.ds(qstart, GRID_W), :] = jnp.concatenate(outs, axis=-1).astype(BF16)
        return carry

    lax.fori_loop(0, NA_ROWS_PER_STEP, row_body, 0)


def _neighbourhood_attention(q, k, v, bias, batch, seq):
    rows = seq // GRID_W
    assert rows >= NA_KH and rows % NA_ROWS_PER_STEP == 0
    nblk = rows // NA_ROWS_PER_STEP
    tq = NA_ROWS_PER_STEP * GRID_W
    qspec = pl.BlockSpec((tq, SECTION), lambda b, j: (b * nblk + j, 0))
    kvspec = pl.BlockSpec((seq, SECTION), lambda b, j: (b, 0))
    return pl.pallas_call(
        functools.partial(_na_kernel, rows=rows),
        grid=(batch, nblk),
        in_specs=[qspec, kvspec, kvspec, _const_spec(bias.shape)],
        out_specs=qspec,
        out_shape=jax.ShapeDtypeStruct(q.shape, BF16),
        compiler_params=_cparams(("parallel", "arbitrary")),
    )(q, k, v, bias)


def _diff_kernel(q_ref, k_ref, v_ref, lam_ref, sub_ref, o_ref, *, seq, tq, tk):
    q = q_ref[...]
    lane = lax.broadcasted_iota(jnp.int32, q.shape, 1)
    zero = jnp.zeros_like(q)
    q_maps = (jnp.where(lane < HEAD_DIM, q, zero), jnp.where(lane >= HEAD_DIM, q, zero))

    lp = lam_ref[...]
    lam = (jnp.exp(jnp.sum(lp[0:1] * lp[1:2], axis=-1, keepdims=True))
           - jnp.exp(jnp.sum(lp[2:3] * lp[3:4], axis=-1, keepdims=True)) + LAMBDA_INIT)

    def kv_body(c, carry):
        start = pl.multiple_of(c * tk, tk)
        kc = k_ref[pl.ds(start, tk), :]
        vc = v_ref[pl.ds(start, tk), :]
        new = []
        for mp in range(2):
            m, l, acc = carry[mp]
            s = lax.dot_general(q_maps[mp], kc, (((1,), (1,)), ((), ())), preferred_element_type=F32)
            mn = jnp.maximum(m, jnp.max(s, axis=-1, keepdims=True))
            alpha = jnp.exp(m - mn)
            p = jnp.exp(s - mn)
            l = alpha * l + jnp.sum(p, axis=-1, keepdims=True)
            acc = alpha * acc + jnp.dot(p.astype(BF16), vc, preferred_element_type=F32)
            new.append((mn, l, acc))
        return tuple(new)

    init = tuple((jnp.full((tq, 1), -jnp.inf, F32), jnp.zeros((tq, 1), F32),
                  jnp.zeros((tq, 2 * HEAD_DIM), F32)) for _ in range(2))
    (_, l0, a0), (_, l1, a1) = lax.fori_loop(0, seq // tk, kv_body, init)
    o = a0 / l0 - lam * (a1 / l1)
    ms = jnp.mean(o * o, axis=-1, keepdims=True)
    y = o * lax.rsqrt(ms + EPS) * sub_ref[...]
    o_ref[...] = (y * (1.0 - LAMBDA_INIT)).astype(BF16)


def _diff_attention(q, k, v, lam_params, subln_w, batch, seq):
    tq, tk = min(TQ, seq), min(TK, seq)
    nq = seq // tq
    width = 2 * HEAD_DIM
    qspec = pl.BlockSpec((tq, width), lambda b, h, i: (b * nq + i, h))
    kvspec = pl.BlockSpec((seq, width), lambda b, h, i: (b, h))
    return pl.pallas_call(
        functools.partial(_diff_kernel, seq=seq, tq=tq, tk=tk),
        grid=(batch, DIFF_HEADS, nq),
        in_specs=[qspec, kvspec, kvspec, _const_spec((4, HEAD_DIM)), _const_spec((1, width))],
        out_specs=qspec,
        out_shape=jax.ShapeDtypeStruct(q.shape, BF16),
        compiler_params=_cparams(("parallel", "parallel", "arbitrary")),
    )(q, k, v, lam_params, subln_w.reshape(1, width))


def _top16(sc, payload=None):
    iota = lax.broadcasted_iota(jnp.int32, sc.shape, 0)
    big = jnp.int32(sc.shape[0])
    vals, picks = [], []
    for _ in range(PEER_TOPK):
        m = jnp.max(sc, axis=0, keepdims=True)
        ix = jnp.min(jnp.where(sc == m, iota, big), axis=0, keepdims=True)
        sel = iota == ix
        if payload is None:
            picks.append(ix)
        else:
            picks.append(jnp.sum(jnp.where(sel, payload, 0), axis=0, keepdims=True))
        sc = jnp.where(sel, -jnp.inf, sc)
        vals.append(m)
    return jnp.concatenate(vals, axis=0), jnp.concatenate(picks, axis=0)


def _route_kernel(na_ref, df_ref, x_ref, wo_ref, n2_ref, wq_ref, sk_ref,
                  x1_ref, xn_ref, eid_ref, g_ref, q_scr, eid_scr, g_scr):
    tm = x_ref.shape[0]
    x1 = (x_ref[...]
          + jnp.dot(na_ref[...], wo_ref[0:SECTION, :], preferred_element_type=F32)
          + jnp.dot(df_ref[...], wo_ref[SECTION:2 * SECTION, :], preferred_element_type=F32))
    x1_ref[...] = x1
    ms = jnp.mean(x1 * x1, axis=-1, keepdims=True)
    xn = x1 * lax.rsqrt(ms + EPS) * n2_ref[...]
    xn_ref[...] = xn
    xb = xn.astype(BF16)
    for c in range(2 * PEER_HEADS):
        q_scr[c] = jnp.dot(xb, wq_ref[:, c * LANES:(c + 1) * LANES],
                           preferred_element_type=F32).astype(BF16)

    def head_body(h, carry):
        tops = []
        for p in range(2):
            sc = lax.dot_general(sk_ref[2 * h + p], q_scr[2 * h + p], (((1,), (1,)), ((), ())),
                                 preferred_element_type=F32)
            tops.append(_top16(sc))
        (s0, i0), (s1, i1) = tops
        cand_s = jnp.concatenate([s0[a:a + 1] + s1 for a in range(PEER_TOPK)], axis=0)
        cand_i = jnp.concatenate([i0[a:a + 1] * N_KEYS + i1 for a in range(PEER_TOPK)], axis=0)
        top_s, eid = _top16(cand_s, cand_i)
        e = jnp.exp(top_s - top_s[0:1])
        row = pl.multiple_of(h * PEER_TOPK, PEER_TOPK)
        eid_scr[pl.ds(row, PEER_TOPK), :] = eid
        g_scr[pl.ds(row, PEER_TOPK), :] = e / jnp.sum(e, axis=0, keepdims=True)
        return carry

    lax.fori_loop(0, PEER_HEADS, head_body, 0)
    for blk in range(tm // TB):
        cols = slice(blk * TB, (blk + 1) * TB)
        eid_ref[blk] = eid_scr[:, cols].T
        g_ref[cols, :] = g_scr[:, cols].T


def _out_and_route(na, df, x2d, w_out_bf, norm2_w, w_q_bf, sk_bf):
    n = x2d.shape[0]
    tm = TM_OUT
    tok = lambda w: pl.BlockSpec((tm, w), lambda i: (i, 0))
    return pl.pallas_call(
        _route_kernel,
        grid=(n // tm,),
        in_specs=[tok(SECTION), tok(SECTION), tok(D_MODEL),
                  _const_spec((D_MODEL, D_MODEL)), _const_spec((1, D_MODEL)),
                  _const_spec(w_q_bf.shape), _const_spec(sk_bf.shape)],
        out_specs=[tok(D_MODEL), tok(D_MODEL),
                   pl.BlockSpec((tm // TB, TB, HK), lambda i: (i, 0, 0)), tok(HK)],
        out_shape=[jax.ShapeDtypeStruct((n, D_MODEL), F32), jax.ShapeDtypeStruct((n, D_MODEL), F32),
                   jax.ShapeDtypeStruct((n // TB, TB, HK), jnp.int32),
                   jax.ShapeDtypeStruct((n, HK), F32)],
        scratch_shapes=[pltpu.VMEM((2 * PEER_HEADS, tm, LANES), BF16),
                        pltpu.VMEM((HK, tm), jnp.int32), pltpu.VMEM((HK, tm), F32)],
        compiler_params=_cparams(("parallel",)),
    )(na, df, x2d, w_out_bf, norm2_w.reshape(1, D_MODEL), w_q_bf, sk_bf)


ROW_SUB = SUBLANES // 2
CHUNK_SLOTS = 32
N_CHUNKS = HK // CHUNK_SLOTS
CHUNK_ROWS = CHUNK_SLOTS * SUBLANES
TOKEN_UNROLL = 4


def _pack_table(tab):
    bits = lax.bitcast_convert_type(tab.astype(BF16), jnp.uint16).astype(jnp.uint32)
    bits = bits.reshape(tab.shape[0], ROW_SUB, 2, LANES)
    return bits[:, :, 0, :] | (bits[:, :, 1, :] << 16)


def _split_bf16(x):
    hi = x.astype(BF16)
    return hi, (x - hi.astype(F32)).astype(BF16)


def _gather_chunk(tab_ref, eid_ref, t, chunk):
    ids = eid_ref.at[0, t]
    rows = [tab_ref[ids[chunk * CHUNK_SLOTS + i]] for i in range(CHUNK_SLOTS)]
    return pltpu.bitcast(jnp.concatenate(rows, axis=0), BF16)


def _diag_mask(width):
    r = lax.broadcasted_iota(jnp.int32, (SUBLANES, width), 0)
    n = lax.broadcasted_iota(jnp.int32, (SUBLANES, width), 1)
    return (n % SUBLANES) == r


def _peer_u_kernel(eid_ref, xn_ref, g_ref, tab_ref, rept_ref, rep_ref, wrep_ref, d_scr):
    mask = _diag_mask(CHUNK_ROWS)
    zeros = jnp.zeros((SUBLANES, LANES), BF16)

    def one_token(t):
        xhi, xlo = _split_bf16(xn_ref[t])
        lhs = jnp.concatenate([jnp.concatenate([xhi, zeros], axis=1),
                               jnp.concatenate([xlo, zeros], axis=1),
                               jnp.concatenate([zeros, xhi], axis=1),
                               jnp.concatenate([zeros, xlo], axis=1)], axis=0)
        for pr in range(N_CHUNKS // 2):
            s = jnp.concatenate([_gather_chunk(tab_ref, eid_ref, t, 2 * pr),
                                 _gather_chunk(tab_ref, eid_ref, t, 2 * pr + 1)], axis=1)
            z = lax.dot_general(lhs, s, (((1,), (1,)), ((), ())), preferred_element_type=F32)
            for half in range(2):
                z8 = z[16 * half:16 * half + 8] + z[16 * half + 8:16 * half + 16]
                zs = jnp.sum(jnp.where(mask, z8, 0.0), axis=0, keepdims=True)
                d_scr[pl.ds(t, 1), pl.ds((2 * pr + half) * CHUNK_ROWS, CHUNK_ROWS)] = zs

    def tok_body(i, carry):
        for u in range(TOKEN_UNROLL):
            one_token(i * TOKEN_UNROLL + u)
        return carry

    lax.fori_loop(0, TB // TOKEN_UNROLL, tok_body, 0)
    dhi, dlo = _split_bf16(d_scr[...])
    a = (jnp.dot(dhi, rept_ref[...], preferred_element_type=F32)
         + jnp.dot(dlo, rept_ref[...], preferred_element_type=F32))
    w = g_ref[...] * (0.5 * a * (1.0 + lax.erf(a * (2.0 ** -0.5))))
    whi, wlo = _split_bf16(w)
    wrep_ref[...] = (jnp.dot(whi, rep_ref[...], preferred_element_type=F32)
                     + jnp.dot(wlo, rep_ref[...], preferred_element_type=F32))


def _peer_v_kernel(eid_ref, wrep_ref, x1_ref, tab_ref, o_ref):
    mask = _diag_mask(CHUNK_ROWS)

    def one_token(t):
        wr = wrep_ref[pl.ds(t, 1), :]
        acc = x1_ref[t]
        for pr in range(N_CHUNKS // 2):
            parts = []
            for half in range(2):
                seg = wr[:, (2 * pr + half) * CHUNK_ROWS:(2 * pr + half + 1) * CHUNK_ROWS]
                parts.extend(_split_bf16(jnp.where(mask, jnp.broadcast_to(seg, mask.shape), 0.0)))
            lhs = jnp.concatenate(parts, axis=0)
            s = jnp.concatenate([_gather_chunk(tab_ref, eid_ref, t, 2 * pr),
                                 _gather_chunk(tab_ref, eid_ref, t, 2 * pr + 1)], axis=1)
            z = jnp.dot(lhs, s, preferred_element_type=F32)
            acc = acc + ((z[0:8, :LANES] + z[8:16, :LANES]) + (z[16:24, LANES:] + z[24:32, LANES:]))
        o_ref[t] = acc

    def tok_body(i, carry):
        for u in range(TOKEN_UNROLL):
            one_token(i * TOKEN_UNROLL + u)
        return carry

    lax.fori_loop(0, TB // TOKEN_UNROLL, tok_body, 0)


def _peer_experts(eid, g, xn3, x13, u_pack, v_pack):
    n = xn3.shape[0]
    nblk = n // TB
    smem_slot = pl.BlockSpec((1, TB, HK), lambda i: (i, 0, 0), memory_space=pltpu.SMEM)
    tok = pl.BlockSpec((TB, SUBLANES, LANES), lambda i: (i, 0, 0))
    flat = lambda w: pl.BlockSpec((TB, w), lambda i: (i, 0))
    table = pl.BlockSpec(u_pack.shape, lambda i: (0, 0, 0), pipeline_mode=pl.Buffered(1))
    rep_np = (np.arange(HK)[:, None] == (np.arange(HK * SUBLANES)[None, :] // SUBLANES)).astype(np.float32)
    rep, rept = jnp.asarray(rep_np, BF16), jnp.asarray(rep_np.T, BF16)
    wrep = pl.pallas_call(
        _peer_u_kernel,
        grid=(nblk,),
        in_specs=[smem_slot, tok, flat(HK), table, _const_spec(rept.shape), _const_spec(rep.shape)],
        out_specs=flat(D_MODEL),
        out_shape=jax.ShapeDtypeStruct((n, D_MODEL), F32),
        scratch_shapes=[pltpu.VMEM((TB, D_MODEL), F32)],
        compiler_params=_cparams(("arbitrary",)),
    )(eid, xn3, g, u_pack, rept, rep)
    return pl.pallas_call(
        _peer_v_kernel,
        grid=(nblk,),
        in_specs=[smem_slot, flat(D_MODEL), tok, table],
        out_specs=tok,
        out_shape=jax.ShapeDtypeStruct((n, SUBLANES, LANES), F32),
        compiler_params=_cparams(("arbitrary",)),
    )(eid, wrep, x13, v_pack)


def _encoder_layer(x, p):
    batch, seq, _ = x.shape
    n = batch * seq
    x2d = x.reshape(n, D_MODEL)
    naq, nak, nav, dq, dk, dv = _in_projection(x2d, seq, p["norm1_w"], p["w_in"], p["qk_w"], p["rope"])
    na = _neighbourhood_attention(naq, nak, nav, p["na_bias"], batch, seq)
    df = _diff_attention(dq, dk, dv, p["lam"], p["subln_w"], batch, seq)
    x1, xn, eid, g = _out_and_route(na, df, x2d, p["w_out"], p["norm2_w"], p["w_q"], p["sub_keys"])
    y3 = _peer_experts(eid, g, xn.reshape(n, SUBLANES, LANES), x1.reshape(n, SUBLANES, LANES),
                       p["u_pack"], p["v_pack"])
    return y3.reshape(batch, seq, D_MODEL)


def kernel(x_prompt, x_sample, norm1_w, w_in, na_q_norm, na_k_norm, na_rpb, diff_q_norm, diff_k_norm,
           diff_lambda_q1, diff_lambda_k1, diff_lambda_q2, diff_lambda_k2, diff_subln_w, w_out, norm2_w,
           peer_w_q, peer_sub_keys, peer_u, peer_v):
    assert norm1_w.shape[0] == 1, "single-layer problem"
    reps = SECTION // HEAD_DIM
    params = {
        "norm1_w": norm1_w[0],
        "w_in": w_in[0].astype(BF16),
        "qk_w": jnp.stack([jnp.tile(w[0], reps) for w in (na_q_norm, na_k_norm, diff_q_norm, diff_k_norm)]),
        "lam": jnp.stack([diff_lambda_q1[0], diff_lambda_k1[0], diff_lambda_q2[0], diff_lambda_k2[0]]),
        "subln_w": diff_subln_w[0],
        "w_out": w_out[0].astype(BF16),
        "norm2_w": norm2_w[0],
        "w_q": peer_w_q[0].astype(BF16),
        "sub_keys": peer_sub_keys[0].reshape(2 * PEER_HEADS, N_KEYS, N_KEYS).astype(BF16),
        "na_bias": _na_bias(na_rpb[0]),
        "u_pack": _pack_table(peer_u[0]),
        "v_pack": _pack_table(peer_v[0]),
    }
    rope_by_seq = {}
    outs = []
    for x in (x_prompt, x_sample):
        seq = x.shape[1]
        if seq not in rope_by_seq:
            rope_by_seq[seq] = _rope_tables(seq)
        outs.append(_encoder_layer(x, dict(params, rope=rope_by_seq[seq])))
    return tuple(outs)
```

```python
import functools
import math

import jax
import jax.numpy as jnp
import numpy as np
from jax import lax
from jax.experimental import pallas as pl
from jax.experimental.pallas import tpu as pltpu

F32 = jnp.float32
BF16 = jnp.bfloat16

D_MODEL = 1024
GRID_W = 64
HEAD_DIM = 64
NA_HEADS = 8
NA_KH = 8
NA_KW = 16
DIFF_HEADS = 4
SECTION = 512
ROT_DIM = HEAD_DIM // 4
ROPE_THETA = 500000.0
PEER_HEADS = 8
N_KEYS = 128
PEER_TOPK = 16
N_EXPERTS = N_KEYS * N_KEYS
HK = PEER_HEADS * PEER_TOPK
EPS = 1e-6
NEG_BIG = -1e30
LAMBDA_INIT = 0.8 - 0.6 * math.exp(-0.3 * 0)

LANES = 128
SUBLANES = 8
VMEM_LIMIT = 56 * 1024 * 1024

TM_IN = 512
NA_ROWS_PER_STEP = 8
TQ = 512
TK = 512
TM_OUT = 256
TB = 128


def _cparams(sem):
    return pltpu.CompilerParams(dimension_semantics=sem, vmem_limit_bytes=VMEM_LIMIT)


def _const_spec(shape):
    nd = len(shape)
    return pl.BlockSpec(shape, lambda *_: (0,) * nd)


def _inproj_kernel(x_ref, n1_ref, w_ref, bd_ref, nw_ref, rc_ref, ra_ref, rb_ref,
                   naq_ref, nak_ref, nav_ref, dq_ref, dk_ref, dv_ref):
    x = x_ref[...]
    ms = jnp.mean(x * x, axis=-1, keepdims=True)
    h = (x * lax.rsqrt(ms + EPS) * n1_ref[...]).astype(BF16)
    bd = bd_ref[...]

    def proj(c):
        return jnp.dot(h, w_ref[:, c * SECTION:(c + 1) * SECTION], preferred_element_type=F32)

    def qknorm(y, row):
        sq = y * y
        hi = sq.astype(BF16)
        lo = (sq - hi.astype(F32)).astype(BF16)
        msq = (jnp.dot(hi, bd, preferred_element_type=F32)
               + jnp.dot(lo, bd, preferred_element_type=F32))
        return y * lax.rsqrt(msq + EPS) * nw_ref[row:row + 1, :]

    def rope(y):
        return (y * rc_ref[...]
                + pltpu.roll(y, SECTION - ROT_DIM // 2, axis=1) * ra_ref[...]
                + pltpu.roll(y, ROT_DIM // 2, axis=1) * rb_ref[...])

    scale = HEAD_DIM ** -0.5
    naq_ref[...] = (qknorm(proj(0), 0) * scale).astype(BF16)
    nak_ref[...] = qknorm(proj(1), 1).astype(BF16)
    nav_ref[...] = proj(2).astype(BF16)
    dq_ref[...] = (rope(qknorm(proj(3), 2)) * scale).astype(BF16)
    dk_ref[...] = rope(qknorm(proj(4), 3)).astype(BF16)
    dv_ref[...] = proj(5).astype(BF16)


def _rope_tables(seq):
    pos = jnp.arange(seq, dtype=F32)
    inv = ROPE_THETA ** (-jnp.arange(0, ROT_DIM, 2, dtype=F32) / ROT_DIM)
    ang = pos[:, None] * inv[None, :]
    cos, sin = jnp.cos(ang), jnp.sin(ang)
    half = ROT_DIM // 2
    pad = HEAD_DIM - ROT_DIM
    c_head = jnp.concatenate([cos, cos, jnp.ones((seq, pad), F32)], -1)
    a_head = jnp.concatenate([-sin, jnp.zeros((seq, pad + half), F32)], -1)
    b_head = jnp.concatenate([jnp.zeros((seq, half), F32), sin, jnp.zeros((seq, pad), F32)], -1)
    reps = SECTION // HEAD_DIM
    return (jnp.tile(c_head, (1, reps)), jnp.tile(a_head, (1, reps)), jnp.tile(b_head, (1, reps)))


def _in_projection(x2d, seq, norm1_w, w_in_bf, qk_w, rope_tabs):
    n = x2d.shape[0]
    tm = min(TM_IN, seq)
    steps_per_seq = seq // tm
    gid = np.arange(SECTION) // HEAD_DIM
    bd = jnp.asarray((gid[:, None] == gid[None, :]).astype(np.float32) / HEAD_DIM, BF16)
    tok = pl.BlockSpec((tm, SECTION), lambda i: (i, 0))
    pos = pl.BlockSpec((tm, SECTION), lambda i: (i % steps_per_seq, 0))
    out = jax.ShapeDtypeStruct((n, SECTION), BF16)
    return pl.pallas_call(
        _inproj_kernel,
        grid=(n // tm,),
        in_specs=[pl.BlockSpec((tm, D_MODEL), lambda i: (i, 0)),
                  _const_spec((1, D_MODEL)),
                  _const_spec((D_MODEL, 6 * SECTION)),
                  _const_spec((SECTION, SECTION)),
                  _const_spec((4, SECTION)),
                  pos, pos, pos],
        out_specs=[tok] * 6,
        out_shape=[out] * 6,
        compiler_params=_cparams(("parallel",)),
    )(x2d, norm1_w.reshape(1, D_MODEL), w_in_bf, bd, qk_w, *rope_tabs)


def _na_bias(rpb):
    c = np.arange(GRID_W)[:, None]
    kc = np.arange(GRID_W)[None, :]
    cs = np.clip(c - NA_KW // 2, 0, GRID_W - NA_KW)
    valid = (kc >= cs) & (kc < cs + NA_KW)
    rel_c = kc - c + (NA_KW - 1)
    onehot = ((np.arange(2 * NA_KW - 1)[:, None, None] == rel_c[None]) & valid[None]).astype(np.float32)
    col = jnp.einsum("hrx,xck->hrck", rpb.astype(F32), onehot, precision=lax.Precision.HIGHEST)
    col = jnp.where(valid, col, NEG_BIG)
    per_off = [col[:, NA_KH - 1 - o:2 * NA_KH - 1 - o] for o in range(NA_KH)]
    b = jnp.stack(per_off).transpose(0, 1, 3, 2, 4)
    return b.reshape(NA_KH, NA_HEADS, GRID_W, NA_KH * GRID_W)


def _na_kernel(q_ref, k_ref, v_ref, b_ref, o_ref, *, rows):
    j = pl.program_id(1)
    win = NA_KH * GRID_W

    def row_body(rr, carry):
        r = j * NA_ROWS_PER_STEP + rr
        rs = jnp.clip(r - NA_KH // 2, 0, rows - NA_KH)
        off = r - rs
        kstart = pl.multiple_of(rs * GRID_W, GRID_W)
        qstart = pl.multiple_of(rr * GRID_W, GRID_W)
        kw = k_ref[pl.ds(kstart, win), :]
        vw = v_ref[pl.ds(kstart, win), :]
        qrow = q_ref[pl.ds(qstart, GRID_W), :]
        outs = []
        for h in range(NA_HEADS):
            sl = slice(h * HEAD_DIM, (h + 1) * HEAD_DIM)
            s = lax.dot_general(qrow[:, sl], kw[:, sl], (((1,), (1,)), ((), ())),
                                preferred_element_type=F32)
            s = s + b_ref[off, h]
            m = jnp.max(s, axis=-1, keepdims=True)
            p = jnp.exp(s - m)
            l = jnp.sum(p, axis=-1, keepdims=True)
            o = jnp.dot(p.astype(BF16), vw[:, sl], preferred_element_type=F32)
            outs.append(o / l)
        o_ref[pl.ds(qstart, GRID_W), :] = jnp.concatenate(outs, axis=-1).astype(BF16)
        return carry

    lax.fori_loop(0, NA_ROWS_PER_STEP, row_body, 0)


def _neighbourhood_attention(q, k, v, bias, batch, seq):
    rows = seq // GRID_W
    assert rows >= NA_KH and rows % NA_ROWS_PER_STEP == 0
    nblk = rows // NA_ROWS_PER_STEP
    tq = NA_ROWS_PER_STEP * GRID_W
    qspec = pl.BlockSpec((tq, SECTION), lambda b, j: (b * nblk + j, 0))
    kvspec = pl.BlockSpec((seq, SECTION), lambda b, j: (b, 0))
    return pl.pallas_call(
        functools.partial(_na_kernel, rows=rows),
        grid=(batch, nblk),
        in_specs=[qspec, kvspec, kvspec, _const_spec(bias.shape)],
        out_specs=qspec,
        out_shape=jax.ShapeDtypeStruct(q.shape, BF16),
        compiler_params=_cparams(("parallel", "arbitrary")),
    )(q, k, v, bias)


def _diff_kernel(q_ref, k_ref, v_ref, lam_ref, sub_ref, o_ref, *, seq, tq, tk):
    q = q_ref[...]
    lane = lax.broadcasted_iota(jnp.int32, q.shape, 1)
    zero = jnp.zeros_like(q)
    q_maps = (jnp.where(lane < HEAD_DIM, q, zero), jnp.where(lane >= HEAD_DIM, q, zero))

    lp = lam_ref[...]
    lam = (jnp.exp(jnp.sum(lp[0:1] * lp[1:2], axis=-1, keepdims=True))
           - jnp.exp(jnp.sum(lp[2:3] * lp[3:4], axis=-1, keepdims=True)) + LAMBDA_INIT)

    def kv_body(c, carry):
        start = pl.multiple_of(c * tk, tk)
        kc = k_ref[pl.ds(start, tk), :]
        vc = v_ref[pl.ds(start, tk), :]
        new = []
        for mp in range(2):
            m, l, acc = carry[mp]
            s = lax.dot_general(q_maps[mp], kc, (((1,), (1,)), ((), ())), preferred_element_type=F32)
            mn = jnp.maximum(m, jnp.max(s, axis=-1, keepdims=True))
            alpha = jnp.exp(m - mn)
            p = jnp.exp(s - mn)
            l = alpha * l + jnp.sum(p, axis=-1, keepdims=True)
            acc = alpha * acc + jnp.dot(p.astype(BF16), vc, preferred_element_type=F32)
            new.append((mn, l, acc))
        return tuple(new)

    init = tuple((jnp.full((tq, 1), -jnp.inf, F32), jnp.zeros((tq, 1), F32),
                  jnp.zeros((tq, 2 * HEAD_DIM), F32)) for _ in range(2))
    (_, l0, a0), (_, l1, a1) = lax.fori_loop(0, seq // tk, kv_body, init)
    o = a0 / l0 - lam * (a1 / l1)
    ms = jnp.mean(o * o, axis=-1, keepdims=True)
    y = o * lax.rsqrt(ms + EPS) * sub_ref[...]
    o_ref[...] = (y * (1.0 - LAMBDA_INIT)).astype(BF16)


def _diff_attention(q, k, v, lam_params, subln_w, batch, seq):
    tq, tk = min(TQ, seq), min(TK, seq)
    nq = seq // tq
    width = 2 * HEAD_DIM
    qspec = pl.BlockSpec((tq, width), lambda b, h, i: (b * nq + i, h))
    kvspec = pl.BlockSpec((seq, width), lambda b, h, i: (b, h))
    return pl.pallas_call(
        functools.partial(_diff_kernel, seq=seq, tq=tq, tk=tk),
        grid=(batch, DIFF_HEADS, nq),
        in_specs=[qspec, kvspec, kvspec, _const_spec((4, HEAD_DIM)), _const_spec((1, width))],
        out_specs=qspec,
        out_shape=jax.ShapeDtypeStruct(q.shape, BF16),
        compiler_params=_cparams(("parallel", "parallel", "arbitrary")),
    )(q, k, v, lam_params, subln_w.reshape(1, width))


def _top16(sc, payload=None):
    iota = lax.broadcasted_iota(jnp.int32, sc.shape, 0)
    big = jnp.int32(sc.shape[0])
    vals, picks = [], []
    for _ in range(PEER_TOPK):
        m = jnp.max(sc, axis=0, keepdims=True)
        ix = jnp.min(jnp.where(sc == m, iota, big), axis=0, keepdims=True)
        sel = iota == ix
        if payload is None:
            picks.append(ix)
        else:
            picks.append(jnp.sum(jnp.where(sel, payload, 0), axis=0, keepdims=True))
        sc = jnp.where(sel, -jnp.inf, sc)
        vals.append(m)
    return jnp.concatenate(vals, axis=0), jnp.concatenate(picks, axis=0)


def _route_kernel(na_ref, df_ref, x_ref, wo_ref, n2_ref, wq_ref, sk_ref,
                  x1_ref, xn_ref, eid_ref, g_ref, q_scr, eid_scr, g_scr):
    tm = x_ref.shape[0]
    x1 = (x_ref[...]
          + jnp.dot(na_ref[...], wo_ref[0:SECTION, :], preferred_element_type=F32)
          + jnp.dot(df_ref[...], wo_ref[SECTION:2 * SECTION, :], preferred_element_type=F32))
    x1_ref[...] = x1
    ms = jnp.mean(x1 * x1, axis=-1, keepdims=True)
    xn = x1 * lax.rsqrt(ms + EPS) * n2_ref[...]
    xn_ref[...] = xn
    xb = xn.astype(BF16)
    for c in range(2 * PEER_HEADS):
        q_scr[c] = jnp.dot(xb, wq_ref[:, c * LANES:(c + 1) * LANES],
                           preferred_element_type=F32).astype(BF16)

    def head_body(h, carry):
        tops = []
        for p in range(2):
            sc = lax.dot_general(sk_ref[2 * h + p], q_scr[2 * h + p], (((1,), (1,)), ((), ())),
                                 preferred_element_type=F32)
            tops.append(_top16(sc))
        (s0, i0), (s1, i1) = tops
        cand_s = jnp.concatenate([s0[a:a + 1] + s1 for a in range(PEER_TOPK)], axis=0)
        cand_i = jnp.concatenate([i0[a:a + 1] * N_KEYS + i1 for a in range(PEER_TOPK)], axis=0)
        top_s, eid = _top16(cand_s, cand_i)
        e = jnp.exp(top_s - top_s[0:1])
        row = pl.multiple_of(h * PEER_TOPK, PEER_TOPK)
        eid_scr[pl.ds(row, PEER_TOPK), :] = eid * ROW_SUB
        g_scr[pl.ds(row, PEER_TOPK), :] = e / jnp.sum(e, axis=0, keepdims=True)
        return carry

    lax.fori_loop(0, PEER_HEADS, head_body, 0)
    for blk in range(tm // TB):
        cols = slice(blk * TB, (blk + 1) * TB)
        eid_ref[blk] = eid_scr[:, cols].T
        g_ref[cols, :] = g_scr[:, cols].T


def _out_and_route(na, df, x2d, w_out_bf, norm2_w, w_q_bf, sk_bf):
    n = x2d.shape[0]
    tm = TM_OUT
    tok = lambda w: pl.BlockSpec((tm, w), lambda i: (i, 0))
    return pl.pallas_call(
        _route_kernel,
        grid=(n // tm,),
        in_specs=[tok(SECTION), tok(SECTION), tok(D_MODEL),
                  _const_spec((D_MODEL, D_MODEL)), _const_spec((1, D_MODEL)),
                  _const_spec(w_q_bf.shape), _const_spec(sk_bf.shape)],
        out_specs=[tok(D_MODEL), tok(D_MODEL),
                   pl.BlockSpec((tm // TB, TB, HK), lambda i: (i, 0, 0)), tok(HK)],
        out_shape=[jax.ShapeDtypeStruct((n, D_MODEL), F32), jax.ShapeDtypeStruct((n, D_MODEL), F32),
                   jax.ShapeDtypeStruct((n // TB, TB, HK), jnp.int32),
                   jax.ShapeDtypeStruct((n, HK), F32)],
        scratch_shapes=[pltpu.VMEM((2 * PEER_HEADS, tm, LANES), BF16),
                        pltpu.VMEM((HK, tm), jnp.int32), pltpu.VMEM((HK, tm), F32)],
        compiler_params=_cparams(("parallel",)),
    )(na, df, x2d, w_out_bf, norm2_w.reshape(1, D_MODEL), w_q_bf, sk_bf)


ROW_SUB = SUBLANES // 2
CHUNK_SLOTS = 32
N_CHUNKS = HK // CHUNK_SLOTS
CHUNK_ROWS = CHUNK_SLOTS * SUBLANES
TOKEN_UNROLL = 8


def _pack_table(tab):
    bits = lax.bitcast_convert_type(tab.astype(BF16), jnp.uint16).astype(jnp.uint32)
    bits = bits.reshape(tab.shape[0], ROW_SUB, 2, LANES)
    return (bits[:, :, 0, :] | (bits[:, :, 1, :] << 16)).reshape(tab.shape[0] * ROW_SUB, LANES)


def _split_bf16(x):
    hi = x.astype(BF16)
    return hi, (x - hi.astype(F32)).astype(BF16)


def _gather_chunk(tab_ref, eid_ref, t, chunk):
    ids = eid_ref.at[0, t]
    rows = [tab_ref[pl.ds(pl.multiple_of(ids[chunk * CHUNK_SLOTS + i], ROW_SUB), ROW_SUB), :]
            for i in range(CHUNK_SLOTS)]
    return pltpu.bitcast(jnp.concatenate(rows, axis=0), BF16)


def _diag_mask(width):
    r = lax.broadcasted_iota(jnp.int32, (SUBLANES, width), 0)
    n = lax.broadcasted_iota(jnp.int32, (SUBLANES, width), 1)
    return (n % SUBLANES) == r


def _peer_u_kernel(eid_ref, xn_ref, g_ref, tab_ref, rept_ref, rep_ref, wrep_ref, d_scr):
    mask = _diag_mask(CHUNK_ROWS)
    zeros = jnp.zeros((SUBLANES, LANES), BF16)

    def one_token(t):
        xhi, xlo = _split_bf16(xn_ref[t])
        lhs = jnp.concatenate([jnp.concatenate([xhi, zeros], axis=1),
                               jnp.concatenate([xlo, zeros], axis=1),
                               jnp.concatenate([zeros, xhi], axis=1),
                               jnp.concatenate([zeros, xlo], axis=1)], axis=0)
        for pr in range(N_CHUNKS // 2):
            s = jnp.concatenate([_gather_chunk(tab_ref, eid_ref, t, 2 * pr),
                                 _gather_chunk(tab_ref, eid_ref, t, 2 * pr + 1)], axis=1)
            z = lax.dot_general(lhs, s, (((1,), (1,)), ((), ())), preferred_element_type=F32)
            for half in range(2):
                z8 = z[16 * half:16 * half + 8] + z[16 * half + 8:16 * half + 16]
                zs = jnp.sum(jnp.where(mask, z8, 0.0), axis=0, keepdims=True)
                d_scr[pl.ds(t, 1), pl.ds((2 * pr + half) * CHUNK_ROWS, CHUNK_ROWS)] = zs

    def tok_body(i, carry):
        for u in range(TOKEN_UNROLL):
            one_token(i * TOKEN_UNROLL + u)
        return carry

    lax.fori_loop(0, TB // TOKEN_UNROLL, tok_body, 0)
    dhi, dlo = _split_bf16(d_scr[...])
    a = (jnp.dot(dhi, rept_ref[...], preferred_element_type=F32)
         + jnp.dot(dlo, rept_ref[...], preferred_element_type=F32))
    w = g_ref[...] * (0.5 * a * (1.0 + lax.erf(a * (2.0 ** -0.5))))
    whi, wlo = _split_bf16(w)
    wrep_ref[...] = (jnp.dot(whi, rep_ref[...], preferred_element_type=F32)
                     + jnp.dot(wlo, rep_ref[...], preferred_element_type=F32))


def _peer_v_kernel(eid_ref, wrep_ref, x1_ref, tab_ref, o_ref):
    mask = _diag_mask(CHUNK_ROWS)

    def one_token(t):
        wr = wrep_ref[pl.ds(t, 1), :]
        acc = x1_ref[t]
        for pr in range(N_CHUNKS // 2):
            parts = []
            for half in range(2):
                seg = wr[:, (2 * pr + half) * CHUNK_ROWS:(2 * pr + half + 1) * CHUNK_ROWS]
                parts.extend(_split_bf16(jnp.where(mask, jnp.broadcast_to(seg, mask.shape), 0.0)))
            lhs = jnp.concatenate(parts, axis=0)
            s = jnp.concatenate([_gather_chunk(tab_ref, eid_ref, t, 2 * pr),
                                 _gather_chunk(tab_ref, eid_ref, t, 2 * pr + 1)], axis=1)
            z = jnp.dot(lhs, s, preferred_element_type=F32)
            acc = acc + ((z[0:8, :LANES] + z[8:16, :LANES]) + (z[16:24, LANES:] + z[24:32, LANES:]))
        o_ref[t] = acc

    def tok_body(i, carry):
        for u in range(TOKEN_UNROLL):
            one_token(i * TOKEN_UNROLL + u)
        return carry

    lax.fori_loop(0, TB // TOKEN_UNROLL, tok_body, 0)


def _peer_experts(eid, g, xn3, x13, u_pack, v_pack):
    n = xn3.shape[0]
    nblk = n // TB
    smem_slot = pl.BlockSpec((1, TB, HK), lambda i: (i, 0, 0), memory_space=pltpu.SMEM)
    tok = pl.BlockSpec((TB, SUBLANES, LANES), lambda i: (i, 0, 0))
    flat = lambda w: pl.BlockSpec((TB, w), lambda i: (i, 0))
    table = pl.BlockSpec(u_pack.shape, lambda i: (0, 0), pipeline_mode=pl.Buffered(1))
    rep_np = (np.arange(HK)[:, None] == (np.arange(HK * SUBLANES)[None, :] // SUBLANES)).astype(np.float32)
    rep, rept = jnp.asarray(rep_np, BF16), jnp.asarray(rep_np.T, BF16)
    wrep = pl.pallas_call(
        _peer_u_kernel,
        grid=(nblk,),
        in_specs=[smem_slot, tok, flat(HK), table, _const_spec(rept.shape), _const_spec(rep.shape)],
        out_specs=flat(D_MODEL),
        out_shape=jax.ShapeDtypeStruct((n, D_MODEL), F32),
        scratch_shapes=[pltpu.VMEM((TB, D_MODEL), F32)],
        compiler_params=_cparams(("arbitrary",)),
    )(eid, xn3, g, u_pack, rept, rep)
    return pl.pallas_call(
        _peer_v_kernel,
        grid=(nblk,),
        in_specs=[smem_slot, flat(D_MODEL), tok, table],
        out_specs=tok,
        out_shape=jax.ShapeDtypeStruct((n, SUBLANES, LANES), F32),
        compiler_params=_cparams(("arbitrary",)),
    )(eid, wrep, x13, v_pack)


def _encoder_layer(x, p):
    batch, seq, _ = x.shape
    n = batch * seq
    x2d = x.reshape(n, D_MODEL)
    naq, nak, nav, dq, dk, dv = _in_projection(x2d, seq, p["norm1_w"], p["w_in"], p["qk_w"], p["rope"])
    na = _neighbourhood_attention(naq, nak, nav, p["na_bias"], batch, seq)
    df = _diff_attention(dq, dk, dv, p["lam"], p["subln_w"], batch, seq)
    x1, xn, eid, g = _out_and_route(na, df, x2d, p["w_out"], p["norm2_w"], p["w_q"], p["sub_keys"])
    y3 = _peer_experts(eid, g, xn.reshape(n, SUBLANES, LANES), x1.reshape(n, SUBLANES, LANES),
                       p["u_pack"], p["v_pack"])
    return y3.reshape(batch, seq, D_MODEL)


def kernel(x_prompt, x_sample, norm1_w, w_in, na_q_norm, na_k_norm, na_rpb, diff_q_norm, diff_k_norm,
           diff_lambda_q1, diff_lambda_k1, diff_lambda_q2, diff_lambda_k2, diff_subln_w, w_out, norm2_w,
           peer_w_q, peer_sub_keys, peer_u, peer_v):
    assert norm1_w.shape[0] == 1, "single-layer problem"
    reps = SECTION // HEAD_DIM
    params = {
        "norm1_w": norm1_w[0],
        "w_in": w_in[0].astype(BF16),
        "qk_w": jnp.stack([jnp.tile(w[0], reps) for w in (na_q_norm, na_k_norm, diff_q_norm, diff_k_norm)]),
        "lam": jnp.stack([diff_lambda_q1[0], diff_lambda_k1[0], diff_lambda_q2[0], diff_lambda_k2[0]]),
        "subln_w": diff_subln_w[0],
        "w_out": w_out[0].astype(BF16),
        "norm2_w": norm2_w[0],
        "w_q": peer_w_q[0].astype(BF16),
        "sub_keys": peer_sub_keys[0].reshape(2 * PEER_HEADS, N_KEYS, N_KEYS).astype(BF16),
        "na_bias": _na_bias(na_rpb[0]),
        "u_pack": _pack_table(peer_u[0]),
        "v_pack": _pack_table(peer_v[0]),
    }
    rope_by_seq = {}
    outs = []
    for x in (x_prompt, x_sample):
        seq = x.shape[1]
        if seq not in rope_by_seq:
            rope_by_seq[seq] = _rope_tables(seq)
        outs.append(_encoder_layer(x, dict(params, rope=rope_by_seq[seq])))
    return tuple(outs)
```

```python
import functools
import math

import jax
import jax.numpy as jnp
import numpy as np
from jax import lax
from jax.experimental import pallas as pl
from jax.experimental.pallas import tpu as pltpu

F32 = jnp.float32
BF16 = jnp.bfloat16

D_MODEL = 1024
GRID_W = 64
HEAD_DIM = 64
NA_HEADS = 8
NA_KH = 8
NA_KW = 16
DIFF_HEADS = 4
SECTION = 512
ROT_DIM = HEAD_DIM // 4
ROPE_THETA = 500000.0
PEER_HEADS = 8
N_KEYS = 128
PEER_TOPK = 16
N_EXPERTS = N_KEYS * N_KEYS
HK = PEER_HEADS * PEER_TOPK
EPS = 1e-6
NEG_BIG = -1e30
LAMBDA_INIT = 0.8 - 0.6 * math.exp(-0.3 * 0)

LANES = 128
SUBLANES = 8
VMEM_LIMIT = 56 * 1024 * 1024

TM_IN = 512
NA_ROWS_PER_STEP = 8
TQ = 512
TK = 512
TM_OUT = 256
TB = 128


def _cparams(sem):
    return pltpu.CompilerParams(dimension_semantics=sem, vmem_limit_bytes=VMEM_LIMIT)


def _const_spec(shape):
    nd = len(shape)
    return pl.BlockSpec(shape, lambda *_: (0,) * nd)


def _inproj_kernel(x_ref, n1_ref, w_ref, bd_ref, nw_ref, rc_ref, ra_ref, rb_ref,
                   naq_ref, nak_ref, nav_ref, dq_ref, dk_ref, dv_ref):
    x = x_ref[...]
    ms = jnp.mean(x * x, axis=-1, keepdims=True)
    h = (x * lax.rsqrt(ms + EPS) * n1_ref[...]).astype(BF16)
    bd = bd_ref[...]

    def proj(c):
        return jnp.dot(h, w_ref[:, c * SECTION:(c + 1) * SECTION], preferred_element_type=F32)

    def qknorm(y, row):
        sq = y * y
        hi = sq.astype(BF16)
        lo = (sq - hi.astype(F32)).astype(BF16)
        msq = (jnp.dot(hi, bd, preferred_element_type=F32)
               + jnp.dot(lo, bd, preferred_element_type=F32))
        return y * lax.rsqrt(msq + EPS) * nw_ref[row:row + 1, :]

    def rope(y):
        return (y * rc_ref[...]
                + pltpu.roll(y, SECTION - ROT_DIM // 2, axis=1) * ra_ref[...]
                + pltpu.roll(y, ROT_DIM // 2, axis=1) * rb_ref[...])

    scale = HEAD_DIM ** -0.5
    naq_ref[...] = (qknorm(proj(0), 0) * scale).astype(BF16)
    nak_ref[...] = qknorm(proj(1), 1).astype(BF16)
    nav_ref[...] = proj(2).astype(BF16)
    dq_ref[...] = (rope(qknorm(proj(3), 2)) * scale).astype(BF16)
    dk_ref[...] = rope(qknorm(proj(4), 3)).astype(BF16)
    dv_ref[...] = proj(5).astype(BF16)


def _rope_tables(seq):
    pos = jnp.arange(seq, dtype=F32)
    inv = ROPE_THETA ** (-jnp.arange(0, ROT_DIM, 2, dtype=F32) / ROT_DIM)
    ang = pos[:, None] * inv[None, :]
    cos, sin = jnp.cos(ang), jnp.sin(ang)
    half = ROT_DIM // 2
    pad = HEAD_DIM - ROT_DIM
    c_head = jnp.concatenate([cos, cos, jnp.ones((seq, pad), F32)], -1)
    a_head = jnp.concatenate([-sin, jnp.zeros((seq, pad + half), F32)], -1)
    b_head = jnp.concatenate([jnp.zeros((seq, half), F32), sin, jnp.zeros((seq, pad), F32)], -1)
    reps = SECTION // HEAD_DIM
    return (jnp.tile(c_head, (1, reps)), jnp.tile(a_head, (1, reps)), jnp.tile(b_head, (1, reps)))


def _in_projection(x2d, seq, norm1_w, w_in_bf, qk_w, rope_tabs):
    n = x2d.shape[0]
    tm = min(TM_IN, seq)
    steps_per_seq = seq // tm
    gid = np.arange(SECTION) // HEAD_DIM
    bd = jnp.asarray((gid[:, None] == gid[None, :]).astype(np.float32) / HEAD_DIM, BF16)
    tok = pl.BlockSpec((tm, SECTION), lambda i: (i, 0))
    pos = pl.BlockSpec((tm, SECTION), lambda i: (i % steps_per_seq, 0))
    out = jax.ShapeDtypeStruct((n, SECTION), BF16)
    return pl.pallas_call(
        _inproj_kernel,
        grid=(n // tm,),
        in_specs=[pl.BlockSpec((tm, D_MODEL), lambda i: (i, 0)),
                  _const_spec((1, D_MODEL)),
                  _const_spec((D_MODEL, 6 * SECTION)),
                  _const_spec((SECTION, SECTION)),
                  _const_spec((4, SECTION)),
                  pos, pos, pos],
        out_specs=[tok] * 6,
        out_shape=[out] * 6,
        compiler_params=_cparams(("parallel",)),
    )(x2d, norm1_w.reshape(1, D_MODEL), w_in_bf, bd, qk_w, *rope_tabs)


def _na_bias(rpb):
    c = np.arange(GRID_W)[:, None]
    kc = np.arange(GRID_W)[None, :]
    cs = np.clip(c - NA_KW // 2, 0, GRID_W - NA_KW)
    valid = (kc >= cs) & (kc < cs + NA_KW)
    rel_c = kc - c + (NA_KW - 1)
    onehot = ((np.arange(2 * NA_KW - 1)[:, None, None] == rel_c[None]) & valid[None]).astype(np.float32)
    col = jnp.einsum("hrx,xck->hrck", rpb.astype(F32), onehot, precision=lax.Precision.HIGHEST)
    col = jnp.where(valid, col, NEG_BIG)
    per_off = [col[:, NA_KH - 1 - o:2 * NA_KH - 1 - o] for o in range(NA_KH)]
    b = jnp.stack(per_off).transpose(0, 1, 3, 2, 4)
    return b.reshape(NA_KH, NA_HEADS, GRID_W, NA_KH * GRID_W)


def _na_kernel(q_ref, k_ref, v_ref, b_ref, o_ref, *, rows):
    j = pl.program_id(1)
    win = NA_KH * GRID_W

    def row_body(rr, carry):
        r = j * NA_ROWS_PER_STEP + rr
        rs = jnp.clip(r - NA_KH // 2, 0, rows - NA_KH)
        off = r - rs
        kstart = pl.multiple_of(rs * GRID_W, GRID_W)
        qstart = pl.multiple_of(rr * GRID_W, GRID_W)
        kw = k_ref[pl.ds(kstart, win), :]
        vw = v_ref[pl.ds(kstart, win), :]
        qrow = q_ref[pl.ds(qstart, GRID_W), :]
        outs = []
        for h in range(NA_HEADS):
            sl = slice(h * HEAD_DIM, (h + 1) * HEAD_DIM)
            s = lax.dot_general(qrow[:, sl], kw[:, sl], (((1,), (1,)), ((), ())),
                                preferred_element_type=F32)
            s = s + b_ref[off, h]
            m = jnp.max(s, axis=-1, keepdims=True)
            p = jnp.exp(s - m)
            l = jnp.sum(p, axis=-1, keepdims=True)
            o = jnp.dot(p.astype(BF16), vw[:, sl], preferred_element_type=F32)
            outs.append(o / l)
        o_ref[pl.ds(qstart, GRID_W), :] = jnp.concatenate(outs, axis=-1).astype(BF16)
        return carry

    lax.fori_loop(0, NA_ROWS_PER_STEP, row_body, 0)


def _neighbourhood_attention(q, k, v, bias, batch, seq):
    rows = seq // GRID_W
    assert rows >= NA_KH and rows % NA_ROWS_PER_STEP == 0
    nblk = rows // NA_ROWS_PER_STEP
    tq = NA_ROWS_PER_STEP * GRID_W
    qspec = pl.BlockSpec((tq, SECTION), lambda b, j: (b * nblk + j, 0))
    kvspec = pl.BlockSpec((seq, SECTION), lambda b, j: (b, 0))
    return pl.pallas_call(
        functools.partial(_na_kernel, rows=rows),
        grid=(batch, nblk),
        in_specs=[qspec, kvspec, kvspec, _const_spec(bias.shape)],
        out_specs=qspec,
        out_shape=jax.ShapeDtypeStruct(q.shape, BF16),
        compiler_params=_cparams(("parallel", "arbitrary")),
    )(q, k, v, bias)


def _diff_kernel(q_ref, k_ref, v_ref, lam_ref, sub_ref, o_ref, *, seq, tq, tk):
    q = q_ref[...]
    lane = lax.broadcasted_iota(jnp.int32, q.shape, 1)
    zero = jnp.zeros_like(q)
    q_maps = (jnp.where(lane < HEAD_DIM, q, zero), jnp.where(lane >= HEAD_DIM, q, zero))

    lp = lam_ref[...]
    lam = (jnp.exp(jnp.sum(lp[0:1] * lp[1:2], axis=-1, keepdims=True))
           - jnp.exp(jnp.sum(lp[2:3] * lp[3:4], axis=-1, keepdims=True)) + LAMBDA_INIT)

    def kv_body(c, carry):
        start = pl.multiple_of(c * tk, tk)
        kc = k_ref[pl.ds(start, tk), :]
        vc = v_ref[pl.ds(start, tk), :]
        new = []
        for mp in range(2):
            m, l, acc = carry[mp]
            s = lax.dot_general(q_maps[mp], kc, (((1,), (1,)), ((), ())), preferred_element_type=F32)
            mn = jnp.maximum(m, jnp.max(s, axis=-1, keepdims=True))
            alpha = jnp.exp(m - mn)
            p = jnp.exp(s - mn)
            l = alpha * l + jnp.sum(p, axis=-1, keepdims=True)
            acc = alpha * acc + jnp.dot(p.astype(BF16), vc, preferred_element_type=F32)
            new.append((mn, l, acc))
        return tuple(new)

    init = tuple((jnp.full((tq, 1), -jnp.inf, F32), jnp.zeros((tq, 1), F32),
                  jnp.zeros((tq, 2 * HEAD_DIM), F32)) for _ in range(2))
    (_, l0, a0), (_, l1, a1) = lax.fori_loop(0, seq // tk, kv_body, init)
    o = a0 / l0 - lam * (a1 / l1)
    ms = jnp.mean(o * o, axis=-1, keepdims=True)
    y = o * lax.rsqrt(ms + EPS) * sub_ref[...]
    o_ref[...] = (y * (1.0 - LAMBDA_INIT)).astype(BF16)


def _diff_attention(q, k, v, lam_params, subln_w, batch, seq):
    tq, tk = min(TQ, seq), min(TK, seq)
    nq = seq // tq
    width = 2 * HEAD_DIM
    qspec = pl.BlockSpec((tq, width), lambda b, h, i: (b * nq + i, h))
    kvspec = pl.BlockSpec((seq, width), lambda b, h, i: (b, h))
    return pl.pallas_call(
        functools.partial(_diff_kernel, seq=seq, tq=tq, tk=tk),
        grid=(batch, DIFF_HEADS, nq),
        in_specs=[qspec, kvspec, kvspec, _const_spec((4, HEAD_DIM)), _const_spec((1, width))],
        out_specs=qspec,
        out_shape=jax.ShapeDtypeStruct(q.shape, BF16),
        compiler_params=_cparams(("parallel", "parallel", "arbitrary")),
    )(q, k, v, lam_params, subln_w.reshape(1, width))


def _top16(sc, payload=None):
    iota = lax.broadcasted_iota(jnp.int32, sc.shape, 0).astype(F32)
    big = float(sc.shape[0])
    vals, picks = [], []
    for _ in range(PEER_TOPK):
        m = jnp.max(sc, axis=0, keepdims=True)
        ix = jnp.min(jnp.where(sc == m, iota, big), axis=0, keepdims=True)
        sel = iota == ix
        if payload is None:
            picks.append(ix)
        else:
            picks.append(jnp.sum(jnp.where(sel, payload, 0), axis=0, keepdims=True))
        sc = jnp.where(sel, -jnp.inf, sc)
        vals.append(m)
    picks = jnp.concatenate(picks, axis=0)
    return jnp.concatenate(vals, axis=0), picks.astype(jnp.int32)


def _pair_candidates(r0, r1):
    half = SUBLANES // 2
    groups = [r0[0:1] + r1[0:8], r0[0:1] + r1[8:16]]
    groups += [r0[a:a + 1] + r1[0:8] for a in (1, 2, 3)]
    groups += [jnp.concatenate([r0[a:a + 1] + r1[0:half], r0[a + 1:a + 2] + r1[0:half]], axis=0)
               for a in (4, 6)]
    groups.append(r0[8:16] + r1[0:1])
    return jnp.concatenate(groups, axis=0)


def _route_kernel(na_ref, df_ref, x_ref, wo_ref, n2_ref, wq_ref, sk_ref,
                  x1_ref, xn_ref, eid_ref, g_ref, q_scr, eid_scr, g_scr):
    tm = x_ref.shape[0]
    x1 = (x_ref[...]
          + jnp.dot(na_ref[...], wo_ref[0:SECTION, :], preferred_element_type=F32)
          + jnp.dot(df_ref[...], wo_ref[SECTION:2 * SECTION, :], preferred_element_type=F32))
    x1_ref[...] = x1
    ms = jnp.mean(x1 * x1, axis=-1, keepdims=True)
    xn = x1 * lax.rsqrt(ms + EPS) * n2_ref[...]
    xn_ref[...] = xn
    xb = xn.astype(BF16)
    for c in range(2 * PEER_HEADS):
        q_scr[c] = jnp.dot(xb, wq_ref[:, c * LANES:(c + 1) * LANES],
                           preferred_element_type=F32).astype(BF16)

    nblk = tm // TB

    def route_block(h, blk):
        tok0 = blk * TB
        tops = []
        for p in range(2):
            sc = lax.dot_general(sk_ref[2 * h + p], q_scr[2 * h + p, pl.ds(tok0, TB), :],
                                 (((1,), (1,)), ((), ())), preferred_element_type=F32)
            tops.append(_top16(sc))
        (s0, i0), (s1, i1) = tops
        cand_s = _pair_candidates(s0, s1)
        cand_i = _pair_candidates(i0 * N_KEYS, i1)
        top_s, eid = _top16(cand_s, cand_i)
        e = jnp.exp(top_s - top_s[0:1])
        row = pl.multiple_of(h * PEER_TOPK, PEER_TOPK)
        eid_scr[blk, pl.ds(row, PEER_TOPK), :] = eid * ROW_SUB
        g_scr[blk, pl.ds(row, PEER_TOPK), :] = e / jnp.sum(e, axis=0, keepdims=True)

    def head_body(h, carry):
        for blk in range(nblk):
            route_block(h, blk)
        return carry

    lax.fori_loop(0, PEER_HEADS, head_body, 0)
    for blk in range(nblk):
        eid_ref[blk] = eid_scr[blk].T
        g_ref[blk * TB:(blk + 1) * TB, :] = g_scr[blk].T


def _out_and_route(na, df, x2d, w_out_bf, norm2_w, w_q_bf, sk_bf):
    n = x2d.shape[0]
    tm = TM_OUT
    tok = lambda w: pl.BlockSpec((tm, w), lambda i: (i, 0))
    return pl.pallas_call(
        _route_kernel,
        grid=(n // tm,),
        in_specs=[tok(SECTION), tok(SECTION), tok(D_MODEL),
                  _const_spec((D_MODEL, D_MODEL)), _const_spec((1, D_MODEL)),
                  _const_spec(w_q_bf.shape), _const_spec(sk_bf.shape)],
        out_specs=[tok(D_MODEL), tok(D_MODEL),
                   pl.BlockSpec((tm // TB, TB, HK), lambda i: (i, 0, 0)), tok(HK)],
        out_shape=[jax.ShapeDtypeStruct((n, D_MODEL), F32), jax.ShapeDtypeStruct((n, D_MODEL), F32),
                   jax.ShapeDtypeStruct((n // TB, TB, HK), jnp.int32),
                   jax.ShapeDtypeStruct((n, HK), F32)],
        scratch_shapes=[pltpu.VMEM((2 * PEER_HEADS, tm, LANES), BF16),
                        pltpu.VMEM((tm // TB, HK, TB), jnp.int32), pltpu.VMEM((tm // TB, HK, TB), F32)],
        compiler_params=_cparams(("parallel",)),
    )(na, df, x2d, w_out_bf, norm2_w.reshape(1, D_MODEL), w_q_bf, sk_bf)


ROW_SUB = SUBLANES // 2
CHUNK_SLOTS = 32
N_CHUNKS = HK // CHUNK_SLOTS
CHUNK_ROWS = CHUNK_SLOTS * SUBLANES
TOKEN_UNROLL = 8


def _pack_table(tab):
    bits = lax.bitcast_convert_type(tab.astype(BF16), jnp.uint16).astype(jnp.uint32)
    bits = bits.reshape(tab.shape[0], ROW_SUB, 2, LANES)
    return (bits[:, :, 0, :] | (bits[:, :, 1, :] << 16)).reshape(tab.shape[0] * ROW_SUB, LANES)


def _split_bf16(x):
    hi = x.astype(BF16)
    return hi, (x - hi.astype(F32)).astype(BF16)


def _gather_chunk(tab_ref, eid_ref, t, chunk):
    ids = eid_ref.at[0, t]
    rows = [tab_ref[pl.ds(pl.multiple_of(ids[chunk * CHUNK_SLOTS + i], ROW_SUB), ROW_SUB), :]
            for i in range(CHUNK_SLOTS)]
    return pltpu.bitcast(jnp.concatenate(rows, axis=0), BF16)


def _diag_mask(width):
    r = lax.broadcasted_iota(jnp.int32, (SUBLANES, width), 0)
    n = lax.broadcasted_iota(jnp.int32, (SUBLANES, width), 1)
    return (n % SUBLANES) == r


def _peer_u_kernel(eid_ref, xn_ref, g_ref, tab_ref, rept_ref, rep_ref, wrep_ref, d_scr):
    mask = _diag_mask(CHUNK_ROWS)
    zeros = jnp.zeros((SUBLANES, LANES), BF16)

    def one_token(t):
        xhi, xlo = _split_bf16(xn_ref[t])
        lhs = jnp.concatenate([jnp.concatenate([xhi, zeros], axis=1),
                               jnp.concatenate([xlo, zeros], axis=1),
                               jnp.concatenate([zeros, xhi], axis=1),
                               jnp.concatenate([zeros, xlo], axis=1)], axis=0)
        for pr in range(N_CHUNKS // 2):
            s = jnp.concatenate([_gather_chunk(tab_ref, eid_ref, t, 2 * pr),
                                 _gather_chunk(tab_ref, eid_ref, t, 2 * pr + 1)], axis=1)
            z = lax.dot_general(lhs, s, (((1,), (1,)), ((), ())), preferred_element_type=F32)
            for half in range(2):
                z8 = z[16 * half:16 * half + 8] + z[16 * half + 8:16 * half + 16]
                zs = jnp.sum(jnp.where(mask, z8, 0.0), axis=0, keepdims=True)
                d_scr[pl.ds(t, 1), pl.ds((2 * pr + half) * CHUNK_ROWS, CHUNK_ROWS)] = zs

    def tok_body(i, carry):
        for u in range(TOKEN_UNROLL):
            one_token(i * TOKEN_UNROLL + u)
        return carry

    lax.fori_loop(0, TB // TOKEN_UNROLL, tok_body, 0)
    dhi, dlo = _split_bf16(d_scr[...])
    a = (jnp.dot(dhi, rept_ref[...], preferred_element_type=F32)
         + jnp.dot(dlo, rept_ref[...], preferred_element_type=F32))
    w = g_ref[...] * (0.5 * a * (1.0 + lax.erf(a * (2.0 ** -0.5))))
    whi, wlo = _split_bf16(w)
    wrep_ref[...] = (jnp.dot(whi, rep_ref[...], preferred_element_type=F32)
                     + jnp.dot(wlo, rep_ref[...], preferred_element_type=F32))


def _peer_v_kernel(eid_ref, wrep_ref, x1_ref, tab_ref, o_ref):
    mask = _diag_mask(CHUNK_ROWS)

    def one_token(t):
        wr = wrep_ref[pl.ds(t, 1), :]
        acc = x1_ref[t]
        for pr in range(N_CHUNKS // 2):
            parts = []
            for half in range(2):
                seg = wr[:, (2 * pr + half) * CHUNK_ROWS:(2 * pr + half + 1) * CHUNK_ROWS]
                parts.extend(_split_bf16(jnp.where(mask, jnp.broadcast_to(seg, mask.shape), 0.0)))
            lhs = jnp.concatenate(parts, axis=0)
            s = jnp.concatenate([_gather_chunk(tab_ref, eid_ref, t, 2 * pr),
                                 _gather_chunk(tab_ref, eid_ref, t, 2 * pr + 1)], axis=1)
            z = jnp.dot(lhs, s, preferred_element_type=F32)
            acc = acc + ((z[0:8, :LANES] + z[8:16, :LANES]) + (z[16:24, LANES:] + z[24:32, LANES:]))
        o_ref[t] = acc

    def tok_body(i, carry):
        for u in range(TOKEN_UNROLL):
            one_token(i * TOKEN_UNROLL + u)
        return carry

    lax.fori_loop(0, TB // TOKEN_UNROLL, tok_body, 0)


def _peer_experts(eid, g, xn3, x13, u_pack, v_pack):
    n = xn3.shape[0]
    nblk = n // TB
    smem_slot = pl.BlockSpec((1, TB, HK), lambda i: (i, 0, 0), memory_space=pltpu.SMEM)
    tok = pl.BlockSpec((TB, SUBLANES, LANES), lambda i: (i, 0, 0))
    flat = lambda w: pl.BlockSpec((TB, w), lambda i: (i, 0))
    table = pl.BlockSpec(u_pack.shape, lambda i: (0, 0), pipeline_mode=pl.Buffered(1))
    rep_np = (np.arange(HK)[:, None] == (np.arange(HK * SUBLANES)[None, :] // SUBLANES)).astype(np.float32)
    rep, rept = jnp.asarray(rep_np, BF16), jnp.asarray(rep_np.T, BF16)
    wrep = pl.pallas_call(
        _peer_u_kernel,
        grid=(nblk,),
        in_specs=[smem_slot, tok, flat(HK), table, _const_spec(rept.shape), _const_spec(rep.shape)],
        out_specs=flat(D_MODEL),
        out_shape=jax.ShapeDtypeStruct((n, D_MODEL), F32),
        scratch_shapes=[pltpu.VMEM((TB, D_MODEL), F32)],
        compiler_params=_cparams(("arbitrary",)),
    )(eid, xn3, g, u_pack, rept, rep)
    return pl.pallas_call(
        _peer_v_kernel,
        grid=(nblk,),
        in_specs=[smem_slot, flat(D_MODEL), tok, table],
        out_specs=tok,
        out_shape=jax.ShapeDtypeStruct((n, SUBLANES, LANES), F32),
        compiler_params=_cparams(("arbitrary",)),
    )(eid, wrep, x13, v_pack)


def _encoder_layer(x, p):
    batch, seq, _ = x.shape
    n = batch * seq
    x2d = x.reshape(n, D_MODEL)
    naq, nak, nav, dq, dk, dv = _in_projection(x2d, seq, p["norm1_w"], p["w_in"], p["qk_w"], p["rope"])
    na = _neighbourhood_attention(naq, nak, nav, p["na_bias"], batch, seq)
    df = _diff_attention(dq, dk, dv, p["lam"], p["subln_w"], batch, seq)
    x1, xn, eid, g = _out_and_route(na, df, x2d, p["w_out"], p["norm2_w"], p["w_q"], p["sub_keys"])
    y3 = _peer_experts(eid, g, xn.reshape(n, SUBLANES, LANES), x1.reshape(n, SUBLANES, LANES),
                       p["u_pack"], p["v_pack"])
    return y3.reshape(batch, seq, D_MODEL)


def kernel(x_prompt, x_sample, norm1_w, w_in, na_q_norm, na_k_norm, na_rpb, diff_q_norm, diff_k_norm,
           diff_lambda_q1, diff_lambda_k1, diff_lambda_q2, diff_lambda_k2, diff_subln_w, w_out, norm2_w,
           peer_w_q, peer_sub_keys, peer_u, peer_v):
    assert norm1_w.shape[0] == 1, "single-layer problem"
    reps = SECTION // HEAD_DIM
    params = {
        "norm1_w": norm1_w[0],
        "w_in": w_in[0].astype(BF16),
        "qk_w": jnp.stack([jnp.tile(w[0], reps) for w in (na_q_norm, na_k_norm, diff_q_norm, diff_k_norm)]),
        "lam": jnp.stack([diff_lambda_q1[0], diff_lambda_k1[0], diff_lambda_q2[0], diff_lambda_k2[0]]),
        "subln_w": diff_subln_w[0],
        "w_out": w_out[0].astype(BF16),
        "norm2_w": norm2_w[0],
        "w_q": peer_w_q[0].astype(BF16),
        "sub_keys": peer_sub_keys[0].reshape(2 * PEER_HEADS, N_KEYS, N_KEYS).astype(BF16),
        "na_bias": _na_bias(na_rpb[0]),
        "u_pack": _pack_table(peer_u[0]),
        "v_pack": _pack_table(peer_v[0]),
    }
    rope_by_seq = {}
    outs = []
    for x in (x_prompt, x_sample):
        seq = x.shape[1]
        if seq not in rope_by_seq:
            rope_by_seq[seq] = _rope_tables(seq)
        outs.append(_encoder_layer(x, dict(params, rope=rope_by_seq[seq])))
    return tuple(outs)
```

```python
import functools
import math

import jax
import jax.numpy as jnp
import numpy as np
from jax import lax
from jax.experimental import pallas as pl
from jax.experimental.pallas import tpu as pltpu

F32 = jnp.float32
BF16 = jnp.bfloat16

D_MODEL = 1024
GRID_W = 64
HEAD_DIM = 64
NA_HEADS = 8
NA_KH = 8
NA_KW = 16
DIFF_HEADS = 4
SECTION = 512
ROT_DIM = HEAD_DIM // 4
ROPE_THETA = 500000.0
PEER_HEADS = 8
N_KEYS = 128
PEER_TOPK = 16
N_EXPERTS = N_KEYS * N_KEYS
HK = PEER_HEADS * PEER_TOPK
EPS = 1e-6
NEG_BIG = -1e30
LOG2E = math.log2(math.e)
LAMBDA_INIT = 0.8 - 0.6 * math.exp(-0.3 * 0)

LANES = 128
SUBLANES = 8
VMEM_LIMIT = 56 * 1024 * 1024

TM_IN = 512
NA_ROWS_PER_STEP = 8
TQ = 512
KV_SUBCHUNKS = 1
TK = 4096
TM_OUT = 256
TB = 128


def _cparams(sem):
    return pltpu.CompilerParams(dimension_semantics=sem, vmem_limit_bytes=VMEM_LIMIT)


def _const_spec(shape):
    nd = len(shape)
    return pl.BlockSpec(shape, lambda *_: (0,) * nd)


def _inproj_kernel(x_ref, n1_ref, w_ref, bd_ref, nw_ref, rc_ref, ra_ref, rb_ref,
                   naq_ref, nak_ref, nav_ref, dq_ref, dk_ref, dv_ref):
    x = x_ref[...]
    ms = jnp.mean(x * x, axis=-1, keepdims=True)
    h = (x * lax.rsqrt(ms + EPS) * n1_ref[...]).astype(BF16)
    bd = bd_ref[...]

    def proj(c):
        return jnp.dot(h, w_ref[:, c * SECTION:(c + 1) * SECTION], preferred_element_type=F32)

    def qknorm(y, row):
        sq = y * y
        hi = sq.astype(BF16)
        lo = (sq - hi.astype(F32)).astype(BF16)
        msq = (jnp.dot(hi, bd, preferred_element_type=F32)
               + jnp.dot(lo, bd, preferred_element_type=F32))
        return y * lax.rsqrt(msq + EPS) * nw_ref[row:row + 1, :]

    def rope(y):
        return (y * rc_ref[...]
                + pltpu.roll(y, SECTION - ROT_DIM // 2, axis=1) * ra_ref[...]
                + pltpu.roll(y, ROT_DIM // 2, axis=1) * rb_ref[...])

    scale = HEAD_DIM ** -0.5 * LOG2E
    naq_ref[...] = (qknorm(proj(0), 0) * scale).astype(BF16)
    nak_ref[...] = qknorm(proj(1), 1).astype(BF16)
    nav_ref[...] = proj(2).astype(BF16)
    dq_ref[...] = (rope(qknorm(proj(3), 2)) * scale).astype(BF16)
    dk_ref[...] = rope(qknorm(proj(4), 3)).astype(BF16)
    dv_ref[...] = proj(5).astype(BF16)


def _rope_tables(seq):
    pos = jnp.arange(seq, dtype=F32)
    inv = ROPE_THETA ** (-jnp.arange(0, ROT_DIM, 2, dtype=F32) / ROT_DIM)
    ang = pos[:, None] * inv[None, :]
    cos, sin = jnp.cos(ang), jnp.sin(ang)
    half = ROT_DIM // 2
    pad = HEAD_DIM - ROT_DIM
    c_head = jnp.concatenate([cos, cos, jnp.ones((seq, pad), F32)], -1)
    a_head = jnp.concatenate([-sin, jnp.zeros((seq, pad + half), F32)], -1)
    b_head = jnp.concatenate([jnp.zeros((seq, half), F32), sin, jnp.zeros((seq, pad), F32)], -1)
    reps = SECTION // HEAD_DIM
    return (jnp.tile(c_head, (1, reps)), jnp.tile(a_head, (1, reps)), jnp.tile(b_head, (1, reps)))


def _in_projection(x2d, seq, norm1_w, w_in_bf, qk_w, rope_tabs):
    n = x2d.shape[0]
    tm = min(TM_IN, seq)
    steps_per_seq = seq // tm
    gid = np.arange(SECTION) // HEAD_DIM
    bd = jnp.asarray((gid[:, None] == gid[None, :]).astype(np.float32) / HEAD_DIM, BF16)
    tok = pl.BlockSpec((tm, SECTION), lambda i: (i, 0))
    pos = pl.BlockSpec((tm, SECTION), lambda i: (i % steps_per_seq, 0))
    out = jax.ShapeDtypeStruct((n, SECTION), BF16)
    return pl.pallas_call(
        _inproj_kernel,
        grid=(n // tm,),
        in_specs=[pl.BlockSpec((tm, D_MODEL), lambda i: (i, 0)),
                  _const_spec((1, D_MODEL)),
                  _const_spec((D_MODEL, 6 * SECTION)),
                  _const_spec((SECTION, SECTION)),
                  _const_spec((4, SECTION)),
                  pos, pos, pos],
        out_specs=[tok] * 6,
        out_shape=[out] * 6,
        compiler_params=_cparams(("parallel",)),
    )(x2d, norm1_w.reshape(1, D_MODEL), w_in_bf, bd, qk_w, *rope_tabs)


def _na_bias(rpb):
    c = np.arange(GRID_W)[:, None]
    kc = np.arange(GRID_W)[None, :]
    cs = np.clip(c - NA_KW // 2, 0, GRID_W - NA_KW)
    valid = (kc >= cs) & (kc < cs + NA_KW)
    rel_c = kc - c + (NA_KW - 1)
    onehot = ((np.arange(2 * NA_KW - 1)[:, None, None] == rel_c[None]) & valid[None]).astype(np.float32)
    col = jnp.einsum("hrx,xck->hrck", rpb.astype(F32), onehot, precision=lax.Precision.HIGHEST)
    col = jnp.where(valid, col * LOG2E, NEG_BIG)
    per_off = [col[:, NA_KH - 1 - o:2 * NA_KH - 1 - o] for o in range(NA_KH)]
    b = jnp.stack(per_off).transpose(0, 1, 3, 2, 4)
    return b.reshape(NA_KH, NA_HEADS, GRID_W, NA_KH * GRID_W)


def _na_kernel(q_ref, k_ref, v_ref, b_ref, o_ref, *, rows):
    j = pl.program_id(1)
    win = NA_KH * GRID_W

    def row_body(rr, carry):
        r = j * NA_ROWS_PER_STEP + rr
        rs = jnp.clip(r - NA_KH // 2, 0, rows - NA_KH)
        off = r - rs
        kstart = pl.multiple_of(rs * GRID_W, GRID_W)
        qstart = pl.multiple_of(rr * GRID_W, GRID_W)
        kw = k_ref[pl.ds(kstart, win), :]
        vw = v_ref[pl.ds(kstart, win), :]
        qrow = q_ref[pl.ds(qstart, GRID_W), :]
        outs = []
        for h in range(NA_HEADS):
            sl = slice(h * HEAD_DIM, (h + 1) * HEAD_DIM)
            s = lax.dot_general(qrow[:, sl], kw[:, sl], (((1,), (1,)), ((), ())),
                                preferred_element_type=F32)
            s = s + b_ref[off, h]
            m = jnp.max(s, axis=-1, keepdims=True)
            p = jnp.exp2(s - m)
            l = jnp.sum(p, axis=-1, keepdims=True)
            o = jnp.dot(p.astype(BF16), vw[:, sl], preferred_element_type=F32)
            outs.append(o / l)
        o_ref[pl.ds(qstart, GRID_W), :] = jnp.concatenate(outs, axis=-1).astype(BF16)
        return carry

    lax.fori_loop(0, NA_ROWS_PER_STEP, row_body, 0)


def _neighbourhood_attention(q, k, v, bias, batch, seq):
    rows = seq // GRID_W
    assert rows >= NA_KH and rows % NA_ROWS_PER_STEP == 0
    nblk = rows // NA_ROWS_PER_STEP
    tq = NA_ROWS_PER_STEP * GRID_W
    qspec = pl.BlockSpec((tq, SECTION), lambda b, j: (b * nblk + j, 0))
    kvspec = pl.BlockSpec((seq, SECTION), lambda b, j: (b, 0))
    return pl.pallas_call(
        functools.partial(_na_kernel, rows=rows),
        grid=(batch, nblk),
        in_specs=[qspec, kvspec, kvspec, _const_spec(bias.shape)],
        out_specs=qspec,
        out_shape=jax.ShapeDtypeStruct(q.shape, BF16),
        compiler_params=_cparams(("parallel", "arbitrary")),
    )(q, k, v, bias)


def _diff_kernel(q_ref, k_ref, v_ref, lam_ref, sub_ref, o_ref, m0_scr, m1_scr, acc0_scr, acc1_scr,
                 *, seq, tq, tk):
    m_scrs, acc_scrs = (m0_scr, m1_scr), (acc0_scr, acc1_scr)
    q = q_ref[...]
    lane = lax.broadcasted_iota(jnp.int32, q.shape, 1)
    zero = jnp.zeros_like(q)
    q_maps = (jnp.where(lane < HEAD_DIM, q, zero), jnp.where(lane >= HEAD_DIM, q, zero))

    lp = lam_ref[...]
    lam = (jnp.exp(jnp.sum(lp[0:1] * lp[1:2], axis=-1, keepdims=True))
           - jnp.exp(jnp.sum(lp[2:3] * lp[3:4], axis=-1, keepdims=True)) + LAMBDA_INIT)

    width = 2 * HEAD_DIM
    ones = jnp.ones((tk, width), BF16)

    for m_scr, acc_scr in zip(m_scrs, acc_scrs):
        m_scr[...] = jnp.full(m_scr.shape, -jnp.inf, F32)
        acc_scr[...] = jnp.zeros(acc_scr.shape, F32)

    def kv_body(c, carry):
        tks = tk // KV_SUBCHUNKS
        kcs, v_exts = [], []
        for sub in range(KV_SUBCHUNKS):
            start = pl.multiple_of(c * tk + sub * tks, tks)
            kcs.append(k_ref[pl.ds(start, tks), :])
            v_exts.append(jnp.concatenate([v_ref[pl.ds(start, tks), :], ones[:tks]], axis=1))
        scores = [[lax.dot_general(q_maps[mp], kcs[sub], (((1,), (1,)), ((), ())),
                                   preferred_element_type=F32) for mp in range(2)]
                  for sub in range(KV_SUBCHUNKS)]
        m = [m_scrs[mp][...] for mp in range(2)]
        probs, alphas = [], []
        for sub in range(KV_SUBCHUNKS):
            probs.append([])
            alphas.append([])
            for mp in range(2):
                mn = jnp.maximum(m[mp], jnp.max(scores[sub][mp], axis=-1, keepdims=True))
                probs[sub].append(jnp.exp2(scores[sub][mp] - mn[:, 0:1]).astype(BF16))
                alpha = jnp.exp2(m[mp] - mn)
                alphas[sub].append(jnp.concatenate([alpha, alpha], axis=1))
                m[mp] = mn
        for mp in range(2):
            m_scrs[mp][...] = m[mp]
            acc = acc_scrs[mp][...]
            for sub in range(KV_SUBCHUNKS):
                acc = alphas[sub][mp] * acc + jnp.dot(probs[sub][mp], v_exts[sub],
                                                      preferred_element_type=F32)
            acc_scrs[mp][...] = acc
        return carry

    lax.fori_loop(0, seq // tk, kv_body, 0)
    a0, a1 = acc0_scr[...], acc1_scr[...]
    o = a0[:, :width] / a0[:, width:width + 1] - lam * (a1[:, :width] / a1[:, width:width + 1])
    ms = jnp.mean(o * o, axis=-1, keepdims=True)
    y = o * lax.rsqrt(ms + EPS) * sub_ref[...]
    o_ref[...] = (y * (1.0 - LAMBDA_INIT)).astype(BF16)


def _diff_attention(q, k, v, lam_params, subln_w, batch, seq):
    tq, tk = min(TQ, seq), min(TK, seq)
    nq = seq // tq
    width = 2 * HEAD_DIM
    qspec = pl.BlockSpec((tq, width), lambda b, h, i: (b * nq + i, h))
    kvspec = pl.BlockSpec((seq, width), lambda b, h, i: (b, h))
    return pl.pallas_call(
        functools.partial(_diff_kernel, seq=seq, tq=tq, tk=tk),
        grid=(batch, DIFF_HEADS, nq),
        in_specs=[qspec, kvspec, kvspec, _const_spec((4, HEAD_DIM)), _const_spec((1, width))],
        out_specs=qspec,
        out_shape=jax.ShapeDtypeStruct(q.shape, BF16),
        scratch_shapes=[pltpu.VMEM((tq, width), F32), pltpu.VMEM((tq, width), F32),
                        pltpu.VMEM((tq, 2 * width), F32), pltpu.VMEM((tq, 2 * width), F32)],
        compiler_params=_cparams(("parallel", "parallel", "arbitrary")),
    )(q, k, v, lam_params, subln_w.reshape(1, width))


def _top16(sc, payload=None):
    iota = lax.broadcasted_iota(jnp.int32, sc.shape, 0).astype(F32)
    big = float(sc.shape[0])
    vals, picks = [], []
    for _ in range(PEER_TOPK):
        m = jnp.max(sc, axis=0, keepdims=True)
        ix = jnp.min(jnp.where(sc == m, iota, big), axis=0, keepdims=True)
        sel = iota == ix
        if payload is None:
            picks.append(ix)
        else:
            picks.append(jnp.sum(jnp.where(sel, payload, 0), axis=0, keepdims=True))
        sc = jnp.where(sel, -jnp.inf, sc)
        vals.append(m)
    picks = jnp.concatenate(picks, axis=0)
    return jnp.concatenate(vals, axis=0), picks.astype(jnp.int32)


def _pair_candidates(r0, r1):
    half = SUBLANES // 2
    groups = [r0[0:1] + r1[0:8], r0[0:1] + r1[8:16]]
    groups += [r0[a:a + 1] + r1[0:8] for a in (1, 2, 3)]
    groups += [jnp.concatenate([r0[a:a + 1] + r1[0:half], r0[a + 1:a + 2] + r1[0:half]], axis=0)
               for a in (4, 6)]
    groups.append(r0[8:16] + r1[0:1])
    return jnp.concatenate(groups, axis=0)


def _route_kernel(na_ref, df_ref, x_ref, wo_ref, n2_ref, wq_ref, sk_ref,
                  x1_ref, xn_ref, eid_ref, g_ref, q_scr, eid_scr, g_scr):
    tm = x_ref.shape[0]
    x1 = (x_ref[...]
          + jnp.dot(na_ref[...], wo_ref[0:SECTION, :], preferred_element_type=F32)
          + jnp.dot(df_ref[...], wo_ref[SECTION:2 * SECTION, :], preferred_element_type=F32))
    x1_ref[...] = x1
    ms = jnp.mean(x1 * x1, axis=-1, keepdims=True)
    xn = x1 * lax.rsqrt(ms + EPS) * n2_ref[...]
    xn_ref[...] = xn
    xb = xn.astype(BF16)
    for c in range(2 * PEER_HEADS):
        q_scr[c] = jnp.dot(xb, wq_ref[:, c * LANES:(c + 1) * LANES],
                           preferred_element_type=F32).astype(BF16)

    nblk = tm // TB

    def route_block(h, blk):
        tok0 = blk * TB
        tops = []
        for p in range(2):
            sc = lax.dot_general(sk_ref[2 * h + p], q_scr[2 * h + p, pl.ds(tok0, TB), :],
                                 (((1,), (1,)), ((), ())), preferred_element_type=F32)
            tops.append(_top16(sc))
        (s0, i0), (s1, i1) = tops
        cand_s = _pair_candidates(s0, s1)
        cand_i = _pair_candidates(i0 * N_KEYS, i1)
        top_s, eid = _top16(cand_s, cand_i)
        e = jnp.exp(top_s - top_s[0:1])
        row = pl.multiple_of(h * PEER_TOPK, PEER_TOPK)
        eid_scr[blk, pl.ds(row, PEER_TOPK), :] = eid * ROW_SUB
        g_scr[blk, pl.ds(row, PEER_TOPK), :] = e / jnp.sum(e, axis=0, keepdims=True)

    def head_body(h, carry):
        for blk in range(nblk):
            route_block(h, blk)
        return carry

    lax.fori_loop(0, PEER_HEADS, head_body, 0)
    for blk in range(nblk):
        eid_ref[blk] = eid_scr[blk].T
        g_ref[blk * TB:(blk + 1) * TB, :] = g_scr[blk].T


def _out_and_route(na, df, x2d, w_out_bf, norm2_w, w_q_bf, sk_bf):
    n = x2d.shape[0]
    tm = TM_OUT
    tok = lambda w: pl.BlockSpec((tm, w), lambda i: (i, 0))
    return pl.pallas_call(
        _route_kernel,
        grid=(n // tm,),
        in_specs=[tok(SECTION), tok(SECTION), tok(D_MODEL),
                  _const_spec((D_MODEL, D_MODEL)), _const_spec((1, D_MODEL)),
                  _const_spec(w_q_bf.shape), _const_spec(sk_bf.shape)],
        out_specs=[tok(D_MODEL), tok(D_MODEL),
                   pl.BlockSpec((tm // TB, TB, HK), lambda i: (i, 0, 0)), tok(HK)],
        out_shape=[jax.ShapeDtypeStruct((n, D_MODEL), F32), jax.ShapeDtypeStruct((n, D_MODEL), F32),
                   jax.ShapeDtypeStruct((n // TB, TB, HK), jnp.int32),
                   jax.ShapeDtypeStruct((n, HK), F32)],
        scratch_shapes=[pltpu.VMEM((2 * PEER_HEADS, tm, LANES), BF16),
                        pltpu.VMEM((tm // TB, HK, TB), jnp.int32), pltpu.VMEM((tm // TB, HK, TB), F32)],
        compiler_params=_cparams(("parallel",)),
    )(na, df, x2d, w_out_bf, norm2_w.reshape(1, D_MODEL), w_q_bf, sk_bf)


ROW_SUB = SUBLANES // 2
CHUNK_SLOTS = 32
N_CHUNKS = HK // CHUNK_SLOTS
CHUNK_ROWS = CHUNK_SLOTS * SUBLANES
TOKEN_UNROLL = 8


def _pack_table(tab):
    bits = lax.bitcast_convert_type(tab.astype(BF16), jnp.uint16).astype(jnp.uint32)
    bits = bits.reshape(tab.shape[0], ROW_SUB, 2, LANES)
    return (bits[:, :, 0, :] | (bits[:, :, 1, :] << 16)).reshape(tab.shape[0] * ROW_SUB, LANES)


def _split_bf16(x):
    hi = x.astype(BF16)
    return hi, (x - hi.astype(F32)).astype(BF16)


def _gather_chunk(tab_ref, eid_ref, t, chunk):
    ids = eid_ref.at[0, t]
    rows = [tab_ref[pl.ds(pl.multiple_of(ids[chunk * CHUNK_SLOTS + i], ROW_SUB), ROW_SUB), :]
            for i in range(CHUNK_SLOTS)]
    return pltpu.bitcast(jnp.concatenate(rows, axis=0), BF16)


def _diag_mask(width):
    r = lax.broadcasted_iota(jnp.int32, (SUBLANES, width), 0)
    n = lax.broadcasted_iota(jnp.int32, (SUBLANES, width), 1)
    return (n % SUBLANES) == r


def _peer_u_kernel(eid_ref, xn_ref, g_ref, tab_ref, rept_ref, rep_ref, wrep_ref, d_scr):
    mask = _diag_mask(CHUNK_ROWS)
    zeros = jnp.zeros((SUBLANES, LANES), BF16)

    def one_token(t):
        xhi, xlo = _split_bf16(xn_ref[t])
        lhs = jnp.concatenate([jnp.concatenate([xhi, zeros], axis=1),
                               jnp.concatenate([xlo, zeros], axis=1),
                               jnp.concatenate([zeros, xhi], axis=1),
                               jnp.concatenate([zeros, xlo], axis=1)], axis=0)
        for pr in range(N_CHUNKS // 2):
            s = jnp.concatenate([_gather_chunk(tab_ref, eid_ref, t, 2 * pr),
                                 _gather_chunk(tab_ref, eid_ref, t, 2 * pr + 1)], axis=1)
            z = lax.dot_general(lhs, s, (((1,), (1,)), ((), ())), preferred_element_type=F32)
            for half in range(2):
                z8 = z[16 * half:16 * half + 8] + z[16 * half + 8:16 * half + 16]
                zs = jnp.sum(jnp.where(mask, z8, 0.0), axis=0, keepdims=True)
                d_scr[pl.ds(t, 1), pl.ds((2 * pr + half) * CHUNK_ROWS, CHUNK_ROWS)] = zs

    def tok_body(i, carry):
        for u in range(TOKEN_UNROLL):
            one_token(i * TOKEN_UNROLL + u)
        return carry

    lax.fori_loop(0, TB // TOKEN_UNROLL, tok_body, 0)
    dhi, dlo = _split_bf16(d_scr[...])
    a = (jnp.dot(dhi, rept_ref[...], preferred_element_type=F32)
         + jnp.dot(dlo, rept_ref[...], preferred_element_type=F32))
    w = g_ref[...] * (0.5 * a * (1.0 + lax.erf(a * (2.0 ** -0.5))))
    whi, wlo = _split_bf16(w)
    wrep_ref[...] = (jnp.dot(whi, rep_ref[...], preferred_element_type=F32)
                     + jnp.dot(wlo, rep_ref[...], preferred_element_type=F32))


def _peer_v_kernel(eid_ref, wrep_ref, x1_ref, tab_ref, o_ref):
    mask = _diag_mask(CHUNK_ROWS)

    def one_token(t):
        wr = wrep_ref[pl.ds(t, 1), :]
        acc = x1_ref[t]
        for pr in range(N_CHUNKS // 2):
            parts = []
            for half in range(2):
                seg = wr[:, (2 * pr + half) * CHUNK_ROWS:(2 * pr + half + 1) * CHUNK_ROWS]
                parts.extend(_split_bf16(jnp.where(mask, jnp.broadcast_to(seg, mask.shape), 0.0)))
            lhs = jnp.concatenate(parts, axis=0)
            s = jnp.concatenate([_gather_chunk(tab_ref, eid_ref, t, 2 * pr),
                                 _gather_chunk(tab_ref, eid_ref, t, 2 * pr + 1)], axis=1)
            z = jnp.dot(lhs, s, preferred_element_type=F32)
            acc = acc + ((z[0:8, :LANES] + z[8:16, :LANES]) + (z[16:24, LANES:] + z[24:32, LANES:]))
        o_ref[t] = acc

    def tok_body(i, carry):
        for u in range(TOKEN_UNROLL):
            one_token(i * TOKEN_UNROLL + u)
        return carry

    lax.fori_loop(0, TB // TOKEN_UNROLL, tok_body, 0)


def _peer_experts(eid, g, xn3, x13, u_pack, v_pack):
    n = xn3.shape[0]
    nblk = n // TB
    smem_slot = pl.BlockSpec((1, TB, HK), lambda i: (i, 0, 0), memory_space=pltpu.SMEM)
    tok = pl.BlockSpec((TB, SUBLANES, LANES), lambda i: (i, 0, 0))
    flat = lambda w: pl.BlockSpec((TB, w), lambda i: (i, 0))
    table = pl.BlockSpec(u_pack.shape, lambda i: (0, 0), pipeline_mode=pl.Buffered(1))
    rep_np = (np.arange(HK)[:, None] == (np.arange(HK * SUBLANES)[None, :] // SUBLANES)).astype(np.float32)
    rep, rept = jnp.asarray(rep_np, BF16), jnp.asarray(rep_np.T, BF16)
    wrep = pl.pallas_call(
        _peer_u_kernel,
        grid=(nblk,),
        in_specs=[smem_slot, tok, flat(HK), table, _const_spec(rept.shape), _const_spec(rep.shape)],
        out_specs=flat(D_MODEL),
        out_shape=jax.ShapeDtypeStruct((n, D_MODEL), F32),
        scratch_shapes=[pltpu.VMEM((TB, D_MODEL), F32)],
        compiler_params=_cparams(("arbitrary",)),
    )(eid, xn3, g, u_pack, rept, rep)
    return pl.pallas_call(
        _peer_v_kernel,
        grid=(nblk,),
        in_specs=[smem_slot, flat(D_MODEL), tok, table],
        out_specs=tok,
        out_shape=jax.ShapeDtypeStruct((n, SUBLANES, LANES), F32),
        compiler_params=_cparams(("arbitrary",)),
    )(eid, wrep, x13, v_pack)


def _encoder_layer(x, p):
    batch, seq, _ = x.shape
    n = batch * seq
    x2d = x.reshape(n, D_MODEL)
    naq, nak, nav, dq, dk, dv = _in_projection(x2d, seq, p["norm1_w"], p["w_in"], p["qk_w"], p["rope"])
    na = _neighbourhood_attention(naq, nak, nav, p["na_bias"], batch, seq)
    df = _diff_attention(dq, dk, dv, p["lam"], p["subln_w"], batch, seq)
    x1, xn, eid, g = _out_and_route(na, df, x2d, p["w_out"], p["norm2_w"], p["w_q"], p["sub_keys"])
    y3 = _peer_experts(eid, g, xn.reshape(n, SUBLANES, LANES), x1.reshape(n, SUBLANES, LANES),
                       p["u_pack"], p["v_pack"])
    return y3.reshape(batch, seq, D_MODEL)


def kernel(x_prompt, x_sample, norm1_w, w_in, na_q_norm, na_k_norm, na_rpb, diff_q_norm, diff_k_norm,
           diff_lambda_q1, diff_lambda_k1, diff_lambda_q2, diff_lambda_k2, diff_subln_w, w_out, norm2_w,
           peer_w_q, peer_sub_keys, peer_u, peer_v):
    assert norm1_w.shape[0] == 1, "single-layer problem"
    reps = SECTION // HEAD_DIM
    params = {
        "norm1_w": norm1_w[0],
        "w_in": w_in[0].astype(BF16),
        "qk_w": jnp.stack([jnp.tile(w[0], reps) for w in (na_q_norm, na_k_norm, diff_q_norm, diff_k_norm)]),
        "lam": jnp.stack([diff_lambda_q1[0], diff_lambda_k1[0], diff_lambda_q2[0], diff_lambda_k2[0]]),
        "subln_w": diff_subln_w[0],
        "w_out": w_out[0].astype(BF16),
        "norm2_w": norm2_w[0],
        "w_q": peer_w_q[0].astype(BF16),
        "sub_keys": peer_sub_keys[0].reshape(2 * PEER_HEADS, N_KEYS, N_KEYS).astype(BF16),
        "na_bias": _na_bias(na_rpb[0]),
        "u_pack": _pack_table(peer_u[0]),
        "v_pack": _pack_table(peer_v[0]),
    }
    rope_by_seq = {}
    outs = []
    for x in (x_prompt, x_sample):
        seq = x.shape[1]
        if seq not in rope_by_seq:
            rope_by_seq[seq] = _rope_tables(seq)
        outs.append(_encoder_layer(x, dict(params, rope=rope_by_seq[seq])))
    return tuple(outs)
```

```python
import functools
import math

import jax
import jax.numpy as jnp
import numpy as np
from jax import lax
from jax.experimental import pallas as pl
from jax.experimental.pallas import tpu as pltpu

F32 = jnp.float32
BF16 = jnp.bfloat16

D_MODEL = 1024
GRID_W = 64
HEAD_DIM = 64
NA_HEADS = 8
NA_KH = 8
NA_KW = 16
DIFF_HEADS = 4
SECTION = 512
ROT_DIM = HEAD_DIM // 4
ROPE_THETA = 500000.0
PEER_HEADS = 8
N_KEYS = 128
PEER_TOPK = 16
N_EXPERTS = N_KEYS * N_KEYS
HK = PEER_HEADS * PEER_TOPK
EPS = 1e-6
NEG_BIG = -1e30
LOG2E = math.log2(math.e)
LAMBDA_INIT = 0.8 - 0.6 * math.exp(-0.3 * 0)

LANES = 128
SUBLANES = 8
VMEM_LIMIT = 56 * 1024 * 1024

TM_IN = 512
NA_ROWS_PER_STEP = 8
NA_ROW_UNROLL = 2
TQ = 512
KV_SUBCHUNKS = 1
TK = 4096
TM_OUT = 256
TB = 128
PB = 256


def _cparams(sem):
    return pltpu.CompilerParams(dimension_semantics=sem, vmem_limit_bytes=VMEM_LIMIT)


def _const_spec(shape):
    nd = len(shape)
    return pl.BlockSpec(shape, lambda *_: (0,) * nd)


def _inproj_kernel(x_ref, n1_ref, w_ref, bd_ref, nw_ref, rc_ref, ra_ref, rb_ref,
                   naq_ref, nak_ref, nav_ref, dq_ref, dk_ref, dv_ref):
    x = x_ref[...]
    ms = jnp.mean(x * x, axis=-1, keepdims=True)
    h = (x * lax.rsqrt(ms + EPS) * n1_ref[...]).astype(BF16)
    bd = bd_ref[...]

    def proj(c):
        return jnp.dot(h, w_ref[:, c * SECTION:(c + 1) * SECTION], preferred_element_type=F32)

    def qknorm(y, row):
        sq = y * y
        hi = sq.astype(BF16)
        lo = (sq - hi.astype(F32)).astype(BF16)
        msq = (jnp.dot(hi, bd, preferred_element_type=F32)
               + jnp.dot(lo, bd, preferred_element_type=F32))
        return y * lax.rsqrt(msq + EPS) * nw_ref[row:row + 1, :]

    def rope(y):
        return (y * rc_ref[...]
                + pltpu.roll(y, SECTION - ROT_DIM // 2, axis=1) * ra_ref[...]
                + pltpu.roll(y, ROT_DIM // 2, axis=1) * rb_ref[...])

    scale = HEAD_DIM ** -0.5 * LOG2E
    naq_ref[...] = (qknorm(proj(0), 0) * scale).astype(BF16)
    nak_ref[...] = qknorm(proj(1), 1).astype(BF16)
    nav_ref[...] = proj(2).astype(BF16)
    dq_ref[...] = (rope(qknorm(proj(3), 2)) * scale).astype(BF16)
    dk_ref[...] = rope(qknorm(proj(4), 3)).astype(BF16)
    dv_ref[...] = proj(5).astype(BF16)


def _rope_tables(seq):
    pos = jnp.arange(seq, dtype=F32)
    inv = ROPE_THETA ** (-jnp.arange(0, ROT_DIM, 2, dtype=F32) / ROT_DIM)
    ang = pos[:, None] * inv[None, :]
    cos, sin = jnp.cos(ang), jnp.sin(ang)
    half = ROT_DIM // 2
    pad = HEAD_DIM - ROT_DIM
    c_head = jnp.concatenate([cos, cos, jnp.ones((seq, pad), F32)], -1)
    a_head = jnp.concatenate([-sin, jnp.zeros((seq, pad + half), F32)], -1)
    b_head = jnp.concatenate([jnp.zeros((seq, half), F32), sin, jnp.zeros((seq, pad), F32)], -1)
    reps = SECTION // HEAD_DIM
    return (jnp.tile(c_head, (1, reps)), jnp.tile(a_head, (1, reps)), jnp.tile(b_head, (1, reps)))


def _in_projection(x2d, seq, norm1_w, w_in_bf, qk_w, rope_tabs):
    n = x2d.shape[0]
    tm = min(TM_IN, seq)
    steps_per_seq = seq // tm
    gid = np.arange(SECTION) // HEAD_DIM
    bd = jnp.asarray((gid[:, None] == gid[None, :]).astype(np.float32) / HEAD_DIM, BF16)
    tok = pl.BlockSpec((tm, SECTION), lambda i: (i, 0))
    pos = pl.BlockSpec((tm, SECTION), lambda i: (i % steps_per_seq, 0))
    out = jax.ShapeDtypeStruct((n, SECTION), BF16)
    return pl.pallas_call(
        _inproj_kernel,
        grid=(n // tm,),
        in_specs=[pl.BlockSpec((tm, D_MODEL), lambda i: (i, 0)),
                  _const_spec((1, D_MODEL)),
                  _const_spec((D_MODEL, 6 * SECTION)),
                  _const_spec((SECTION, SECTION)),
                  _const_spec((4, SECTION)),
                  pos, pos, pos],
        out_specs=[tok] * 6,
        out_shape=[out] * 6,
        compiler_params=_cparams(("parallel",)),
    )(x2d, norm1_w.reshape(1, D_MODEL), w_in_bf, bd, qk_w, *rope_tabs)


def _na_bias(rpb):
    c = np.arange(GRID_W)[:, None]
    kc = np.arange(GRID_W)[None, :]
    cs = np.clip(c - NA_KW // 2, 0, GRID_W - NA_KW)
    valid = (kc >= cs) & (kc < cs + NA_KW)
    rel_c = kc - c + (NA_KW - 1)
    onehot = ((np.arange(2 * NA_KW - 1)[:, None, None] == rel_c[None]) & valid[None]).astype(np.float32)
    col = jnp.einsum("hrx,xck->hrck", rpb.astype(F32), onehot, precision=lax.Precision.HIGHEST)
    col = jnp.where(valid, col * LOG2E, NEG_BIG)
    per_off = [col[:, NA_KH - 1 - o:2 * NA_KH - 1 - o] for o in range(NA_KH)]
    b = jnp.stack(per_off).transpose(0, 1, 3, 2, 4)
    return b.reshape(NA_KH, NA_HEADS, GRID_W, NA_KH * GRID_W)


def _na_kernel(q_ref, k_ref, v_ref, b_ref, o_ref, *, rows):
    j = pl.program_id(1)
    win = NA_KH * GRID_W
    pair = 2 * HEAD_DIM
    lane = lax.broadcasted_iota(jnp.int32, (GRID_W, pair), 1)
    first = lane < HEAD_DIM
    ones = jnp.ones((win, pair), BF16)

    def one_row(rr):
        r = j * NA_ROWS_PER_STEP + rr
        rs = jnp.clip(r - NA_KH // 2, 0, rows - NA_KH)
        off = r - rs
        kstart = pl.multiple_of(rs * GRID_W, GRID_W)
        qstart = pl.multiple_of(rr * GRID_W, GRID_W)
        outs = []
        for hp in range(NA_HEADS // 2):
            sl = slice(hp * pair, (hp + 1) * pair)
            qp = q_ref[pl.ds(qstart, GRID_W), sl]
            kp = k_ref[pl.ds(kstart, win), sl]
            v_ext = jnp.concatenate([v_ref[pl.ds(kstart, win), sl], ones], axis=1)
            halves = []
            for hh in range(2):
                qm = jnp.where(first if hh == 0 else ~first, qp, jnp.zeros_like(qp))
                s = lax.dot_general(qm, kp, (((1,), (1,)), ((), ())), preferred_element_type=F32)
                s = s + b_ref[off, 2 * hp + hh]
                p = jnp.exp2(s - jnp.max(s, axis=-1, keepdims=True)).astype(BF16)
                z = jnp.dot(p, v_ext, preferred_element_type=F32)
                halves.append(z[:, :pair] / z[:, pair:pair + 1])
            outs.append(jnp.where(first, halves[0], halves[1]))
        o_ref[pl.ds(qstart, GRID_W), :] = jnp.concatenate(outs, axis=-1).astype(BF16)

    def row_body(i, carry):
        for u in range(NA_ROW_UNROLL):
            one_row(i * NA_ROW_UNROLL + u)
        return carry

    lax.fori_loop(0, NA_ROWS_PER_STEP // NA_ROW_UNROLL, row_body, 0)


def _neighbourhood_attention(q, k, v, bias, batch, seq):
    rows = seq // GRID_W
    assert rows >= NA_KH and rows % NA_ROWS_PER_STEP == 0
    nblk = rows // NA_ROWS_PER_STEP
    tq = NA_ROWS_PER_STEP * GRID_W
    qspec = pl.BlockSpec((tq, SECTION), lambda b, j: (b * nblk + j, 0))
    kvspec = pl.BlockSpec((seq, SECTION), lambda b, j: (b, 0))
    return pl.pallas_call(
        functools.partial(_na_kernel, rows=rows),
        grid=(batch, nblk),
        in_specs=[qspec, kvspec, kvspec, _const_spec(bias.shape)],
        out_specs=qspec,
        out_shape=jax.ShapeDtypeStruct(q.shape, BF16),
        compiler_params=_cparams(("parallel", "arbitrary")),
    )(q, k, v, bias)


def _diff_kernel(q_ref, k_ref, v_ref, lam_ref, sub_ref, o_ref, m0_scr, m1_scr, acc0_scr, acc1_scr,
                 *, seq, tq, tk):
    m_scrs, acc_scrs = (m0_scr, m1_scr), (acc0_scr, acc1_scr)
    q = q_ref[...]
    lane = lax.broadcasted_iota(jnp.int32, q.shape, 1)
    zero = jnp.zeros_like(q)
    q_maps = (jnp.where(lane < HEAD_DIM, q, zero), jnp.where(lane >= HEAD_DIM, q, zero))

    lp = lam_ref[...]
    lam = (jnp.exp(jnp.sum(lp[0:1] * lp[1:2], axis=-1, keepdims=True))
           - jnp.exp(jnp.sum(lp[2:3] * lp[3:4], axis=-1, keepdims=True)) + LAMBDA_INIT)

    width = 2 * HEAD_DIM
    ones = jnp.ones((tk, width), BF16)

    for m_scr, acc_scr in zip(m_scrs, acc_scrs):
        m_scr[...] = jnp.full(m_scr.shape, -jnp.inf, F32)
        acc_scr[...] = jnp.zeros(acc_scr.shape, F32)

    def kv_body(c, carry):
        tks = tk // KV_SUBCHUNKS
        kcs, v_exts = [], []
        for sub in range(KV_SUBCHUNKS):
            start = pl.multiple_of(c * tk + sub * tks, tks)
            kcs.append(k_ref[pl.ds(start, tks), :])
            v_exts.append(jnp.concatenate([v_ref[pl.ds(start, tks), :], ones[:tks]], axis=1))
        scores = [[lax.dot_general(q_maps[mp], kcs[sub], (((1,), (1,)), ((), ())),
                                   preferred_element_type=F32) for mp in range(2)]
                  for sub in range(KV_SUBCHUNKS)]
        m = [m_scrs[mp][...] for mp in range(2)]
        probs, alphas = [], []
        for sub in range(KV_SUBCHUNKS):
            probs.append([])
            alphas.append([])
            for mp in range(2):
                mn = jnp.maximum(m[mp], jnp.max(scores[sub][mp], axis=-1, keepdims=True))
                probs[sub].append(jnp.exp2(scores[sub][mp] - mn[:, 0:1]).astype(BF16))
                alpha = jnp.exp2(m[mp] - mn)
                alphas[sub].append(jnp.concatenate([alpha, alpha], axis=1))
                m[mp] = mn
        for mp in range(2):
            m_scrs[mp][...] = m[mp]
            acc = acc_scrs[mp][...]
            for sub in range(KV_SUBCHUNKS):
                acc = alphas[sub][mp] * acc + jnp.dot(probs[sub][mp], v_exts[sub],
                                                      preferred_element_type=F32)
            acc_scrs[mp][...] = acc
        return carry

    lax.fori_loop(0, seq // tk, kv_body, 0)
    a0, a1 = acc0_scr[...], acc1_scr[...]
    o = a0[:, :width] / a0[:, width:width + 1] - lam * (a1[:, :width] / a1[:, width:width + 1])
    ms = jnp.mean(o * o, axis=-1, keepdims=True)
    y = o * lax.rsqrt(ms + EPS) * sub_ref[...]
    o_ref[...] = (y * (1.0 - LAMBDA_INIT)).astype(BF16)


def _diff_attention(q, k, v, lam_params, subln_w, batch, seq):
    tq, tk = min(TQ, seq), min(TK, seq)
    nq = seq // tq
    width = 2 * HEAD_DIM
    qspec = pl.BlockSpec((tq, width), lambda b, h, i: (b * nq + i, h))
    kvspec = pl.BlockSpec((seq, width), lambda b, h, i: (b, h))
    return pl.pallas_call(
        functools.partial(_diff_kernel, seq=seq, tq=tq, tk=tk),
        grid=(batch, DIFF_HEADS, nq),
        in_specs=[qspec, kvspec, kvspec, _const_spec((4, HEAD_DIM)), _const_spec((1, width))],
        out_specs=qspec,
        out_shape=jax.ShapeDtypeStruct(q.shape, BF16),
        scratch_shapes=[pltpu.VMEM((tq, width), F32), pltpu.VMEM((tq, width), F32),
                        pltpu.VMEM((tq, 2 * width), F32), pltpu.VMEM((tq, 2 * width), F32)],
        compiler_params=_cparams(("parallel", "parallel", "arbitrary")),
    )(q, k, v, lam_params, subln_w.reshape(1, width))


def _top16(sc, payload=None):
    iota = lax.broadcasted_iota(jnp.int32, sc.shape, 0).astype(F32)
    big = float(sc.shape[0])
    vals, picks = [], []
    for _ in range(PEER_TOPK):
        m = jnp.max(sc, axis=0, keepdims=True)
        ix = jnp.min(jnp.where(sc == m, iota, big), axis=0, keepdims=True)
        sel = iota == ix
        if payload is None:
            picks.append(ix)
        else:
            picks.append(jnp.sum(jnp.where(sel, payload, 0), axis=0, keepdims=True))
        sc = jnp.where(sel, -jnp.inf, sc)
        vals.append(m)
    picks = jnp.concatenate(picks, axis=0)
    return jnp.concatenate(vals, axis=0), picks.astype(jnp.int32)


def _pair_candidates(r0, r1):
    half = SUBLANES // 2
    groups = [r0[0:1] + r1[0:8], r0[0:1] + r1[8:16]]
    groups += [r0[a:a + 1] + r1[0:8] for a in (1, 2, 3)]
    groups += [jnp.concatenate([r0[a:a + 1] + r1[0:half], r0[a + 1:a + 2] + r1[0:half]], axis=0)
               for a in (4, 6)]
    groups.append(r0[8:16] + r1[0:1])
    return jnp.concatenate(groups, axis=0)


def _route_kernel(na_ref, df_ref, x_ref, wo_ref, n2_ref, wq_ref, sk_ref,
                  x1_ref, xn_ref, eid_ref, g_ref, q_scr, eid_scr, g_scr):
    tm = x_ref.shape[0]
    x1 = (x_ref[...]
          + jnp.dot(na_ref[...], wo_ref[0:SECTION, :], preferred_element_type=F32)
          + jnp.dot(df_ref[...], wo_ref[SECTION:2 * SECTION, :], preferred_element_type=F32))
    x1_ref[...] = x1
    ms = jnp.mean(x1 * x1, axis=-1, keepdims=True)
    xn = x1 * lax.rsqrt(ms + EPS) * n2_ref[...]
    xn_ref[...] = xn
    xb = xn.astype(BF16)
    for c in range(2 * PEER_HEADS):
        q_scr[c] = jnp.dot(xb, wq_ref[:, c * LANES:(c + 1) * LANES],
                           preferred_element_type=F32).astype(BF16)

    nblk = tm // TB

    def route_block(h, blk):
        tok0 = blk * TB
        tops = []
        for p in range(2):
            sc = lax.dot_general(sk_ref[2 * h + p], q_scr[2 * h + p, pl.ds(tok0, TB), :],
                                 (((1,), (1,)), ((), ())), preferred_element_type=F32)
            tops.append(_top16(sc))
        (s0, i0), (s1, i1) = tops
        cand_s = _pair_candidates(s0, s1)
        cand_i = _pair_candidates(i0 * N_KEYS, i1)
        top_s, eid = _top16(cand_s, cand_i)
        e = jnp.exp(top_s - top_s[0:1])
        row = pl.multiple_of(h * PEER_TOPK, PEER_TOPK)
        eid_scr[blk, pl.ds(row, PEER_TOPK), :] = eid * ROW_SUB
        g_scr[blk, pl.ds(row, PEER_TOPK), :] = e / jnp.sum(e, axis=0, keepdims=True)

    def head_body(h, carry):
        for blk in range(nblk):
            route_block(h, blk)
        return carry

    lax.fori_loop(0, PEER_HEADS, head_body, 0)
    for blk in range(nblk):
        eid_ref[blk] = eid_scr[blk].T
        g_ref[blk * TB:(blk + 1) * TB, :] = g_scr[blk].T


def _out_and_route(na, df, x2d, w_out_bf, norm2_w, w_q_bf, sk_bf):
    n = x2d.shape[0]
    tm = TM_OUT
    tok = lambda w: pl.BlockSpec((tm, w), lambda i: (i, 0))
    return pl.pallas_call(
        _route_kernel,
        grid=(n // tm,),
        in_specs=[tok(SECTION), tok(SECTION), tok(D_MODEL),
                  _const_spec((D_MODEL, D_MODEL)), _const_spec((1, D_MODEL)),
                  _const_spec(w_q_bf.shape), _const_spec(sk_bf.shape)],
        out_specs=[tok(D_MODEL), tok(D_MODEL),
                   pl.BlockSpec((tm // TB, TB, HK), lambda i: (i, 0, 0)), tok(HK)],
        out_shape=[jax.ShapeDtypeStruct((n, D_MODEL), F32), jax.ShapeDtypeStruct((n, D_MODEL), F32),
                   jax.ShapeDtypeStruct((n // TB, TB, HK), jnp.int32),
                   jax.ShapeDtypeStruct((n, HK), F32)],
        scratch_shapes=[pltpu.VMEM((2 * PEER_HEADS, tm, LANES), BF16),
                        pltpu.VMEM((tm // TB, HK, TB), jnp.int32), pltpu.VMEM((tm // TB, HK, TB), F32)],
        compiler_params=_cparams(("parallel",)),
    )(na, df, x2d, w_out_bf, norm2_w.reshape(1, D_MODEL), w_q_bf, sk_bf)


ROW_SUB = SUBLANES // 2
CHUNK_SLOTS = 32
N_CHUNKS = HK // CHUNK_SLOTS
CHUNK_ROWS = CHUNK_SLOTS * SUBLANES
TOKEN_UNROLL = 8


def _pack_table(tab):
    bits = lax.bitcast_convert_type(tab.astype(BF16), jnp.uint16).astype(jnp.uint32)
    bits = bits.reshape(tab.shape[0], ROW_SUB, 2, LANES)
    return (bits[:, :, 0, :] | (bits[:, :, 1, :] << 16)).reshape(tab.shape[0] * ROW_SUB, LANES)


def _split_bf16(x):
    hi = x.astype(BF16)
    return hi, (x - hi.astype(F32)).astype(BF16)


def _gather_chunk(tab_ref, eid_ref, t, chunk):
    ids = eid_ref.at[0, t]
    rows = [tab_ref[pl.ds(pl.multiple_of(ids[chunk * CHUNK_SLOTS + i], ROW_SUB), ROW_SUB), :]
            for i in range(CHUNK_SLOTS)]
    return pltpu.bitcast(jnp.concatenate(rows, axis=0), BF16)


def _diag_mask(width):
    r = lax.broadcasted_iota(jnp.int32, (SUBLANES, width), 0)
    n = lax.broadcasted_iota(jnp.int32, (SUBLANES, width), 1)
    return (n % SUBLANES) == r


def _peer_u_kernel(eid_ref, xn_ref, g_ref, tab_ref, rept_ref, rep_ref, wrep_ref, d_scr):
    mask = _diag_mask(CHUNK_ROWS)
    zeros = jnp.zeros((SUBLANES, LANES), BF16)

    def one_token(t):
        xhi, xlo = _split_bf16(xn_ref[t])
        lhs = jnp.concatenate([jnp.concatenate([xhi, zeros], axis=1),
                               jnp.concatenate([xlo, zeros], axis=1),
                               jnp.concatenate([zeros, xhi], axis=1),
                               jnp.concatenate([zeros, xlo], axis=1)], axis=0)
        for pr in range(N_CHUNKS // 2):
            s = jnp.concatenate([_gather_chunk(tab_ref, eid_ref, t, 2 * pr),
                                 _gather_chunk(tab_ref, eid_ref, t, 2 * pr + 1)], axis=1)
            z = lax.dot_general(lhs, s, (((1,), (1,)), ((), ())), preferred_element_type=F32)
            for half in range(2):
                z8 = z[16 * half:16 * half + 8] + z[16 * half + 8:16 * half + 16]
                zs = jnp.sum(jnp.where(mask, z8, 0.0), axis=0, keepdims=True)
                d_scr[pl.ds(t, 1), pl.ds((2 * pr + half) * CHUNK_ROWS, CHUNK_ROWS)] = zs

    def tok_body(i, carry):
        for u in range(TOKEN_UNROLL):
            one_token(i * TOKEN_UNROLL + u)
        return carry

    lax.fori_loop(0, PB // TOKEN_UNROLL, tok_body, 0)
    dhi, dlo = _split_bf16(d_scr[...])
    a = (jnp.dot(dhi, rept_ref[...], preferred_element_type=F32)
         + jnp.dot(dlo, rept_ref[...], preferred_element_type=F32))
    w = g_ref[...] * (0.5 * a * (1.0 + lax.erf(a * (2.0 ** -0.5))))
    whi, wlo = _split_bf16(w)
    wrep_ref[...] = (jnp.dot(whi, rep_ref[...], preferred_element_type=F32)
                     + jnp.dot(wlo, rep_ref[...], preferred_element_type=F32))


def _peer_v_kernel(eid_ref, wrep_ref, x1_ref, tab_ref, o_ref):
    mask = _diag_mask(CHUNK_ROWS)

    def one_token(t):
        wr = wrep_ref[pl.ds(t, 1), :]
        acc = x1_ref[t]
        for pr in range(N_CHUNKS // 2):
            parts = []
            for half in range(2):
                seg = wr[:, (2 * pr + half) * CHUNK_ROWS:(2 * pr + half + 1) * CHUNK_ROWS]
                parts.extend(_split_bf16(jnp.where(mask, jnp.broadcast_to(seg, mask.shape), 0.0)))
            lhs = jnp.concatenate(parts, axis=0)
            s = jnp.concatenate([_gather_chunk(tab_ref, eid_ref, t, 2 * pr),
                                 _gather_chunk(tab_ref, eid_ref, t, 2 * pr + 1)], axis=1)
            z = jnp.dot(lhs, s, preferred_element_type=F32)
            acc = acc + ((z[0:8, :LANES] + z[8:16, :LANES]) + (z[16:24, LANES:] + z[24:32, LANES:]))
        o_ref[t] = acc

    def tok_body(i, carry):
        for u in range(TOKEN_UNROLL):
            one_token(i * TOKEN_UNROLL + u)
        return carry

    lax.fori_loop(0, PB // TOKEN_UNROLL, tok_body, 0)


def _peer_experts(eid, g, xn3, x13, u_pack, v_pack):
    n = xn3.shape[0]
    nblk = n // PB
    eid = eid.reshape(nblk, PB, HK)
    smem_slot = pl.BlockSpec((1, PB, HK), lambda i: (i, 0, 0), memory_space=pltpu.SMEM)
    tok = pl.BlockSpec((PB, SUBLANES, LANES), lambda i: (i, 0, 0))
    flat = lambda w: pl.BlockSpec((PB, w), lambda i: (i, 0))
    table = pl.BlockSpec(u_pack.shape, lambda i: (0, 0), pipeline_mode=pl.Buffered(1))
    rep_np = (np.arange(HK)[:, None] == (np.arange(HK * SUBLANES)[None, :] // SUBLANES)).astype(np.float32)
    rep, rept = jnp.asarray(rep_np, BF16), jnp.asarray(rep_np.T, BF16)
    wrep = pl.pallas_call(
        _peer_u_kernel,
        grid=(nblk,),
        in_specs=[smem_slot, tok, flat(HK), table, _const_spec(rept.shape), _const_spec(rep.shape)],
        out_specs=flat(D_MODEL),
        out_shape=jax.ShapeDtypeStruct((n, D_MODEL), F32),
        scratch_shapes=[pltpu.VMEM((PB, D_MODEL), F32)],
        compiler_params=_cparams(("arbitrary",)),
    )(eid, xn3, g, u_pack, rept, rep)
    return pl.pallas_call(
        _peer_v_kernel,
        grid=(nblk,),
        in_specs=[smem_slot, flat(D_MODEL), tok, table],
        out_specs=tok,
        out_shape=jax.ShapeDtypeStruct((n, SUBLANES, LANES), F32),
        compiler_params=_cparams(("arbitrary",)),
    )(eid, wrep, x13, v_pack)


def _encoder_layer(x, p):
    batch, seq, _ = x.shape
    n = batch * seq
    x2d = x.reshape(n, D_MODEL)
    naq, nak, nav, dq, dk, dv = _in_projection(x2d, seq, p["norm1_w"], p["w_in"], p["qk_w"], p["rope"])
    na = _neighbourhood_attention(naq, nak, nav, p["na_bias"], batch, seq)
    df = _diff_attention(dq, dk, dv, p["lam"], p["subln_w"], batch, seq)
    x1, xn, eid, g = _out_and_route(na, df, x2d, p["w_out"], p["norm2_w"], p["w_q"], p["sub_keys"])
    y3 = _peer_experts(eid, g, xn.reshape(n, SUBLANES, LANES), x1.reshape(n, SUBLANES, LANES),
                       p["u_pack"], p["v_pack"])
    return y3.reshape(batch, seq, D_MODEL)


def kernel(x_prompt, x_sample, norm1_w, w_in, na_q_norm, na_k_norm, na_rpb, diff_q_norm, diff_k_norm,
           diff_lambda_q1, diff_lambda_k1, diff_lambda_q2, diff_lambda_k2, diff_subln_w, w_out, norm2_w,
           peer_w_q, peer_sub_keys, peer_u, peer_v):
    assert norm1_w.shape[0] == 1, "single-layer problem"
    reps = SECTION // HEAD_DIM
    params = {
        "norm1_w": norm1_w[0],
        "w_in": w_in[0].astype(BF16),
        "qk_w": jnp.stack([jnp.tile(w[0], reps) for w in (na_q_norm, na_k_norm, diff_q_norm, diff_k_norm)]),
        "lam": jnp.stack([diff_lambda_q1[0], diff_lambda_k1[0], diff_lambda_q2[0], diff_lambda_k2[0]]),
        "subln_w": diff_subln_w[0],
        "w_out": w_out[0].astype(BF16),
        "norm2_w": norm2_w[0],
        "w_q": peer_w_q[0].astype(BF16),
        "sub_keys": peer_sub_keys[0].reshape(2 * PEER_HEADS, N_KEYS, N_KEYS).astype(BF16),
        "na_bias": _na_bias(na_rpb[0]),
        "u_pack": _pack_table(peer_u[0]),
        "v_pack": _pack_table(peer_v[0]),
    }
    rope_by_seq = {}
    outs = []
    for x in (x_prompt, x_sample):
        seq = x.shape[1]
        if seq not in rope_by_seq:
            rope_by_seq[seq] = _rope_tables(seq)
        outs.append(_encoder_layer(x, dict(params, rope=rope_by_seq[seq])))
    return tuple(outs)
```

```python
import functools
import math

import jax
import jax.numpy as jnp
import numpy as np
from jax import lax
from jax.experimental import pallas as pl
from jax.experimental.pallas import tpu as pltpu

F32 = jnp.float32
BF16 = jnp.bfloat16

D_MODEL = 1024
GRID_W = 64
HEAD_DIM = 64
NA_HEADS = 8
NA_KH = 8
NA_KW = 16
DIFF_HEADS = 4
SECTION = 512
ROT_DIM = HEAD_DIM // 4
ROPE_THETA = 500000.0
PEER_HEADS = 8
N_KEYS = 128
PEER_TOPK = 16
N_EXPERTS = N_KEYS * N_KEYS
HK = PEER_HEADS * PEER_TOPK
EPS = 1e-6
NEG_BIG = -1e30
LOG2E = math.log2(math.e)
LAMBDA_INIT = 0.8 - 0.6 * math.exp(-0.3 * 0)

LANES = 128
SUBLANES = 8
VMEM_LIMIT = 56 * 1024 * 1024

TM_IN = 512
NA_ROWS_PER_STEP = 8
NA_ROW_UNROLL = 2
TQ = 512
KV_SUBCHUNKS = 1
TK = 4096
TM_OUT = 256
TB = 128
PB = 256


def _cparams(sem):
    return pltpu.CompilerParams(dimension_semantics=sem, vmem_limit_bytes=VMEM_LIMIT)


def _const_spec(shape):
    nd = len(shape)
    return pl.BlockSpec(shape, lambda *_: (0,) * nd)


def _inproj_kernel(x_ref, n1_ref, w_ref, bd_ref, nw_ref, rc_ref, ra_ref, rb_ref,
                   naq_ref, nak_ref, nav_ref, dq_ref, dk_ref, dv_ref):
    x = x_ref[...]
    ms = jnp.mean(x * x, axis=-1, keepdims=True)
    h = (x * lax.rsqrt(ms + EPS) * n1_ref[...]).astype(BF16)
    bd = bd_ref[...]

    def proj(c):
        return jnp.dot(h, w_ref[:, c * SECTION:(c + 1) * SECTION], preferred_element_type=F32)

    def qknorm(y, row):
        sq = y * y
        hi = sq.astype(BF16)
        lo = (sq - hi.astype(F32)).astype(BF16)
        msq = (jnp.dot(hi, bd, preferred_element_type=F32)
               + jnp.dot(lo, bd, preferred_element_type=F32))
        return y * lax.rsqrt(msq + EPS) * nw_ref[row:row + 1, :]

    def rope(y):
        return (y * rc_ref[...]
                + pltpu.roll(y, SECTION - ROT_DIM // 2, axis=1) * ra_ref[...]
                + pltpu.roll(y, ROT_DIM // 2, axis=1) * rb_ref[...])

    scale = HEAD_DIM ** -0.5 * LOG2E
    naq_ref[...] = (qknorm(proj(0), 0) * scale).astype(BF16)
    nak_ref[...] = qknorm(proj(1), 1).astype(BF16)
    nav_ref[...] = proj(2).astype(BF16)
    dq_ref[...] = (rope(qknorm(proj(3), 2)) * scale).astype(BF16)
    dk_ref[...] = rope(qknorm(proj(4), 3)).astype(BF16)
    dv_ref[...] = proj(5).astype(BF16)


def _rope_tables(seq):
    pos = jnp.arange(seq, dtype=F32)
    inv = ROPE_THETA ** (-jnp.arange(0, ROT_DIM, 2, dtype=F32) / ROT_DIM)
    ang = pos[:, None] * inv[None, :]
    cos, sin = jnp.cos(ang), jnp.sin(ang)
    half = ROT_DIM // 2
    pad = HEAD_DIM - ROT_DIM
    c_head = jnp.concatenate([cos, cos, jnp.ones((seq, pad), F32)], -1)
    a_head = jnp.concatenate([-sin, jnp.zeros((seq, pad + half), F32)], -1)
    b_head = jnp.concatenate([jnp.zeros((seq, half), F32), sin, jnp.zeros((seq, pad), F32)], -1)
    reps = SECTION // HEAD_DIM
    return (jnp.tile(c_head, (1, reps)), jnp.tile(a_head, (1, reps)), jnp.tile(b_head, (1, reps)))


def _in_projection(x2d, seq, norm1_w, w_in_bf, qk_w, rope_tabs):
    n = x2d.shape[0]
    tm = min(TM_IN, seq)
    steps_per_seq = seq // tm
    gid = np.arange(SECTION) // HEAD_DIM
    bd = jnp.asarray((gid[:, None] == gid[None, :]).astype(np.float32) / HEAD_DIM, BF16)
    tok = pl.BlockSpec((tm, SECTION), lambda i: (i, 0))
    pos = pl.BlockSpec((tm, SECTION), lambda i: (i % steps_per_seq, 0))
    out = jax.ShapeDtypeStruct((n, SECTION), BF16)
    return pl.pallas_call(
        _inproj_kernel,
        grid=(n // tm,),
        in_specs=[pl.BlockSpec((tm, D_MODEL), lambda i: (i, 0)),
                  _const_spec((1, D_MODEL)),
                  _const_spec((D_MODEL, 6 * SECTION)),
                  _const_spec((SECTION, SECTION)),
                  _const_spec((4, SECTION)),
                  pos, pos, pos],
        out_specs=[tok] * 6,
        out_shape=[out] * 6,
        compiler_params=_cparams(("parallel",)),
    )(x2d, norm1_w.reshape(1, D_MODEL), w_in_bf, bd, qk_w, *rope_tabs)


def _na_bias(rpb):
    c = np.arange(GRID_W)[:, None]
    kc = np.arange(GRID_W)[None, :]
    cs = np.clip(c - NA_KW // 2, 0, GRID_W - NA_KW)
    valid = (kc >= cs) & (kc < cs + NA_KW)
    rel_c = kc - c + (NA_KW - 1)
    onehot = ((np.arange(2 * NA_KW - 1)[:, None, None] == rel_c[None]) & valid[None]).astype(np.float32)
    col = jnp.einsum("hrx,xck->hrck", rpb.astype(F32), onehot, precision=lax.Precision.HIGHEST)
    col = jnp.where(valid, col * LOG2E, NEG_BIG)
    per_off = [col[:, NA_KH - 1 - o:2 * NA_KH - 1 - o] for o in range(NA_KH)]
    b = jnp.stack(per_off).transpose(0, 1, 3, 2, 4)
    return b.reshape(NA_KH, NA_HEADS, GRID_W, NA_KH * GRID_W)


def _na_kernel(q_ref, k_ref, v_ref, b_ref, o_ref, *, rows):
    j = pl.program_id(1)
    win = NA_KH * GRID_W
    pair = 2 * HEAD_DIM
    lane = lax.broadcasted_iota(jnp.int32, (GRID_W, pair), 1)
    first = lane < HEAD_DIM
    ones = jnp.ones((win, pair), BF16)

    def one_row(rr):
        r = j * NA_ROWS_PER_STEP + rr
        rs = jnp.clip(r - NA_KH // 2, 0, rows - NA_KH)
        off = r - rs
        kstart = pl.multiple_of(rs * GRID_W, GRID_W)
        qstart = pl.multiple_of(rr * GRID_W, GRID_W)
        outs = []
        for hp in range(NA_HEADS // 2):
            sl = slice(hp * pair, (hp + 1) * pair)
            qp = q_ref[pl.ds(qstart, GRID_W), sl]
            kp = k_ref[pl.ds(kstart, win), sl]
            v_ext = jnp.concatenate([v_ref[pl.ds(kstart, win), sl], ones], axis=1)
            halves = []
            for hh in range(2):
                qm = jnp.where(first if hh == 0 else ~first, qp, jnp.zeros_like(qp))
                s = lax.dot_general(qm, kp, (((1,), (1,)), ((), ())), preferred_element_type=F32)
                s = s + b_ref[off, 2 * hp + hh]
                p = jnp.exp2(s - jnp.max(s, axis=-1, keepdims=True)).astype(BF16)
                z = jnp.dot(p, v_ext, preferred_element_type=F32)
                halves.append(z[:, :pair] / z[:, pair:pair + 1])
            outs.append(jnp.where(first, halves[0], halves[1]))
        o_ref[pl.ds(qstart, GRID_W), :] = jnp.concatenate(outs, axis=-1).astype(BF16)

    def row_body(i, carry):
        for u in range(NA_ROW_UNROLL):
            one_row(i * NA_ROW_UNROLL + u)
        return carry

    lax.fori_loop(0, NA_ROWS_PER_STEP // NA_ROW_UNROLL, row_body, 0)


def _neighbourhood_attention(q, k, v, bias, batch, seq):
    rows = seq // GRID_W
    assert rows >= NA_KH and rows % NA_ROWS_PER_STEP == 0
    nblk = rows // NA_ROWS_PER_STEP
    tq = NA_ROWS_PER_STEP * GRID_W
    qspec = pl.BlockSpec((tq, SECTION), lambda b, j: (b * nblk + j, 0))
    kvspec = pl.BlockSpec((seq, SECTION), lambda b, j: (b, 0))
    return pl.pallas_call(
        functools.partial(_na_kernel, rows=rows),
        grid=(batch, nblk),
        in_specs=[qspec, kvspec, kvspec, _const_spec(bias.shape)],
        out_specs=qspec,
        out_shape=jax.ShapeDtypeStruct(q.shape, BF16),
        compiler_params=_cparams(("parallel", "arbitrary")),
    )(q, k, v, bias)


def _diff_kernel(q_ref, k_ref, v_ref, lam_ref, sub_ref, o_ref, m0_scr, m1_scr, acc0_scr, acc1_scr,
                 *, seq, tq, tk):
    m_scrs, acc_scrs = (m0_scr, m1_scr), (acc0_scr, acc1_scr)
    q = q_ref[...]
    lane = lax.broadcasted_iota(jnp.int32, q.shape, 1)
    zero = jnp.zeros_like(q)
    q_maps = (jnp.where(lane < HEAD_DIM, q, zero), jnp.where(lane >= HEAD_DIM, q, zero))

    lp = lam_ref[...]
    lam = (jnp.exp(jnp.sum(lp[0:1] * lp[1:2], axis=-1, keepdims=True))
           - jnp.exp(jnp.sum(lp[2:3] * lp[3:4], axis=-1, keepdims=True)) + LAMBDA_INIT)

    width = 2 * HEAD_DIM
    ones = jnp.ones((tk, width), BF16)

    for m_scr, acc_scr in zip(m_scrs, acc_scrs):
        m_scr[...] = jnp.full(m_scr.shape, -jnp.inf, F32)
        acc_scr[...] = jnp.zeros(acc_scr.shape, F32)

    def kv_body(c, carry):
        tks = tk // KV_SUBCHUNKS
        kcs, v_exts = [], []
        for sub in range(KV_SUBCHUNKS):
            start = pl.multiple_of(c * tk + sub * tks, tks)
            kcs.append(k_ref[pl.ds(start, tks), :])
            v_exts.append(jnp.concatenate([v_ref[pl.ds(start, tks), :], ones[:tks]], axis=1))
        scores = [[lax.dot_general(q_maps[mp], kcs[sub], (((1,), (1,)), ((), ())),
                                   preferred_element_type=F32) for mp in range(2)]
                  for sub in range(KV_SUBCHUNKS)]
        m = [m_scrs[mp][...] for mp in range(2)]
        probs, alphas = [], []
        for sub in range(KV_SUBCHUNKS):
            probs.append([])
            alphas.append([])
            for mp in range(2):
                mn = jnp.maximum(m[mp], jnp.max(scores[sub][mp], axis=-1, keepdims=True))
                probs[sub].append(jnp.exp2(scores[sub][mp] - mn[:, 0:1]).astype(BF16))
                alpha = jnp.exp2(m[mp] - mn)
                alphas[sub].append(jnp.concatenate([alpha, alpha], axis=1))
                m[mp] = mn
        for mp in range(2):
            m_scrs[mp][...] = m[mp]
            acc = acc_scrs[mp][...]
            for sub in range(KV_SUBCHUNKS):
                acc = alphas[sub][mp] * acc + jnp.dot(probs[sub][mp], v_exts[sub],
                                                      preferred_element_type=F32)
            acc_scrs[mp][...] = acc
        return carry

    lax.fori_loop(0, seq // tk, kv_body, 0)
    a0, a1 = acc0_scr[...], acc1_scr[...]
    o = a0[:, :width] / a0[:, width:width + 1] - lam * (a1[:, :width] / a1[:, width:width + 1])
    ms = jnp.mean(o * o, axis=-1, keepdims=True)
    y = o * lax.rsqrt(ms + EPS) * sub_ref[...]
    o_ref[...] = (y * (1.0 - LAMBDA_INIT)).astype(BF16)


def _diff_attention(q, k, v, lam_params, subln_w, batch, seq):
    tq, tk = min(TQ, seq), min(TK, seq)
    nq = seq // tq
    width = 2 * HEAD_DIM
    qspec = pl.BlockSpec((tq, width), lambda b, h, i: (b * nq + i, h))
    kvspec = pl.BlockSpec((seq, width), lambda b, h, i: (b, h))
    return pl.pallas_call(
        functools.partial(_diff_kernel, seq=seq, tq=tq, tk=tk),
        grid=(batch, DIFF_HEADS, nq),
        in_specs=[qspec, kvspec, kvspec, _const_spec((4, HEAD_DIM)), _const_spec((1, width))],
        out_specs=qspec,
        out_shape=jax.ShapeDtypeStruct(q.shape, BF16),
        scratch_shapes=[pltpu.VMEM((tq, width), F32), pltpu.VMEM((tq, width), F32),
                        pltpu.VMEM((tq, 2 * width), F32), pltpu.VMEM((tq, 2 * width), F32)],
        compiler_params=_cparams(("parallel", "parallel", "arbitrary")),
    )(q, k, v, lam_params, subln_w.reshape(1, width))


def _top16(sc, payload=None):
    iota = lax.broadcasted_iota(jnp.int32, sc.shape, 0).astype(F32)
    big = float(sc.shape[0])
    vals, picks = [], []
    for _ in range(PEER_TOPK):
        m = jnp.max(sc, axis=0, keepdims=True)
        ix = jnp.min(jnp.where(sc == m, iota, big), axis=0, keepdims=True)
        sel = iota == ix
        if payload is None:
            picks.append(ix)
        else:
            picks.append(jnp.sum(jnp.where(sel, payload, 0), axis=0, keepdims=True))
        sc = jnp.where(sel, -jnp.inf, sc)
        vals.append(m)
    picks = jnp.concatenate(picks, axis=0)
    return jnp.concatenate(vals, axis=0), picks.astype(jnp.int32)


def _pair_candidates(r0, r1):
    half = SUBLANES // 2
    groups = [r0[0:1] + r1[0:8], r0[0:1] + r1[8:16]]
    groups += [r0[a:a + 1] + r1[0:8] for a in (1, 2, 3)]
    groups += [jnp.concatenate([r0[a:a + 1] + r1[0:half], r0[a + 1:a + 2] + r1[0:half]], axis=0)
               for a in (4, 6)]
    groups.append(r0[8:16] + r1[0:1])
    return jnp.concatenate(groups, axis=0)


def _route_kernel(na_ref, df_ref, x_ref, wo_ref, n2_ref, wq_ref, sk_ref,
                  x1_ref, xn_ref, eid_ref, g_ref, q_scr, eid_scr, g_scr):
    tm = x_ref.shape[0]
    x1 = (x_ref[...]
          + jnp.dot(na_ref[...], wo_ref[0:SECTION, :], preferred_element_type=F32)
          + jnp.dot(df_ref[...], wo_ref[SECTION:2 * SECTION, :], preferred_element_type=F32))
    ms = jnp.mean(x1 * x1, axis=-1, keepdims=True)
    xn = x1 * lax.rsqrt(ms + EPS) * n2_ref[...]
    for r in range(SUBLANES):
        x1_ref[:, r, :] = x1[:, r * LANES:(r + 1) * LANES]
        xn_ref[:, r, :] = xn[:, r * LANES:(r + 1) * LANES]
    xb = xn.astype(BF16)
    for c in range(2 * PEER_HEADS):
        q_scr[c] = jnp.dot(xb, wq_ref[:, c * LANES:(c + 1) * LANES],
                           preferred_element_type=F32).astype(BF16)

    nblk = tm // TB

    def route_block(h, blk):
        tok0 = blk * TB
        tops = []
        for p in range(2):
            sc = lax.dot_general(sk_ref[2 * h + p], q_scr[2 * h + p, pl.ds(tok0, TB), :],
                                 (((1,), (1,)), ((), ())), preferred_element_type=F32)
            tops.append(_top16(sc))
        (s0, i0), (s1, i1) = tops
        cand_s = _pair_candidates(s0, s1)
        cand_i = _pair_candidates(i0 * N_KEYS, i1)
        top_s, eid = _top16(cand_s, cand_i)
        e = jnp.exp(top_s - top_s[0:1])
        row = pl.multiple_of(h * PEER_TOPK, PEER_TOPK)
        eid_scr[blk, pl.ds(row, PEER_TOPK), :] = eid * ROW_SUB
        g_scr[blk, pl.ds(row, PEER_TOPK), :] = e / jnp.sum(e, axis=0, keepdims=True)

    def head_body(h, carry):
        for blk in range(nblk):
            route_block(h, blk)
        return carry

    lax.fori_loop(0, PEER_HEADS, head_body, 0)
    for blk in range(nblk):
        eid_ref[blk] = eid_scr[blk].T
        g_ref[blk * TB:(blk + 1) * TB, :] = g_scr[blk].T


def _out_and_route(na, df, x2d, w_out_bf, norm2_w, w_q_bf, sk_bf):
    n = x2d.shape[0]
    tm = TM_OUT
    tok = lambda w: pl.BlockSpec((tm, w), lambda i: (i, 0))
    tiles = pl.BlockSpec((tm, SUBLANES, LANES), lambda i: (i, 0, 0))
    return pl.pallas_call(
        _route_kernel,
        grid=(n // tm,),
        in_specs=[tok(SECTION), tok(SECTION), tok(D_MODEL),
                  _const_spec((D_MODEL, D_MODEL)), _const_spec((1, D_MODEL)),
                  _const_spec(w_q_bf.shape), _const_spec(sk_bf.shape)],
        out_specs=[tiles, tiles,
                   pl.BlockSpec((tm // TB, TB, HK), lambda i: (i, 0, 0)), tok(HK)],
        out_shape=[jax.ShapeDtypeStruct((n, SUBLANES, LANES), F32),
                   jax.ShapeDtypeStruct((n, SUBLANES, LANES), F32),
                   jax.ShapeDtypeStruct((n // TB, TB, HK), jnp.int32),
                   jax.ShapeDtypeStruct((n, HK), F32)],
        scratch_shapes=[pltpu.VMEM((2 * PEER_HEADS, tm, LANES), BF16),
                        pltpu.VMEM((tm // TB, HK, TB), jnp.int32), pltpu.VMEM((tm // TB, HK, TB), F32)],
        compiler_params=_cparams(("parallel",)),
    )(na, df, x2d, w_out_bf, norm2_w.reshape(1, D_MODEL), w_q_bf, sk_bf)


ROW_SUB = SUBLANES // 2
CHUNK_SLOTS = 32
N_CHUNKS = HK // CHUNK_SLOTS
CHUNK_ROWS = CHUNK_SLOTS * SUBLANES
TOKEN_UNROLL = 8


def _pack_table(tab):
    bits = lax.bitcast_convert_type(tab.astype(BF16), jnp.uint16).astype(jnp.uint32)
    bits = bits.reshape(tab.shape[0], ROW_SUB, 2, LANES)
    return (bits[:, :, 0, :] | (bits[:, :, 1, :] << 16)).reshape(tab.shape[0] * ROW_SUB, LANES)


def _split_bf16(x):
    hi = x.astype(BF16)
    return hi, (x - hi.astype(F32)).astype(BF16)


def _gather_chunk(tab_ref, eid_ref, t, chunk):
    ids = eid_ref.at[0, t]
    rows = [tab_ref[pl.ds(pl.multiple_of(ids[chunk * CHUNK_SLOTS + i], ROW_SUB), ROW_SUB), :]
            for i in range(CHUNK_SLOTS)]
    return pltpu.bitcast(jnp.concatenate(rows, axis=0), BF16)


def _diag_mask(width):
    r = lax.broadcasted_iota(jnp.int32, (SUBLANES, width), 0)
    n = lax.broadcasted_iota(jnp.int32, (SUBLANES, width), 1)
    return (n % SUBLANES) == r


def _peer_u_kernel(eid_ref, xn_ref, g_ref, tab_ref, rept_ref, rep_ref, wrep_ref, d_scr):
    mask = _diag_mask(CHUNK_ROWS)
    zeros = jnp.zeros((SUBLANES, LANES), BF16)

    def one_token(t):
        xhi, xlo = _split_bf16(xn_ref[t])
        lhs = jnp.concatenate([jnp.concatenate([xhi, zeros], axis=1),
                               jnp.concatenate([xlo, zeros], axis=1),
                               jnp.concatenate([zeros, xhi], axis=1),
                               jnp.concatenate([zeros, xlo], axis=1)], axis=0)
        for pr in range(N_CHUNKS // 2):
            s = jnp.concatenate([_gather_chunk(tab_ref, eid_ref, t, 2 * pr),
                                 _gather_chunk(tab_ref, eid_ref, t, 2 * pr + 1)], axis=1)
            z = lax.dot_general(lhs, s, (((1,), (1,)), ((), ())), preferred_element_type=F32)
            for half in range(2):
                z8 = z[16 * half:16 * half + 8] + z[16 * half + 8:16 * half + 16]
                zs = jnp.sum(jnp.where(mask, z8, 0.0), axis=0, keepdims=True)
                d_scr[pl.ds(t, 1), pl.ds((2 * pr + half) * CHUNK_ROWS, CHUNK_ROWS)] = zs

    def tok_body(i, carry):
        for u in range(TOKEN_UNROLL):
            one_token(i * TOKEN_UNROLL + u)
        return carry

    lax.fori_loop(0, PB // TOKEN_UNROLL, tok_body, 0)
    dhi, dlo = _split_bf16(d_scr[...])
    a = (jnp.dot(dhi, rept_ref[...], preferred_element_type=F32)
         + jnp.dot(dlo, rept_ref[...], preferred_element_type=F32))
    w = g_ref[...] * (0.5 * a * (1.0 + lax.erf(a * (2.0 ** -0.5))))
    whi, wlo = _split_bf16(w)
    wrep_ref[...] = (jnp.dot(whi, rep_ref[...], preferred_element_type=F32)
                     + jnp.dot(wlo, rep_ref[...], preferred_element_type=F32))


def _peer_v_kernel(eid_ref, wrep_ref, x1_ref, tab_ref, o_ref):
    mask = _diag_mask(CHUNK_ROWS)

    def one_token(t):
        wr = wrep_ref[pl.ds(t, 1), :]
        acc = x1_ref[t]
        for pr in range(N_CHUNKS // 2):
            parts = []
            for half in range(2):
                seg = wr[:, (2 * pr + half) * CHUNK_ROWS:(2 * pr + half + 1) * CHUNK_ROWS]
                parts.extend(_split_bf16(jnp.where(mask, jnp.broadcast_to(seg, mask.shape), 0.0)))
            lhs = jnp.concatenate(parts, axis=0)
            s = jnp.concatenate([_gather_chunk(tab_ref, eid_ref, t, 2 * pr),
                                 _gather_chunk(tab_ref, eid_ref, t, 2 * pr + 1)], axis=1)
            z = jnp.dot(lhs, s, preferred_element_type=F32)
            acc = acc + ((z[0:8, :LANES] + z[8:16, :LANES]) + (z[16:24, LANES:] + z[24:32, LANES:]))
        return acc

    def tok_body(i, carry):
        accs = [one_token(i * TOKEN_UNROLL + u) for u in range(TOKEN_UNROLL)]
        base = pl.multiple_of(i * TOKEN_UNROLL, TOKEN_UNROLL)
        for r in range(SUBLANES):
            o_ref[pl.ds(base, TOKEN_UNROLL), r * LANES:(r + 1) * LANES] = jnp.concatenate(
                [acc[r:r + 1, :] for acc in accs], axis=0)
        return carry

    lax.fori_loop(0, PB // TOKEN_UNROLL, tok_body, 0)


def _peer_experts(eid, g, xn3, x13, u_pack, v_pack):
    n = xn3.shape[0]
    nblk = n // PB
    eid = eid.reshape(nblk, PB, HK)
    smem_slot = pl.BlockSpec((1, PB, HK), lambda i: (i, 0, 0), memory_space=pltpu.SMEM)
    tok = pl.BlockSpec((PB, SUBLANES, LANES), lambda i: (i, 0, 0))
    flat = lambda w: pl.BlockSpec((PB, w), lambda i: (i, 0))
    table = pl.BlockSpec(u_pack.shape, lambda i: (0, 0), pipeline_mode=pl.Buffered(1))
    rep_np = (np.arange(HK)[:, None] == (np.arange(HK * SUBLANES)[None, :] // SUBLANES)).astype(np.float32)
    rep, rept = jnp.asarray(rep_np, BF16), jnp.asarray(rep_np.T, BF16)
    wrep = pl.pallas_call(
        _peer_u_kernel,
        grid=(nblk,),
        in_specs=[smem_slot, tok, flat(HK), table, _const_spec(rept.shape), _const_spec(rep.shape)],
        out_specs=flat(D_MODEL),
        out_shape=jax.ShapeDtypeStruct((n, D_MODEL), F32),
        scratch_shapes=[pltpu.VMEM((PB, D_MODEL), F32)],
        compiler_params=_cparams(("arbitrary",)),
    )(eid, xn3, g, u_pack, rept, rep)
    return pl.pallas_call(
        _peer_v_kernel,
        grid=(nblk,),
        in_specs=[smem_slot, flat(D_MODEL), tok, table],
        out_specs=flat(D_MODEL),
        out_shape=jax.ShapeDtypeStruct((n, D_MODEL), F32),
        compiler_params=_cparams(("arbitrary",)),
    )(eid, wrep, x13, v_pack)


def _encoder_layer(x, p):
    batch, seq, _ = x.shape
    n = batch * seq
    x2d = x.reshape(n, D_MODEL)
    naq, nak, nav, dq, dk, dv = _in_projection(x2d, seq, p["norm1_w"], p["w_in"], p["qk_w"], p["rope"])
    na = _neighbourhood_attention(naq, nak, nav, p["na_bias"], batch, seq)
    df = _diff_attention(dq, dk, dv, p["lam"], p["subln_w"], batch, seq)
    x1, xn, eid, g = _out_and_route(na, df, x2d, p["w_out"], p["norm2_w"], p["w_q"], p["sub_keys"])
    y3 = _peer_experts(eid, g, xn, x1, p["u_pack"], p["v_pack"])
    return y3.reshape(batch, seq, D_MODEL)


def kernel(x_prompt, x_sample, norm1_w, w_in, na_q_norm, na_k_norm, na_rpb, diff_q_norm, diff_k_norm,
           diff_lambda_q1, diff_lambda_k1, diff_lambda_q2, diff_lambda_k2, diff_subln_w, w_out, norm2_w,
           peer_w_q, peer_sub_keys, peer_u, peer_v):
    assert norm1_w.shape[0] == 1, "single-layer problem"
    reps = SECTION // HEAD_DIM
    params = {
        "norm1_w": norm1_w[0],
        "w_in": w_in[0].astype(BF16),
        "qk_w": jnp.stack([jnp.tile(w[0], reps) for w in (na_q_norm, na_k_norm, diff_q_norm, diff_k_norm)]),
        "lam": jnp.stack([diff_lambda_q1[0], diff_lambda_k1[0], diff_lambda_q2[0], diff_lambda_k2[0]]),
        "subln_w": diff_subln_w[0],
        "w_out": w_out[0].astype(BF16),
        "norm2_w": norm2_w[0],
        "w_q": peer_w_q[0].astype(BF16),
        "sub_keys": peer_sub_keys[0].reshape(2 * PEER_HEADS, N_KEYS, N_KEYS).astype(BF16),
        "na_bias": _na_bias(na_rpb[0]),
        "u_pack": _pack_table(peer_u[0]),
        "v_pack": _pack_table(peer_v[0]),
    }
    rope_by_seq = {}
    outs = []
    for x in (x_prompt, x_sample):
        seq = x.shape[1]
        if seq not in rope_by_seq:
            rope_by_seq[seq] = _rope_tables(seq)
        outs.append(_encoder_layer(x, dict(params, rope=rope_by_seq[seq])))
    return tuple(outs)
```

```python
import functools
import math

import jax
import jax.numpy as jnp
import numpy as np
from jax import lax
from jax.experimental import pallas as pl
from jax.experimental.pallas import tpu as pltpu
from jax.experimental.pallas import tpu_sc as plsc

F32 = jnp.float32
BF16 = jnp.bfloat16

D_MODEL = 1024
GRID_W = 64
HEAD_DIM = 64
NA_HEADS = 8
NA_KH = 8
NA_KW = 16
DIFF_HEADS = 4
SECTION = 512
ROT_DIM = HEAD_DIM // 4
ROPE_THETA = 500000.0
PEER_HEADS = 8
N_KEYS = 128
PEER_TOPK = 16
N_EXPERTS = N_KEYS * N_KEYS
HK = PEER_HEADS * PEER_TOPK
EPS = 1e-6
NEG_BIG = -1e30
LOG2E = math.log2(math.e)
LAMBDA_INIT = 0.8 - 0.6 * math.exp(-0.3 * 0)

LANES = 128
SUBLANES = 8
VMEM_LIMIT = 56 * 1024 * 1024

TM_IN = 512
NA_ROWS_PER_STEP = 8
NA_ROW_UNROLL = 2
TQ = 512
KV_SUBCHUNKS = 1
TK = 4096
TM_OUT = 256
TB = 128
PB = 256
SC_TOKENS = {8 * 4096: 16384}


def _cparams(sem):
    return pltpu.CompilerParams(dimension_semantics=sem, vmem_limit_bytes=VMEM_LIMIT)


def _const_spec(shape):
    nd = len(shape)
    return pl.BlockSpec(shape, lambda *_: (0,) * nd)


def _inproj_kernel(x_ref, n1_ref, w_ref, bd_ref, nw_ref, rc_ref, ra_ref, rb_ref,
                   naq_ref, nak_ref, nav_ref, dq_ref, dk_ref, dv_ref):
    x = x_ref[...]
    ms = jnp.mean(x * x, axis=-1, keepdims=True)
    h = (x * lax.rsqrt(ms + EPS) * n1_ref[...]).astype(BF16)
    bd = bd_ref[...]

    def proj(c):
        return jnp.dot(h, w_ref[:, c * SECTION:(c + 1) * SECTION], preferred_element_type=F32)

    def qknorm(y, row):
        sq = y * y
        hi = sq.astype(BF16)
        lo = (sq - hi.astype(F32)).astype(BF16)
        msq = (jnp.dot(hi, bd, preferred_element_type=F32)
               + jnp.dot(lo, bd, preferred_element_type=F32))
        return y * lax.rsqrt(msq + EPS) * nw_ref[row:row + 1, :]

    def rope(y):
        return (y * rc_ref[...]
                + pltpu.roll(y, SECTION - ROT_DIM // 2, axis=1) * ra_ref[...]
                + pltpu.roll(y, ROT_DIM // 2, axis=1) * rb_ref[...])

    scale = HEAD_DIM ** -0.5 * LOG2E
    naq_ref[...] = (qknorm(proj(0), 0) * scale).astype(BF16)
    nak_ref[...] = qknorm(proj(1), 1).astype(BF16)
    nav_ref[...] = proj(2).astype(BF16)
    dq_ref[...] = (rope(qknorm(proj(3), 2)) * scale).astype(BF16)
    dk_ref[...] = rope(qknorm(proj(4), 3)).astype(BF16)
    dv_ref[...] = proj(5).astype(BF16)


def _rope_tables(seq):
    pos = jnp.arange(seq, dtype=F32)
    inv = ROPE_THETA ** (-jnp.arange(0, ROT_DIM, 2, dtype=F32) / ROT_DIM)
    ang = pos[:, None] * inv[None, :]
    cos, sin = jnp.cos(ang), jnp.sin(ang)
    half = ROT_DIM // 2
    pad = HEAD_DIM - ROT_DIM
    c_head = jnp.concatenate([cos, cos, jnp.ones((seq, pad), F32)], -1)
    a_head = jnp.concatenate([-sin, jnp.zeros((seq, pad + half), F32)], -1)
    b_head = jnp.concatenate([jnp.zeros((seq, half), F32), sin, jnp.zeros((seq, pad), F32)], -1)
    reps = SECTION // HEAD_DIM
    return (jnp.tile(c_head, (1, reps)), jnp.tile(a_head, (1, reps)), jnp.tile(b_head, (1, reps)))


def _in_projection(x2d, seq, norm1_w, w_in_bf, qk_w, rope_tabs):
    n = x2d.shape[0]
    tm = min(TM_IN, seq)
    steps_per_seq = seq // tm
    gid = np.arange(SECTION) // HEAD_DIM
    bd = jnp.asarray((gid[:, None] == gid[None, :]).astype(np.float32) / HEAD_DIM, BF16)
    tok = pl.BlockSpec((tm, SECTION), lambda i: (i, 0))
    pos = pl.BlockSpec((tm, SECTION), lambda i: (i % steps_per_seq, 0))
    out = jax.ShapeDtypeStruct((n, SECTION), BF16)
    return pl.pallas_call(
        _inproj_kernel,
        grid=(n // tm,),
        in_specs=[pl.BlockSpec((tm, D_MODEL), lambda i: (i, 0)),
                  _const_spec((1, D_MODEL)),
                  _const_spec((D_MODEL, 6 * SECTION)),
                  _const_spec((SECTION, SECTION)),
                  _const_spec((4, SECTION)),
                  pos, pos, pos],
        out_specs=[tok] * 6,
        out_shape=[out] * 6,
        compiler_params=_cparams(("parallel",)),
    )(x2d, norm1_w.reshape(1, D_MODEL), w_in_bf, bd, qk_w, *rope_tabs)


def _na_bias(rpb):
    c = np.arange(GRID_W)[:, None]
    kc = np.arange(GRID_W)[None, :]
    cs = np.clip(c - NA_KW // 2, 0, GRID_W - NA_KW)
    valid = (kc >= cs) & (kc < cs + NA_KW)
    rel_c = kc - c + (NA_KW - 1)
    onehot = ((np.arange(2 * NA_KW - 1)[:, None, None] == rel_c[None]) & valid[None]).astype(np.float32)
    col = jnp.einsum("hrx,xck->hrck", rpb.astype(F32), onehot, precision=lax.Precision.HIGHEST)
    col = jnp.where(valid, col * LOG2E, NEG_BIG)
    per_off = [col[:, NA_KH - 1 - o:2 * NA_KH - 1 - o] for o in range(NA_KH)]
    b = jnp.stack(per_off).transpose(0, 1, 3, 2, 4)
    return b.reshape(NA_KH, NA_HEADS, GRID_W, NA_KH * GRID_W)


def _na_kernel(q_ref, k_ref, v_ref, b_ref, o_ref, *, rows):
    j = pl.program_id(1)
    win = NA_KH * GRID_W
    pair = 2 * HEAD_DIM
    lane = lax.broadcasted_iota(jnp.int32, (GRID_W, pair), 1)
    first = lane < HEAD_DIM
    ones = jnp.ones((win, pair), BF16)

    def one_row(rr):
        r = j * NA_ROWS_PER_STEP + rr
        rs = jnp.clip(r - NA_KH // 2, 0, rows - NA_KH)
        off = r - rs
        kstart = pl.multiple_of(rs * GRID_W, GRID_W)
        qstart = pl.multiple_of(rr * GRID_W, GRID_W)
        outs = []
        for hp in range(NA_HEADS // 2):
            sl = slice(hp * pair, (hp + 1) * pair)
            qp = q_ref[pl.ds(qstart, GRID_W), sl]
            kp = k_ref[pl.ds(kstart, win), sl]
            v_ext = jnp.concatenate([v_ref[pl.ds(kstart, win), sl], ones], axis=1)
            halves = []
            for hh in range(2):
                qm = jnp.where(first if hh == 0 else ~first, qp, jnp.zeros_like(qp))
                s = lax.dot_general(qm, kp, (((1,), (1,)), ((), ())), preferred_element_type=F32)
                s = s + b_ref[off, 2 * hp + hh]
                p = jnp.exp2(s - jnp.max(s, axis=-1, keepdims=True)).astype(BF16)
                z = jnp.dot(p, v_ext, preferred_element_type=F32)
                halves.append(z[:, :pair] / z[:, pair:pair + 1])
            outs.append(jnp.where(first, halves[0], halves[1]))
        o_ref[pl.ds(qstart, GRID_W), :] = jnp.concatenate(outs, axis=-1).astype(BF16)

    def row_body(i, carry):
        for u in range(NA_ROW_UNROLL):
            one_row(i * NA_ROW_UNROLL + u)
        return carry

    lax.fori_loop(0, NA_ROWS_PER_STEP // NA_ROW_UNROLL, row_body, 0)


def _neighbourhood_attention(q, k, v, bias, batch, seq):
    rows = seq // GRID_W
    assert rows >= NA_KH and rows % NA_ROWS_PER_STEP == 0
    nblk = rows // NA_ROWS_PER_STEP
    tq = NA_ROWS_PER_STEP * GRID_W
    qspec = pl.BlockSpec((tq, SECTION), lambda b, j: (b * nblk + j, 0))
    kvspec = pl.BlockSpec((seq, SECTION), lambda b, j: (b, 0))
    return pl.pallas_call(
        functools.partial(_na_kernel, rows=rows),
        grid=(batch, nblk),
        in_specs=[qspec, kvspec, kvspec, _const_spec(bias.shape)],
        out_specs=qspec,
        out_shape=jax.ShapeDtypeStruct(q.shape, BF16),
        compiler_params=_cparams(("parallel", "arbitrary")),
    )(q, k, v, bias)


def _diff_kernel(q_ref, k_ref, v_ref, lam_ref, sub_ref, o_ref, m0_scr, m1_scr, acc0_scr, acc1_scr,
                 *, seq, tq, tk):
    m_scrs, acc_scrs = (m0_scr, m1_scr), (acc0_scr, acc1_scr)
    q = q_ref[...]
    lane = lax.broadcasted_iota(jnp.int32, q.shape, 1)
    zero = jnp.zeros_like(q)
    q_maps = (jnp.where(lane < HEAD_DIM, q, zero), jnp.where(lane >= HEAD_DIM, q, zero))

    lp = lam_ref[...]
    lam = (jnp.exp(jnp.sum(lp[0:1] * lp[1:2], axis=-1, keepdims=True))
           - jnp.exp(jnp.sum(lp[2:3] * lp[3:4], axis=-1, keepdims=True)) + LAMBDA_INIT)

    width = 2 * HEAD_DIM
    ones = jnp.ones((tk, width), BF16)

    for m_scr, acc_scr in zip(m_scrs, acc_scrs):
        m_scr[...] = jnp.full(m_scr.shape, -jnp.inf, F32)
        acc_scr[...] = jnp.zeros(acc_scr.shape, F32)

    def kv_body(c, carry):
        tks = tk // KV_SUBCHUNKS
        kcs, v_exts = [], []
        for sub in range(KV_SUBCHUNKS):
            start = pl.multiple_of(c * tk + sub * tks, tks)
            kcs.append(k_ref[pl.ds(start, tks), :])
            v_exts.append(jnp.concatenate([v_ref[pl.ds(start, tks), :], ones[:tks]], axis=1))
        scores = [[lax.dot_general(q_maps[mp], kcs[sub], (((1,), (1,)), ((), ())),
                                   preferred_element_type=F32) for mp in range(2)]
                  for sub in range(KV_SUBCHUNKS)]
        m = [m_scrs[mp][...] for mp in range(2)]
        probs, alphas = [], []
        for sub in range(KV_SUBCHUNKS):
            probs.append([])
            alphas.append([])
            for mp in range(2):
                mn = jnp.maximum(m[mp], jnp.max(scores[sub][mp], axis=-1, keepdims=True))
                probs[sub].append(jnp.exp2(scores[sub][mp] - mn[:, 0:1]).astype(BF16))
                alpha = jnp.exp2(m[mp] - mn)
                alphas[sub].append(jnp.concatenate([alpha, alpha], axis=1))
                m[mp] = mn
        for mp in range(2):
            m_scrs[mp][...] = m[mp]
            acc = acc_scrs[mp][...]
            for sub in range(KV_SUBCHUNKS):
                acc = alphas[sub][mp] * acc + jnp.dot(probs[sub][mp], v_exts[sub],
                                                      preferred_element_type=F32)
            acc_scrs[mp][...] = acc
        return carry

    lax.fori_loop(0, seq // tk, kv_body, 0)
    a0, a1 = acc0_scr[...], acc1_scr[...]
    o = a0[:, :width] / a0[:, width:width + 1] - lam * (a1[:, :width] / a1[:, width:width + 1])
    ms = jnp.mean(o * o, axis=-1, keepdims=True)
    y = o * lax.rsqrt(ms + EPS) * sub_ref[...]
    o_ref[...] = (y * (1.0 - LAMBDA_INIT)).astype(BF16)


def _diff_attention(q, k, v, lam_params, subln_w, batch, seq):
    tq, tk = min(TQ, seq), min(TK, seq)
    nq = seq // tq
    width = 2 * HEAD_DIM
    qspec = pl.BlockSpec((tq, width), lambda b, h, i: (b * nq + i, h))
    kvspec = pl.BlockSpec((seq, width), lambda b, h, i: (b, h))
    return pl.pallas_call(
        functools.partial(_diff_kernel, seq=seq, tq=tq, tk=tk),
        grid=(batch, DIFF_HEADS, nq),
        in_specs=[qspec, kvspec, kvspec, _const_spec((4, HEAD_DIM)), _const_spec((1, width))],
        out_specs=qspec,
        out_shape=jax.ShapeDtypeStruct(q.shape, BF16),
        scratch_shapes=[pltpu.VMEM((tq, width), F32), pltpu.VMEM((tq, width), F32),
                        pltpu.VMEM((tq, 2 * width), F32), pltpu.VMEM((tq, 2 * width), F32)],
        compiler_params=_cparams(("parallel", "parallel", "arbitrary")),
    )(q, k, v, lam_params, subln_w.reshape(1, width))


def _top16(sc, payload=None):
    iota = lax.broadcasted_iota(jnp.int32, sc.shape, 0).astype(F32)
    big = float(sc.shape[0])
    vals, picks = [], []
    for _ in range(PEER_TOPK):
        m = jnp.max(sc, axis=0, keepdims=True)
        ix = jnp.min(jnp.where(sc == m, iota, big), axis=0, keepdims=True)
        sel = iota == ix
        if payload is None:
            picks.append(ix)
        else:
            picks.append(jnp.sum(jnp.where(sel, payload, 0), axis=0, keepdims=True))
        sc = jnp.where(sel, -jnp.inf, sc)
        vals.append(m)
    picks = jnp.concatenate(picks, axis=0)
    return jnp.concatenate(vals, axis=0), picks.astype(jnp.int32)


def _pair_candidates(r0, r1):
    half = SUBLANES // 2
    groups = [r0[0:1] + r1[0:8], r0[0:1] + r1[8:16]]
    groups += [r0[a:a + 1] + r1[0:8] for a in (1, 2, 3)]
    groups += [jnp.concatenate([r0[a:a + 1] + r1[0:half], r0[a + 1:a + 2] + r1[0:half]], axis=0)
               for a in (4, 6)]
    groups.append(r0[8:16] + r1[0:1])
    return jnp.concatenate(groups, axis=0)


def _route_kernel(na_ref, df_ref, x_ref, wo_ref, n2_ref, wq_ref, sk_ref,
                  x1_ref, xn_ref, eid_ref, g_ref, q_scr, eid_scr, g_scr):
    tm = x_ref.shape[0]
    x1 = (x_ref[...]
          + jnp.dot(na_ref[...], wo_ref[0:SECTION, :], preferred_element_type=F32)
          + jnp.dot(df_ref[...], wo_ref[SECTION:2 * SECTION, :], preferred_element_type=F32))
    ms = jnp.mean(x1 * x1, axis=-1, keepdims=True)
    xn = x1 * lax.rsqrt(ms + EPS) * n2_ref[...]
    for r in range(SUBLANES):
        x1_ref[:, r, :] = x1[:, r * LANES:(r + 1) * LANES]
        xn_ref[:, r, :] = xn[:, r * LANES:(r + 1) * LANES]
    xb = xn.astype(BF16)
    for c in range(2 * PEER_HEADS):
        q_scr[c] = jnp.dot(xb, wq_ref[:, c * LANES:(c + 1) * LANES],
                           preferred_element_type=F32).astype(BF16)

    nblk = tm // TB

    def route_block(h, blk):
        tok0 = blk * TB
        tops = []
        for p in range(2):
            sc = lax.dot_general(sk_ref[2 * h + p], q_scr[2 * h + p, pl.ds(tok0, TB), :],
                                 (((1,), (1,)), ((), ())), preferred_element_type=F32)
            tops.append(_top16(sc))
        (s0, i0), (s1, i1) = tops
        cand_s = _pair_candidates(s0, s1)
        cand_i = _pair_candidates(i0 * N_KEYS, i1)
        top_s, eid = _top16(cand_s, cand_i)
        e = jnp.exp(top_s - top_s[0:1])
        row = pl.multiple_of(h * PEER_TOPK, PEER_TOPK)
        eid_scr[blk, pl.ds(row, PEER_TOPK), :] = eid * ROW_SUB
        g_scr[blk, pl.ds(row, PEER_TOPK), :] = e / jnp.sum(e, axis=0, keepdims=True)

    def head_body(h, carry):
        for blk in range(nblk):
            route_block(h, blk)
        return carry

    lax.fori_loop(0, PEER_HEADS, head_body, 0)
    for blk in range(nblk):
        eid_ref[blk] = eid_scr[blk].T
        g_ref[blk * TB:(blk + 1) * TB, :] = g_scr[blk].T


def _out_and_route(na, df, x2d, w_out_bf, norm2_w, w_q_bf, sk_bf):
    n = x2d.shape[0]
    tm = TM_OUT
    tok = lambda w: pl.BlockSpec((tm, w), lambda i: (i, 0))
    tiles = pl.BlockSpec((tm, SUBLANES, LANES), lambda i: (i, 0, 0))
    return pl.pallas_call(
        _route_kernel,
        grid=(n // tm,),
        in_specs=[tok(SECTION), tok(SECTION), tok(D_MODEL),
                  _const_spec((D_MODEL, D_MODEL)), _const_spec((1, D_MODEL)),
                  _const_spec(w_q_bf.shape), _const_spec(sk_bf.shape)],
        out_specs=[tiles, tiles,
                   pl.BlockSpec((tm // TB, TB, HK), lambda i: (i, 0, 0)), tok(HK)],
        out_shape=[jax.ShapeDtypeStruct((n, SUBLANES, LANES), F32),
                   jax.ShapeDtypeStruct((n, SUBLANES, LANES), F32),
                   jax.ShapeDtypeStruct((n // TB, TB, HK), jnp.int32),
                   jax.ShapeDtypeStruct((n, HK), F32)],
        scratch_shapes=[pltpu.VMEM((2 * PEER_HEADS, tm, LANES), BF16),
                        pltpu.VMEM((tm // TB, HK, TB), jnp.int32), pltpu.VMEM((tm // TB, HK, TB), F32)],
        compiler_params=_cparams(("parallel",)),
    )(na, df, x2d, w_out_bf, norm2_w.reshape(1, D_MODEL), w_q_bf, sk_bf)


ROW_SUB = SUBLANES // 2
CHUNK_SLOTS = 32
N_CHUNKS = HK // CHUNK_SLOTS
CHUNK_ROWS = CHUNK_SLOTS * SUBLANES
TOKEN_UNROLL = 8
ROW_WORDS = ROW_SUB * LANES
SC_CORES, SC_SUBCORES, SC_LANES = 2, 16, 16
SC_WORKERS = SC_CORES * SC_SUBCORES


def _pack_table(tab):
    bits = lax.bitcast_convert_type(tab.astype(BF16), jnp.uint16).astype(jnp.uint32)
    bits = bits.reshape(tab.shape[0], ROW_SUB, 2, LANES)
    return (bits[:, :, 0, :] | (bits[:, :, 1, :] << 16)).reshape(tab.shape[0] * ROW_SUB, LANES)


def _split_bf16(x):
    hi = x.astype(BF16)
    return hi, (x - hi.astype(F32)).astype(BF16)


def _gather_chunk(tab_ref, eid_ref, t, chunk):
    ids = eid_ref.at[0, t]
    rows = [tab_ref[pl.ds(pl.multiple_of(ids[chunk * CHUNK_SLOTS + i], ROW_SUB), ROW_SUB), :]
            for i in range(CHUNK_SLOTS)]
    return pltpu.bitcast(jnp.concatenate(rows, axis=0), BF16)


def _diag_mask(width):
    r = lax.broadcasted_iota(jnp.int32, (SUBLANES, width), 0)
    n = lax.broadcasted_iota(jnp.int32, (SUBLANES, width), 1)
    return (n % SUBLANES) == r


def _peer_u_kernel(eid_ref, xn_ref, g_ref, tab_ref, rept_ref, rep_ref, rep16_ref, wrep_ref, w16_ref, d_scr):
    mask = _diag_mask(CHUNK_ROWS)
    zeros = jnp.zeros((SUBLANES, LANES), BF16)

    def one_token(t):
        xhi, xlo = _split_bf16(xn_ref[t])
        lhs = jnp.concatenate([jnp.concatenate([xhi, zeros], axis=1),
                               jnp.concatenate([xlo, zeros], axis=1),
                               jnp.concatenate([zeros, xhi], axis=1),
                               jnp.concatenate([zeros, xlo], axis=1)], axis=0)
        for pr in range(N_CHUNKS // 2):
            s = jnp.concatenate([_gather_chunk(tab_ref, eid_ref, t, 2 * pr),
                                 _gather_chunk(tab_ref, eid_ref, t, 2 * pr + 1)], axis=1)
            z = lax.dot_general(lhs, s, (((1,), (1,)), ((), ())), preferred_element_type=F32)
            for half in range(2):
                z8 = z[16 * half:16 * half + 8] + z[16 * half + 8:16 * half + 16]
                zs = jnp.sum(jnp.where(mask, z8, 0.0), axis=0, keepdims=True)
                d_scr[pl.ds(t, 1), pl.ds((2 * pr + half) * CHUNK_ROWS, CHUNK_ROWS)] = zs

    def tok_body(i, carry):
        for u in range(TOKEN_UNROLL):
            one_token(i * TOKEN_UNROLL + u)
        return carry

    lax.fori_loop(0, PB // TOKEN_UNROLL, tok_body, 0)
    dhi, dlo = _split_bf16(d_scr[...])
    a = (jnp.dot(dhi, rept_ref[...], preferred_element_type=F32)
         + jnp.dot(dlo, rept_ref[...], preferred_element_type=F32))
    w = g_ref[...] * (0.5 * a * (1.0 + lax.erf(a * (2.0 ** -0.5))))
    whi, wlo = _split_bf16(w)
    wrep_ref[...] = (jnp.dot(whi, rep_ref[...], preferred_element_type=F32)
                     + jnp.dot(wlo, rep_ref[...], preferred_element_type=F32))
    w16_ref[...] = (jnp.dot(whi, rep16_ref[...], preferred_element_type=F32)
                    + jnp.dot(wlo, rep16_ref[...], preferred_element_type=F32))


def _peer_v_kernel(eid_ref, wrep_ref, x1_ref, tab_ref, o_ref):
    mask = _diag_mask(CHUNK_ROWS)

    def one_token(t):
        wr = wrep_ref[pl.ds(t, 1), :]
        acc = x1_ref[t]
        for pr in range(N_CHUNKS // 2):
            parts = []
            for half in range(2):
                seg = wr[:, (2 * pr + half) * CHUNK_ROWS:(2 * pr + half + 1) * CHUNK_ROWS]
                parts.extend(_split_bf16(jnp.where(mask, jnp.broadcast_to(seg, mask.shape), 0.0)))
            lhs = jnp.concatenate(parts, axis=0)
            s = jnp.concatenate([_gather_chunk(tab_ref, eid_ref, t, 2 * pr),
                                 _gather_chunk(tab_ref, eid_ref, t, 2 * pr + 1)], axis=1)
            z = jnp.dot(lhs, s, preferred_element_type=F32)
            acc = acc + ((z[0:8, :LANES] + z[8:16, :LANES]) + (z[16:24, LANES:] + z[24:32, LANES:]))
        return acc

    def tok_body(i, carry):
        accs = [one_token(i * TOKEN_UNROLL + u) for u in range(TOKEN_UNROLL)]
        base = pl.multiple_of(i * TOKEN_UNROLL, TOKEN_UNROLL)
        for r in range(SUBLANES):
            o_ref[pl.ds(base, TOKEN_UNROLL), r * LANES:(r + 1) * LANES] = jnp.concatenate(
                [acc[r:r + 1, :] for acc in accs], axis=0)
        return carry

    lax.fori_loop(0, PB // TOKEN_UNROLL, tok_body, 0)


def _sc_v_phase(n_tok, tok0):
    per_w = n_tok // SC_WORKERS
    assert per_w * SC_WORKERS == n_tok and per_w % 2 == 0
    half = HK // 2
    mesh = plsc.VectorSubcoreMesh(core_axis_name="c", subcore_axis_name="s")

    def small():
        return [pltpu.VMEM((2, half), jnp.int32), pltpu.VMEM((HK * SC_LANES,), F32),
                pltpu.VMEM((SUBLANES, LANES), F32)]

    @functools.partial(
        pl.kernel, mesh=mesh,
        out_type=jax.ShapeDtypeStruct((n_tok, SUBLANES, LANES), F32),
        scratch_types=small() + small() + [
            pltpu.VMEM((half, ROW_WORDS), jnp.uint32), pltpu.VMEM((half, ROW_WORDS), jnp.uint32),
            pltpu.SemaphoreType.DMA, pltpu.SemaphoreType.DMA,
            pltpu.SemaphoreType.DMA, pltpu.SemaphoreType.DMA,
            pltpu.SemaphoreType.DMA, pltpu.SemaphoreType.DMA,
        ],
        compiler_params=pltpu.CompilerParams(needs_layout_passes=False),
    )
    def sc_kernel(tab_hbm, eid_hbm, w16_hbm, x1_hbm, out_hbm,
                  idx_a, w_a, x_a, idx_b, w_b, x_b, g0, g1, s_a, s_b, sg0, sg1, so_a, so_b):
        wid = lax.axis_index("s") * SC_CORES + lax.axis_index("c")
        first = wid * per_w
        par = ((idx_a, w_a, x_a, s_a, so_a), (idx_b, w_b, x_b, s_b, so_b))
        gbuf, gsem = (g0, g1), (sg0, sg1)

        def input_copies(t, p):
            idx_v, w_v, x_v, sem, _ = par[p]
            return (pltpu.make_async_copy(eid_hbm.at[tok0 + t], idx_v, sem),
                    pltpu.make_async_copy(w16_hbm.at[tok0 + t], w_v, sem),
                    pltpu.make_async_copy(x1_hbm.at[tok0 + t], x_v, sem))

        def gather(p, h):
            return pltpu.make_async_copy(tab_hbm.at[par[p][0].at[h]], gbuf[h], gsem[h])

        def out_copy(t, p):
            return pltpu.make_async_copy(par[p][2], out_hbm.at[t], par[p][4])

        def accumulate_half(p, h):
            _, w_v, x_v, _, _ = par[p]
            g = gbuf[h]
            groups = LANES // SC_LANES
            for c in range(ROW_SUB):
                lo0 = tuple(x_v[2 * c, pl.ds(SC_LANES * kk, SC_LANES)] for kk in range(groups))
                hi0 = tuple(x_v[2 * c + 1, pl.ds(SC_LANES * kk, SC_LANES)] for kk in range(groups))

                def row_body(j, acc):
                    lo, hi = acc
                    wj = w_v[pl.ds(pl.multiple_of((h * half + j) * SC_LANES, SC_LANES), SC_LANES)]
                    nlo, nhi = [], []
                    for kk in range(groups):
                        words = g[j, pl.ds(LANES * c + SC_LANES * kk, SC_LANES)]
                        lov = plsc.bitcast(lax.shift_left(words, jnp.uint32(16)), F32)
                        hiv = plsc.bitcast(words & jnp.uint32(0xFFFF0000), F32)
                        nlo.append(lo[kk] + wj * lov)
                        nhi.append(hi[kk] + wj * hiv)
                    return tuple(nlo), tuple(nhi)

                lo, hi = lax.fori_loop(0, half, row_body, (lo0, hi0))
                for kk in range(groups):
                    x_v[2 * c, pl.ds(SC_LANES * kk, SC_LANES)] = lo[kk]
                    x_v[2 * c + 1, pl.ds(SC_LANES * kk, SC_LANES)] = hi[kk]

        def token(t, p):
            q = 1 - p
            has_next = t + 1 < first + per_w
            gather(p, 0).wait()
            gather(p, 1).start()

            @pl.when(has_next)
            def _():
                @pl.when(t >= first + 1)
                def _():
                    out_copy(t - 1, q).wait()
                for cp in input_copies(t + 1, q):
                    cp.start()

            accumulate_half(p, 0)
            gather(p, 1).wait()

            @pl.when(has_next)
            def _():
                for cp in input_copies(t + 1, q):
                    cp.wait()
                gather(q, 0).start()

            accumulate_half(p, 1)
            out_copy(t, p).start()

        for cp in input_copies(first, 0):
            cp.start()
        for cp in input_copies(first, 0):
            cp.wait()
        gather(0, 0).start()

        def pair_body(i, carry):
            token(first + 2 * i, 0)
            token(first + 2 * i + 1, 1)
            return carry

        lax.fori_loop(0, per_w // 2, pair_body, 0)
        out_copy(first + per_w - 2, 0).wait()
        out_copy(first + per_w - 1, 1).wait()

    return sc_kernel


def _peer_experts(eid, g, xn3, x13, u_pack, v_pack, n_sc):
    n = xn3.shape[0]
    n_tc = n - n_sc
    assert n_tc % PB == 0
    nblk = n // PB
    eid = eid.reshape(nblk, PB, HK)
    smem_slot = pl.BlockSpec((1, PB, HK), lambda i: (i, 0, 0), memory_space=pltpu.SMEM)
    tok = pl.BlockSpec((PB, SUBLANES, LANES), lambda i: (i, 0, 0))
    flat = lambda w: pl.BlockSpec((PB, w), lambda i: (i, 0))
    table = pl.BlockSpec(u_pack.shape, lambda i: (0, 0), pipeline_mode=pl.Buffered(1))

    def repeat_matrix(times):
        m = (np.arange(HK)[:, None] == (np.arange(HK * times)[None, :] // times)).astype(np.float32)
        return jnp.asarray(m, BF16)

    rep, rep16 = repeat_matrix(SUBLANES), repeat_matrix(SC_LANES)
    wrep, w16 = pl.pallas_call(
        _peer_u_kernel,
        grid=(nblk,),
        in_specs=[smem_slot, tok, flat(HK), table, _const_spec(rep.T.shape), _const_spec(rep.shape),
                  _const_spec(rep16.shape)],
        out_specs=[flat(D_MODEL), flat(HK * SC_LANES)],
        out_shape=[jax.ShapeDtypeStruct((n, D_MODEL), F32), jax.ShapeDtypeStruct((n, HK * SC_LANES), F32)],
        scratch_shapes=[pltpu.VMEM((PB, D_MODEL), F32)],
        compiler_params=_cparams(("arbitrary",)),
    )(eid, xn3, g, u_pack, rep.T, rep, rep16)
    parts = []
    if n_tc:
        parts.append(pl.pallas_call(
            _peer_v_kernel,
            grid=(n_tc // PB,),
            in_specs=[smem_slot, flat(D_MODEL), tok, table],
            out_specs=flat(D_MODEL),
            out_shape=jax.ShapeDtypeStruct((n_tc, D_MODEL), F32),
            compiler_params=_cparams(("arbitrary",)),
        )(eid, wrep, x13, v_pack))
    if n_sc:
        expert = eid.reshape(n, 2, HK // 2) // ROW_SUB
        rows = v_pack.reshape(N_EXPERTS, ROW_WORDS)
        parts.append(_sc_v_phase(n_sc, n_tc)(rows, expert, w16, x13).reshape(n_sc, D_MODEL))
    return parts[0] if len(parts) == 1 else jnp.concatenate(parts, axis=0)


def _encoder_layer(x, p):
    batch, seq, _ = x.shape
    n = batch * seq
    x2d = x.reshape(n, D_MODEL)
    naq, nak, nav, dq, dk, dv = _in_projection(x2d, seq, p["norm1_w"], p["w_in"], p["qk_w"], p["rope"])
    na = _neighbourhood_attention(naq, nak, nav, p["na_bias"], batch, seq)
    df = _diff_attention(dq, dk, dv, p["lam"], p["subln_w"], batch, seq)
    x1, xn, eid, g = _out_and_route(na, df, x2d, p["w_out"], p["norm2_w"], p["w_q"], p["sub_keys"])
    y = _peer_experts(eid, g, xn, x1, p["u_pack"], p["v_pack"], SC_TOKENS.get(n, 0))
    return y.reshape(batch, seq, D_MODEL)


def kernel(x_prompt, x_sample, norm1_w, w_in, na_q_norm, na_k_norm, na_rpb, diff_q_norm, diff_k_norm,
           diff_lambda_q1, diff_lambda_k1, diff_lambda_q2, diff_lambda_k2, diff_subln_w, w_out, norm2_w,
           peer_w_q, peer_sub_keys, peer_u, peer_v):
    assert norm1_w.shape[0] == 1, "single-layer problem"
    reps = SECTION // HEAD_DIM
    params = {
        "norm1_w": norm1_w[0],
        "w_in": w_in[0].astype(BF16),
        "qk_w": jnp.stack([jnp.tile(w[0], reps) for w in (na_q_norm, na_k_norm, diff_q_norm, diff_k_norm)]),
        "lam": jnp.stack([diff_lambda_q1[0], diff_lambda_k1[0], diff_lambda_q2[0], diff_lambda_k2[0]]),
        "subln_w": diff_subln_w[0],
        "w_out": w_out[0].astype(BF16),
        "norm2_w": norm2_w[0],
        "w_q": peer_w_q[0].astype(BF16),
        "sub_keys": peer_sub_keys[0].reshape(2 * PEER_HEADS, N_KEYS, N_KEYS).astype(BF16),
        "na_bias": _na_bias(na_rpb[0]),
        "u_pack": _pack_table(peer_u[0]),
        "v_pack": _pack_table(peer_v[0]),
    }
    rope_by_seq = {}
    outs = []
    for x in (x_prompt, x_sample):
        seq = x.shape[1]
        if seq not in rope_by_seq:
            rope_by_seq[seq] = _rope_tables(seq)
        outs.append(_encoder_layer(x, dict(params, rope=rope_by_seq[seq])))
    return tuple(outs)
```

```python
import functools
import math

import jax
import jax.numpy as jnp
import numpy as np
from jax import lax
from jax.experimental import pallas as pl
from jax.experimental.pallas import tpu as pltpu
from jax.experimental.pallas import tpu_sc as plsc

F32 = jnp.float32
BF16 = jnp.bfloat16

D_MODEL = 1024
GRID_W = 64
HEAD_DIM = 64
NA_HEADS = 8
NA_KH = 8
NA_KW = 16
DIFF_HEADS = 4
SECTION = 512
ROT_DIM = HEAD_DIM // 4
ROPE_THETA = 500000.0
PEER_HEADS = 8
N_KEYS = 128
PEER_TOPK = 16
N_EXPERTS = N_KEYS * N_KEYS
HK = PEER_HEADS * PEER_TOPK
EPS = 1e-6
NEG_BIG = -1e30
LOG2E = math.log2(math.e)
LAMBDA_INIT = 0.8 - 0.6 * math.exp(-0.3 * 0)

LANES = 128
SUBLANES = 8
VMEM_LIMIT = 56 * 1024 * 1024

TM_IN = 512
NA_ROWS_PER_STEP = 8
NA_ROW_UNROLL = 2
TQ = 512
KV_SUBCHUNKS = 1
TK = 4096
TM_OUT = 256
TB = 128
PB = 256
SC_TOKENS = {8 * 4096: 8 * 4096, 4 * 4096: 5632}


def _cparams(sem):
    return pltpu.CompilerParams(dimension_semantics=sem, vmem_limit_bytes=VMEM_LIMIT)


def _const_spec(shape):
    nd = len(shape)
    return pl.BlockSpec(shape, lambda *_: (0,) * nd)


def _inproj_kernel(x_ref, n1_ref, w_ref, bd_ref, nw_ref, rc_ref, ra_ref, rb_ref,
                   naq_ref, nak_ref, nav_ref, dq_ref, dk_ref, dv_ref):
    x = x_ref[...]
    ms = jnp.mean(x * x, axis=-1, keepdims=True)
    h = (x * lax.rsqrt(ms + EPS) * n1_ref[...]).astype(BF16)
    bd = bd_ref[...]

    def proj(c):
        return jnp.dot(h, w_ref[:, c * SECTION:(c + 1) * SECTION], preferred_element_type=F32)

    def qknorm(y, row):
        sq = y * y
        hi = sq.astype(BF16)
        lo = (sq - hi.astype(F32)).astype(BF16)
        msq = (jnp.dot(hi, bd, preferred_element_type=F32)
               + jnp.dot(lo, bd, preferred_element_type=F32))
        return y * lax.rsqrt(msq + EPS) * nw_ref[row:row + 1, :]

    def rope(y):
        return (y * rc_ref[...]
                + pltpu.roll(y, SECTION - ROT_DIM // 2, axis=1) * ra_ref[...]
                + pltpu.roll(y, ROT_DIM // 2, axis=1) * rb_ref[...])

    scale = HEAD_DIM ** -0.5 * LOG2E
    naq_ref[...] = (qknorm(proj(0), 0) * scale).astype(BF16)
    nak_ref[...] = qknorm(proj(1), 1).astype(BF16)
    nav_ref[...] = proj(2).astype(BF16)
    dq_ref[...] = (rope(qknorm(proj(3), 2)) * scale).astype(BF16)
    dk_ref[...] = rope(qknorm(proj(4), 3)).astype(BF16)
    dv_ref[...] = proj(5).astype(BF16)


def _rope_tables(seq):
    pos = jnp.arange(seq, dtype=F32)
    inv = ROPE_THETA ** (-jnp.arange(0, ROT_DIM, 2, dtype=F32) / ROT_DIM)
    ang = pos[:, None] * inv[None, :]
    cos, sin = jnp.cos(ang), jnp.sin(ang)
    half = ROT_DIM // 2
    pad = HEAD_DIM - ROT_DIM
    c_head = jnp.concatenate([cos, cos, jnp.ones((seq, pad), F32)], -1)
    a_head = jnp.concatenate([-sin, jnp.zeros((seq, pad + half), F32)], -1)
    b_head = jnp.concatenate([jnp.zeros((seq, half), F32), sin, jnp.zeros((seq, pad), F32)], -1)
    reps = SECTION // HEAD_DIM
    return (jnp.tile(c_head, (1, reps)), jnp.tile(a_head, (1, reps)), jnp.tile(b_head, (1, reps)))


def _in_projection(x2d, seq, norm1_w, w_in_bf, qk_w, rope_tabs):
    n = x2d.shape[0]
    tm = min(TM_IN, seq)
    steps_per_seq = seq // tm
    gid = np.arange(SECTION) // HEAD_DIM
    bd = jnp.asarray((gid[:, None] == gid[None, :]).astype(np.float32) / HEAD_DIM, BF16)
    tok = pl.BlockSpec((tm, SECTION), lambda i: (i, 0))
    pos = pl.BlockSpec((tm, SECTION), lambda i: (i % steps_per_seq, 0))
    out = jax.ShapeDtypeStruct((n, SECTION), BF16)
    return pl.pallas_call(
        _inproj_kernel,
        grid=(n // tm,),
        in_specs=[pl.BlockSpec((tm, D_MODEL), lambda i: (i, 0)),
                  _const_spec((1, D_MODEL)),
                  _const_spec((D_MODEL, 6 * SECTION)),
                  _const_spec((SECTION, SECTION)),
                  _const_spec((4, SECTION)),
                  pos, pos, pos],
        out_specs=[tok] * 6,
        out_shape=[out] * 6,
        compiler_params=_cparams(("parallel",)),
    )(x2d, norm1_w.reshape(1, D_MODEL), w_in_bf, bd, qk_w, *rope_tabs)


def _na_bias(rpb):
    c = np.arange(GRID_W)[:, None]
    kc = np.arange(GRID_W)[None, :]
    cs = np.clip(c - NA_KW // 2, 0, GRID_W - NA_KW)
    valid = (kc >= cs) & (kc < cs + NA_KW)
    rel_c = kc - c + (NA_KW - 1)
    onehot = ((np.arange(2 * NA_KW - 1)[:, None, None] == rel_c[None]) & valid[None]).astype(np.float32)
    col = jnp.einsum("hrx,xck->hrck", rpb.astype(F32), onehot, precision=lax.Precision.HIGHEST)
    col = jnp.where(valid, col * LOG2E, NEG_BIG)
    per_off = [col[:, NA_KH - 1 - o:2 * NA_KH - 1 - o] for o in range(NA_KH)]
    b = jnp.stack(per_off).transpose(0, 1, 3, 2, 4)
    return b.reshape(NA_KH, NA_HEADS, GRID_W, NA_KH * GRID_W)


def _na_kernel(q_ref, k_ref, v_ref, b_ref, o_ref, *, rows):
    j = pl.program_id(1)
    win = NA_KH * GRID_W
    pair = 2 * HEAD_DIM
    lane = lax.broadcasted_iota(jnp.int32, (GRID_W, pair), 1)
    first = lane < HEAD_DIM
    ones = jnp.ones((win, pair), BF16)

    def one_row(rr):
        r = j * NA_ROWS_PER_STEP + rr
        rs = jnp.clip(r - NA_KH // 2, 0, rows - NA_KH)
        off = r - rs
        kstart = pl.multiple_of(rs * GRID_W, GRID_W)
        qstart = pl.multiple_of(rr * GRID_W, GRID_W)
        outs = []
        for hp in range(NA_HEADS // 2):
            sl = slice(hp * pair, (hp + 1) * pair)
            qp = q_ref[pl.ds(qstart, GRID_W), sl]
            kp = k_ref[pl.ds(kstart, win), sl]
            v_ext = jnp.concatenate([v_ref[pl.ds(kstart, win), sl], ones], axis=1)
            halves = []
            for hh in range(2):
                qm = jnp.where(first if hh == 0 else ~first, qp, jnp.zeros_like(qp))
                s = lax.dot_general(qm, kp, (((1,), (1,)), ((), ())), preferred_element_type=F32)
                s = s + b_ref[off, 2 * hp + hh]
                p = jnp.exp2(s - jnp.max(s, axis=-1, keepdims=True)).astype(BF16)
                z = jnp.dot(p, v_ext, preferred_element_type=F32)
                halves.append(z[:, :pair] / z[:, pair:pair + 1])
            outs.append(jnp.where(first, halves[0], halves[1]))
        o_ref[pl.ds(qstart, GRID_W), :] = jnp.concatenate(outs, axis=-1).astype(BF16)

    def row_body(i, carry):
        for u in range(NA_ROW_UNROLL):
            one_row(i * NA_ROW_UNROLL + u)
        return carry

    lax.fori_loop(0, NA_ROWS_PER_STEP // NA_ROW_UNROLL, row_body, 0)


def _neighbourhood_attention(q, k, v, bias, batch, seq):
    rows = seq // GRID_W
    assert rows >= NA_KH and rows % NA_ROWS_PER_STEP == 0
    nblk = rows // NA_ROWS_PER_STEP
    tq = NA_ROWS_PER_STEP * GRID_W
    qspec = pl.BlockSpec((tq, SECTION), lambda b, j: (b * nblk + j, 0))
    kvspec = pl.BlockSpec((seq, SECTION), lambda b, j: (b, 0))
    return pl.pallas_call(
        functools.partial(_na_kernel, rows=rows),
        grid=(batch, nblk),
        in_specs=[qspec, kvspec, kvspec, _const_spec(bias.shape)],
        out_specs=qspec,
        out_shape=jax.ShapeDtypeStruct(q.shape, BF16),
        compiler_params=_cparams(("parallel", "arbitrary")),
    )(q, k, v, bias)


def _diff_kernel(q_ref, k_ref, v_ref, lam_ref, sub_ref, o_ref, m0_scr, m1_scr, acc0_scr, acc1_scr,
                 *, seq, tq, tk):
    m_scrs, acc_scrs = (m0_scr, m1_scr), (acc0_scr, acc1_scr)
    q = q_ref[...]
    lane = lax.broadcasted_iota(jnp.int32, q.shape, 1)
    zero = jnp.zeros_like(q)
    q_maps = (jnp.where(lane < HEAD_DIM, q, zero), jnp.where(lane >= HEAD_DIM, q, zero))

    lp = lam_ref[...]
    lam = (jnp.exp(jnp.sum(lp[0:1] * lp[1:2], axis=-1, keepdims=True))
           - jnp.exp(jnp.sum(lp[2:3] * lp[3:4], axis=-1, keepdims=True)) + LAMBDA_INIT)

    width = 2 * HEAD_DIM
    ones = jnp.ones((tk, width), BF16)

    for m_scr, acc_scr in zip(m_scrs, acc_scrs):
        m_scr[...] = jnp.full(m_scr.shape, -jnp.inf, F32)
        acc_scr[...] = jnp.zeros(acc_scr.shape, F32)

    def kv_body(c, carry):
        tks = tk // KV_SUBCHUNKS
        kcs, v_exts = [], []
        for sub in range(KV_SUBCHUNKS):
            start = pl.multiple_of(c * tk + sub * tks, tks)
            kcs.append(k_ref[pl.ds(start, tks), :])
            v_exts.append(jnp.concatenate([v_ref[pl.ds(start, tks), :], ones[:tks]], axis=1))
        scores = [[lax.dot_general(q_maps[mp], kcs[sub], (((1,), (1,)), ((), ())),
                                   preferred_element_type=F32) for mp in range(2)]
                  for sub in range(KV_SUBCHUNKS)]
        m = [m_scrs[mp][...] for mp in range(2)]
        probs, alphas = [], []
        for sub in range(KV_SUBCHUNKS):
            probs.append([])
            alphas.append([])
            for mp in range(2):
                mn = jnp.maximum(m[mp], jnp.max(scores[sub][mp], axis=-1, keepdims=True))
                probs[sub].append(jnp.exp2(scores[sub][mp] - mn[:, 0:1]).astype(BF16))
                alpha = jnp.exp2(m[mp] - mn)
                alphas[sub].append(jnp.concatenate([alpha, alpha], axis=1))
                m[mp] = mn
        for mp in range(2):
            m_scrs[mp][...] = m[mp]
            acc = acc_scrs[mp][...]
            for sub in range(KV_SUBCHUNKS):
                acc = alphas[sub][mp] * acc + jnp.dot(probs[sub][mp], v_exts[sub],
                                                      preferred_element_type=F32)
            acc_scrs[mp][...] = acc
        return carry

    lax.fori_loop(0, seq // tk, kv_body, 0)
    a0, a1 = acc0_scr[...], acc1_scr[...]
    o = a0[:, :width] / a0[:, width:width + 1] - lam * (a1[:, :width] / a1[:, width:width + 1])
    ms = jnp.mean(o * o, axis=-1, keepdims=True)
    y = o * lax.rsqrt(ms + EPS) * sub_ref[...]
    o_ref[...] = (y * (1.0 - LAMBDA_INIT)).astype(BF16)


def _diff_attention(q, k, v, lam_params, subln_w, batch, seq):
    tq, tk = min(TQ, seq), min(TK, seq)
    nq = seq // tq
    width = 2 * HEAD_DIM
    qspec = pl.BlockSpec((tq, width), lambda b, h, i: (b * nq + i, h))
    kvspec = pl.BlockSpec((seq, width), lambda b, h, i: (b, h))
    return pl.pallas_call(
        functools.partial(_diff_kernel, seq=seq, tq=tq, tk=tk),
        grid=(batch, DIFF_HEADS, nq),
        in_specs=[qspec, kvspec, kvspec, _const_spec((4, HEAD_DIM)), _const_spec((1, width))],
        out_specs=qspec,
        out_shape=jax.ShapeDtypeStruct(q.shape, BF16),
        scratch_shapes=[pltpu.VMEM((tq, width), F32), pltpu.VMEM((tq, width), F32),
                        pltpu.VMEM((tq, 2 * width), F32), pltpu.VMEM((tq, 2 * width), F32)],
        compiler_params=_cparams(("parallel", "parallel", "arbitrary")),
    )(q, k, v, lam_params, subln_w.reshape(1, width))


def _top16(sc, payload=None):
    iota = lax.broadcasted_iota(jnp.int32, sc.shape, 0).astype(F32)
    big = float(sc.shape[0])
    vals, picks = [], []
    for _ in range(PEER_TOPK):
        m = jnp.max(sc, axis=0, keepdims=True)
        ix = jnp.min(jnp.where(sc == m, iota, big), axis=0, keepdims=True)
        sel = iota == ix
        if payload is None:
            picks.append(ix)
        else:
            picks.append(jnp.sum(jnp.where(sel, payload, 0), axis=0, keepdims=True))
        sc = jnp.where(sel, -jnp.inf, sc)
        vals.append(m)
    picks = jnp.concatenate(picks, axis=0)
    return jnp.concatenate(vals, axis=0), picks.astype(jnp.int32)


def _pair_candidates(r0, r1):
    half = SUBLANES // 2
    groups = [r0[0:1] + r1[0:8], r0[0:1] + r1[8:16]]
    groups += [r0[a:a + 1] + r1[0:8] for a in (1, 2, 3)]
    groups += [jnp.concatenate([r0[a:a + 1] + r1[0:half], r0[a + 1:a + 2] + r1[0:half]], axis=0)
               for a in (4, 6)]
    groups.append(r0[8:16] + r1[0:1])
    return jnp.concatenate(groups, axis=0)


def _route_kernel(na_ref, df_ref, x_ref, wo_ref, n2_ref, wq_ref, sk_ref,
                  x1_ref, xn_ref, eid_ref, g_ref, q_scr, eid_scr, g_scr):
    tm = x_ref.shape[0]
    x1 = (x_ref[...]
          + jnp.dot(na_ref[...], wo_ref[0:SECTION, :], preferred_element_type=F32)
          + jnp.dot(df_ref[...], wo_ref[SECTION:2 * SECTION, :], preferred_element_type=F32))
    ms = jnp.mean(x1 * x1, axis=-1, keepdims=True)
    xn = x1 * lax.rsqrt(ms + EPS) * n2_ref[...]
    for r in range(SUBLANES):
        x1_ref[:, r, :] = x1[:, r * LANES:(r + 1) * LANES]
        xn_ref[:, r, :] = xn[:, r * LANES:(r + 1) * LANES]
    xb = xn.astype(BF16)
    for c in range(2 * PEER_HEADS):
        q_scr[c] = jnp.dot(xb, wq_ref[:, c * LANES:(c + 1) * LANES],
                           preferred_element_type=F32).astype(BF16)

    nblk = tm // TB

    def route_block(h, blk):
        tok0 = blk * TB
        tops = []
        for p in range(2):
            sc = lax.dot_general(sk_ref[2 * h + p], q_scr[2 * h + p, pl.ds(tok0, TB), :],
                                 (((1,), (1,)), ((), ())), preferred_element_type=F32)
            tops.append(_top16(sc))
        (s0, i0), (s1, i1) = tops
        cand_s = _pair_candidates(s0, s1)
        cand_i = _pair_candidates(i0 * N_KEYS, i1)
        top_s, eid = _top16(cand_s, cand_i)
        e = jnp.exp(top_s - top_s[0:1])
        row = pl.multiple_of(h * PEER_TOPK, PEER_TOPK)
        eid_scr[blk, pl.ds(row, PEER_TOPK), :] = eid * ROW_SUB
        g_scr[blk, pl.ds(row, PEER_TOPK), :] = e / jnp.sum(e, axis=0, keepdims=True)

    def head_body(h, carry):
        for blk in range(nblk):
            route_block(h, blk)
        return carry

    lax.fori_loop(0, PEER_HEADS, head_body, 0)
    for blk in range(nblk):
        eid_ref[blk] = eid_scr[blk].T
        g_ref[blk * TB:(blk + 1) * TB, :] = g_scr[blk].T


def _out_and_route(na, df, x2d, w_out_bf, norm2_w, w_q_bf, sk_bf):
    n = x2d.shape[0]
    tm = TM_OUT
    tok = lambda w: pl.BlockSpec((tm, w), lambda i: (i, 0))
    tiles = pl.BlockSpec((tm, SUBLANES, LANES), lambda i: (i, 0, 0))
    return pl.pallas_call(
        _route_kernel,
        grid=(n // tm,),
        in_specs=[tok(SECTION), tok(SECTION), tok(D_MODEL),
                  _const_spec((D_MODEL, D_MODEL)), _const_spec((1, D_MODEL)),
                  _const_spec(w_q_bf.shape), _const_spec(sk_bf.shape)],
        out_specs=[tiles, tiles,
                   pl.BlockSpec((tm // TB, TB, HK), lambda i: (i, 0, 0)), tok(HK)],
        out_shape=[jax.ShapeDtypeStruct((n, SUBLANES, LANES), F32),
                   jax.ShapeDtypeStruct((n, SUBLANES, LANES), F32),
                   jax.ShapeDtypeStruct((n // TB, TB, HK), jnp.int32),
                   jax.ShapeDtypeStruct((n, HK), F32)],
        scratch_shapes=[pltpu.VMEM((2 * PEER_HEADS, tm, LANES), BF16),
                        pltpu.VMEM((tm // TB, HK, TB), jnp.int32), pltpu.VMEM((tm // TB, HK, TB), F32)],
        compiler_params=_cparams(("parallel",)),
    )(na, df, x2d, w_out_bf, norm2_w.reshape(1, D_MODEL), w_q_bf, sk_bf)


ROW_SUB = SUBLANES // 2
CHUNK_SLOTS = 32
N_CHUNKS = HK // CHUNK_SLOTS
CHUNK_ROWS = CHUNK_SLOTS * SUBLANES
TOKEN_UNROLL = 8
ROW_WORDS = ROW_SUB * LANES
SC_CORES, SC_SUBCORES, SC_LANES = 2, 16, 16
SC_WORKERS = SC_CORES * SC_SUBCORES


def _pack_table(tab):
    bits = lax.bitcast_convert_type(tab.astype(BF16), jnp.uint16).astype(jnp.uint32)
    bits = bits.reshape(tab.shape[0], ROW_SUB, 2, LANES)
    return (bits[:, :, 0, :] | (bits[:, :, 1, :] << 16)).reshape(tab.shape[0] * ROW_SUB, LANES)


def _split_bf16(x):
    hi = x.astype(BF16)
    return hi, (x - hi.astype(F32)).astype(BF16)


def _gather_chunk(tab_ref, eid_ref, t, chunk):
    ids = eid_ref.at[0, t]
    rows = [tab_ref[pl.ds(pl.multiple_of(ids[chunk * CHUNK_SLOTS + i], ROW_SUB), ROW_SUB), :]
            for i in range(CHUNK_SLOTS)]
    return pltpu.bitcast(jnp.concatenate(rows, axis=0), BF16)


def _diag_mask(width):
    r = lax.broadcasted_iota(jnp.int32, (SUBLANES, width), 0)
    n = lax.broadcasted_iota(jnp.int32, (SUBLANES, width), 1)
    return (n % SUBLANES) == r


def _peer_u_kernel(eid_ref, xn_ref, g_ref, tab_ref, rept_ref, rep_ref, rep16_ref, wrep_ref, w16_ref, d_scr):
    mask = _diag_mask(CHUNK_ROWS)
    zeros = jnp.zeros((SUBLANES, LANES), BF16)

    def one_token(t):
        xhi, xlo = _split_bf16(xn_ref[t])
        lhs = jnp.concatenate([jnp.concatenate([xhi, zeros], axis=1),
                               jnp.concatenate([xlo, zeros], axis=1),
                               jnp.concatenate([zeros, xhi], axis=1),
                               jnp.concatenate([zeros, xlo], axis=1)], axis=0)
        for pr in range(N_CHUNKS // 2):
            s = jnp.concatenate([_gather_chunk(tab_ref, eid_ref, t, 2 * pr),
                                 _gather_chunk(tab_ref, eid_ref, t, 2 * pr + 1)], axis=1)
            z = lax.dot_general(lhs, s, (((1,), (1,)), ((), ())), preferred_element_type=F32)
            for half in range(2):
                z8 = z[16 * half:16 * half + 8] + z[16 * half + 8:16 * half + 16]
                zs = jnp.sum(jnp.where(mask, z8, 0.0), axis=0, keepdims=True)
                d_scr[pl.ds(t, 1), pl.ds((2 * pr + half) * CHUNK_ROWS, CHUNK_ROWS)] = zs

    def tok_body(i, carry):
        for u in range(TOKEN_UNROLL):
            one_token(i * TOKEN_UNROLL + u)
        return carry

    lax.fori_loop(0, PB // TOKEN_UNROLL, tok_body, 0)
    dhi, dlo = _split_bf16(d_scr[...])
    a = (jnp.dot(dhi, rept_ref[...], preferred_element_type=F32)
         + jnp.dot(dlo, rept_ref[...], preferred_element_type=F32))
    w = g_ref[...] * (0.5 * a * (1.0 + lax.erf(a * (2.0 ** -0.5))))
    whi, wlo = _split_bf16(w)
    wrep_ref[...] = (jnp.dot(whi, rep_ref[...], preferred_element_type=F32)
                     + jnp.dot(wlo, rep_ref[...], preferred_element_type=F32))
    w16_ref[...] = (jnp.dot(whi, rep16_ref[...], preferred_element_type=F32)
                    + jnp.dot(wlo, rep16_ref[...], preferred_element_type=F32))


def _peer_v_kernel(eid_ref, wrep_ref, x1_ref, tab_ref, o_ref):
    mask = _diag_mask(CHUNK_ROWS)

    def one_token(t):
        wr = wrep_ref[pl.ds(t, 1), :]
        acc = x1_ref[t]
        for pr in range(N_CHUNKS // 2):
            parts = []
            for half in range(2):
                seg = wr[:, (2 * pr + half) * CHUNK_ROWS:(2 * pr + half + 1) * CHUNK_ROWS]
                parts.extend(_split_bf16(jnp.where(mask, jnp.broadcast_to(seg, mask.shape), 0.0)))
            lhs = jnp.concatenate(parts, axis=0)
            s = jnp.concatenate([_gather_chunk(tab_ref, eid_ref, t, 2 * pr),
                                 _gather_chunk(tab_ref, eid_ref, t, 2 * pr + 1)], axis=1)
            z = jnp.dot(lhs, s, preferred_element_type=F32)
            acc = acc + ((z[0:8, :LANES] + z[8:16, :LANES]) + (z[16:24, LANES:] + z[24:32, LANES:]))
        return acc

    def tok_body(i, carry):
        accs = [one_token(i * TOKEN_UNROLL + u) for u in range(TOKEN_UNROLL)]
        base = pl.multiple_of(i * TOKEN_UNROLL, TOKEN_UNROLL)
        for r in range(SUBLANES):
            o_ref[pl.ds(base, TOKEN_UNROLL), r * LANES:(r + 1) * LANES] = jnp.concatenate(
                [acc[r:r + 1, :] for acc in accs], axis=0)
        return carry

    lax.fori_loop(0, PB // TOKEN_UNROLL, tok_body, 0)


def _sc_v_phase(n_tok, tok0):
    per_w = n_tok // SC_WORKERS
    assert per_w * SC_WORKERS == n_tok and per_w % 2 == 0
    half = HK // 2
    mesh = plsc.VectorSubcoreMesh(core_axis_name="c", subcore_axis_name="s")

    def small():
        return [pltpu.VMEM((2, half), jnp.int32), pltpu.VMEM((HK * SC_LANES,), F32),
                pltpu.VMEM((SUBLANES, LANES), F32)]

    @functools.partial(
        pl.kernel, mesh=mesh,
        out_type=jax.ShapeDtypeStruct((n_tok, SUBLANES, LANES), F32),
        scratch_types=small() + small() + [
            pltpu.VMEM((half, ROW_WORDS), jnp.uint32), pltpu.VMEM((half, ROW_WORDS), jnp.uint32),
            pltpu.SemaphoreType.DMA, pltpu.SemaphoreType.DMA,
            pltpu.SemaphoreType.DMA, pltpu.SemaphoreType.DMA,
            pltpu.SemaphoreType.DMA, pltpu.SemaphoreType.DMA,
        ],
        compiler_params=pltpu.CompilerParams(needs_layout_passes=False),
    )
    def sc_kernel(tab_hbm, eid_hbm, w16_hbm, x1_hbm, out_hbm,
                  idx_a, w_a, x_a, idx_b, w_b, x_b, g0, g1, s_a, s_b, sg0, sg1, so_a, so_b):
        wid = lax.axis_index("s") * SC_CORES + lax.axis_index("c")
        first = wid * per_w
        par = ((idx_a, w_a, x_a, s_a, so_a), (idx_b, w_b, x_b, s_b, so_b))
        gbuf, gsem = (g0, g1), (sg0, sg1)

        def input_copies(t, p):
            idx_v, w_v, x_v, sem, _ = par[p]
            return (pltpu.make_async_copy(eid_hbm.at[tok0 + t], idx_v, sem),
                    pltpu.make_async_copy(w16_hbm.at[tok0 + t], w_v, sem),
                    pltpu.make_async_copy(x1_hbm.at[tok0 + t], x_v, sem))

        def gather(p, h):
            return pltpu.make_async_copy(tab_hbm.at[par[p][0].at[h]], gbuf[h], gsem[h])

        def out_copy(t, p):
            return pltpu.make_async_copy(par[p][2], out_hbm.at[t], par[p][4])

        def accumulate_half(p, h):
            _, w_v, x_v, _, _ = par[p]
            g = gbuf[h]
            groups = LANES // SC_LANES
            for c in range(ROW_SUB):
                lo0 = tuple(x_v[2 * c, pl.ds(SC_LANES * kk, SC_LANES)] for kk in range(groups))
                hi0 = tuple(x_v[2 * c + 1, pl.ds(SC_LANES * kk, SC_LANES)] for kk in range(groups))

                def row_body(j, acc):
                    lo, hi = acc
                    wj = w_v[pl.ds(pl.multiple_of((h * half + j) * SC_LANES, SC_LANES), SC_LANES)]
                    nlo, nhi = [], []
                    for kk in range(groups):
                        words = g[j, pl.ds(LANES * c + SC_LANES * kk, SC_LANES)]
                        lov = plsc.bitcast(lax.shift_left(words, jnp.uint32(16)), F32)
                        hiv = plsc.bitcast(words & jnp.uint32(0xFFFF0000), F32)
                        nlo.append(lo[kk] + wj * lov)
                        nhi.append(hi[kk] + wj * hiv)
                    return tuple(nlo), tuple(nhi)

                lo, hi = lax.fori_loop(0, half, row_body, (lo0, hi0))
                for kk in range(groups):
                    x_v[2 * c, pl.ds(SC_LANES * kk, SC_LANES)] = lo[kk]
                    x_v[2 * c + 1, pl.ds(SC_LANES * kk, SC_LANES)] = hi[kk]

        def token(t, p):
            q = 1 - p
            has_next = t + 1 < first + per_w
            gather(p, 0).wait()
            gather(p, 1).start()

            @pl.when(has_next)
            def _():
                @pl.when(t >= first + 1)
                def _():
                    out_copy(t - 1, q).wait()
                for cp in input_copies(t + 1, q):
                    cp.start()

            accumulate_half(p, 0)
            gather(p, 1).wait()

            @pl.when(has_next)
            def _():
                for cp in input_copies(t + 1, q):
                    cp.wait()
                gather(q, 0).start()

            accumulate_half(p, 1)
            out_copy(t, p).start()

        for cp in input_copies(first, 0):
            cp.start()
        for cp in input_copies(first, 0):
            cp.wait()
        gather(0, 0).start()

        def pair_body(i, carry):
            token(first + 2 * i, 0)
            token(first + 2 * i + 1, 1)
            return carry

        lax.fori_loop(0, per_w // 2, pair_body, 0)
        out_copy(first + per_w - 2, 0).wait()
        out_copy(first + per_w - 1, 1).wait()

    return sc_kernel


def _peer_experts(eid, g, xn3, x13, u_pack, v_pack, n_sc):
    n = xn3.shape[0]
    n_tc = n - n_sc
    assert n_tc % PB == 0
    nblk = n // PB
    eid = eid.reshape(nblk, PB, HK)
    smem_slot = pl.BlockSpec((1, PB, HK), lambda i: (i, 0, 0), memory_space=pltpu.SMEM)
    tok = pl.BlockSpec((PB, SUBLANES, LANES), lambda i: (i, 0, 0))
    flat = lambda w: pl.BlockSpec((PB, w), lambda i: (i, 0))
    table = pl.BlockSpec(u_pack.shape, lambda i: (0, 0), pipeline_mode=pl.Buffered(1))

    def repeat_matrix(times):
        m = (np.arange(HK)[:, None] == (np.arange(HK * times)[None, :] // times)).astype(np.float32)
        return jnp.asarray(m, BF16)

    rep, rep16 = repeat_matrix(SUBLANES), repeat_matrix(SC_LANES)
    wrep, w16 = pl.pallas_call(
        _peer_u_kernel,
        grid=(nblk,),
        in_specs=[smem_slot, tok, flat(HK), table, _const_spec(rep.T.shape), _const_spec(rep.shape),
                  _const_spec(rep16.shape)],
        out_specs=[flat(D_MODEL), flat(HK * SC_LANES)],
        out_shape=[jax.ShapeDtypeStruct((n, D_MODEL), F32), jax.ShapeDtypeStruct((n, HK * SC_LANES), F32)],
        scratch_shapes=[pltpu.VMEM((PB, D_MODEL), F32)],
        compiler_params=_cparams(("arbitrary",)),
    )(eid, xn3, g, u_pack, rep.T, rep, rep16)
    parts = []
    if n_tc:
        parts.append(pl.pallas_call(
            _peer_v_kernel,
            grid=(n_tc // PB,),
            in_specs=[smem_slot, flat(D_MODEL), tok, table],
            out_specs=flat(D_MODEL),
            out_shape=jax.ShapeDtypeStruct((n_tc, D_MODEL), F32),
            compiler_params=_cparams(("arbitrary",)),
        )(eid, wrep, x13, v_pack))
    if n_sc:
        expert = eid.reshape(n, 2, HK // 2) // ROW_SUB
        rows = v_pack.reshape(N_EXPERTS, ROW_WORDS)
        parts.append(_sc_v_phase(n_sc, n_tc)(rows, expert, w16, x13).reshape(n_sc, D_MODEL))
    return parts[0] if len(parts) == 1 else jnp.concatenate(parts, axis=0)


def _encoder_layer(x, p):
    batch, seq, _ = x.shape
    n = batch * seq
    x2d = x.reshape(n, D_MODEL)
    naq, nak, nav, dq, dk, dv = _in_projection(x2d, seq, p["norm1_w"], p["w_in"], p["qk_w"], p["rope"])
    na = _neighbourhood_attention(naq, nak, nav, p["na_bias"], batch, seq)
    df = _diff_attention(dq, dk, dv, p["lam"], p["subln_w"], batch, seq)
    x1, xn, eid, g = _out_and_route(na, df, x2d, p["w_out"], p["norm2_w"], p["w_q"], p["sub_keys"])
    y = _peer_experts(eid, g, xn, x1, p["u_pack"], p["v_pack"], SC_TOKENS.get(n, 0))
    return y.reshape(batch, seq, D_MODEL)


def kernel(x_prompt, x_sample, norm1_w, w_in, na_q_norm, na_k_norm, na_rpb, diff_q_norm, diff_k_norm,
           diff_lambda_q1, diff_lambda_k1, diff_lambda_q2, diff_lambda_k2, diff_subln_w, w_out, norm2_w,
           peer_w_q, peer_sub_keys, peer_u, peer_v):
    assert norm1_w.shape[0] == 1, "single-layer problem"
    reps = SECTION // HEAD_DIM
    params = {
        "norm1_w": norm1_w[0],
        "w_in": w_in[0].astype(BF16),
        "qk_w": jnp.stack([jnp.tile(w[0], reps) for w in (na_q_norm, na_k_norm, diff_q_norm, diff_k_norm)]),
        "lam": jnp.stack([diff_lambda_q1[0], diff_lambda_k1[0], diff_lambda_q2[0], diff_lambda_k2[0]]),
        "subln_w": diff_subln_w[0],
        "w_out": w_out[0].astype(BF16),
        "norm2_w": norm2_w[0],
        "w_q": peer_w_q[0].astype(BF16),
        "sub_keys": peer_sub_keys[0].reshape(2 * PEER_HEADS, N_KEYS, N_KEYS).astype(BF16),
        "na_bias": _na_bias(na_rpb[0]),
        "u_pack": _pack_table(peer_u[0]),
        "v_pack": _pack_table(peer_v[0]),
    }
    rope_by_seq = {}
    outs = []
    for x in (x_prompt, x_sample):
        seq = x.shape[1]
        if seq not in rope_by_seq:
            rope_by_seq[seq] = _rope_tables(seq)
        outs.append(_encoder_layer(x, dict(params, rope=rope_by_seq[seq])))
    return tuple(outs)
```

```python
import functools
import math

import jax
import jax.numpy as jnp
import numpy as np
from jax import lax
from jax.experimental import pallas as pl
from jax.experimental.pallas import tpu as pltpu
from jax.experimental.pallas import tpu_sc as plsc

F32 = jnp.float32
BF16 = jnp.bfloat16

D_MODEL = 1024
GRID_W = 64
HEAD_DIM = 64
NA_HEADS = 8
NA_KH = 8
NA_KW = 16
DIFF_HEADS = 4
SECTION = 512
ROT_DIM = HEAD_DIM // 4
ROPE_THETA = 500000.0
PEER_HEADS = 8
N_KEYS = 128
PEER_TOPK = 16
N_EXPERTS = N_KEYS * N_KEYS
HK = PEER_HEADS * PEER_TOPK
EPS = 1e-6
NEG_BIG = -1e30
LOG2E = math.log2(math.e)
LAMBDA_INIT = 0.8 - 0.6 * math.exp(-0.3 * 0)

LANES = 128
SUBLANES = 8
VMEM_LIMIT = 56 * 1024 * 1024

TM_IN = 512
NA_ROWS_PER_STEP = 8
NA_ROW_UNROLL = 2
TQ = 512
KV_SUBCHUNKS = 1
TK = 4096
TM_OUT = 256
TB = 128
PB = 256
SC_TOKENS = {8 * 4096: 8 * 4096, 4 * 4096: 5632}


def _cparams(sem):
    return pltpu.CompilerParams(dimension_semantics=sem, vmem_limit_bytes=VMEM_LIMIT)


def _const_spec(shape):
    nd = len(shape)
    return pl.BlockSpec(shape, lambda *_: (0,) * nd)


def _inproj_kernel(x_ref, n1_ref, w_ref, bd_ref, nw_ref, rc_ref, ra_ref, rb_ref,
                   naq_ref, nak_ref, nav_ref, dq_ref, dk_ref, dv_ref):
    x = x_ref[...]
    ms = jnp.mean(x * x, axis=-1, keepdims=True)
    h = (x * lax.rsqrt(ms + EPS) * n1_ref[...]).astype(BF16)
    bd = bd_ref[...]

    def proj(c):
        return jnp.dot(h, w_ref[:, c * SECTION:(c + 1) * SECTION], preferred_element_type=F32)

    def qknorm(y, row):
        sq = y * y
        hi = sq.astype(BF16)
        lo = (sq - hi.astype(F32)).astype(BF16)
        msq = (jnp.dot(hi, bd, preferred_element_type=F32)
               + jnp.dot(lo, bd, preferred_element_type=F32))
        return y * lax.rsqrt(msq + EPS) * nw_ref[row:row + 1, :]

    def rope(y):
        return (y * rc_ref[...]
                + pltpu.roll(y, SECTION - ROT_DIM // 2, axis=1) * ra_ref[...]
                + pltpu.roll(y, ROT_DIM // 2, axis=1) * rb_ref[...])

    scale = HEAD_DIM ** -0.5 * LOG2E
    naq_ref[...] = (qknorm(proj(0), 0) * scale).astype(BF16)
    nak_ref[...] = qknorm(proj(1), 1).astype(BF16)
    nav_ref[...] = proj(2).astype(BF16)
    dq_ref[...] = (rope(qknorm(proj(3), 2)) * scale).astype(BF16)
    dk_ref[...] = rope(qknorm(proj(4), 3)).astype(BF16)
    dv_ref[...] = proj(5).astype(BF16)


def _rope_tables(seq):
    pos = jnp.arange(seq, dtype=F32)
    inv = ROPE_THETA ** (-jnp.arange(0, ROT_DIM, 2, dtype=F32) / ROT_DIM)
    ang = pos[:, None] * inv[None, :]
    cos, sin = jnp.cos(ang), jnp.sin(ang)
    half = ROT_DIM // 2
    pad = HEAD_DIM - ROT_DIM
    c_head = jnp.concatenate([cos, cos, jnp.ones((seq, pad), F32)], -1)
    a_head = jnp.concatenate([-sin, jnp.zeros((seq, pad + half), F32)], -1)
    b_head = jnp.concatenate([jnp.zeros((seq, half), F32), sin, jnp.zeros((seq, pad), F32)], -1)
    reps = SECTION // HEAD_DIM
    return (jnp.tile(c_head, (1, reps)), jnp.tile(a_head, (1, reps)), jnp.tile(b_head, (1, reps)))


def _in_projection(x2d, seq, norm1_w, w_in_bf, qk_w, rope_tabs):
    n = x2d.shape[0]
    tm = min(TM_IN, seq)
    steps_per_seq = seq // tm
    gid = np.arange(SECTION) // HEAD_DIM
    bd = jnp.asarray((gid[:, None] == gid[None, :]).astype(np.float32) / HEAD_DIM, BF16)
    tok = pl.BlockSpec((tm, SECTION), lambda i: (i, 0))
    pos = pl.BlockSpec((tm, SECTION), lambda i: (i % steps_per_seq, 0))
    out = jax.ShapeDtypeStruct((n, SECTION), BF16)
    return pl.pallas_call(
        _inproj_kernel,
        grid=(n // tm,),
        in_specs=[pl.BlockSpec((tm, D_MODEL), lambda i: (i, 0)),
                  _const_spec((1, D_MODEL)),
                  _const_spec((D_MODEL, 6 * SECTION)),
                  _const_spec((SECTION, SECTION)),
                  _const_spec((4, SECTION)),
                  pos, pos, pos],
        out_specs=[tok] * 6,
        out_shape=[out] * 6,
        compiler_params=_cparams(("parallel",)),
    )(x2d, norm1_w.reshape(1, D_MODEL), w_in_bf, bd, qk_w, *rope_tabs)


def _na_bias(rpb):
    c = np.arange(GRID_W)[:, None]
    kc = np.arange(GRID_W)[None, :]
    cs = np.clip(c - NA_KW // 2, 0, GRID_W - NA_KW)
    valid = (kc >= cs) & (kc < cs + NA_KW)
    rel_c = kc - c + (NA_KW - 1)
    onehot = ((np.arange(2 * NA_KW - 1)[:, None, None] == rel_c[None]) & valid[None]).astype(np.float32)
    col = jnp.einsum("hrx,xck->hrck", rpb.astype(F32), onehot, precision=lax.Precision.HIGHEST)
    col = jnp.where(valid, col * LOG2E, NEG_BIG)
    per_off = [col[:, NA_KH - 1 - o:2 * NA_KH - 1 - o] for o in range(NA_KH)]
    b = jnp.stack(per_off).transpose(0, 1, 3, 2, 4)
    return b.reshape(NA_KH, NA_HEADS, GRID_W, NA_KH * GRID_W)


def _na_kernel(q_ref, k_ref, v_ref, b_ref, o_ref, *, rows):
    j = pl.program_id(1)
    win = NA_KH * GRID_W
    pair = 2 * HEAD_DIM
    lane = lax.broadcasted_iota(jnp.int32, (GRID_W, pair), 1)
    first = lane < HEAD_DIM
    ones = jnp.ones((win, pair), BF16)

    def one_row(rr):
        r = j * NA_ROWS_PER_STEP + rr
        rs = jnp.clip(r - NA_KH // 2, 0, rows - NA_KH)
        off = r - rs
        kstart = pl.multiple_of(rs * GRID_W, GRID_W)
        qstart = pl.multiple_of(rr * GRID_W, GRID_W)
        outs = []
        for hp in range(NA_HEADS // 2):
            sl = slice(hp * pair, (hp + 1) * pair)
            qp = q_ref[pl.ds(qstart, GRID_W), sl]
            kp = k_ref[pl.ds(kstart, win), sl]
            v_ext = jnp.concatenate([v_ref[pl.ds(kstart, win), sl], ones], axis=1)
            halves = []
            for hh in range(2):
                qm = jnp.where(first if hh == 0 else ~first, qp, jnp.zeros_like(qp))
                s = lax.dot_general(qm, kp, (((1,), (1,)), ((), ())), preferred_element_type=F32)
                s = s + b_ref[off, 2 * hp + hh]
                p = jnp.exp2(s - jnp.max(s, axis=-1, keepdims=True)).astype(BF16)
                z = jnp.dot(p, v_ext, preferred_element_type=F32)
                halves.append(z[:, :pair] / z[:, pair:pair + 1])
            outs.append(jnp.where(first, halves[0], halves[1]))
        o_ref[pl.ds(qstart, GRID_W), :] = jnp.concatenate(outs, axis=-1).astype(BF16)

    def row_body(i, carry):
        for u in range(NA_ROW_UNROLL):
            one_row(i * NA_ROW_UNROLL + u)
        return carry

    lax.fori_loop(0, NA_ROWS_PER_STEP // NA_ROW_UNROLL, row_body, 0)


def _neighbourhood_attention(q, k, v, bias, batch, seq):
    rows = seq // GRID_W
    assert rows >= NA_KH and rows % NA_ROWS_PER_STEP == 0
    nblk = rows // NA_ROWS_PER_STEP
    tq = NA_ROWS_PER_STEP * GRID_W
    qspec = pl.BlockSpec((tq, SECTION), lambda b, j: (b * nblk + j, 0))
    kvspec = pl.BlockSpec((seq, SECTION), lambda b, j: (b, 0))
    return pl.pallas_call(
        functools.partial(_na_kernel, rows=rows),
        grid=(batch, nblk),
        in_specs=[qspec, kvspec, kvspec, _const_spec(bias.shape)],
        out_specs=qspec,
        out_shape=jax.ShapeDtypeStruct(q.shape, BF16),
        compiler_params=_cparams(("parallel", "arbitrary")),
    )(q, k, v, bias)


def _diff_kernel(q_ref, k_ref, v_ref, lam_ref, sub_ref, o_ref, m0_scr, m1_scr, acc0_scr, acc1_scr,
                 *, seq, tq, tk):
    m_scrs, acc_scrs = (m0_scr, m1_scr), (acc0_scr, acc1_scr)
    q = q_ref[...]
    lane = lax.broadcasted_iota(jnp.int32, q.shape, 1)
    zero = jnp.zeros_like(q)
    q_maps = (jnp.where(lane < HEAD_DIM, q, zero), jnp.where(lane >= HEAD_DIM, q, zero))

    lp = lam_ref[...]
    lam = (jnp.exp(jnp.sum(lp[0:1] * lp[1:2], axis=-1, keepdims=True))
           - jnp.exp(jnp.sum(lp[2:3] * lp[3:4], axis=-1, keepdims=True)) + LAMBDA_INIT)

    width = 2 * HEAD_DIM
    ones = jnp.ones((tk, width), BF16)

    for m_scr, acc_scr in zip(m_scrs, acc_scrs):
        m_scr[...] = jnp.full(m_scr.shape, -jnp.inf, F32)
        acc_scr[...] = jnp.zeros(acc_scr.shape, F32)

    def kv_body(c, carry):
        tks = tk // KV_SUBCHUNKS
        kcs, v_exts = [], []
        for sub in range(KV_SUBCHUNKS):
            start = pl.multiple_of(c * tk + sub * tks, tks)
            kcs.append(k_ref[pl.ds(start, tks), :])
            v_exts.append(jnp.concatenate([v_ref[pl.ds(start, tks), :], ones[:tks]], axis=1))
        scores = [[lax.dot_general(q_maps[mp], kcs[sub], (((1,), (1,)), ((), ())),
                                   preferred_element_type=F32) for mp in range(2)]
                  for sub in range(KV_SUBCHUNKS)]
        m = [m_scrs[mp][...] for mp in range(2)]
        probs, alphas = [], []
        for sub in range(KV_SUBCHUNKS):
            probs.append([])
            alphas.append([])
            for mp in range(2):
                mn = jnp.maximum(m[mp], jnp.max(scores[sub][mp], axis=-1, keepdims=True))
                probs[sub].append(jnp.exp2(scores[sub][mp] - mn[:, 0:1]).astype(BF16))
                alpha = jnp.exp2(m[mp] - mn)
                alphas[sub].append(jnp.concatenate([alpha, alpha], axis=1))
                m[mp] = mn
        for mp in range(2):
            m_scrs[mp][...] = m[mp]
            acc = acc_scrs[mp][...]
            for sub in range(KV_SUBCHUNKS):
                acc = alphas[sub][mp] * acc + jnp.dot(probs[sub][mp], v_exts[sub],
                                                      preferred_element_type=F32)
            acc_scrs[mp][...] = acc
        return carry

    lax.fori_loop(0, seq // tk, kv_body, 0)
    a0, a1 = acc0_scr[...], acc1_scr[...]
    o = a0[:, :width] / a0[:, width:width + 1] - lam * (a1[:, :width] / a1[:, width:width + 1])
    ms = jnp.mean(o * o, axis=-1, keepdims=True)
    y = o * lax.rsqrt(ms + EPS) * sub_ref[...]
    o_ref[...] = (y * (1.0 - LAMBDA_INIT)).astype(BF16)


def _diff_attention(q, k, v, lam_params, subln_w, batch, seq):
    tq, tk = min(TQ, seq), min(TK, seq)
    nq = seq // tq
    width = 2 * HEAD_DIM
    qspec = pl.BlockSpec((tq, width), lambda b, h, i: (b * nq + i, h))
    kvspec = pl.BlockSpec((seq, width), lambda b, h, i: (b, h))
    return pl.pallas_call(
        functools.partial(_diff_kernel, seq=seq, tq=tq, tk=tk),
        grid=(batch, DIFF_HEADS, nq),
        in_specs=[qspec, kvspec, kvspec, _const_spec((4, HEAD_DIM)), _const_spec((1, width))],
        out_specs=qspec,
        out_shape=jax.ShapeDtypeStruct(q.shape, BF16),
        scratch_shapes=[pltpu.VMEM((tq, width), F32), pltpu.VMEM((tq, width), F32),
                        pltpu.VMEM((tq, 2 * width), F32), pltpu.VMEM((tq, 2 * width), F32)],
        compiler_params=_cparams(("parallel", "parallel", "arbitrary")),
    )(q, k, v, lam_params, subln_w.reshape(1, width))


def _top16(sc, payload=None):
    iota = lax.broadcasted_iota(jnp.int32, sc.shape, 0).astype(F32)
    big = float(sc.shape[0])
    vals, picks = [], []
    for _ in range(PEER_TOPK):
        m = jnp.max(sc, axis=0, keepdims=True)
        ix = jnp.min(jnp.where(sc == m, iota, big), axis=0, keepdims=True)
        sel = iota == ix
        if payload is None:
            picks.append(ix)
        else:
            picks.append(jnp.sum(jnp.where(sel, payload, 0), axis=0, keepdims=True))
        sc = jnp.where(sel, -jnp.inf, sc)
        vals.append(m)
    picks = jnp.concatenate(picks, axis=0)
    return jnp.concatenate(vals, axis=0), picks.astype(jnp.int32)


def _pair_candidates(r0, r1):
    half = SUBLANES // 2
    groups = [r0[0:1] + r1[0:8], r0[0:1] + r1[8:16]]
    groups += [r0[a:a + 1] + r1[0:8] for a in (1, 2, 3)]
    groups += [jnp.concatenate([r0[a:a + 1] + r1[0:half], r0[a + 1:a + 2] + r1[0:half]], axis=0)
               for a in (4, 6)]
    groups.append(r0[8:16] + r1[0:1])
    return jnp.concatenate(groups, axis=0)


def _route_kernel(na_ref, df_ref, x_ref, wo_ref, n2_ref, wq_ref, sk_ref,
                  x1_ref, xn_ref, eid_ref, g_ref, q_scr, eid_scr, g_scr):
    tm = x_ref.shape[0]
    x1 = (x_ref[...]
          + jnp.dot(na_ref[...], wo_ref[0:SECTION, :], preferred_element_type=F32)
          + jnp.dot(df_ref[...], wo_ref[SECTION:2 * SECTION, :], preferred_element_type=F32))
    ms = jnp.mean(x1 * x1, axis=-1, keepdims=True)
    xn = x1 * lax.rsqrt(ms + EPS) * n2_ref[...]
    for r in range(SUBLANES):
        x1_ref[:, r, :] = x1[:, r * LANES:(r + 1) * LANES]
        xn_ref[:, r, :] = xn[:, r * LANES:(r + 1) * LANES]
    xb = xn.astype(BF16)
    for c in range(2 * PEER_HEADS):
        q_scr[c] = jnp.dot(xb, wq_ref[:, c * LANES:(c + 1) * LANES],
                           preferred_element_type=F32).astype(BF16)

    nblk = tm // TB

    def route_block(h, blk):
        tok0 = blk * TB
        tops = []
        for p in range(2):
            sc = lax.dot_general(sk_ref[2 * h + p], q_scr[2 * h + p, pl.ds(tok0, TB), :],
                                 (((1,), (1,)), ((), ())), preferred_element_type=F32)
            tops.append(_top16(sc))
        (s0, i0), (s1, i1) = tops
        cand_s = _pair_candidates(s0, s1)
        cand_i = _pair_candidates(i0 * N_KEYS, i1)
        top_s, eid = _top16(cand_s, cand_i)
        e = jnp.exp(top_s - top_s[0:1])
        row = pl.multiple_of(h * PEER_TOPK, PEER_TOPK)
        eid_scr[blk, pl.ds(row, PEER_TOPK), :] = eid * ROW_SUB
        g_scr[blk, pl.ds(row, PEER_TOPK), :] = e / jnp.sum(e, axis=0, keepdims=True)

    def head_body(h, carry):
        for blk in range(nblk):
            route_block(h, blk)
        return carry

    lax.fori_loop(0, PEER_HEADS, head_body, 0)
    for blk in range(nblk):
        eid_ref[blk] = eid_scr[blk].T
        g_ref[blk * TB:(blk + 1) * TB, :] = g_scr[blk].T


def _out_and_route(na, df, x2d, w_out_bf, norm2_w, w_q_bf, sk_bf):
    n = x2d.shape[0]
    tm = TM_OUT
    tok = lambda w: pl.BlockSpec((tm, w), lambda i: (i, 0))
    tiles = pl.BlockSpec((tm, SUBLANES, LANES), lambda i: (i, 0, 0))
    return pl.pallas_call(
        _route_kernel,
        grid=(n // tm,),
        in_specs=[tok(SECTION), tok(SECTION), tok(D_MODEL),
                  _const_spec((D_MODEL, D_MODEL)), _const_spec((1, D_MODEL)),
                  _const_spec(w_q_bf.shape), _const_spec(sk_bf.shape)],
        out_specs=[tiles, tiles,
                   pl.BlockSpec((tm // TB, TB, HK), lambda i: (i, 0, 0)), tok(HK)],
        out_shape=[jax.ShapeDtypeStruct((n, SUBLANES, LANES), F32),
                   jax.ShapeDtypeStruct((n, SUBLANES, LANES), F32),
                   jax.ShapeDtypeStruct((n // TB, TB, HK), jnp.int32),
                   jax.ShapeDtypeStruct((n, HK), F32)],
        scratch_shapes=[pltpu.VMEM((2 * PEER_HEADS, tm, LANES), BF16),
                        pltpu.VMEM((tm // TB, HK, TB), jnp.int32), pltpu.VMEM((tm // TB, HK, TB), F32)],
        compiler_params=_cparams(("parallel",)),
    )(na, df, x2d, w_out_bf, norm2_w.reshape(1, D_MODEL), w_q_bf, sk_bf)


ROW_SUB = SUBLANES // 2
CHUNK_SLOTS = 32
N_CHUNKS = HK // CHUNK_SLOTS
CHUNK_ROWS = CHUNK_SLOTS * SUBLANES
TOKEN_UNROLL = 8
ROW_WORDS = ROW_SUB * LANES
SC_CORES, SC_SUBCORES, SC_LANES = 2, 16, 16
SC_WORKERS = SC_CORES * SC_SUBCORES


def _pack_table(tab):
    bits = lax.bitcast_convert_type(tab.astype(BF16), jnp.uint16).astype(jnp.uint32)
    bits = bits.reshape(tab.shape[0], ROW_SUB, 2, LANES)
    return (bits[:, :, 0, :] | (bits[:, :, 1, :] << 16)).reshape(tab.shape[0] * ROW_SUB, LANES)


def _split_bf16(x):
    hi = x.astype(BF16)
    return hi, (x - hi.astype(F32)).astype(BF16)


def _gather_chunk(tab_ref, eid_ref, t, chunk):
    ids = eid_ref.at[0, t]
    rows = [tab_ref[pl.ds(pl.multiple_of(ids[chunk * CHUNK_SLOTS + i], ROW_SUB), ROW_SUB), :]
            for i in range(CHUNK_SLOTS)]
    return pltpu.bitcast(jnp.concatenate(rows, axis=0), BF16)


def _diag_mask(width):
    r = lax.broadcasted_iota(jnp.int32, (SUBLANES, width), 0)
    n = lax.broadcasted_iota(jnp.int32, (SUBLANES, width), 1)
    return (n % SUBLANES) == r


def _peer_u_kernel(eid_ref, xn_ref, g_ref, tab_ref, rept_ref, rep_ref, rep16_ref, wrep_ref, w16_ref, d_scr):
    mask = _diag_mask(CHUNK_ROWS)
    zeros = jnp.zeros((SUBLANES, LANES), BF16)

    def one_token(t):
        xhi, xlo = _split_bf16(xn_ref[t])
        lhs = jnp.concatenate([jnp.concatenate([xhi, zeros], axis=1),
                               jnp.concatenate([xlo, zeros], axis=1),
                               jnp.concatenate([zeros, xhi], axis=1),
                               jnp.concatenate([zeros, xlo], axis=1)], axis=0)
        for pr in range(N_CHUNKS // 2):
            s = jnp.concatenate([_gather_chunk(tab_ref, eid_ref, t, 2 * pr),
                                 _gather_chunk(tab_ref, eid_ref, t, 2 * pr + 1)], axis=1)
            z = lax.dot_general(lhs, s, (((1,), (1,)), ((), ())), preferred_element_type=F32)
            for half in range(2):
                z8 = z[16 * half:16 * half + 8] + z[16 * half + 8:16 * half + 16]
                zs = jnp.sum(jnp.where(mask, z8, 0.0), axis=0, keepdims=True)
                d_scr[pl.ds(t, 1), pl.ds((2 * pr + half) * CHUNK_ROWS, CHUNK_ROWS)] = zs

    def tok_body(i, carry):
        for u in range(TOKEN_UNROLL):
            one_token(i * TOKEN_UNROLL + u)
        return carry

    lax.fori_loop(0, PB // TOKEN_UNROLL, tok_body, 0)
    dhi, dlo = _split_bf16(d_scr[...])
    a = (jnp.dot(dhi, rept_ref[...], preferred_element_type=F32)
         + jnp.dot(dlo, rept_ref[...], preferred_element_type=F32))
    w = g_ref[...] * (0.5 * a * (1.0 + lax.erf(a * (2.0 ** -0.5))))
    whi, wlo = _split_bf16(w)
    wrep_ref[...] = (jnp.dot(whi, rep_ref[...], preferred_element_type=F32)
                     + jnp.dot(wlo, rep_ref[...], preferred_element_type=F32))
    w16_ref[...] = (jnp.dot(whi, rep16_ref[...], preferred_element_type=F32)
                    + jnp.dot(wlo, rep16_ref[...], preferred_element_type=F32))


def _peer_v_kernel(eid_ref, wrep_ref, x1_ref, tab_ref, o_ref):
    mask = _diag_mask(CHUNK_ROWS)

    def one_token(t):
        wr = wrep_ref[pl.ds(t, 1), :]
        acc = x1_ref[t]
        for pr in range(N_CHUNKS // 2):
            parts = []
            for half in range(2):
                seg = wr[:, (2 * pr + half) * CHUNK_ROWS:(2 * pr + half + 1) * CHUNK_ROWS]
                parts.extend(_split_bf16(jnp.where(mask, jnp.broadcast_to(seg, mask.shape), 0.0)))
            lhs = jnp.concatenate(parts, axis=0)
            s = jnp.concatenate([_gather_chunk(tab_ref, eid_ref, t, 2 * pr),
                                 _gather_chunk(tab_ref, eid_ref, t, 2 * pr + 1)], axis=1)
            z = jnp.dot(lhs, s, preferred_element_type=F32)
            acc = acc + ((z[0:8, :LANES] + z[8:16, :LANES]) + (z[16:24, LANES:] + z[24:32, LANES:]))
        return acc

    def tok_body(i, carry):
        accs = [one_token(i * TOKEN_UNROLL + u) for u in range(TOKEN_UNROLL)]
        base = pl.multiple_of(i * TOKEN_UNROLL, TOKEN_UNROLL)
        for r in range(SUBLANES):
            o_ref[pl.ds(base, TOKEN_UNROLL), r * LANES:(r + 1) * LANES] = jnp.concatenate(
                [acc[r:r + 1, :] for acc in accs], axis=0)
        return carry

    lax.fori_loop(0, PB // TOKEN_UNROLL, tok_body, 0)


def _sc_v_phase(n_tok, tok0):
    per_w = n_tok // SC_WORKERS
    assert per_w * SC_WORKERS == n_tok and per_w % 2 == 0
    half = HK // 2
    mesh = plsc.VectorSubcoreMesh(core_axis_name="c", subcore_axis_name="s")

    def small():
        return [pltpu.VMEM((2, half), jnp.int32), pltpu.VMEM((HK * SC_LANES,), F32),
                pltpu.VMEM((SUBLANES, LANES), F32)]

    @functools.partial(
        pl.kernel, mesh=mesh,
        out_type=jax.ShapeDtypeStruct((n_tok, SUBLANES, LANES), F32),
        scratch_types=small() + small() + [
            pltpu.VMEM((half, ROW_WORDS), jnp.uint32), pltpu.VMEM((half, ROW_WORDS), jnp.uint32),
            pltpu.SemaphoreType.DMA, pltpu.SemaphoreType.DMA,
            pltpu.SemaphoreType.DMA, pltpu.SemaphoreType.DMA,
            pltpu.SemaphoreType.DMA, pltpu.SemaphoreType.DMA,
        ],
        compiler_params=pltpu.CompilerParams(needs_layout_passes=False),
    )
    def sc_kernel(tab_hbm, eid_hbm, w16_hbm, x1_hbm, out_hbm,
                  idx_a, w_a, x_a, idx_b, w_b, x_b, g0, g1, s_a, s_b, sg0, sg1, so_a, so_b):
        wid = lax.axis_index("s") * SC_CORES + lax.axis_index("c")
        first = wid * per_w
        par = ((idx_a, w_a, x_a, s_a, so_a), (idx_b, w_b, x_b, s_b, so_b))
        gbuf, gsem = (g0, g1), (sg0, sg1)

        def input_copies(t, p):
            idx_v, w_v, x_v, sem, _ = par[p]
            return (pltpu.make_async_copy(eid_hbm.at[tok0 + t], idx_v, sem),
                    pltpu.make_async_copy(w16_hbm.at[tok0 + t], w_v, sem),
                    pltpu.make_async_copy(x1_hbm.at[tok0 + t], x_v, sem))

        def gather(p, h):
            return pltpu.make_async_copy(tab_hbm.at[par[p][0].at[h]], gbuf[h], gsem[h])

        def out_copy(t, p):
            return pltpu.make_async_copy(par[p][2], out_hbm.at[t], par[p][4])

        def accumulate_half(p, h):
            _, w_v, x_v, _, _ = par[p]
            g = gbuf[h]
            groups = LANES // SC_LANES
            for c in range(ROW_SUB):
                lo0 = tuple(x_v[2 * c, pl.ds(SC_LANES * kk, SC_LANES)] for kk in range(groups))
                hi0 = tuple(x_v[2 * c + 1, pl.ds(SC_LANES * kk, SC_LANES)] for kk in range(groups))

                def row_body(j, acc):
                    lo, hi = acc
                    wj = w_v[pl.ds(pl.multiple_of((h * half + j) * SC_LANES, SC_LANES), SC_LANES)]
                    nlo, nhi = [], []
                    for kk in range(groups):
                        words = g[j, pl.ds(LANES * c + SC_LANES * kk, SC_LANES)]
                        lov = plsc.bitcast(lax.shift_left(words, jnp.uint32(16)), F32)
                        hiv = plsc.bitcast(words & jnp.uint32(0xFFFF0000), F32)
                        nlo.append(lo[kk] + wj * lov)
                        nhi.append(hi[kk] + wj * hiv)
                    return tuple(nlo), tuple(nhi)

                lo, hi = lax.fori_loop(0, half, row_body, (lo0, hi0))
                for kk in range(groups):
                    x_v[2 * c, pl.ds(SC_LANES * kk, SC_LANES)] = lo[kk]
                    x_v[2 * c + 1, pl.ds(SC_LANES * kk, SC_LANES)] = hi[kk]

        def token(t, p):
            q = 1 - p
            has_next = t + 1 < first + per_w
            gather(p, 0).wait()
            gather(p, 1).start()

            @pl.when(has_next)
            def _():
                @pl.when(t >= first + 1)
                def _():
                    out_copy(t - 1, q).wait()
                for cp in input_copies(t + 1, q):
                    cp.start()

            accumulate_half(p, 0)
            gather(p, 1).wait()

            @pl.when(has_next)
            def _():
                for cp in input_copies(t + 1, q):
                    cp.wait()
                gather(q, 0).start()

            accumulate_half(p, 1)
            out_copy(t, p).start()

        for cp in input_copies(first, 0):
            cp.start()
        for cp in input_copies(first, 0):
            cp.wait()
        gather(0, 0).start()

        def pair_body(i, carry):
            token(first + 2 * i, 0)
            token(first + 2 * i + 1, 1)
            return carry

        lax.fori_loop(0, per_w // 2, pair_body, 0)
        out_copy(first + per_w - 2, 0).wait()
        out_copy(first + per_w - 1, 1).wait()

    return sc_kernel


def _peer_experts(eid, g, xn3, x13, u_pack, v_pack, n_sc):
    n = xn3.shape[0]
    n_tc = n - n_sc
    assert n_tc % PB == 0
    nblk = n // PB
    eid = eid.reshape(nblk, PB, HK)
    smem_slot = pl.BlockSpec((1, PB, HK), lambda i: (i, 0, 0), memory_space=pltpu.SMEM)
    tok = pl.BlockSpec((PB, SUBLANES, LANES), lambda i: (i, 0, 0))
    flat = lambda w: pl.BlockSpec((PB, w), lambda i: (i, 0))
    table = pl.BlockSpec(u_pack.shape, lambda i: (0, 0), pipeline_mode=pl.Buffered(1))

    def repeat_matrix(times):
        m = (np.arange(HK)[:, None] == (np.arange(HK * times)[None, :] // times)).astype(np.float32)
        return jnp.asarray(m, BF16)

    rep, rep16 = repeat_matrix(SUBLANES), repeat_matrix(SC_LANES)
    wrep, w16 = pl.pallas_call(
        _peer_u_kernel,
        grid=(nblk,),
        in_specs=[smem_slot, tok, flat(HK), table, _const_spec(rep.T.shape), _const_spec(rep.shape),
                  _const_spec(rep16.shape)],
        out_specs=[flat(D_MODEL), flat(HK * SC_LANES)],
        out_shape=[jax.ShapeDtypeStruct((n, D_MODEL), F32), jax.ShapeDtypeStruct((n, HK * SC_LANES), F32)],
        scratch_shapes=[pltpu.VMEM((PB, D_MODEL), F32)],
        compiler_params=_cparams(("arbitrary",)),
    )(eid, xn3, g, u_pack, rep.T, rep, rep16)
    parts = []
    if n_tc:
        parts.append(pl.pallas_call(
            _peer_v_kernel,
            grid=(n_tc // PB,),
            in_specs=[smem_slot, flat(D_MODEL), tok, table],
            out_specs=flat(D_MODEL),
            out_shape=jax.ShapeDtypeStruct((n_tc, D_MODEL), F32),
            compiler_params=_cparams(("arbitrary",)),
        )(eid, wrep, x13, v_pack))
    if n_sc:
        expert = eid.reshape(n, 2, HK // 2) // ROW_SUB
        rows = v_pack.reshape(N_EXPERTS, ROW_WORDS)
        parts.append(_sc_v_phase(n_sc, n_tc)(rows, expert, w16, x13).reshape(n_sc, D_MODEL))
    y = parts[0] if len(parts) == 1 else jnp.concatenate(parts, axis=0)
    return y, w16


def _encoder_layer(x, p, after):
    if after is not None:
        x, _ = lax.optimization_barrier((x, after))
    batch, seq, _ = x.shape
    n = batch * seq
    x2d = x.reshape(n, D_MODEL)
    naq, nak, nav, dq, dk, dv = _in_projection(x2d, seq, p["norm1_w"], p["w_in"], p["qk_w"], p["rope"])
    na = _neighbourhood_attention(naq, nak, nav, p["na_bias"], batch, seq)
    df = _diff_attention(dq, dk, dv, p["lam"], p["subln_w"], batch, seq)
    x1, xn, eid, g = _out_and_route(na, df, x2d, p["w_out"], p["norm2_w"], p["w_q"], p["sub_keys"])
    y, marker = _peer_experts(eid, g, xn, x1, p["u_pack"], p["v_pack"], SC_TOKENS.get(n, 0))
    return y.reshape(batch, seq, D_MODEL), marker


def kernel(x_prompt, x_sample, norm1_w, w_in, na_q_norm, na_k_norm, na_rpb, diff_q_norm, diff_k_norm,
           diff_lambda_q1, diff_lambda_k1, diff_lambda_q2, diff_lambda_k2, diff_subln_w, w_out, norm2_w,
           peer_w_q, peer_sub_keys, peer_u, peer_v):
    assert norm1_w.shape[0] == 1, "single-layer problem"
    reps = SECTION // HEAD_DIM
    params = {
        "norm1_w": norm1_w[0],
        "w_in": w_in[0].astype(BF16),
        "qk_w": jnp.stack([jnp.tile(w[0], reps) for w in (na_q_norm, na_k_norm, diff_q_norm, diff_k_norm)]),
        "lam": jnp.stack([diff_lambda_q1[0], diff_lambda_k1[0], diff_lambda_q2[0], diff_lambda_k2[0]]),
        "subln_w": diff_subln_w[0],
        "w_out": w_out[0].astype(BF16),
        "norm2_w": norm2_w[0],
        "w_q": peer_w_q[0].astype(BF16),
        "sub_keys": peer_sub_keys[0].reshape(2 * PEER_HEADS, N_KEYS, N_KEYS).astype(BF16),
        "na_bias": _na_bias(na_rpb[0]),
        "u_pack": _pack_table(peer_u[0]),
        "v_pack": _pack_table(peer_v[0]),
    }
    rope_by_seq = {}
    outs = []
    marker = None
    for x in (x_prompt, x_sample):
        seq = x.shape[1]
        if seq not in rope_by_seq:
            rope_by_seq[seq] = _rope_tables(seq)
        y, marker = _encoder_layer(x, dict(params, rope=rope_by_seq[seq]), marker)
        outs.append(y)
    return tuple(outs)
```

```python
import functools
import math

import jax
import jax.numpy as jnp
import numpy as np
from jax import lax
from jax.experimental import pallas as pl
from jax.experimental.pallas import tpu as pltpu
from jax.experimental.pallas import tpu_sc as plsc

F32 = jnp.float32
BF16 = jnp.bfloat16

D_MODEL = 1024
GRID_W = 64
HEAD_DIM = 64
NA_HEADS = 8
NA_KH = 8
NA_KW = 16
DIFF_HEADS = 4
SECTION = 512
ROT_DIM = HEAD_DIM // 4
ROPE_THETA = 500000.0
PEER_HEADS = 8
N_KEYS = 128
PEER_TOPK = 16
N_EXPERTS = N_KEYS * N_KEYS
HK = PEER_HEADS * PEER_TOPK
EPS = 1e-6
NEG_BIG = -1e30
LOG2E = math.log2(math.e)
LAMBDA_INIT = 0.8 - 0.6 * math.exp(-0.3 * 0)

LANES = 128
SUBLANES = 8
VMEM_LIMIT = 56 * 1024 * 1024

TM_IN = 512
NA_ROWS_PER_STEP = 8
NA_ROW_UNROLL = 2
TQ = 512
KV_SUBCHUNKS = 1
TK = 4096
TM_OUT = 256
TB = 128
PB = 256
SC_TOKENS = {8 * 4096: 8 * 4096}


def _cparams(sem):
    return pltpu.CompilerParams(dimension_semantics=sem, vmem_limit_bytes=VMEM_LIMIT)


def _const_spec(shape):
    nd = len(shape)
    return pl.BlockSpec(shape, lambda *_: (0,) * nd)


def _inproj_kernel(x_ref, n1_ref, w_ref, bd_ref, nw_ref, rc_ref, ra_ref, rb_ref,
                   naq_ref, nak_ref, nav_ref, dq_ref, dk_ref, dv_ref):
    x = x_ref[...]
    ms = jnp.mean(x * x, axis=-1, keepdims=True)
    h = (x * lax.rsqrt(ms + EPS) * n1_ref[...]).astype(BF16)
    bd = bd_ref[...]

    def proj(c):
        return jnp.dot(h, w_ref[:, c * SECTION:(c + 1) * SECTION], preferred_element_type=F32)

    def qknorm(y, row):
        sq = y * y
        hi = sq.astype(BF16)
        lo = (sq - hi.astype(F32)).astype(BF16)
        msq = (jnp.dot(hi, bd, preferred_element_type=F32)
               + jnp.dot(lo, bd, preferred_element_type=F32))
        return y * lax.rsqrt(msq + EPS) * nw_ref[row:row + 1, :]

    def rope(y):
        return (y * rc_ref[...]
                + pltpu.roll(y, SECTION - ROT_DIM // 2, axis=1) * ra_ref[...]
                + pltpu.roll(y, ROT_DIM // 2, axis=1) * rb_ref[...])

    scale = HEAD_DIM ** -0.5 * LOG2E
    naq_ref[...] = (qknorm(proj(0), 0) * scale).astype(BF16)
    nak_ref[...] = qknorm(proj(1), 1).astype(BF16)
    nav_ref[...] = proj(2).astype(BF16)
    dq_ref[...] = (rope(qknorm(proj(3), 2)) * scale).astype(BF16)
    dk_ref[...] = rope(qknorm(proj(4), 3)).astype(BF16)
    dv_ref[...] = proj(5).astype(BF16)


def _rope_tables(seq):
    pos = jnp.arange(seq, dtype=F32)
    inv = ROPE_THETA ** (-jnp.arange(0, ROT_DIM, 2, dtype=F32) / ROT_DIM)
    ang = pos[:, None] * inv[None, :]
    cos, sin = jnp.cos(ang), jnp.sin(ang)
    half = ROT_DIM // 2
    pad = HEAD_DIM - ROT_DIM
    c_head = jnp.concatenate([cos, cos, jnp.ones((seq, pad), F32)], -1)
    a_head = jnp.concatenate([-sin, jnp.zeros((seq, pad + half), F32)], -1)
    b_head = jnp.concatenate([jnp.zeros((seq, half), F32), sin, jnp.zeros((seq, pad), F32)], -1)
    reps = SECTION // HEAD_DIM
    return (jnp.tile(c_head, (1, reps)), jnp.tile(a_head, (1, reps)), jnp.tile(b_head, (1, reps)))


def _in_projection(x2d, seq, norm1_w, w_in_bf, qk_w, rope_tabs):
    n = x2d.shape[0]
    tm = min(TM_IN, seq)
    steps_per_seq = seq // tm
    gid = np.arange(SECTION) // HEAD_DIM
    bd = jnp.asarray((gid[:, None] == gid[None, :]).astype(np.float32) / HEAD_DIM, BF16)
    tok = pl.BlockSpec((tm, SECTION), lambda i: (i, 0))
    pos = pl.BlockSpec((tm, SECTION), lambda i: (i % steps_per_seq, 0))
    out = jax.ShapeDtypeStruct((n, SECTION), BF16)
    return pl.pallas_call(
        _inproj_kernel,
        grid=(n // tm,),
        in_specs=[pl.BlockSpec((tm, D_MODEL), lambda i: (i, 0)),
                  _const_spec((1, D_MODEL)),
                  _const_spec((D_MODEL, 6 * SECTION)),
                  _const_spec((SECTION, SECTION)),
                  _const_spec((4, SECTION)),
                  pos, pos, pos],
        out_specs=[tok] * 6,
        out_shape=[out] * 6,
        compiler_params=_cparams(("parallel",)),
    )(x2d, norm1_w.reshape(1, D_MODEL), w_in_bf, bd, qk_w, *rope_tabs)


def _na_bias(rpb):
    c = np.arange(GRID_W)[:, None]
    kc = np.arange(GRID_W)[None, :]
    cs = np.clip(c - NA_KW // 2, 0, GRID_W - NA_KW)
    valid = (kc >= cs) & (kc < cs + NA_KW)
    rel_c = kc - c + (NA_KW - 1)
    onehot = ((np.arange(2 * NA_KW - 1)[:, None, None] == rel_c[None]) & valid[None]).astype(np.float32)
    col = jnp.einsum("hrx,xck->hrck", rpb.astype(F32), onehot, precision=lax.Precision.HIGHEST)
    col = jnp.where(valid, col * LOG2E, NEG_BIG)
    per_off = [col[:, NA_KH - 1 - o:2 * NA_KH - 1 - o] for o in range(NA_KH)]
    b = jnp.stack(per_off).transpose(0, 1, 3, 2, 4)
    return b.reshape(NA_KH, NA_HEADS, GRID_W, NA_KH * GRID_W)


def _na_kernel(q_ref, k_ref, v_ref, b_ref, o_ref, *, rows):
    j = pl.program_id(1)
    win = NA_KH * GRID_W
    pair = 2 * HEAD_DIM
    lane = lax.broadcasted_iota(jnp.int32, (GRID_W, pair), 1)
    first = lane < HEAD_DIM
    ones = jnp.ones((win, pair), BF16)

    def one_row(rr):
        r = j * NA_ROWS_PER_STEP + rr
        rs = jnp.clip(r - NA_KH // 2, 0, rows - NA_KH)
        off = r - rs
        kstart = pl.multiple_of(rs * GRID_W, GRID_W)
        qstart = pl.multiple_of(rr * GRID_W, GRID_W)
        outs = []
        for hp in range(NA_HEADS // 2):
            sl = slice(hp * pair, (hp + 1) * pair)
            qp = q_ref[pl.ds(qstart, GRID_W), sl]
            kp = k_ref[pl.ds(kstart, win), sl]
            v_ext = jnp.concatenate([v_ref[pl.ds(kstart, win), sl], ones], axis=1)
            halves = []
            for hh in range(2):
                qm = jnp.where(first if hh == 0 else ~first, qp, jnp.zeros_like(qp))
                s = lax.dot_general(qm, kp, (((1,), (1,)), ((), ())), preferred_element_type=F32)
                s = s + b_ref[off, 2 * hp + hh]
                p = jnp.exp2(s - jnp.max(s, axis=-1, keepdims=True)).astype(BF16)
                z = jnp.dot(p, v_ext, preferred_element_type=F32)
                halves.append(z[:, :pair] / z[:, pair:pair + 1])
            outs.append(jnp.where(first, halves[0], halves[1]))
        o_ref[pl.ds(qstart, GRID_W), :] = jnp.concatenate(outs, axis=-1).astype(BF16)

    def row_body(i, carry):
        for u in range(NA_ROW_UNROLL):
            one_row(i * NA_ROW_UNROLL + u)
        return carry

    lax.fori_loop(0, NA_ROWS_PER_STEP // NA_ROW_UNROLL, row_body, 0)


def _neighbourhood_attention(q, k, v, bias, batch, seq):
    rows = seq // GRID_W
    assert rows >= NA_KH and rows % NA_ROWS_PER_STEP == 0
    nblk = rows // NA_ROWS_PER_STEP
    tq = NA_ROWS_PER_STEP * GRID_W
    qspec = pl.BlockSpec((tq, SECTION), lambda b, j: (b * nblk + j, 0))
    kvspec = pl.BlockSpec((seq, SECTION), lambda b, j: (b, 0))
    return pl.pallas_call(
        functools.partial(_na_kernel, rows=rows),
        grid=(batch, nblk),
        in_specs=[qspec, kvspec, kvspec, _const_spec(bias.shape)],
        out_specs=qspec,
        out_shape=jax.ShapeDtypeStruct(q.shape, BF16),
        compiler_params=_cparams(("parallel", "arbitrary")),
    )(q, k, v, bias)


def _diff_kernel(q_ref, k_ref, v_ref, lam_ref, sub_ref, o_ref, m0_scr, m1_scr, acc0_scr, acc1_scr,
                 *, seq, tq, tk):
    m_scrs, acc_scrs = (m0_scr, m1_scr), (acc0_scr, acc1_scr)
    q = q_ref[...]
    lane = lax.broadcasted_iota(jnp.int32, q.shape, 1)
    zero = jnp.zeros_like(q)
    q_maps = (jnp.where(lane < HEAD_DIM, q, zero), jnp.where(lane >= HEAD_DIM, q, zero))

    lp = lam_ref[...]
    lam = (jnp.exp(jnp.sum(lp[0:1] * lp[1:2], axis=-1, keepdims=True))
           - jnp.exp(jnp.sum(lp[2:3] * lp[3:4], axis=-1, keepdims=True)) + LAMBDA_INIT)

    width = 2 * HEAD_DIM
    ones = jnp.ones((tk, width), BF16)

    for m_scr, acc_scr in zip(m_scrs, acc_scrs):
        m_scr[...] = jnp.full(m_scr.shape, -jnp.inf, F32)
        acc_scr[...] = jnp.zeros(acc_scr.shape, F32)

    def kv_body(c, carry):
        tks = tk // KV_SUBCHUNKS
        kcs, v_exts = [], []
        for sub in range(KV_SUBCHUNKS):
            start = pl.multiple_of(c * tk + sub * tks, tks)
            kcs.append(k_ref[pl.ds(start, tks), :])
            v_exts.append(jnp.concatenate([v_ref[pl.ds(start, tks), :], ones[:tks]], axis=1))
        scores = [[lax.dot_general(q_maps[mp], kcs[sub], (((1,), (1,)), ((), ())),
                                   preferred_element_type=F32) for mp in range(2)]
                  for sub in range(KV_SUBCHUNKS)]
        m = [m_scrs[mp][...] for mp in range(2)]
        probs, alphas = [], []
        for sub in range(KV_SUBCHUNKS):
            probs.append([])
            alphas.append([])
            for mp in range(2):
                mn = jnp.maximum(m[mp], jnp.max(scores[sub][mp], axis=-1, keepdims=True))
                probs[sub].append(jnp.exp2(scores[sub][mp] - mn[:, 0:1]).astype(BF16))
                alpha = jnp.exp2(m[mp] - mn)
                alphas[sub].append(jnp.concatenate([alpha, alpha], axis=1))
                m[mp] = mn
        for mp in range(2):
            m_scrs[mp][...] = m[mp]
            acc = acc_scrs[mp][...]
            for sub in range(KV_SUBCHUNKS):
                acc = alphas[sub][mp] * acc + jnp.dot(probs[sub][mp], v_exts[sub],
                                                      preferred_element_type=F32)
            acc_scrs[mp][...] = acc
        return carry

    lax.fori_loop(0, seq // tk, kv_body, 0)
    a0, a1 = acc0_scr[...], acc1_scr[...]
    o = a0[:, :width] / a0[:, width:width + 1] - lam * (a1[:, :width] / a1[:, width:width + 1])
    ms = jnp.mean(o * o, axis=-1, keepdims=True)
    y = o * lax.rsqrt(ms + EPS) * sub_ref[...]
    o_ref[...] = (y * (1.0 - LAMBDA_INIT)).astype(BF16)


def _diff_attention(q, k, v, lam_params, subln_w, batch, seq):
    tq, tk = min(TQ, seq), min(TK, seq)
    nq = seq // tq
    width = 2 * HEAD_DIM
    qspec = pl.BlockSpec((tq, width), lambda b, h, i: (b * nq + i, h))
    kvspec = pl.BlockSpec((seq, width), lambda b, h, i: (b, h))
    return pl.pallas_call(
        functools.partial(_diff_kernel, seq=seq, tq=tq, tk=tk),
        grid=(batch, DIFF_HEADS, nq),
        in_specs=[qspec, kvspec, kvspec, _const_spec((4, HEAD_DIM)), _const_spec((1, width))],
        out_specs=qspec,
        out_shape=jax.ShapeDtypeStruct(q.shape, BF16),
        scratch_shapes=[pltpu.VMEM((tq, width), F32), pltpu.VMEM((tq, width), F32),
                        pltpu.VMEM((tq, 2 * width), F32), pltpu.VMEM((tq, 2 * width), F32)],
        compiler_params=_cparams(("parallel", "parallel", "arbitrary")),
    )(q, k, v, lam_params, subln_w.reshape(1, width))


def _top16(sc, payload=None):
    iota = lax.broadcasted_iota(jnp.int32, sc.shape, 0).astype(F32)
    big = float(sc.shape[0])
    vals, picks = [], []
    for _ in range(PEER_TOPK):
        m = jnp.max(sc, axis=0, keepdims=True)
        ix = jnp.min(jnp.where(sc == m, iota, big), axis=0, keepdims=True)
        sel = iota == ix
        if payload is None:
            picks.append(ix)
        else:
            picks.append(jnp.sum(jnp.where(sel, payload, 0), axis=0, keepdims=True))
        sc = jnp.where(sel, -jnp.inf, sc)
        vals.append(m)
    picks = jnp.concatenate(picks, axis=0)
    return jnp.concatenate(vals, axis=0), picks.astype(jnp.int32)


def _pair_candidates(r0, r1):
    half = SUBLANES // 2
    groups = [r0[0:1] + r1[0:8], r0[0:1] + r1[8:16]]
    groups += [r0[a:a + 1] + r1[0:8] for a in (1, 2, 3)]
    groups += [jnp.concatenate([r0[a:a + 1] + r1[0:half], r0[a + 1:a + 2] + r1[0:half]], axis=0)
               for a in (4, 6)]
    groups.append(r0[8:16] + r1[0:1])
    return jnp.concatenate(groups, axis=0)


def _route_kernel(na_ref, df_ref, x_ref, wo_ref, n2_ref, wq_ref, sk_ref,
                  x1_ref, xn_ref, eid_ref, g_ref, q_scr, eid_scr, g_scr):
    tm = x_ref.shape[0]
    x1 = (x_ref[...]
          + jnp.dot(na_ref[...], wo_ref[0:SECTION, :], preferred_element_type=F32)
          + jnp.dot(df_ref[...], wo_ref[SECTION:2 * SECTION, :], preferred_element_type=F32))
    ms = jnp.mean(x1 * x1, axis=-1, keepdims=True)
    xn = x1 * lax.rsqrt(ms + EPS) * n2_ref[...]
    for r in range(SUBLANES):
        x1_ref[:, r, :] = x1[:, r * LANES:(r + 1) * LANES]
        xn_ref[:, r, :] = xn[:, r * LANES:(r + 1) * LANES]
    xb = xn.astype(BF16)
    for c in range(2 * PEER_HEADS):
        q_scr[c] = jnp.dot(xb, wq_ref[:, c * LANES:(c + 1) * LANES],
                           preferred_element_type=F32).astype(BF16)

    nblk = tm // TB

    def route_block(h, blk):
        tok0 = blk * TB
        tops = []
        for p in range(2):
            sc = lax.dot_general(sk_ref[2 * h + p], q_scr[2 * h + p, pl.ds(tok0, TB), :],
                                 (((1,), (1,)), ((), ())), preferred_element_type=F32)
            tops.append(_top16(sc))
        (s0, i0), (s1, i1) = tops
        cand_s = _pair_candidates(s0, s1)
        cand_i = _pair_candidates(i0 * N_KEYS, i1)
        top_s, eid = _top16(cand_s, cand_i)
        e = jnp.exp(top_s - top_s[0:1])
        row = pl.multiple_of(h * PEER_TOPK, PEER_TOPK)
        eid_scr[blk, pl.ds(row, PEER_TOPK), :] = eid * ROW_SUB
        g_scr[blk, pl.ds(row, PEER_TOPK), :] = e / jnp.sum(e, axis=0, keepdims=True)

    def head_body(h, carry):
        for blk in range(nblk):
            route_block(h, blk)
        return carry

    lax.fori_loop(0, PEER_HEADS, head_body, 0)
    for blk in range(nblk):
        eid_ref[blk] = eid_scr[blk].T
        g_ref[blk * TB:(blk + 1) * TB, :] = g_scr[blk].T


def _out_and_route(na, df, x2d, w_out_bf, norm2_w, w_q_bf, sk_bf):
    n = x2d.shape[0]
    tm = TM_OUT
    tok = lambda w: pl.BlockSpec((tm, w), lambda i: (i, 0))
    tiles = pl.BlockSpec((tm, SUBLANES, LANES), lambda i: (i, 0, 0))
    return pl.pallas_call(
        _route_kernel,
        grid=(n // tm,),
        in_specs=[tok(SECTION), tok(SECTION), tok(D_MODEL),
                  _const_spec((D_MODEL, D_MODEL)), _const_spec((1, D_MODEL)),
                  _const_spec(w_q_bf.shape), _const_spec(sk_bf.shape)],
        out_specs=[tiles, tiles,
                   pl.BlockSpec((tm // TB, TB, HK), lambda i: (i, 0, 0)), tok(HK)],
        out_shape=[jax.ShapeDtypeStruct((n, SUBLANES, LANES), F32),
                   jax.ShapeDtypeStruct((n, SUBLANES, LANES), F32),
                   jax.ShapeDtypeStruct((n // TB, TB, HK), jnp.int32),
                   jax.ShapeDtypeStruct((n, HK), F32)],
        scratch_shapes=[pltpu.VMEM((2 * PEER_HEADS, tm, LANES), BF16),
                        pltpu.VMEM((tm // TB, HK, TB), jnp.int32), pltpu.VMEM((tm // TB, HK, TB), F32)],
        compiler_params=_cparams(("parallel",)),
    )(na, df, x2d, w_out_bf, norm2_w.reshape(1, D_MODEL), w_q_bf, sk_bf)


ROW_SUB = SUBLANES // 2
CHUNK_SLOTS = 32
N_CHUNKS = HK // CHUNK_SLOTS
CHUNK_ROWS = CHUNK_SLOTS * SUBLANES
TOKEN_UNROLL = 8
ROW_WORDS = ROW_SUB * LANES
SC_CORES, SC_SUBCORES, SC_LANES = 2, 16, 16
SC_WORKERS = SC_CORES * SC_SUBCORES


def _pack_table(tab):
    bits = lax.bitcast_convert_type(tab.astype(BF16), jnp.uint16).astype(jnp.uint32)
    bits = bits.reshape(tab.shape[0], ROW_SUB, 2, LANES)
    return (bits[:, :, 0, :] | (bits[:, :, 1, :] << 16)).reshape(tab.shape[0] * ROW_SUB, LANES)


def _split_bf16(x):
    hi = x.astype(BF16)
    return hi, (x - hi.astype(F32)).astype(BF16)


def _gather_chunk(tab_ref, eid_ref, t, chunk):
    ids = eid_ref.at[0, t]
    rows = [tab_ref[pl.ds(pl.multiple_of(ids[chunk * CHUNK_SLOTS + i], ROW_SUB), ROW_SUB), :]
            for i in range(CHUNK_SLOTS)]
    return pltpu.bitcast(jnp.concatenate(rows, axis=0), BF16)


def _diag_mask(width):
    r = lax.broadcasted_iota(jnp.int32, (SUBLANES, width), 0)
    n = lax.broadcasted_iota(jnp.int32, (SUBLANES, width), 1)
    return (n % SUBLANES) == r


def _peer_u_kernel(eid_ref, xn_ref, g_ref, tab_ref, rept_ref, rep_ref, rep16_ref, wrep_ref, w16_ref, d_scr):
    mask = _diag_mask(CHUNK_ROWS)
    zeros = jnp.zeros((SUBLANES, LANES), BF16)

    def one_token(t):
        xhi, xlo = _split_bf16(xn_ref[t])
        lhs = jnp.concatenate([jnp.concatenate([xhi, zeros], axis=1),
                               jnp.concatenate([xlo, zeros], axis=1),
                               jnp.concatenate([zeros, xhi], axis=1),
                               jnp.concatenate([zeros, xlo], axis=1)], axis=0)
        for pr in range(N_CHUNKS // 2):
            s = jnp.concatenate([_gather_chunk(tab_ref, eid_ref, t, 2 * pr),
                                 _gather_chunk(tab_ref, eid_ref, t, 2 * pr + 1)], axis=1)
            z = lax.dot_general(lhs, s, (((1,), (1,)), ((), ())), preferred_element_type=F32)
            for half in range(2):
                z8 = z[16 * half:16 * half + 8] + z[16 * half + 8:16 * half + 16]
                zs = jnp.sum(jnp.where(mask, z8, 0.0), axis=0, keepdims=True)
                d_scr[pl.ds(t, 1), pl.ds((2 * pr + half) * CHUNK_ROWS, CHUNK_ROWS)] = zs

    def tok_body(i, carry):
        for u in range(TOKEN_UNROLL):
            one_token(i * TOKEN_UNROLL + u)
        return carry

    lax.fori_loop(0, PB // TOKEN_UNROLL, tok_body, 0)
    dhi, dlo = _split_bf16(d_scr[...])
    a = (jnp.dot(dhi, rept_ref[...], preferred_element_type=F32)
         + jnp.dot(dlo, rept_ref[...], preferred_element_type=F32))
    w = g_ref[...] * (0.5 * a * (1.0 + lax.erf(a * (2.0 ** -0.5))))
    whi, wlo = _split_bf16(w)
    wrep_ref[...] = (jnp.dot(whi, rep_ref[...], preferred_element_type=F32)
                     + jnp.dot(wlo, rep_ref[...], preferred_element_type=F32))
    w16_ref[...] = (jnp.dot(whi, rep16_ref[...], preferred_element_type=F32)
                    + jnp.dot(wlo, rep16_ref[...], preferred_element_type=F32))


def _peer_v_kernel(eid_ref, wrep_ref, x1_ref, tab_ref, o_ref):
    mask = _diag_mask(CHUNK_ROWS)

    def one_token(t):
        wr = wrep_ref[pl.ds(t, 1), :]
        acc = x1_ref[t]
        for pr in range(N_CHUNKS // 2):
            parts = []
            for half in range(2):
                seg = wr[:, (2 * pr + half) * CHUNK_ROWS:(2 * pr + half + 1) * CHUNK_ROWS]
                parts.extend(_split_bf16(jnp.where(mask, jnp.broadcast_to(seg, mask.shape), 0.0)))
            lhs = jnp.concatenate(parts, axis=0)
            s = jnp.concatenate([_gather_chunk(tab_ref, eid_ref, t, 2 * pr),
                                 _gather_chunk(tab_ref, eid_ref, t, 2 * pr + 1)], axis=1)
            z = jnp.dot(lhs, s, preferred_element_type=F32)
            acc = acc + ((z[0:8, :LANES] + z[8:16, :LANES]) + (z[16:24, LANES:] + z[24:32, LANES:]))
        return acc

    def tok_body(i, carry):
        accs = [one_token(i * TOKEN_UNROLL + u) for u in range(TOKEN_UNROLL)]
        base = pl.multiple_of(i * TOKEN_UNROLL, TOKEN_UNROLL)
        for r in range(SUBLANES):
            o_ref[pl.ds(base, TOKEN_UNROLL), r * LANES:(r + 1) * LANES] = jnp.concatenate(
                [acc[r:r + 1, :] for acc in accs], axis=0)
        return carry

    lax.fori_loop(0, PB // TOKEN_UNROLL, tok_body, 0)


def _sc_v_phase(n_tok, tok0):
    per_w = n_tok // SC_WORKERS
    assert per_w * SC_WORKERS == n_tok and per_w % 2 == 0
    half = HK // 2
    mesh = plsc.VectorSubcoreMesh(core_axis_name="c", subcore_axis_name="s")

    def small():
        return [pltpu.VMEM((2, half), jnp.int32), pltpu.VMEM((HK * SC_LANES,), F32),
                pltpu.VMEM((SUBLANES, LANES), F32)]

    @functools.partial(
        pl.kernel, mesh=mesh,
        out_type=jax.ShapeDtypeStruct((n_tok, SUBLANES, LANES), F32),
        scratch_types=small() + small() + [
            pltpu.VMEM((half, ROW_WORDS), jnp.uint32), pltpu.VMEM((half, ROW_WORDS), jnp.uint32),
            pltpu.SemaphoreType.DMA, pltpu.SemaphoreType.DMA,
            pltpu.SemaphoreType.DMA, pltpu.SemaphoreType.DMA,
            pltpu.SemaphoreType.DMA, pltpu.SemaphoreType.DMA,
        ],
        compiler_params=pltpu.CompilerParams(needs_layout_passes=False),
    )
    def sc_kernel(tab_hbm, eid_hbm, w16_hbm, x1_hbm, out_hbm,
                  idx_a, w_a, x_a, idx_b, w_b, x_b, g0, g1, s_a, s_b, sg0, sg1, so_a, so_b):
        wid = lax.axis_index("s") * SC_CORES + lax.axis_index("c")
        first = wid * per_w
        par = ((idx_a, w_a, x_a, s_a, so_a), (idx_b, w_b, x_b, s_b, so_b))
        gbuf, gsem = (g0, g1), (sg0, sg1)

        def input_copies(t, p):
            idx_v, w_v, x_v, sem, _ = par[p]
            return (pltpu.make_async_copy(eid_hbm.at[tok0 + t], idx_v, sem),
                    pltpu.make_async_copy(w16_hbm.at[tok0 + t], w_v, sem),
                    pltpu.make_async_copy(x1_hbm.at[tok0 + t], x_v, sem))

        def gather(p, h):
            return pltpu.make_async_copy(tab_hbm.at[par[p][0].at[h]], gbuf[h], gsem[h])

        def out_copy(t, p):
            return pltpu.make_async_copy(par[p][2], out_hbm.at[t], par[p][4])

        def accumulate_half(p, h):
            _, w_v, x_v, _, _ = par[p]
            g = gbuf[h]
            groups = LANES // SC_LANES
            for c in range(ROW_SUB):
                lo0 = tuple(x_v[2 * c, pl.ds(SC_LANES * kk, SC_LANES)] for kk in range(groups))
                hi0 = tuple(x_v[2 * c + 1, pl.ds(SC_LANES * kk, SC_LANES)] for kk in range(groups))

                def row_body(j, acc):
                    lo, hi = acc
                    wj = w_v[pl.ds(pl.multiple_of((h * half + j) * SC_LANES, SC_LANES), SC_LANES)]
                    nlo, nhi = [], []
                    for kk in range(groups):
                        words = g[j, pl.ds(LANES * c + SC_LANES * kk, SC_LANES)]
                        lov = plsc.bitcast(lax.shift_left(words, jnp.uint32(16)), F32)
                        hiv = plsc.bitcast(words & jnp.uint32(0xFFFF0000), F32)
                        nlo.append(lo[kk] + wj * lov)
                        nhi.append(hi[kk] + wj * hiv)
                    return tuple(nlo), tuple(nhi)

                lo, hi = lax.fori_loop(0, half, row_body, (lo0, hi0))
                for kk in range(groups):
                    x_v[2 * c, pl.ds(SC_LANES * kk, SC_LANES)] = lo[kk]
                    x_v[2 * c + 1, pl.ds(SC_LANES * kk, SC_LANES)] = hi[kk]

        def token(t, p):
            q = 1 - p
            has_next = t + 1 < first + per_w
            gather(p, 0).wait()
            gather(p, 1).start()

            @pl.when(has_next)
            def _():
                @pl.when(t >= first + 1)
                def _():
                    out_copy(t - 1, q).wait()
                for cp in input_copies(t + 1, q):
                    cp.start()

            accumulate_half(p, 0)
            gather(p, 1).wait()

            @pl.when(has_next)
            def _():
                for cp in input_copies(t + 1, q):
                    cp.wait()
                gather(q, 0).start()

            accumulate_half(p, 1)
            out_copy(t, p).start()

        for cp in input_copies(first, 0):
            cp.start()
        for cp in input_copies(first, 0):
            cp.wait()
        gather(0, 0).start()

        def pair_body(i, carry):
            token(first + 2 * i, 0)
            token(first + 2 * i + 1, 1)
            return carry

        lax.fori_loop(0, per_w // 2, pair_body, 0)
        out_copy(first + per_w - 2, 0).wait()
        out_copy(first + per_w - 1, 1).wait()

    return sc_kernel


def _peer_experts(eid, g, xn3, x13, u_pack, v_pack, n_sc):
    n = xn3.shape[0]
    n_tc = n - n_sc
    assert n_tc % PB == 0
    nblk = n // PB
    eid = eid.reshape(nblk, PB, HK)
    smem_slot = pl.BlockSpec((1, PB, HK), lambda i: (i, 0, 0), memory_space=pltpu.SMEM)
    tok = pl.BlockSpec((PB, SUBLANES, LANES), lambda i: (i, 0, 0))
    flat = lambda w: pl.BlockSpec((PB, w), lambda i: (i, 0))
    table = pl.BlockSpec(u_pack.shape, lambda i: (0, 0), pipeline_mode=pl.Buffered(1))

    def repeat_matrix(times):
        m = (np.arange(HK)[:, None] == (np.arange(HK * times)[None, :] // times)).astype(np.float32)
        return jnp.asarray(m, BF16)

    rep, rep16 = repeat_matrix(SUBLANES), repeat_matrix(SC_LANES)
    wrep, w16 = pl.pallas_call(
        _peer_u_kernel,
        grid=(nblk,),
        in_specs=[smem_slot, tok, flat(HK), table, _const_spec(rep.T.shape), _const_spec(rep.shape),
                  _const_spec(rep16.shape)],
        out_specs=[flat(D_MODEL), flat(HK * SC_LANES)],
        out_shape=[jax.ShapeDtypeStruct((n, D_MODEL), F32), jax.ShapeDtypeStruct((n, HK * SC_LANES), F32)],
        scratch_shapes=[pltpu.VMEM((PB, D_MODEL), F32)],
        compiler_params=_cparams(("arbitrary",)),
    )(eid, xn3, g, u_pack, rep.T, rep, rep16)
    parts = []
    if n_tc:
        parts.append(pl.pallas_call(
            _peer_v_kernel,
            grid=(n_tc // PB,),
            in_specs=[smem_slot, flat(D_MODEL), tok, table],
            out_specs=flat(D_MODEL),
            out_shape=jax.ShapeDtypeStruct((n_tc, D_MODEL), F32),
            compiler_params=_cparams(("arbitrary",)),
        )(eid, wrep, x13, v_pack))
    if n_sc:
        expert = eid.reshape(n, 2, HK // 2) // ROW_SUB
        rows = v_pack.reshape(N_EXPERTS, ROW_WORDS)
        parts.append(_sc_v_phase(n_sc, n_tc)(rows, expert, w16, x13).reshape(n_sc, D_MODEL))
    y = parts[0] if len(parts) == 1 else jnp.concatenate(parts, axis=0)
    return y, w16


def _encoder_layer(x, p, after):
    if after is not None:
        x, _ = lax.optimization_barrier((x, after))
    batch, seq, _ = x.shape
    n = batch * seq
    x2d = x.reshape(n, D_MODEL)
    naq, nak, nav, dq, dk, dv = _in_projection(x2d, seq, p["norm1_w"], p["w_in"], p["qk_w"], p["rope"])
    na = _neighbourhood_attention(naq, nak, nav, p["na_bias"], batch, seq)
    df = _diff_attention(dq, dk, dv, p["lam"], p["subln_w"], batch, seq)
    x1, xn, eid, g = _out_and_route(na, df, x2d, p["w_out"], p["norm2_w"], p["w_q"], p["sub_keys"])
    y, marker = _peer_experts(eid, g, xn, x1, p["u_pack"], p["v_pack"], SC_TOKENS.get(n, 0))
    return y.reshape(batch, seq, D_MODEL), marker


def kernel(x_prompt, x_sample, norm1_w, w_in, na_q_norm, na_k_norm, na_rpb, diff_q_norm, diff_k_norm,
           diff_lambda_q1, diff_lambda_k1, diff_lambda_q2, diff_lambda_k2, diff_subln_w, w_out, norm2_w,
           peer_w_q, peer_sub_keys, peer_u, peer_v):
    assert norm1_w.shape[0] == 1, "single-layer problem"
    reps = SECTION // HEAD_DIM
    params = {
        "norm1_w": norm1_w[0],
        "w_in": w_in[0].astype(BF16),
        "qk_w": jnp.stack([jnp.tile(w[0], reps) for w in (na_q_norm, na_k_norm, diff_q_norm, diff_k_norm)]),
        "lam": jnp.stack([diff_lambda_q1[0], diff_lambda_k1[0], diff_lambda_q2[0], diff_lambda_k2[0]]),
        "subln_w": diff_subln_w[0],
        "w_out": w_out[0].astype(BF16),
        "norm2_w": norm2_w[0],
        "w_q": peer_w_q[0].astype(BF16),
        "sub_keys": peer_sub_keys[0].reshape(2 * PEER_HEADS, N_KEYS, N_KEYS).astype(BF16),
        "na_bias": _na_bias(na_rpb[0]),
        "u_pack": _pack_table(peer_u[0]),
        "v_pack": _pack_table(peer_v[0]),
    }
    rope_by_seq = {}
    outs = []
    marker = None
    for x in (x_prompt, x_sample):
        seq = x.shape[1]
        if seq not in rope_by_seq:
            rope_by_seq[seq] = _rope_tables(seq)
        y, marker = _encoder_layer(x, dict(params, rope=rope_by_seq[seq]), marker)
        outs.append(y)
    return tuple(outs)
```

```python
import functools
import math

import jax
import jax.numpy as jnp
import numpy as np
from jax import lax
from jax.experimental import pallas as pl
from jax.experimental.pallas import tpu as pltpu
from jax.experimental.pallas import tpu_sc as plsc

F32 = jnp.float32
BF16 = jnp.bfloat16

D_MODEL = 1024
GRID_W = 64
HEAD_DIM = 64
NA_HEADS = 8
NA_KH = 8
NA_KW = 16
DIFF_HEADS = 4
SECTION = 512
ROT_DIM = HEAD_DIM // 4
ROPE_THETA = 500000.0
PEER_HEADS = 8
N_KEYS = 128
PEER_TOPK = 16
N_EXPERTS = N_KEYS * N_KEYS
HK = PEER_HEADS * PEER_TOPK
EPS = 1e-6
NEG_BIG = -1e30
LOG2E = math.log2(math.e)
LAMBDA_INIT = 0.8 - 0.6 * math.exp(-0.3 * 0)

LANES = 128
SUBLANES = 8
VMEM_LIMIT = 56 * 1024 * 1024

TM_IN = 512
NA_ROWS_PER_STEP = 8
NA_ROW_UNROLL = 2
TQ = 512
KV_SUBCHUNKS = 1
TK = 4096
TM_OUT = 256
TB = 128
PB = 256
SC_TOKENS = {8 * 4096: (14336, 8 * 4096)}


def _cparams(sem):
    return pltpu.CompilerParams(dimension_semantics=sem, vmem_limit_bytes=VMEM_LIMIT)


def _const_spec(shape):
    nd = len(shape)
    return pl.BlockSpec(shape, lambda *_: (0,) * nd)


def _inproj_kernel(x_ref, n1_ref, w_ref, bd_ref, nw_ref, rc_ref, ra_ref, rb_ref,
                   naq_ref, nak_ref, nav_ref, dq_ref, dk_ref, dv_ref):
    x = x_ref[...]
    ms = jnp.mean(x * x, axis=-1, keepdims=True)
    h = (x * lax.rsqrt(ms + EPS) * n1_ref[...]).astype(BF16)
    bd = bd_ref[...]

    def proj(c):
        return jnp.dot(h, w_ref[:, c * SECTION:(c + 1) * SECTION], preferred_element_type=F32)

    def qknorm(y, row):
        sq = y * y
        hi = sq.astype(BF16)
        lo = (sq - hi.astype(F32)).astype(BF16)
        msq = (jnp.dot(hi, bd, preferred_element_type=F32)
               + jnp.dot(lo, bd, preferred_element_type=F32))
        return y * lax.rsqrt(msq + EPS) * nw_ref[row:row + 1, :]

    def rope(y):
        return (y * rc_ref[...]
                + pltpu.roll(y, SECTION - ROT_DIM // 2, axis=1) * ra_ref[...]
                + pltpu.roll(y, ROT_DIM // 2, axis=1) * rb_ref[...])

    scale = HEAD_DIM ** -0.5 * LOG2E
    naq_ref[...] = (qknorm(proj(0), 0) * scale).astype(BF16)
    nak_ref[...] = qknorm(proj(1), 1).astype(BF16)
    nav_ref[...] = proj(2).astype(BF16)
    dq_ref[...] = (rope(qknorm(proj(3), 2)) * scale).astype(BF16)
    dk_ref[...] = rope(qknorm(proj(4), 3)).astype(BF16)
    dv_ref[...] = proj(5).astype(BF16)


def _rope_tables(seq):
    pos = jnp.arange(seq, dtype=F32)
    inv = ROPE_THETA ** (-jnp.arange(0, ROT_DIM, 2, dtype=F32) / ROT_DIM)
    ang = pos[:, None] * inv[None, :]
    cos, sin = jnp.cos(ang), jnp.sin(ang)
    half = ROT_DIM // 2
    pad = HEAD_DIM - ROT_DIM
    c_head = jnp.concatenate([cos, cos, jnp.ones((seq, pad), F32)], -1)
    a_head = jnp.concatenate([-sin, jnp.zeros((seq, pad + half), F32)], -1)
    b_head = jnp.concatenate([jnp.zeros((seq, half), F32), sin, jnp.zeros((seq, pad), F32)], -1)
    reps = SECTION // HEAD_DIM
    return (jnp.tile(c_head, (1, reps)), jnp.tile(a_head, (1, reps)), jnp.tile(b_head, (1, reps)))


def _in_projection(x2d, seq, norm1_w, w_in_bf, qk_w, rope_tabs):
    n = x2d.shape[0]
    tm = min(TM_IN, seq)
    steps_per_seq = seq // tm
    gid = np.arange(SECTION) // HEAD_DIM
    bd = jnp.asarray((gid[:, None] == gid[None, :]).astype(np.float32) / HEAD_DIM, BF16)
    tok = pl.BlockSpec((tm, SECTION), lambda i: (i, 0))
    pos = pl.BlockSpec((tm, SECTION), lambda i: (i % steps_per_seq, 0))
    out = jax.ShapeDtypeStruct((n, SECTION), BF16)
    return pl.pallas_call(
        _inproj_kernel,
        grid=(n // tm,),
        in_specs=[pl.BlockSpec((tm, D_MODEL), lambda i: (i, 0)),
                  _const_spec((1, D_MODEL)),
                  _const_spec((D_MODEL, 6 * SECTION)),
                  _const_spec((SECTION, SECTION)),
                  _const_spec((4, SECTION)),
                  pos, pos, pos],
        out_specs=[tok] * 6,
        out_shape=[out] * 6,
        compiler_params=_cparams(("parallel",)),
    )(x2d, norm1_w.reshape(1, D_MODEL), w_in_bf, bd, qk_w, *rope_tabs)


def _na_bias(rpb):
    c = np.arange(GRID_W)[:, None]
    kc = np.arange(GRID_W)[None, :]
    cs = np.clip(c - NA_KW // 2, 0, GRID_W - NA_KW)
    valid = (kc >= cs) & (kc < cs + NA_KW)
    rel_c = kc - c + (NA_KW - 1)
    onehot = ((np.arange(2 * NA_KW - 1)[:, None, None] == rel_c[None]) & valid[None]).astype(np.float32)
    col = jnp.einsum("hrx,xck->hrck", rpb.astype(F32), onehot, precision=lax.Precision.HIGHEST)
    col = jnp.where(valid, col * LOG2E, NEG_BIG)
    per_off = [col[:, NA_KH - 1 - o:2 * NA_KH - 1 - o] for o in range(NA_KH)]
    b = jnp.stack(per_off).transpose(0, 1, 3, 2, 4)
    return b.reshape(NA_KH, NA_HEADS, GRID_W, NA_KH * GRID_W)


def _na_kernel(q_ref, k_ref, v_ref, b_ref, o_ref, *, rows):
    j = pl.program_id(1)
    win = NA_KH * GRID_W
    pair = 2 * HEAD_DIM
    lane = lax.broadcasted_iota(jnp.int32, (GRID_W, pair), 1)
    first = lane < HEAD_DIM
    ones = jnp.ones((win, pair), BF16)

    def one_row(rr):
        r = j * NA_ROWS_PER_STEP + rr
        rs = jnp.clip(r - NA_KH // 2, 0, rows - NA_KH)
        off = r - rs
        kstart = pl.multiple_of(rs * GRID_W, GRID_W)
        qstart = pl.multiple_of(rr * GRID_W, GRID_W)
        outs = []
        for hp in range(NA_HEADS // 2):
            sl = slice(hp * pair, (hp + 1) * pair)
            qp = q_ref[pl.ds(qstart, GRID_W), sl]
            kp = k_ref[pl.ds(kstart, win), sl]
            v_ext = jnp.concatenate([v_ref[pl.ds(kstart, win), sl], ones], axis=1)
            halves = []
            for hh in range(2):
                qm = jnp.where(first if hh == 0 else ~first, qp, jnp.zeros_like(qp))
                s = lax.dot_general(qm, kp, (((1,), (1,)), ((), ())), preferred_element_type=F32)
                s = s + b_ref[off, 2 * hp + hh]
                p = jnp.exp2(s - jnp.max(s, axis=-1, keepdims=True)).astype(BF16)
                z = jnp.dot(p, v_ext, preferred_element_type=F32)
                halves.append(z[:, :pair] / z[:, pair:pair + 1])
            outs.append(jnp.where(first, halves[0], halves[1]))
        o_ref[pl.ds(qstart, GRID_W), :] = jnp.concatenate(outs, axis=-1).astype(BF16)

    def row_body(i, carry):
        for u in range(NA_ROW_UNROLL):
            one_row(i * NA_ROW_UNROLL + u)
        return carry

    lax.fori_loop(0, NA_ROWS_PER_STEP // NA_ROW_UNROLL, row_body, 0)


def _neighbourhood_attention(q, k, v, bias, batch, seq):
    rows = seq // GRID_W
    assert rows >= NA_KH and rows % NA_ROWS_PER_STEP == 0
    nblk = rows // NA_ROWS_PER_STEP
    tq = NA_ROWS_PER_STEP * GRID_W
    qspec = pl.BlockSpec((tq, SECTION), lambda b, j: (b * nblk + j, 0))
    kvspec = pl.BlockSpec((seq, SECTION), lambda b, j: (b, 0))
    return pl.pallas_call(
        functools.partial(_na_kernel, rows=rows),
        grid=(batch, nblk),
        in_specs=[qspec, kvspec, kvspec, _const_spec(bias.shape)],
        out_specs=qspec,
        out_shape=jax.ShapeDtypeStruct(q.shape, BF16),
        compiler_params=_cparams(("parallel", "arbitrary")),
    )(q, k, v, bias)


def _diff_kernel(q_ref, k_ref, v_ref, lam_ref, sub_ref, o_ref, m0_scr, m1_scr, acc0_scr, acc1_scr,
                 *, seq, tq, tk):
    m_scrs, acc_scrs = (m0_scr, m1_scr), (acc0_scr, acc1_scr)
    q = q_ref[...]
    lane = lax.broadcasted_iota(jnp.int32, q.shape, 1)
    zero = jnp.zeros_like(q)
    q_maps = (jnp.where(lane < HEAD_DIM, q, zero), jnp.where(lane >= HEAD_DIM, q, zero))

    lp = lam_ref[...]
    lam = (jnp.exp(jnp.sum(lp[0:1] * lp[1:2], axis=-1, keepdims=True))
           - jnp.exp(jnp.sum(lp[2:3] * lp[3:4], axis=-1, keepdims=True)) + LAMBDA_INIT)

    width = 2 * HEAD_DIM
    ones = jnp.ones((tk, width), BF16)

    for m_scr, acc_scr in zip(m_scrs, acc_scrs):
        m_scr[...] = jnp.full(m_scr.shape, -jnp.inf, F32)
        acc_scr[...] = jnp.zeros(acc_scr.shape, F32)

    def kv_body(c, carry):
        tks = tk // KV_SUBCHUNKS
        kcs, v_exts = [], []
        for sub in range(KV_SUBCHUNKS):
            start = pl.multiple_of(c * tk + sub * tks, tks)
            kcs.append(k_ref[pl.ds(start, tks), :])
            v_exts.append(jnp.concatenate([v_ref[pl.ds(start, tks), :], ones[:tks]], axis=1))
        scores = [[lax.dot_general(q_maps[mp], kcs[sub], (((1,), (1,)), ((), ())),
                                   preferred_element_type=F32) for mp in range(2)]
                  for sub in range(KV_SUBCHUNKS)]
        m = [m_scrs[mp][...] for mp in range(2)]
        probs, alphas = [], []
        for sub in range(KV_SUBCHUNKS):
            probs.append([])
            alphas.append([])
            for mp in range(2):
                mn = jnp.maximum(m[mp], jnp.max(scores[sub][mp], axis=-1, keepdims=True))
                probs[sub].append(jnp.exp2(scores[sub][mp] - mn[:, 0:1]).astype(BF16))
                alpha = jnp.exp2(m[mp] - mn)
                alphas[sub].append(jnp.concatenate([alpha, alpha], axis=1))
                m[mp] = mn
        for mp in range(2):
            m_scrs[mp][...] = m[mp]
            acc = acc_scrs[mp][...]
            for sub in range(KV_SUBCHUNKS):
                acc = alphas[sub][mp] * acc + jnp.dot(probs[sub][mp], v_exts[sub],
                                                      preferred_element_type=F32)
            acc_scrs[mp][...] = acc
        return carry

    lax.fori_loop(0, seq // tk, kv_body, 0)
    a0, a1 = acc0_scr[...], acc1_scr[...]
    o = a0[:, :width] / a0[:, width:width + 1] - lam * (a1[:, :width] / a1[:, width:width + 1])
    ms = jnp.mean(o * o, axis=-1, keepdims=True)
    y = o * lax.rsqrt(ms + EPS) * sub_ref[...]
    o_ref[...] = (y * (1.0 - LAMBDA_INIT)).astype(BF16)


def _diff_attention(q, k, v, lam_params, subln_w, batch, seq):
    tq, tk = min(TQ, seq), min(TK, seq)
    nq = seq // tq
    width = 2 * HEAD_DIM
    qspec = pl.BlockSpec((tq, width), lambda b, h, i: (b * nq + i, h))
    kvspec = pl.BlockSpec((seq, width), lambda b, h, i: (b, h))
    return pl.pallas_call(
        functools.partial(_diff_kernel, seq=seq, tq=tq, tk=tk),
        grid=(batch, DIFF_HEADS, nq),
        in_specs=[qspec, kvspec, kvspec, _const_spec((4, HEAD_DIM)), _const_spec((1, width))],
        out_specs=qspec,
        out_shape=jax.ShapeDtypeStruct(q.shape, BF16),
        scratch_shapes=[pltpu.VMEM((tq, width), F32), pltpu.VMEM((tq, width), F32),
                        pltpu.VMEM((tq, 2 * width), F32), pltpu.VMEM((tq, 2 * width), F32)],
        compiler_params=_cparams(("parallel", "parallel", "arbitrary")),
    )(q, k, v, lam_params, subln_w.reshape(1, width))


def _top16(sc, payload=None):
    iota = lax.broadcasted_iota(jnp.int32, sc.shape, 0).astype(F32)
    big = float(sc.shape[0])
    vals, picks = [], []
    for _ in range(PEER_TOPK):
        m = jnp.max(sc, axis=0, keepdims=True)
        ix = jnp.min(jnp.where(sc == m, iota, big), axis=0, keepdims=True)
        sel = iota == ix
        if payload is None:
            picks.append(ix)
        else:
            picks.append(jnp.sum(jnp.where(sel, payload, 0), axis=0, keepdims=True))
        sc = jnp.where(sel, -jnp.inf, sc)
        vals.append(m)
    picks = jnp.concatenate(picks, axis=0)
    return jnp.concatenate(vals, axis=0), picks.astype(jnp.int32)


def _pair_candidates(r0, r1):
    half = SUBLANES // 2
    groups = [r0[0:1] + r1[0:8], r0[0:1] + r1[8:16]]
    groups += [r0[a:a + 1] + r1[0:8] for a in (1, 2, 3)]
    groups += [jnp.concatenate([r0[a:a + 1] + r1[0:half], r0[a + 1:a + 2] + r1[0:half]], axis=0)
               for a in (4, 6)]
    groups.append(r0[8:16] + r1[0:1])
    return jnp.concatenate(groups, axis=0)


def _route_kernel(na_ref, df_ref, x_ref, wo_ref, n2_ref, wq_ref, sk_ref,
                  x1_ref, xn_ref, eid_ref, g_ref, q_scr, eid_scr, g_scr):
    tm = x_ref.shape[0]
    x1 = (x_ref[...]
          + jnp.dot(na_ref[...], wo_ref[0:SECTION, :], preferred_element_type=F32)
          + jnp.dot(df_ref[...], wo_ref[SECTION:2 * SECTION, :], preferred_element_type=F32))
    ms = jnp.mean(x1 * x1, axis=-1, keepdims=True)
    xn = x1 * lax.rsqrt(ms + EPS) * n2_ref[...]
    for r in range(SUBLANES):
        x1_ref[:, r, :] = x1[:, r * LANES:(r + 1) * LANES]
        xn_ref[:, r, :] = xn[:, r * LANES:(r + 1) * LANES]
    xb = xn.astype(BF16)
    for c in range(2 * PEER_HEADS):
        q_scr[c] = jnp.dot(xb, wq_ref[:, c * LANES:(c + 1) * LANES],
                           preferred_element_type=F32).astype(BF16)

    nblk = tm // TB

    def route_block(h, blk):
        tok0 = blk * TB
        tops = []
        for p in range(2):
            sc = lax.dot_general(sk_ref[2 * h + p], q_scr[2 * h + p, pl.ds(tok0, TB), :],
                                 (((1,), (1,)), ((), ())), preferred_element_type=F32)
            tops.append(_top16(sc))
        (s0, i0), (s1, i1) = tops
        cand_s = _pair_candidates(s0, s1)
        cand_i = _pair_candidates(i0 * N_KEYS, i1)
        top_s, eid = _top16(cand_s, cand_i)
        e = jnp.exp(top_s - top_s[0:1])
        row = pl.multiple_of(h * PEER_TOPK, PEER_TOPK)
        eid_scr[blk, pl.ds(row, PEER_TOPK), :] = eid * ROW_SUB
        g_scr[blk, pl.ds(row, PEER_TOPK), :] = e / jnp.sum(e, axis=0, keepdims=True)

    def head_body(h, carry):
        for blk in range(nblk):
            route_block(h, blk)
        return carry

    lax.fori_loop(0, PEER_HEADS, head_body, 0)
    for blk in range(nblk):
        eid_ref[blk] = eid_scr[blk].T
        g_ref[blk * TB:(blk + 1) * TB, :] = g_scr[blk].T


def _out_and_route(na, df, x2d, w_out_bf, norm2_w, w_q_bf, sk_bf):
    n = x2d.shape[0]
    tm = TM_OUT
    tok = lambda w: pl.BlockSpec((tm, w), lambda i: (i, 0))
    tiles = pl.BlockSpec((tm, SUBLANES, LANES), lambda i: (i, 0, 0))
    return pl.pallas_call(
        _route_kernel,
        grid=(n // tm,),
        in_specs=[tok(SECTION), tok(SECTION), tok(D_MODEL),
                  _const_spec((D_MODEL, D_MODEL)), _const_spec((1, D_MODEL)),
                  _const_spec(w_q_bf.shape), _const_spec(sk_bf.shape)],
        out_specs=[tiles, tiles,
                   pl.BlockSpec((tm // TB, TB, HK), lambda i: (i, 0, 0)), tok(HK)],
        out_shape=[jax.ShapeDtypeStruct((n, SUBLANES, LANES), F32),
                   jax.ShapeDtypeStruct((n, SUBLANES, LANES), F32),
                   jax.ShapeDtypeStruct((n // TB, TB, HK), jnp.int32),
                   jax.ShapeDtypeStruct((n, HK), F32)],
        scratch_shapes=[pltpu.VMEM((2 * PEER_HEADS, tm, LANES), BF16),
                        pltpu.VMEM((tm // TB, HK, TB), jnp.int32), pltpu.VMEM((tm // TB, HK, TB), F32)],
        compiler_params=_cparams(("parallel",)),
    )(na, df, x2d, w_out_bf, norm2_w.reshape(1, D_MODEL), w_q_bf, sk_bf)


ROW_SUB = SUBLANES // 2
CHUNK_SLOTS = 32
N_CHUNKS = HK // CHUNK_SLOTS
CHUNK_ROWS = CHUNK_SLOTS * SUBLANES
TOKEN_UNROLL = 8
ROW_WORDS = ROW_SUB * LANES
SC_CORES, SC_SUBCORES, SC_LANES = 2, 16, 16
SC_WORKERS = SC_CORES * SC_SUBCORES


def _pack_table(tab):
    bits = lax.bitcast_convert_type(tab.astype(BF16), jnp.uint16).astype(jnp.uint32)
    bits = bits.reshape(tab.shape[0], ROW_SUB, 2, LANES)
    return (bits[:, :, 0, :] | (bits[:, :, 1, :] << 16)).reshape(tab.shape[0] * ROW_SUB, LANES)


def _split_bf16(x):
    hi = x.astype(BF16)
    return hi, (x - hi.astype(F32)).astype(BF16)


def _gather_chunk(tab_ref, eid_ref, t, chunk):
    ids = eid_ref.at[0, t]
    rows = [tab_ref[pl.ds(pl.multiple_of(ids[chunk * CHUNK_SLOTS + i], ROW_SUB), ROW_SUB), :]
            for i in range(CHUNK_SLOTS)]
    return pltpu.bitcast(jnp.concatenate(rows, axis=0), BF16)


def _diag_mask(width):
    r = lax.broadcasted_iota(jnp.int32, (SUBLANES, width), 0)
    n = lax.broadcasted_iota(jnp.int32, (SUBLANES, width), 1)
    return (n % SUBLANES) == r


def _peer_u_kernel(eid_ref, xn_ref, g_ref, tab_ref, rept_ref, rep_ref, rep16_ref, wrep_ref, w16_ref, d_scr):
    mask = _diag_mask(CHUNK_ROWS)
    zeros = jnp.zeros((SUBLANES, LANES), BF16)

    def one_token(t):
        xhi, xlo = _split_bf16(xn_ref[t])
        lhs = jnp.concatenate([jnp.concatenate([xhi, zeros], axis=1),
                               jnp.concatenate([xlo, zeros], axis=1),
                               jnp.concatenate([zeros, xhi], axis=1),
                               jnp.concatenate([zeros, xlo], axis=1)], axis=0)
        for pr in range(N_CHUNKS // 2):
            s = jnp.concatenate([_gather_chunk(tab_ref, eid_ref, t, 2 * pr),
                                 _gather_chunk(tab_ref, eid_ref, t, 2 * pr + 1)], axis=1)
            z = lax.dot_general(lhs, s, (((1,), (1,)), ((), ())), preferred_element_type=F32)
            for half in range(2):
                z8 = z[16 * half:16 * half + 8] + z[16 * half + 8:16 * half + 16]
                zs = jnp.sum(jnp.where(mask, z8, 0.0), axis=0, keepdims=True)
                d_scr[pl.ds(t, 1), pl.ds((2 * pr + half) * CHUNK_ROWS, CHUNK_ROWS)] = zs

    def tok_body(i, carry):
        for u in range(TOKEN_UNROLL):
            one_token(i * TOKEN_UNROLL + u)
        return carry

    lax.fori_loop(0, PB // TOKEN_UNROLL, tok_body, 0)
    _gate_outputs(d_scr[...], rept_ref, g_ref, rep_ref, rep16_ref, wrep_ref, w16_ref)


def _gate_outputs(partial, sum_ref, g_ref, rep_ref, rep16_ref, wrep_ref, w16_ref):
    dhi, dlo = _split_bf16(partial)
    a = (jnp.dot(dhi, sum_ref[...], preferred_element_type=F32)
         + jnp.dot(dlo, sum_ref[...], preferred_element_type=F32))
    w = g_ref[...] * (0.5 * a * (1.0 + lax.erf(a * (2.0 ** -0.5))))
    whi, wlo = _split_bf16(w)
    wrep_ref[...] = (jnp.dot(whi, rep_ref[...], preferred_element_type=F32)
                     + jnp.dot(wlo, rep_ref[...], preferred_element_type=F32))
    w16_ref[...] = (jnp.dot(whi, rep16_ref[...], preferred_element_type=F32)
                    + jnp.dot(wlo, rep16_ref[...], preferred_element_type=F32))


def _gate_kernel(part_ref, g_ref, sum_ref, rep_ref, rep16_ref, wrep_in, w16_in, wrep_ref, w16_ref):
    del wrep_in, w16_in
    _gate_outputs(part_ref[...], sum_ref, g_ref, rep_ref, rep16_ref, wrep_ref, w16_ref)


def _peer_v_kernel(eid_ref, wrep_ref, x1_ref, tab_ref, o_ref):
    mask = _diag_mask(CHUNK_ROWS)

    def one_token(t):
        wr = wrep_ref[pl.ds(t, 1), :]
        acc = x1_ref[t]
        for pr in range(N_CHUNKS // 2):
            parts = []
            for half in range(2):
                seg = wr[:, (2 * pr + half) * CHUNK_ROWS:(2 * pr + half + 1) * CHUNK_ROWS]
                parts.extend(_split_bf16(jnp.where(mask, jnp.broadcast_to(seg, mask.shape), 0.0)))
            lhs = jnp.concatenate(parts, axis=0)
            s = jnp.concatenate([_gather_chunk(tab_ref, eid_ref, t, 2 * pr),
                                 _gather_chunk(tab_ref, eid_ref, t, 2 * pr + 1)], axis=1)
            z = jnp.dot(lhs, s, preferred_element_type=F32)
            acc = acc + ((z[0:8, :LANES] + z[8:16, :LANES]) + (z[16:24, LANES:] + z[24:32, LANES:]))
        return acc

    def tok_body(i, carry):
        accs = [one_token(i * TOKEN_UNROLL + u) for u in range(TOKEN_UNROLL)]
        base = pl.multiple_of(i * TOKEN_UNROLL, TOKEN_UNROLL)
        for r in range(SUBLANES):
            o_ref[pl.ds(base, TOKEN_UNROLL), r * LANES:(r + 1) * LANES] = jnp.concatenate(
                [acc[r:r + 1, :] for acc in accs], axis=0)
        return carry

    lax.fori_loop(0, PB // TOKEN_UNROLL, tok_body, 0)


def _sc_scratch(mid_shape, last_shape):
    half = HK // 2
    per_parity = [pltpu.VMEM((2, half), jnp.int32), pltpu.VMEM(mid_shape, F32), pltpu.VMEM(last_shape, F32)]
    return per_parity + per_parity + [
        pltpu.VMEM((half, ROW_WORDS), jnp.uint32), pltpu.VMEM((half, ROW_WORDS), jnp.uint32),
        pltpu.SemaphoreType.DMA, pltpu.SemaphoreType.DMA,
        pltpu.SemaphoreType.DMA, pltpu.SemaphoreType.DMA,
        pltpu.SemaphoreType.DMA, pltpu.SemaphoreType.DMA,
    ]


def _sc_u_phase(n_tok, tok0):
    per_w = n_tok // SC_WORKERS
    assert per_w * SC_WORKERS == n_tok and per_w % 2 == 0
    half = HK // 2
    mesh = plsc.VectorSubcoreMesh(core_axis_name="c", subcore_axis_name="s")

    @functools.partial(
        pl.kernel, mesh=mesh,
        out_type=jax.ShapeDtypeStruct((n_tok, HK * SC_LANES), F32),
        scratch_types=_sc_scratch((SUBLANES, LANES), (HK * SC_LANES,)),
        compiler_params=pltpu.CompilerParams(needs_layout_passes=False),
    )
    def sc_kernel(tab_hbm, eid_hbm, xn_hbm, out_hbm,
                  idx_a, x_a, a_a, idx_b, x_b, a_b, g0, g1, s_a, s_b, sg0, sg1, so_a, so_b):
        wid = lax.axis_index("s") * SC_CORES + lax.axis_index("c")
        first = wid * per_w
        par = ((idx_a, x_a, a_a, s_a, so_a), (idx_b, x_b, a_b, s_b, so_b))
        gbuf, gsem = (g0, g1), (sg0, sg1)

        def input_copies(t, p):
            idx_v, x_v, _, sem, _ = par[p]
            return (pltpu.make_async_copy(eid_hbm.at[tok0 + t], idx_v, sem),
                    pltpu.make_async_copy(xn_hbm.at[tok0 + t], x_v, sem))

        def gather(p, h):
            return pltpu.make_async_copy(tab_hbm.at[par[p][0].at[h]], gbuf[h], gsem[h])

        def out_copy(t, p):
            return pltpu.make_async_copy(par[p][2], out_hbm.at[t], par[p][4])

        def dot_half(p, h):
            _, x_v, a_v, _, _ = par[p]
            g = gbuf[h]
            groups = LANES // SC_LANES
            block = SUBLANES
            for rb in range(half // block):
                def col_body(k, accs):
                    c, kk = k // groups, k % groups
                    lane0 = pl.multiple_of(kk * SC_LANES, SC_LANES)
                    xl = x_v[2 * c, pl.ds(lane0, SC_LANES)]
                    xh = x_v[2 * c + 1, pl.ds(lane0, SC_LANES)]
                    out = []
                    for r in range(block):
                        words = g[rb * block + r, pl.ds(pl.multiple_of(k * SC_LANES, SC_LANES), SC_LANES)]
                        lov = plsc.bitcast(lax.shift_left(words, jnp.uint32(16)), F32)
                        hiv = plsc.bitcast(words & jnp.uint32(0xFFFF0000), F32)
                        out.append(accs[r] + lov * xl + hiv * xh)
                    return tuple(out)

                zero = jnp.zeros((SC_LANES,), F32)
                accs = lax.fori_loop(0, ROW_WORDS // SC_LANES, col_body, (zero,) * block)
                for r in range(block):
                    a_v[pl.ds((h * half + rb * block + r) * SC_LANES, SC_LANES)] = accs[r]

        def token(t, p):
            q = 1 - p
            has_next = t + 1 < first + per_w
            gather(p, 0).wait()
            gather(p, 1).start()

            @pl.when(has_next)
            def _():
                for cp in input_copies(t + 1, q):
                    cp.start()

            @pl.when(t >= first + 2)
            def _():
                out_copy(t - 2, p).wait()

            dot_half(p, 0)
            gather(p, 1).wait()

            @pl.when(has_next)
            def _():
                for cp in input_copies(t + 1, q):
                    cp.wait()
                gather(q, 0).start()

            dot_half(p, 1)
            out_copy(t, p).start()

        for cp in input_copies(first, 0):
            cp.start()
        for cp in input_copies(first, 0):
            cp.wait()
        gather(0, 0).start()

        def pair_body(i, carry):
            token(first + 2 * i, 0)
            token(first + 2 * i + 1, 1)
            return carry

        lax.fori_loop(0, per_w // 2, pair_body, 0)
        out_copy(first + per_w - 2, 0).wait()
        out_copy(first + per_w - 1, 1).wait()

    return sc_kernel


def _sc_v_phase(n_tok, tok0):
    per_w = n_tok // SC_WORKERS
    assert per_w * SC_WORKERS == n_tok and per_w % 2 == 0
    half = HK // 2
    mesh = plsc.VectorSubcoreMesh(core_axis_name="c", subcore_axis_name="s")

    @functools.partial(
        pl.kernel, mesh=mesh,
        out_type=jax.ShapeDtypeStruct((n_tok, SUBLANES, LANES), F32),
        scratch_types=_sc_scratch((HK * SC_LANES,), (SUBLANES, LANES)),
        compiler_params=pltpu.CompilerParams(needs_layout_passes=False),
    )
    def sc_kernel(tab_hbm, eid_hbm, w16_hbm, x1_hbm, out_hbm,
                  idx_a, w_a, x_a, idx_b, w_b, x_b, g0, g1, s_a, s_b, sg0, sg1, so_a, so_b):
        wid = lax.axis_index("s") * SC_CORES + lax.axis_index("c")
        first = wid * per_w
        par = ((idx_a, w_a, x_a, s_a, so_a), (idx_b, w_b, x_b, s_b, so_b))
        gbuf, gsem = (g0, g1), (sg0, sg1)

        def input_copies(t, p):
            idx_v, w_v, x_v, sem, _ = par[p]
            return (pltpu.make_async_copy(eid_hbm.at[tok0 + t], idx_v, sem),
                    pltpu.make_async_copy(w16_hbm.at[tok0 + t], w_v, sem),
                    pltpu.make_async_copy(x1_hbm.at[tok0 + t], x_v, sem))

        def gather(p, h):
            return pltpu.make_async_copy(tab_hbm.at[par[p][0].at[h]], gbuf[h], gsem[h])

        def out_copy(t, p):
            return pltpu.make_async_copy(par[p][2], out_hbm.at[t], par[p][4])

        def accumulate_half(p, h):
            _, w_v, x_v, _, _ = par[p]
            g = gbuf[h]
            groups = LANES // SC_LANES
            for c in range(ROW_SUB):
                lo0 = tuple(x_v[2 * c, pl.ds(SC_LANES * kk, SC_LANES)] for kk in range(groups))
                hi0 = tuple(x_v[2 * c + 1, pl.ds(SC_LANES * kk, SC_LANES)] for kk in range(groups))

                def row_body(j, acc):
                    lo, hi = acc
                    wj = w_v[pl.ds(pl.multiple_of((h * half + j) * SC_LANES, SC_LANES), SC_LANES)]
                    nlo, nhi = [], []
                    for kk in range(groups):
                        words = g[j, pl.ds(LANES * c + SC_LANES * kk, SC_LANES)]
                        lov = plsc.bitcast(lax.shift_left(words, jnp.uint32(16)), F32)
                        hiv = plsc.bitcast(words & jnp.uint32(0xFFFF0000), F32)
                        nlo.append(lo[kk] + wj * lov)
                        nhi.append(hi[kk] + wj * hiv)
                    return tuple(nlo), tuple(nhi)

                lo, hi = lax.fori_loop(0, half, row_body, (lo0, hi0))
                for kk in range(groups):
                    x_v[2 * c, pl.ds(SC_LANES * kk, SC_LANES)] = lo[kk]
                    x_v[2 * c + 1, pl.ds(SC_LANES * kk, SC_LANES)] = hi[kk]

        def token(t, p):
            q = 1 - p
            has_next = t + 1 < first + per_w
            gather(p, 0).wait()
            gather(p, 1).start()

            @pl.when(has_next)
            def _():
                @pl.when(t >= first + 1)
                def _():
                    out_copy(t - 1, q).wait()
                for cp in input_copies(t + 1, q):
                    cp.start()

            accumulate_half(p, 0)
            gather(p, 1).wait()

            @pl.when(has_next)
            def _():
                for cp in input_copies(t + 1, q):
                    cp.wait()
                gather(q, 0).start()

            accumulate_half(p, 1)
            out_copy(t, p).start()

        for cp in input_copies(first, 0):
            cp.start()
        for cp in input_copies(first, 0):
            cp.wait()
        gather(0, 0).start()

        def pair_body(i, carry):
            token(first + 2 * i, 0)
            token(first + 2 * i + 1, 1)
            return carry

        lax.fori_loop(0, per_w // 2, pair_body, 0)
        out_copy(first + per_w - 2, 0).wait()
        out_copy(first + per_w - 1, 1).wait()

    return sc_kernel


def _peer_experts(eid, g, xn3, x13, u_pack, v_pack, n_sc_u, n_sc_v):
    n = xn3.shape[0]
    n_tc_u, n_tc_v = n - n_sc_u, n - n_sc_v
    assert n_tc_u % PB == 0 and n_tc_v % PB == 0 and n_tc_u > 0
    eid = eid.reshape(n // PB, PB, HK)
    smem_slot = pl.BlockSpec((1, PB, HK), lambda i: (i, 0, 0), memory_space=pltpu.SMEM)
    tok = pl.BlockSpec((PB, SUBLANES, LANES), lambda i: (i, 0, 0))
    flat = lambda w, off=0: pl.BlockSpec((PB, w), lambda i: (i + off, 0))
    table = pl.BlockSpec(u_pack.shape, lambda i: (0, 0), pipeline_mode=pl.Buffered(1))

    def repeat_matrix(times):
        m = (np.arange(HK)[:, None] == (np.arange(HK * times)[None, :] // times)).astype(np.float32)
        return jnp.asarray(m, BF16)

    rep, rep16 = repeat_matrix(SUBLANES), repeat_matrix(SC_LANES)
    gate_shapes = [jax.ShapeDtypeStruct((n, D_MODEL), F32), jax.ShapeDtypeStruct((n, HK * SC_LANES), F32)]
    expert = eid.reshape(n, 2, HK // 2) // ROW_SUB
    if n_sc_u:
        u_rows = u_pack.reshape(N_EXPERTS, ROW_WORDS)
        partial = _sc_u_phase(n_sc_u, n_tc_u)(u_rows, expert, xn3)
    wrep, w16 = pl.pallas_call(
        _peer_u_kernel,
        grid=(n_tc_u // PB,),
        in_specs=[smem_slot, tok, flat(HK), table, _const_spec(rep.T.shape), _const_spec(rep.shape),
                  _const_spec(rep16.shape)],
        out_specs=[flat(D_MODEL), flat(HK * SC_LANES)],
        out_shape=gate_shapes,
        scratch_shapes=[pltpu.VMEM((PB, D_MODEL), F32)],
        compiler_params=_cparams(("arbitrary",)),
    )(eid, xn3, g, u_pack, rep.T, rep, rep16)
    if n_sc_u:
        off = n_tc_u // PB
        anyspec = pl.BlockSpec(memory_space=pl.ANY)
        wrep, w16 = pl.pallas_call(
            _gate_kernel,
            grid=(n_sc_u // PB,),
            in_specs=[flat(HK * SC_LANES), flat(HK, off), _const_spec(rep16.T.shape), _const_spec(rep.shape),
                      _const_spec(rep16.shape), anyspec, anyspec],
            out_specs=[flat(D_MODEL, off), flat(HK * SC_LANES, off)],
            out_shape=gate_shapes,
            input_output_aliases={5: 0, 6: 1},
            compiler_params=_cparams(("arbitrary",)),
        )(partial, g, rep16.T, rep, rep16, wrep, w16)
    parts = []
    if n_tc_v:
        parts.append(pl.pallas_call(
            _peer_v_kernel,
            grid=(n_tc_v // PB,),
            in_specs=[smem_slot, flat(D_MODEL), tok, table],
            out_specs=flat(D_MODEL),
            out_shape=jax.ShapeDtypeStruct((n_tc_v, D_MODEL), F32),
            compiler_params=_cparams(("arbitrary",)),
        )(eid, wrep, x13, v_pack))
    if n_sc_v:
        v_rows = v_pack.reshape(N_EXPERTS, ROW_WORDS)
        parts.append(_sc_v_phase(n_sc_v, n_tc_v)(v_rows, expert, w16, x13).reshape(n_sc_v, D_MODEL))
    y = parts[0] if len(parts) == 1 else jnp.concatenate(parts, axis=0)
    return y, w16


def _encoder_layer(x, p, after):
    if after is not None:
        x, _ = lax.optimization_barrier((x, after))
    batch, seq, _ = x.shape
    n = batch * seq
    x2d = x.reshape(n, D_MODEL)
    naq, nak, nav, dq, dk, dv = _in_projection(x2d, seq, p["norm1_w"], p["w_in"], p["qk_w"], p["rope"])
    na = _neighbourhood_attention(naq, nak, nav, p["na_bias"], batch, seq)
    df = _diff_attention(dq, dk, dv, p["lam"], p["subln_w"], batch, seq)
    x1, xn, eid, g = _out_and_route(na, df, x2d, p["w_out"], p["norm2_w"], p["w_q"], p["sub_keys"])
    y, marker = _peer_experts(eid, g, xn, x1, p["u_pack"], p["v_pack"], *SC_TOKENS.get(n, (0, 0)))
    return y.reshape(batch, seq, D_MODEL), marker


def kernel(x_prompt, x_sample, norm1_w, w_in, na_q_norm, na_k_norm, na_rpb, diff_q_norm, diff_k_norm,
           diff_lambda_q1, diff_lambda_k1, diff_lambda_q2, diff_lambda_k2, diff_subln_w, w_out, norm2_w,
           peer_w_q, peer_sub_keys, peer_u, peer_v):
    assert norm1_w.shape[0] == 1, "single-layer problem"
    reps = SECTION // HEAD_DIM
    params = {
        "norm1_w": norm1_w[0],
        "w_in": w_in[0].astype(BF16),
        "qk_w": jnp.stack([jnp.tile(w[0], reps) for w in (na_q_norm, na_k_norm, diff_q_norm, diff_k_norm)]),
        "lam": jnp.stack([diff_lambda_q1[0], diff_lambda_k1[0], diff_lambda_q2[0], diff_lambda_k2[0]]),
        "subln_w": diff_subln_w[0],
        "w_out": w_out[0].astype(BF16),
        "norm2_w": norm2_w[0],
        "w_q": peer_w_q[0].astype(BF16),
        "sub_keys": peer_sub_keys[0].reshape(2 * PEER_HEADS, N_KEYS, N_KEYS).astype(BF16),
        "na_bias": _na_bias(na_rpb[0]),
        "u_pack": _pack_table(peer_u[0]),
        "v_pack": _pack_table(peer_v[0]),
    }
    rope_by_seq = {}
    outs = []
    marker = None
    for x in (x_prompt, x_sample):
        seq = x.shape[1]
        if seq not in rope_by_seq:
            rope_by_seq[seq] = _rope_tables(seq)
        y, marker = _encoder_layer(x, dict(params, rope=rope_by_seq[seq]), marker)
        outs.append(y)
    return tuple(outs)
```

```python
import functools
import math

import jax
import jax.numpy as jnp
import numpy as np
from jax import lax
from jax.experimental import pallas as pl
from jax.experimental.pallas import tpu as pltpu
from jax.experimental.pallas import tpu_sc as plsc

F32 = jnp.float32
BF16 = jnp.bfloat16

D_MODEL = 1024
GRID_W = 64
HEAD_DIM = 64
NA_HEADS = 8
NA_KH = 8
NA_KW = 16
DIFF_HEADS = 4
SECTION = 512
ROT_DIM = HEAD_DIM // 4
ROPE_THETA = 500000.0
PEER_HEADS = 8
N_KEYS = 128
PEER_TOPK = 16
N_EXPERTS = N_KEYS * N_KEYS
HK = PEER_HEADS * PEER_TOPK
EPS = 1e-6
NEG_BIG = -1e30
LOG2E = math.log2(math.e)
LAMBDA_INIT = 0.8 - 0.6 * math.exp(-0.3 * 0)

LANES = 128
SUBLANES = 8
VMEM_LIMIT = 56 * 1024 * 1024

TM_IN = 512
NA_ROWS_PER_STEP = 8
NA_ROW_UNROLL = 2
TQ = 512
KV_SUBCHUNKS = 1
TK = 4096
TM_OUT = 256
TB = 128
PB = 256
SC_TOKENS = {4 * 4096: (4 * 4096, 4 * 4096), 8 * 4096: (14336, 14336)}


def _cparams(sem):
    return pltpu.CompilerParams(dimension_semantics=sem, vmem_limit_bytes=VMEM_LIMIT)


def _const_spec(shape):
    nd = len(shape)
    return pl.BlockSpec(shape, lambda *_: (0,) * nd)


def _inproj_kernel(x_ref, n1_ref, w_ref, bd_ref, nw_ref, rc_ref, ra_ref, rb_ref,
                   naq_ref, nak_ref, nav_ref, dq_ref, dk_ref, dv_ref):
    x = x_ref[...]
    ms = jnp.mean(x * x, axis=-1, keepdims=True)
    h = (x * lax.rsqrt(ms + EPS) * n1_ref[...]).astype(BF16)
    bd = bd_ref[...]

    def proj(c):
        return jnp.dot(h, w_ref[:, c * SECTION:(c + 1) * SECTION], preferred_element_type=F32)

    def qknorm(y, row):
        sq = y * y
        hi = sq.astype(BF16)
        lo = (sq - hi.astype(F32)).astype(BF16)
        msq = (jnp.dot(hi, bd, preferred_element_type=F32)
               + jnp.dot(lo, bd, preferred_element_type=F32))
        return y * lax.rsqrt(msq + EPS) * nw_ref[row:row + 1, :]

    def rope(y):
        return (y * rc_ref[...]
                + pltpu.roll(y, SECTION - ROT_DIM // 2, axis=1) * ra_ref[...]
                + pltpu.roll(y, ROT_DIM // 2, axis=1) * rb_ref[...])

    scale = HEAD_DIM ** -0.5 * LOG2E
    naq_ref[...] = (qknorm(proj(0), 0) * scale).astype(BF16)
    nak_ref[...] = qknorm(proj(1), 1).astype(BF16)
    nav_ref[...] = proj(2).astype(BF16)
    dq_ref[...] = (rope(qknorm(proj(3), 2)) * scale).astype(BF16)
    dk_ref[...] = rope(qknorm(proj(4), 3)).astype(BF16)
    dv_ref[...] = proj(5).astype(BF16)


def _rope_tables(seq):
    pos = jnp.arange(seq, dtype=F32)
    inv = ROPE_THETA ** (-jnp.arange(0, ROT_DIM, 2, dtype=F32) / ROT_DIM)
    ang = pos[:, None] * inv[None, :]
    cos, sin = jnp.cos(ang), jnp.sin(ang)
    half = ROT_DIM // 2
    pad = HEAD_DIM - ROT_DIM
    c_head = jnp.concatenate([cos, cos, jnp.ones((seq, pad), F32)], -1)
    a_head = jnp.concatenate([-sin, jnp.zeros((seq, pad + half), F32)], -1)
    b_head = jnp.concatenate([jnp.zeros((seq, half), F32), sin, jnp.zeros((seq, pad), F32)], -1)
    reps = SECTION // HEAD_DIM
    return (jnp.tile(c_head, (1, reps)), jnp.tile(a_head, (1, reps)), jnp.tile(b_head, (1, reps)))


def _in_projection(x2d, seq, norm1_w, w_in_bf, qk_w, rope_tabs):
    n = x2d.shape[0]
    tm = min(TM_IN, seq)
    steps_per_seq = seq // tm
    gid = np.arange(SECTION) // HEAD_DIM
    bd = jnp.asarray((gid[:, None] == gid[None, :]).astype(np.float32) / HEAD_DIM, BF16)
    tok = pl.BlockSpec((tm, SECTION), lambda i: (i, 0))
    pos = pl.BlockSpec((tm, SECTION), lambda i: (i % steps_per_seq, 0))
    out = jax.ShapeDtypeStruct((n, SECTION), BF16)
    return pl.pallas_call(
        _inproj_kernel,
        grid=(n // tm,),
        in_specs=[pl.BlockSpec((tm, D_MODEL), lambda i: (i, 0)),
                  _const_spec((1, D_MODEL)),
                  _const_spec((D_MODEL, 6 * SECTION)),
                  _const_spec((SECTION, SECTION)),
                  _const_spec((4, SECTION)),
                  pos, pos, pos],
        out_specs=[tok] * 6,
        out_shape=[out] * 6,
        compiler_params=_cparams(("parallel",)),
    )(x2d, norm1_w.reshape(1, D_MODEL), w_in_bf, bd, qk_w, *rope_tabs)


def _na_bias(rpb):
    c = np.arange(GRID_W)[:, None]
    kc = np.arange(GRID_W)[None, :]
    cs = np.clip(c - NA_KW // 2, 0, GRID_W - NA_KW)
    valid = (kc >= cs) & (kc < cs + NA_KW)
    rel_c = kc - c + (NA_KW - 1)
    onehot = ((np.arange(2 * NA_KW - 1)[:, None, None] == rel_c[None]) & valid[None]).astype(np.float32)
    col = jnp.einsum("hrx,xck->hrck", rpb.astype(F32), onehot, precision=lax.Precision.HIGHEST)
    col = jnp.where(valid, col * LOG2E, NEG_BIG)
    per_off = [col[:, NA_KH - 1 - o:2 * NA_KH - 1 - o] for o in range(NA_KH)]
    b = jnp.stack(per_off).transpose(0, 1, 3, 2, 4)
    return b.reshape(NA_KH, NA_HEADS, GRID_W, NA_KH * GRID_W)


def _na_kernel(q_ref, k_ref, v_ref, b_ref, o_ref, *, rows):
    j = pl.program_id(1)
    win = NA_KH * GRID_W
    pair = 2 * HEAD_DIM
    lane = lax.broadcasted_iota(jnp.int32, (GRID_W, pair), 1)
    first = lane < HEAD_DIM
    ones = jnp.ones((win, pair), BF16)

    def one_row(rr):
        r = j * NA_ROWS_PER_STEP + rr
        rs = jnp.clip(r - NA_KH // 2, 0, rows - NA_KH)
        off = r - rs
        kstart = pl.multiple_of(rs * GRID_W, GRID_W)
        qstart = pl.multiple_of(rr * GRID_W, GRID_W)
        outs = []
        for hp in range(NA_HEADS // 2):
            sl = slice(hp * pair, (hp + 1) * pair)
            qp = q_ref[pl.ds(qstart, GRID_W), sl]
            kp = k_ref[pl.ds(kstart, win), sl]
            v_ext = jnp.concatenate([v_ref[pl.ds(kstart, win), sl], ones], axis=1)
            halves = []
            for hh in range(2):
                qm = jnp.where(first if hh == 0 else ~first, qp, jnp.zeros_like(qp))
                s = lax.dot_general(qm, kp, (((1,), (1,)), ((), ())), preferred_element_type=F32)
                s = s + b_ref[off, 2 * hp + hh]
                p = jnp.exp2(s - jnp.max(s, axis=-1, keepdims=True)).astype(BF16)
                z = jnp.dot(p, v_ext, preferred_element_type=F32)
                halves.append(z[:, :pair] / z[:, pair:pair + 1])
            outs.append(jnp.where(first, halves[0], halves[1]))
        o_ref[pl.ds(qstart, GRID_W), :] = jnp.concatenate(outs, axis=-1).astype(BF16)

    def row_body(i, carry):
        for u in range(NA_ROW_UNROLL):
            one_row(i * NA_ROW_UNROLL + u)
        return carry

    lax.fori_loop(0, NA_ROWS_PER_STEP // NA_ROW_UNROLL, row_body, 0)


def _neighbourhood_attention(q, k, v, bias, batch, seq):
    rows = seq // GRID_W
    assert rows >= NA_KH and rows % NA_ROWS_PER_STEP == 0
    nblk = rows // NA_ROWS_PER_STEP
    tq = NA_ROWS_PER_STEP * GRID_W
    qspec = pl.BlockSpec((tq, SECTION), lambda b, j: (b * nblk + j, 0))
    kvspec = pl.BlockSpec((seq, SECTION), lambda b, j: (b, 0))
    return pl.pallas_call(
        functools.partial(_na_kernel, rows=rows),
        grid=(batch, nblk),
        in_specs=[qspec, kvspec, kvspec, _const_spec(bias.shape)],
        out_specs=qspec,
        out_shape=jax.ShapeDtypeStruct(q.shape, BF16),
        compiler_params=_cparams(("parallel", "arbitrary")),
    )(q, k, v, bias)


def _diff_kernel(q_ref, k_ref, v_ref, lam_ref, sub_ref, o_ref, m0_scr, m1_scr, acc0_scr, acc1_scr,
                 *, seq, tq, tk):
    m_scrs, acc_scrs = (m0_scr, m1_scr), (acc0_scr, acc1_scr)
    q = q_ref[...]
    lane = lax.broadcasted_iota(jnp.int32, q.shape, 1)
    zero = jnp.zeros_like(q)
    q_maps = (jnp.where(lane < HEAD_DIM, q, zero), jnp.where(lane >= HEAD_DIM, q, zero))

    lp = lam_ref[...]
    lam = (jnp.exp(jnp.sum(lp[0:1] * lp[1:2], axis=-1, keepdims=True))
           - jnp.exp(jnp.sum(lp[2:3] * lp[3:4], axis=-1, keepdims=True)) + LAMBDA_INIT)

    width = 2 * HEAD_DIM
    ones = jnp.ones((tk, width), BF16)

    for m_scr, acc_scr in zip(m_scrs, acc_scrs):
        m_scr[...] = jnp.full(m_scr.shape, -jnp.inf, F32)
        acc_scr[...] = jnp.zeros(acc_scr.shape, F32)

    def kv_body(c, carry):
        tks = tk // KV_SUBCHUNKS
        kcs, v_exts = [], []
        for sub in range(KV_SUBCHUNKS):
            start = pl.multiple_of(c * tk + sub * tks, tks)
            kcs.append(k_ref[pl.ds(start, tks), :])
            v_exts.append(jnp.concatenate([v_ref[pl.ds(start, tks), :], ones[:tks]], axis=1))
        scores = [[lax.dot_general(q_maps[mp], kcs[sub], (((1,), (1,)), ((), ())),
                                   preferred_element_type=F32) for mp in range(2)]
                  for sub in range(KV_SUBCHUNKS)]
        m = [m_scrs[mp][...] for mp in range(2)]
        probs, alphas = [], []
        for sub in range(KV_SUBCHUNKS):
            probs.append([])
            alphas.append([])
            for mp in range(2):
                mn = jnp.maximum(m[mp], jnp.max(scores[sub][mp], axis=-1, keepdims=True))
                probs[sub].append(jnp.exp2(scores[sub][mp] - mn[:, 0:1]).astype(BF16))
                alpha = jnp.exp2(m[mp] - mn)
                alphas[sub].append(jnp.concatenate([alpha, alpha], axis=1))
                m[mp] = mn
        for mp in range(2):
            m_scrs[mp][...] = m[mp]
            acc = acc_scrs[mp][...]
            for sub in range(KV_SUBCHUNKS):
                acc = alphas[sub][mp] * acc + jnp.dot(probs[sub][mp], v_exts[sub],
                                                      preferred_element_type=F32)
            acc_scrs[mp][...] = acc
        return carry

    lax.fori_loop(0, seq // tk, kv_body, 0)
    a0, a1 = acc0_scr[...], acc1_scr[...]
    o = a0[:, :width] / a0[:, width:width + 1] - lam * (a1[:, :width] / a1[:, width:width + 1])
    ms = jnp.mean(o * o, axis=-1, keepdims=True)
    y = o * lax.rsqrt(ms + EPS) * sub_ref[...]
    o_ref[...] = (y * (1.0 - LAMBDA_INIT)).astype(BF16)


def _diff_attention(q, k, v, lam_params, subln_w, batch, seq):
    tq, tk = min(TQ, seq), min(TK, seq)
    nq = seq // tq
    width = 2 * HEAD_DIM
    qspec = pl.BlockSpec((tq, width), lambda b, h, i: (b * nq + i, h))
    kvspec = pl.BlockSpec((seq, width), lambda b, h, i: (b, h))
    return pl.pallas_call(
        functools.partial(_diff_kernel, seq=seq, tq=tq, tk=tk),
        grid=(batch, DIFF_HEADS, nq),
        in_specs=[qspec, kvspec, kvspec, _const_spec((4, HEAD_DIM)), _const_spec((1, width))],
        out_specs=qspec,
        out_shape=jax.ShapeDtypeStruct(q.shape, BF16),
        scratch_shapes=[pltpu.VMEM((tq, width), F32), pltpu.VMEM((tq, width), F32),
                        pltpu.VMEM((tq, 2 * width), F32), pltpu.VMEM((tq, 2 * width), F32)],
        compiler_params=_cparams(("parallel", "parallel", "arbitrary")),
    )(q, k, v, lam_params, subln_w.reshape(1, width))


def _top16(sc, payload=None):
    iota = lax.broadcasted_iota(jnp.int32, sc.shape, 0).astype(F32)
    big = float(sc.shape[0])
    vals, picks = [], []
    for _ in range(PEER_TOPK):
        m = jnp.max(sc, axis=0, keepdims=True)
        ix = jnp.min(jnp.where(sc == m, iota, big), axis=0, keepdims=True)
        sel = iota == ix
        if payload is None:
            picks.append(ix)
        else:
            picks.append(jnp.sum(jnp.where(sel, payload, 0), axis=0, keepdims=True))
        sc = jnp.where(sel, -jnp.inf, sc)
        vals.append(m)
    picks = jnp.concatenate(picks, axis=0)
    return jnp.concatenate(vals, axis=0), picks.astype(jnp.int32)


def _pair_candidates(r0, r1):
    half = SUBLANES // 2
    groups = [r0[0:1] + r1[0:8], r0[0:1] + r1[8:16]]
    groups += [r0[a:a + 1] + r1[0:8] for a in (1, 2, 3)]
    groups += [jnp.concatenate([r0[a:a + 1] + r1[0:half], r0[a + 1:a + 2] + r1[0:half]], axis=0)
               for a in (4, 6)]
    groups.append(r0[8:16] + r1[0:1])
    return jnp.concatenate(groups, axis=0)


def _route_kernel(na_ref, df_ref, x_ref, wo_ref, n2_ref, wq_ref, sk_ref,
                  x1_ref, xn_ref, eid_ref, g_ref, q_scr, eid_scr, g_scr):
    tm = x_ref.shape[0]
    x1 = (x_ref[...]
          + jnp.dot(na_ref[...], wo_ref[0:SECTION, :], preferred_element_type=F32)
          + jnp.dot(df_ref[...], wo_ref[SECTION:2 * SECTION, :], preferred_element_type=F32))
    ms = jnp.mean(x1 * x1, axis=-1, keepdims=True)
    xn = x1 * lax.rsqrt(ms + EPS) * n2_ref[...]
    for r in range(SUBLANES):
        x1_ref[:, r, :] = x1[:, r * LANES:(r + 1) * LANES]
        xn_ref[:, r, :] = xn[:, r * LANES:(r + 1) * LANES]
    xb = xn.astype(BF16)
    for c in range(2 * PEER_HEADS):
        q_scr[c] = jnp.dot(xb, wq_ref[:, c * LANES:(c + 1) * LANES],
                           preferred_element_type=F32).astype(BF16)

    nblk = tm // TB

    def route_block(h, blk):
        tok0 = blk * TB
        tops = []
        for p in range(2):
            sc = lax.dot_general(sk_ref[2 * h + p], q_scr[2 * h + p, pl.ds(tok0, TB), :],
                                 (((1,), (1,)), ((), ())), preferred_element_type=F32)
            tops.append(_top16(sc))
        (s0, i0), (s1, i1) = tops
        cand_s = _pair_candidates(s0, s1)
        cand_i = _pair_candidates(i0 * N_KEYS, i1)
        top_s, eid = _top16(cand_s, cand_i)
        e = jnp.exp(top_s - top_s[0:1])
        row = pl.multiple_of(h * PEER_TOPK, PEER_TOPK)
        eid_scr[blk, pl.ds(row, PEER_TOPK), :] = eid * ROW_SUB
        g_scr[blk, pl.ds(row, PEER_TOPK), :] = e / jnp.sum(e, axis=0, keepdims=True)

    def head_body(h, carry):
        for blk in range(nblk):
            route_block(h, blk)
        return carry

    lax.fori_loop(0, PEER_HEADS, head_body, 0)
    for blk in range(nblk):
        eid_ref[blk] = eid_scr[blk].T
        g_ref[blk * TB:(blk + 1) * TB, :] = g_scr[blk].T


def _out_and_route(na, df, x2d, w_out_bf, norm2_w, w_q_bf, sk_bf):
    n = x2d.shape[0]
    tm = TM_OUT
    tok = lambda w: pl.BlockSpec((tm, w), lambda i: (i, 0))
    tiles = pl.BlockSpec((tm, SUBLANES, LANES), lambda i: (i, 0, 0))
    return pl.pallas_call(
        _route_kernel,
        grid=(n // tm,),
        in_specs=[tok(SECTION), tok(SECTION), tok(D_MODEL),
                  _const_spec((D_MODEL, D_MODEL)), _const_spec((1, D_MODEL)),
                  _const_spec(w_q_bf.shape), _const_spec(sk_bf.shape)],
        out_specs=[tiles, tiles,
                   pl.BlockSpec((tm // TB, TB, HK), lambda i: (i, 0, 0)), tok(HK)],
        out_shape=[jax.ShapeDtypeStruct((n, SUBLANES, LANES), F32),
                   jax.ShapeDtypeStruct((n, SUBLANES, LANES), F32),
                   jax.ShapeDtypeStruct((n // TB, TB, HK), jnp.int32),
                   jax.ShapeDtypeStruct((n, HK), F32)],
        scratch_shapes=[pltpu.VMEM((2 * PEER_HEADS, tm, LANES), BF16),
                        pltpu.VMEM((tm // TB, HK, TB), jnp.int32), pltpu.VMEM((tm // TB, HK, TB), F32)],
        compiler_params=_cparams(("parallel",)),
    )(na, df, x2d, w_out_bf, norm2_w.reshape(1, D_MODEL), w_q_bf, sk_bf)


ROW_SUB = SUBLANES // 2
CHUNK_SLOTS = 32
N_CHUNKS = HK // CHUNK_SLOTS
CHUNK_ROWS = CHUNK_SLOTS * SUBLANES
TOKEN_UNROLL = 8
ROW_WORDS = ROW_SUB * LANES
SC_CORES, SC_SUBCORES, SC_LANES = 2, 16, 16
SC_WORKERS = SC_CORES * SC_SUBCORES


def _pack_table(tab):
    bits = lax.bitcast_convert_type(tab.astype(BF16), jnp.uint16).astype(jnp.uint32)
    bits = bits.reshape(tab.shape[0], ROW_SUB, 2, LANES)
    return (bits[:, :, 0, :] | (bits[:, :, 1, :] << 16)).reshape(tab.shape[0] * ROW_SUB, LANES)


def _split_bf16(x):
    hi = x.astype(BF16)
    return hi, (x - hi.astype(F32)).astype(BF16)


def _gather_chunk(tab_ref, eid_ref, t, chunk):
    ids = eid_ref.at[0, t]
    rows = [tab_ref[pl.ds(pl.multiple_of(ids[chunk * CHUNK_SLOTS + i], ROW_SUB), ROW_SUB), :]
            for i in range(CHUNK_SLOTS)]
    return pltpu.bitcast(jnp.concatenate(rows, axis=0), BF16)


def _diag_mask(width):
    r = lax.broadcasted_iota(jnp.int32, (SUBLANES, width), 0)
    n = lax.broadcasted_iota(jnp.int32, (SUBLANES, width), 1)
    return (n % SUBLANES) == r


def _peer_u_kernel(eid_ref, xn_ref, g_ref, tab_ref, rept_ref, rep_ref, rep16_ref, wrep_ref, w16_ref, d_scr):
    mask = _diag_mask(CHUNK_ROWS)
    zeros = jnp.zeros((SUBLANES, LANES), BF16)

    def one_token(t):
        xhi, xlo = _split_bf16(xn_ref[t])
        lhs = jnp.concatenate([jnp.concatenate([xhi, zeros], axis=1),
                               jnp.concatenate([xlo, zeros], axis=1),
                               jnp.concatenate([zeros, xhi], axis=1),
                               jnp.concatenate([zeros, xlo], axis=1)], axis=0)
        for pr in range(N_CHUNKS // 2):
            s = jnp.concatenate([_gather_chunk(tab_ref, eid_ref, t, 2 * pr),
                                 _gather_chunk(tab_ref, eid_ref, t, 2 * pr + 1)], axis=1)
            z = lax.dot_general(lhs, s, (((1,), (1,)), ((), ())), preferred_element_type=F32)
            for half in range(2):
                z8 = z[16 * half:16 * half + 8] + z[16 * half + 8:16 * half + 16]
                zs = jnp.sum(jnp.where(mask, z8, 0.0), axis=0, keepdims=True)
                d_scr[pl.ds(t, 1), pl.ds((2 * pr + half) * CHUNK_ROWS, CHUNK_ROWS)] = zs

    def tok_body(i, carry):
        for u in range(TOKEN_UNROLL):
            one_token(i * TOKEN_UNROLL + u)
        return carry

    lax.fori_loop(0, PB // TOKEN_UNROLL, tok_body, 0)
    _gate_outputs(d_scr[...], rept_ref, g_ref, rep_ref, rep16_ref, wrep_ref, w16_ref)


def _gate_outputs(partial, sum_ref, g_ref, rep_ref, rep16_ref, wrep_ref, w16_ref):
    dhi, dlo = _split_bf16(partial)
    a = (jnp.dot(dhi, sum_ref[...], preferred_element_type=F32)
         + jnp.dot(dlo, sum_ref[...], preferred_element_type=F32))
    w = g_ref[...] * (0.5 * a * (1.0 + lax.erf(a * (2.0 ** -0.5))))
    whi, wlo = _split_bf16(w)
    wrep_ref[...] = (jnp.dot(whi, rep_ref[...], preferred_element_type=F32)
                     + jnp.dot(wlo, rep_ref[...], preferred_element_type=F32))
    w16_ref[...] = (jnp.dot(whi, rep16_ref[...], preferred_element_type=F32)
                    + jnp.dot(wlo, rep16_ref[...], preferred_element_type=F32))


def _gate_outputs_kernel(part_ref, g_ref, sum_ref, rep_ref, rep16_ref, wrep_ref, w16_ref):
    _gate_outputs(part_ref[...], sum_ref, g_ref, rep_ref, rep16_ref, wrep_ref, w16_ref)


def _gate_kernel(part_ref, g_ref, sum_ref, rep_ref, rep16_ref, wrep_in, w16_in, wrep_ref, w16_ref):
    del wrep_in, w16_in
    _gate_outputs(part_ref[...], sum_ref, g_ref, rep_ref, rep16_ref, wrep_ref, w16_ref)


def _peer_v_kernel(eid_ref, wrep_ref, x1_ref, tab_ref, o_ref):
    mask = _diag_mask(CHUNK_ROWS)

    def one_token(t):
        wr = wrep_ref[pl.ds(t, 1), :]
        acc = x1_ref[t]
        for pr in range(N_CHUNKS // 2):
            parts = []
            for half in range(2):
                seg = wr[:, (2 * pr + half) * CHUNK_ROWS:(2 * pr + half + 1) * CHUNK_ROWS]
                parts.extend(_split_bf16(jnp.where(mask, jnp.broadcast_to(seg, mask.shape), 0.0)))
            lhs = jnp.concatenate(parts, axis=0)
            s = jnp.concatenate([_gather_chunk(tab_ref, eid_ref, t, 2 * pr),
                                 _gather_chunk(tab_ref, eid_ref, t, 2 * pr + 1)], axis=1)
            z = jnp.dot(lhs, s, preferred_element_type=F32)
            acc = acc + ((z[0:8, :LANES] + z[8:16, :LANES]) + (z[16:24, LANES:] + z[24:32, LANES:]))
        return acc

    def tok_body(i, carry):
        accs = [one_token(i * TOKEN_UNROLL + u) for u in range(TOKEN_UNROLL)]
        base = pl.multiple_of(i * TOKEN_UNROLL, TOKEN_UNROLL)
        for r in range(SUBLANES):
            o_ref[pl.ds(base, TOKEN_UNROLL), r * LANES:(r + 1) * LANES] = jnp.concatenate(
                [acc[r:r + 1, :] for acc in accs], axis=0)
        return carry

    lax.fori_loop(0, PB // TOKEN_UNROLL, tok_body, 0)


def _sc_scratch(mid_shape, last_shape):
    half = HK // 2
    per_parity = [pltpu.VMEM((2, half), jnp.int32), pltpu.VMEM(mid_shape, F32), pltpu.VMEM(last_shape, F32)]
    return per_parity + per_parity + [
        pltpu.VMEM((half, ROW_WORDS), jnp.uint32), pltpu.VMEM((half, ROW_WORDS), jnp.uint32),
        pltpu.SemaphoreType.DMA, pltpu.SemaphoreType.DMA,
        pltpu.SemaphoreType.DMA, pltpu.SemaphoreType.DMA,
        pltpu.SemaphoreType.DMA, pltpu.SemaphoreType.DMA,
    ]


def _sc_u_phase(n_tok, tok0):
    per_w = n_tok // SC_WORKERS
    assert per_w * SC_WORKERS == n_tok and per_w % 2 == 0
    half = HK // 2
    mesh = plsc.VectorSubcoreMesh(core_axis_name="c", subcore_axis_name="s")

    @functools.partial(
        pl.kernel, mesh=mesh,
        out_type=jax.ShapeDtypeStruct((n_tok, HK * SC_LANES), F32),
        scratch_types=_sc_scratch((SUBLANES, LANES), (HK * SC_LANES,)),
        compiler_params=pltpu.CompilerParams(needs_layout_passes=False),
    )
    def sc_kernel(tab_hbm, eid_hbm, xn_hbm, out_hbm,
                  idx_a, x_a, a_a, idx_b, x_b, a_b, g0, g1, s_a, s_b, sg0, sg1, so_a, so_b):
        wid = lax.axis_index("s") * SC_CORES + lax.axis_index("c")
        first = wid * per_w
        par = ((idx_a, x_a, a_a, s_a, so_a), (idx_b, x_b, a_b, s_b, so_b))
        gbuf, gsem = (g0, g1), (sg0, sg1)

        def input_copies(t, p):
            idx_v, x_v, _, sem, _ = par[p]
            return (pltpu.make_async_copy(eid_hbm.at[tok0 + t], idx_v, sem),
                    pltpu.make_async_copy(xn_hbm.at[tok0 + t], x_v, sem))

        def gather(p, h):
            return pltpu.make_async_copy(tab_hbm.at[par[p][0].at[h]], gbuf[h], gsem[h])

        def out_copy(t, p):
            return pltpu.make_async_copy(par[p][2], out_hbm.at[t], par[p][4])

        def dot_half(p, h):
            _, x_v, a_v, _, _ = par[p]
            g = gbuf[h]
            groups = LANES // SC_LANES
            block = SUBLANES
            for rb in range(half // block):
                def col_body(k, accs):
                    c, kk = k // groups, k % groups
                    lane0 = pl.multiple_of(kk * SC_LANES, SC_LANES)
                    xl = x_v[2 * c, pl.ds(lane0, SC_LANES)]
                    xh = x_v[2 * c + 1, pl.ds(lane0, SC_LANES)]
                    out = []
                    for r in range(block):
                        words = g[rb * block + r, pl.ds(pl.multiple_of(k * SC_LANES, SC_LANES), SC_LANES)]
                        lov = plsc.bitcast(lax.shift_left(words, jnp.uint32(16)), F32)
                        hiv = plsc.bitcast(words & jnp.uint32(0xFFFF0000), F32)
                        out.append(accs[r] + lov * xl + hiv * xh)
                    return tuple(out)

                zero = jnp.zeros((SC_LANES,), F32)
                accs = lax.fori_loop(0, ROW_WORDS // SC_LANES, col_body, (zero,) * block)
                for r in range(block):
                    a_v[pl.ds((h * half + rb * block + r) * SC_LANES, SC_LANES)] = accs[r]

        def token(t, p):
            q = 1 - p
            has_next = t + 1 < first + per_w
            gather(p, 0).wait()
            gather(p, 1).start()

            @pl.when(has_next)
            def _():
                for cp in input_copies(t + 1, q):
                    cp.start()

            @pl.when(t >= first + 2)
            def _():
                out_copy(t - 2, p).wait()

            dot_half(p, 0)
            gather(p, 1).wait()

            @pl.when(has_next)
            def _():
                for cp in input_copies(t + 1, q):
                    cp.wait()
                gather(q, 0).start()

            dot_half(p, 1)
            out_copy(t, p).start()

        for cp in input_copies(first, 0):
            cp.start()
        for cp in input_copies(first, 0):
            cp.wait()
        gather(0, 0).start()

        def pair_body(i, carry):
            token(first + 2 * i, 0)
            token(first + 2 * i + 1, 1)
            return carry

        lax.fori_loop(0, per_w // 2, pair_body, 0)
        out_copy(first + per_w - 2, 0).wait()
        out_copy(first + per_w - 1, 1).wait()

    return sc_kernel


def _sc_v_phase(n_tok, tok0):
    per_w = n_tok // SC_WORKERS
    assert per_w * SC_WORKERS == n_tok and per_w % 2 == 0
    half = HK // 2
    mesh = plsc.VectorSubcoreMesh(core_axis_name="c", subcore_axis_name="s")

    @functools.partial(
        pl.kernel, mesh=mesh,
        out_type=jax.ShapeDtypeStruct((n_tok, SUBLANES, LANES), F32),
        scratch_types=_sc_scratch((HK * SC_LANES,), (SUBLANES, LANES)),
        compiler_params=pltpu.CompilerParams(needs_layout_passes=False),
    )
    def sc_kernel(tab_hbm, eid_hbm, w16_hbm, x1_hbm, out_hbm,
                  idx_a, w_a, x_a, idx_b, w_b, x_b, g0, g1, s_a, s_b, sg0, sg1, so_a, so_b):
        wid = lax.axis_index("s") * SC_CORES + lax.axis_index("c")
        first = wid * per_w
        par = ((idx_a, w_a, x_a, s_a, so_a), (idx_b, w_b, x_b, s_b, so_b))
        gbuf, gsem = (g0, g1), (sg0, sg1)

        def input_copies(t, p):
            idx_v, w_v, x_v, sem, _ = par[p]
            return (pltpu.make_async_copy(eid_hbm.at[tok0 + t], idx_v, sem),
                    pltpu.make_async_copy(w16_hbm.at[tok0 + t], w_v, sem),
                    pltpu.make_async_copy(x1_hbm.at[tok0 + t], x_v, sem))

        def gather(p, h):
            return pltpu.make_async_copy(tab_hbm.at[par[p][0].at[h]], gbuf[h], gsem[h])

        def out_copy(t, p):
            return pltpu.make_async_copy(par[p][2], out_hbm.at[t], par[p][4])

        def accumulate_half(p, h):
            _, w_v, x_v, _, _ = par[p]
            g = gbuf[h]
            groups = LANES // SC_LANES
            for c in range(ROW_SUB):
                lo0 = tuple(x_v[2 * c, pl.ds(SC_LANES * kk, SC_LANES)] for kk in range(groups))
                hi0 = tuple(x_v[2 * c + 1, pl.ds(SC_LANES * kk, SC_LANES)] for kk in range(groups))

                def row_body(j, acc):
                    lo, hi = acc
                    wj = w_v[pl.ds(pl.multiple_of((h * half + j) * SC_LANES, SC_LANES), SC_LANES)]
                    nlo, nhi = [], []
                    for kk in range(groups):
                        words = g[j, pl.ds(LANES * c + SC_LANES * kk, SC_LANES)]
                        lov = plsc.bitcast(lax.shift_left(words, jnp.uint32(16)), F32)
                        hiv = plsc.bitcast(words & jnp.uint32(0xFFFF0000), F32)
                        nlo.append(lo[kk] + wj * lov)
                        nhi.append(hi[kk] + wj * hiv)
                    return tuple(nlo), tuple(nhi)

                lo, hi = lax.fori_loop(0, half, row_body, (lo0, hi0))
                for kk in range(groups):
                    x_v[2 * c, pl.ds(SC_LANES * kk, SC_LANES)] = lo[kk]
                    x_v[2 * c + 1, pl.ds(SC_LANES * kk, SC_LANES)] = hi[kk]

        def token(t, p):
            q = 1 - p
            has_next = t + 1 < first + per_w
            gather(p, 0).wait()
            gather(p, 1).start()

            @pl.when(has_next)
            def _():
                @pl.when(t >= first + 1)
                def _():
                    out_copy(t - 1, q).wait()
                for cp in input_copies(t + 1, q):
                    cp.start()

            accumulate_half(p, 0)
            gather(p, 1).wait()

            @pl.when(has_next)
            def _():
                for cp in input_copies(t + 1, q):
                    cp.wait()
                gather(q, 0).start()

            accumulate_half(p, 1)
            out_copy(t, p).start()

        for cp in input_copies(first, 0):
            cp.start()
        for cp in input_copies(first, 0):
            cp.wait()
        gather(0, 0).start()

        def pair_body(i, carry):
            token(first + 2 * i, 0)
            token(first + 2 * i + 1, 1)
            return carry

        lax.fori_loop(0, per_w // 2, pair_body, 0)
        out_copy(first + per_w - 2, 0).wait()
        out_copy(first + per_w - 1, 1).wait()

    return sc_kernel


def _peer_experts(eid, g, xn3, x13, u_pack, v_pack, n_sc_u, n_sc_v):
    n = xn3.shape[0]
    n_tc_u, n_tc_v = n - n_sc_u, n - n_sc_v
    assert n_tc_u % PB == 0 and n_tc_v % PB == 0
    eid = eid.reshape(n // PB, PB, HK)
    smem_slot = pl.BlockSpec((1, PB, HK), lambda i: (i, 0, 0), memory_space=pltpu.SMEM)
    tok = pl.BlockSpec((PB, SUBLANES, LANES), lambda i: (i, 0, 0))
    flat = lambda w, off=0: pl.BlockSpec((PB, w), lambda i: (i + off, 0))
    table = pl.BlockSpec(u_pack.shape, lambda i: (0, 0), pipeline_mode=pl.Buffered(1))

    def repeat_matrix(times):
        m = (np.arange(HK)[:, None] == (np.arange(HK * times)[None, :] // times)).astype(np.float32)
        return jnp.asarray(m, BF16)

    rep, rep16 = repeat_matrix(SUBLANES), repeat_matrix(SC_LANES)
    gate_shapes = [jax.ShapeDtypeStruct((n, D_MODEL), F32), jax.ShapeDtypeStruct((n, HK * SC_LANES), F32)]
    expert = eid.reshape(n, 2, HK // 2) // ROW_SUB
    if n_sc_u:
        u_rows = u_pack.reshape(N_EXPERTS, ROW_WORDS)
        partial = _sc_u_phase(n_sc_u, n_tc_u)(u_rows, expert, xn3)
    if n_tc_u:
        wrep, w16 = pl.pallas_call(
            _peer_u_kernel,
            grid=(n_tc_u // PB,),
            in_specs=[smem_slot, tok, flat(HK), table, _const_spec(rep.T.shape), _const_spec(rep.shape),
                      _const_spec(rep16.shape)],
            out_specs=[flat(D_MODEL), flat(HK * SC_LANES)],
            out_shape=gate_shapes,
            scratch_shapes=[pltpu.VMEM((PB, D_MODEL), F32)],
            compiler_params=_cparams(("arbitrary",)),
        )(eid, xn3, g, u_pack, rep.T, rep, rep16)
    if n_sc_u:
        off = n_tc_u // PB
        gate_in = [flat(HK * SC_LANES), flat(HK, off), _const_spec(rep16.T.shape), _const_spec(rep.shape),
                   _const_spec(rep16.shape)]
        gate_args = (partial, g, rep16.T, rep, rep16)
        if n_tc_u:
            anyspec = pl.BlockSpec(memory_space=pl.ANY)
            gate_fn, gate_in, gate_args = _gate_kernel, gate_in + [anyspec, anyspec], gate_args + (wrep, w16)
            aliases = {5: 0, 6: 1}
        else:
            gate_fn, aliases = _gate_outputs_kernel, {}
        wrep, w16 = pl.pallas_call(
            gate_fn,
            grid=(n_sc_u // PB,),
            in_specs=gate_in,
            out_specs=[flat(D_MODEL, off), flat(HK * SC_LANES, off)],
            out_shape=gate_shapes,
            input_output_aliases=aliases,
            compiler_params=_cparams(("arbitrary",)),
        )(*gate_args)
    parts = []
    if n_tc_v:
        parts.append(pl.pallas_call(
            _peer_v_kernel,
            grid=(n_tc_v // PB,),
            in_specs=[smem_slot, flat(D_MODEL), tok, table],
            out_specs=flat(D_MODEL),
            out_shape=jax.ShapeDtypeStruct((n_tc_v, D_MODEL), F32),
            compiler_params=_cparams(("arbitrary",)),
        )(eid, wrep, x13, v_pack))
    if n_sc_v:
        v_rows = v_pack.reshape(N_EXPERTS, ROW_WORDS)
        parts.append(_sc_v_phase(n_sc_v, n_tc_v)(v_rows, expert, w16, x13).reshape(n_sc_v, D_MODEL))
    y = parts[0] if len(parts) == 1 else jnp.concatenate(parts, axis=0)
    return y, w16


def _encoder_layer(x, p, after):
    if after is not None:
        x, _ = lax.optimization_barrier((x, after))
    batch, seq, _ = x.shape
    n = batch * seq
    x2d = x.reshape(n, D_MODEL)
    naq, nak, nav, dq, dk, dv = _in_projection(x2d, seq, p["norm1_w"], p["w_in"], p["qk_w"], p["rope"])
    na = _neighbourhood_attention(naq, nak, nav, p["na_bias"], batch, seq)
    df = _diff_attention(dq, dk, dv, p["lam"], p["subln_w"], batch, seq)
    x1, xn, eid, g = _out_and_route(na, df, x2d, p["w_out"], p["norm2_w"], p["w_q"], p["sub_keys"])
    y, _ = _peer_experts(eid, g, xn, x1, p["u_pack"], p["v_pack"], *SC_TOKENS.get(n, (0, 0)))
    return y.reshape(batch, seq, D_MODEL), g


def kernel(x_prompt, x_sample, norm1_w, w_in, na_q_norm, na_k_norm, na_rpb, diff_q_norm, diff_k_norm,
           diff_lambda_q1, diff_lambda_k1, diff_lambda_q2, diff_lambda_k2, diff_subln_w, w_out, norm2_w,
           peer_w_q, peer_sub_keys, peer_u, peer_v):
    assert norm1_w.shape[0] == 1, "single-layer problem"
    reps = SECTION // HEAD_DIM
    params = {
        "norm1_w": norm1_w[0],
        "w_in": w_in[0].astype(BF16),
        "qk_w": jnp.stack([jnp.tile(w[0], reps) for w in (na_q_norm, na_k_norm, diff_q_norm, diff_k_norm)]),
        "lam": jnp.stack([diff_lambda_q1[0], diff_lambda_k1[0], diff_lambda_q2[0], diff_lambda_k2[0]]),
        "subln_w": diff_subln_w[0],
        "w_out": w_out[0].astype(BF16),
        "norm2_w": norm2_w[0],
        "w_q": peer_w_q[0].astype(BF16),
        "sub_keys": peer_sub_keys[0].reshape(2 * PEER_HEADS, N_KEYS, N_KEYS).astype(BF16),
        "na_bias": _na_bias(na_rpb[0]),
        "u_pack": _pack_table(peer_u[0]),
        "v_pack": _pack_table(peer_v[0]),
    }
    rope_by_seq = {}
    outs = [None, None]
    marker = None
    for slot, x in sorted(enumerate((x_prompt, x_sample)), key=lambda item: item[1].shape[0] * item[1].shape[1]):
        seq = x.shape[1]
        if seq not in rope_by_seq:
            rope_by_seq[seq] = _rope_tables(seq)
        outs[slot], marker = _encoder_layer(x, dict(params, rope=rope_by_seq[seq]), marker)
    return tuple(outs)
```

```python
import functools
import math

import jax
import jax.numpy as jnp
import numpy as np
from jax import lax
from jax.experimental import pallas as pl
from jax.experimental.pallas import tpu as pltpu
from jax.experimental.pallas import tpu_sc as plsc

F32 = jnp.float32
BF16 = jnp.bfloat16

D_MODEL = 1024
GRID_W = 64
HEAD_DIM = 64
NA_HEADS = 8
NA_KH = 8
NA_KW = 16
DIFF_HEADS = 4
SECTION = 512
ROT_DIM = HEAD_DIM // 4
ROPE_THETA = 500000.0
PEER_HEADS = 8
N_KEYS = 128
PEER_TOPK = 16
N_EXPERTS = N_KEYS * N_KEYS
HK = PEER_HEADS * PEER_TOPK
EPS = 1e-6
NEG_BIG = -1e30
LOG2E = math.log2(math.e)
LAMBDA_INIT = 0.8 - 0.6 * math.exp(-0.3 * 0)

LANES = 128
SUBLANES = 8
VMEM_LIMIT = 56 * 1024 * 1024

TM_IN = 512
NA_ROWS_PER_STEP = 8
NA_ROW_UNROLL = 2
TQ = 512
KV_SUBCHUNKS = 1
TK = 4096
TM_OUT = 256
TB = 128
PB = 256
SC_TOKENS = {4 * 4096: (4 * 4096, 4 * 4096), 8 * 4096: (14336, 14336)}


def _cparams(sem):
    return pltpu.CompilerParams(dimension_semantics=sem, vmem_limit_bytes=VMEM_LIMIT)


def _const_spec(shape):
    nd = len(shape)
    return pl.BlockSpec(shape, lambda *_: (0,) * nd)


def _inproj_kernel(x_ref, n1_ref, w_ref, bd_ref, nw_ref, rc_ref, ra_ref, rb_ref,
                   naq_ref, nak_ref, nav_ref, dq_ref, dk_ref, dv_ref):
    x = x_ref[...]
    ms = jnp.mean(x * x, axis=-1, keepdims=True)
    h = (x * lax.rsqrt(ms + EPS) * n1_ref[...]).astype(BF16)
    bd = bd_ref[...]

    def proj(c):
        return jnp.dot(h, w_ref[:, c * SECTION:(c + 1) * SECTION], preferred_element_type=F32)

    def qknorm(y, row):
        sq = y * y
        hi = sq.astype(BF16)
        lo = (sq - hi.astype(F32)).astype(BF16)
        msq = (jnp.dot(hi, bd, preferred_element_type=F32)
               + jnp.dot(lo, bd, preferred_element_type=F32))
        return y * lax.rsqrt(msq + EPS) * nw_ref[row:row + 1, :]

    def rope(y):
        return (y * rc_ref[...]
                + pltpu.roll(y, SECTION - ROT_DIM // 2, axis=1) * ra_ref[...]
                + pltpu.roll(y, ROT_DIM // 2, axis=1) * rb_ref[...])

    scale = HEAD_DIM ** -0.5 * LOG2E
    naq_ref[...] = (qknorm(proj(0), 0) * scale).astype(BF16)
    nak_ref[...] = qknorm(proj(1), 1).astype(BF16)
    nav_ref[...] = proj(2).astype(BF16)
    dq_ref[...] = (rope(qknorm(proj(3), 2)) * scale).astype(BF16)
    dk_ref[...] = rope(qknorm(proj(4), 3)).astype(BF16)
    dv_ref[...] = proj(5).astype(BF16)


def _rope_tables(seq):
    pos = jnp.arange(seq, dtype=F32)
    inv = ROPE_THETA ** (-jnp.arange(0, ROT_DIM, 2, dtype=F32) / ROT_DIM)
    ang = pos[:, None] * inv[None, :]
    cos, sin = jnp.cos(ang), jnp.sin(ang)
    half = ROT_DIM // 2
    pad = HEAD_DIM - ROT_DIM
    c_head = jnp.concatenate([cos, cos, jnp.ones((seq, pad), F32)], -1)
    a_head = jnp.concatenate([-sin, jnp.zeros((seq, pad + half), F32)], -1)
    b_head = jnp.concatenate([jnp.zeros((seq, half), F32), sin, jnp.zeros((seq, pad), F32)], -1)
    reps = SECTION // HEAD_DIM
    return (jnp.tile(c_head, (1, reps)), jnp.tile(a_head, (1, reps)), jnp.tile(b_head, (1, reps)))


def _in_projection(x2d, seq, norm1_w, w_in_bf, qk_w, rope_tabs):
    n = x2d.shape[0]
    tm = min(TM_IN, seq)
    steps_per_seq = seq // tm
    gid = np.arange(SECTION) // HEAD_DIM
    bd = jnp.asarray((gid[:, None] == gid[None, :]).astype(np.float32) / HEAD_DIM, BF16)
    tok = pl.BlockSpec((tm, SECTION), lambda i: (i, 0))
    pos = pl.BlockSpec((tm, SECTION), lambda i: (i % steps_per_seq, 0))
    out = jax.ShapeDtypeStruct((n, SECTION), BF16)
    return pl.pallas_call(
        _inproj_kernel,
        grid=(n // tm,),
        in_specs=[pl.BlockSpec((tm, D_MODEL), lambda i: (i, 0)),
                  _const_spec((1, D_MODEL)),
                  _const_spec((D_MODEL, 6 * SECTION)),
                  _const_spec((SECTION, SECTION)),
                  _const_spec((4, SECTION)),
                  pos, pos, pos],
        out_specs=[tok] * 6,
        out_shape=[out] * 6,
        compiler_params=_cparams(("parallel",)),
    )(x2d, norm1_w.reshape(1, D_MODEL), w_in_bf, bd, qk_w, *rope_tabs)


def _na_bias(rpb):
    c = np.arange(GRID_W)[:, None]
    kc = np.arange(GRID_W)[None, :]
    cs = np.clip(c - NA_KW // 2, 0, GRID_W - NA_KW)
    valid = (kc >= cs) & (kc < cs + NA_KW)
    rel_c = kc - c + (NA_KW - 1)
    onehot = ((np.arange(2 * NA_KW - 1)[:, None, None] == rel_c[None]) & valid[None]).astype(np.float32)
    col = jnp.einsum("hrx,xck->hrck", rpb.astype(F32), onehot, precision=lax.Precision.HIGHEST)
    col = jnp.where(valid, col * LOG2E, NEG_BIG)
    per_off = [col[:, NA_KH - 1 - o:2 * NA_KH - 1 - o] for o in range(NA_KH)]
    b = jnp.stack(per_off).transpose(0, 1, 3, 2, 4)
    return b.reshape(NA_KH, NA_HEADS, GRID_W, NA_KH * GRID_W)


def _na_kernel(q_ref, k_ref, v_ref, b_ref, o_ref, *, rows):
    j = pl.program_id(1)
    win = NA_KH * GRID_W
    pair = 2 * HEAD_DIM
    lane = lax.broadcasted_iota(jnp.int32, (GRID_W, pair), 1)
    first = lane < HEAD_DIM
    ones = jnp.ones((win, pair), BF16)

    def one_row(rr):
        r = j * NA_ROWS_PER_STEP + rr
        rs = jnp.clip(r - NA_KH // 2, 0, rows - NA_KH)
        off = r - rs
        kstart = pl.multiple_of(rs * GRID_W, GRID_W)
        qstart = pl.multiple_of(rr * GRID_W, GRID_W)
        outs = []
        for hp in range(NA_HEADS // 2):
            sl = slice(hp * pair, (hp + 1) * pair)
            qp = q_ref[pl.ds(qstart, GRID_W), sl]
            kp = k_ref[pl.ds(kstart, win), sl]
            v_ext = jnp.concatenate([v_ref[pl.ds(kstart, win), sl], ones], axis=1)
            halves = []
            for hh in range(2):
                qm = jnp.where(first if hh == 0 else ~first, qp, jnp.zeros_like(qp))
                s = lax.dot_general(qm, kp, (((1,), (1,)), ((), ())), preferred_element_type=F32)
                s = s + b_ref[off, 2 * hp + hh]
                p = jnp.exp2(s - jnp.max(s, axis=-1, keepdims=True)).astype(BF16)
                z = jnp.dot(p, v_ext, preferred_element_type=F32)
                halves.append(z[:, :pair] / z[:, pair:pair + 1])
            outs.append(jnp.where(first, halves[0], halves[1]))
        o_ref[pl.ds(qstart, GRID_W), :] = jnp.concatenate(outs, axis=-1).astype(BF16)

    def row_body(i, carry):
        for u in range(NA_ROW_UNROLL):
            one_row(i * NA_ROW_UNROLL + u)
        return carry

    lax.fori_loop(0, NA_ROWS_PER_STEP // NA_ROW_UNROLL, row_body, 0)


def _neighbourhood_attention(q, k, v, bias, batch, seq):
    rows = seq // GRID_W
    assert rows >= NA_KH and rows % NA_ROWS_PER_STEP == 0
    nblk = rows // NA_ROWS_PER_STEP
    tq = NA_ROWS_PER_STEP * GRID_W
    qspec = pl.BlockSpec((tq, SECTION), lambda b, j: (b * nblk + j, 0))
    kvspec = pl.BlockSpec((seq, SECTION), lambda b, j: (b, 0))
    return pl.pallas_call(
        functools.partial(_na_kernel, rows=rows),
        grid=(batch, nblk),
        in_specs=[qspec, kvspec, kvspec, _const_spec(bias.shape)],
        out_specs=qspec,
        out_shape=jax.ShapeDtypeStruct(q.shape, BF16),
        compiler_params=_cparams(("parallel", "arbitrary")),
    )(q, k, v, bias)


def _diff_kernel(q_ref, k_ref, v_ref, lam_ref, sub_ref, o_ref, m0_scr, m1_scr, acc0_scr, acc1_scr,
                 *, seq, tq, tk):
    m_scrs, acc_scrs = (m0_scr, m1_scr), (acc0_scr, acc1_scr)
    q = q_ref[...]
    lane = lax.broadcasted_iota(jnp.int32, q.shape, 1)
    zero = jnp.zeros_like(q)
    q_maps = (jnp.where(lane < HEAD_DIM, q, zero), jnp.where(lane >= HEAD_DIM, q, zero))

    lp = lam_ref[...]
    lam = (jnp.exp(jnp.sum(lp[0:1] * lp[1:2], axis=-1, keepdims=True))
           - jnp.exp(jnp.sum(lp[2:3] * lp[3:4], axis=-1, keepdims=True)) + LAMBDA_INIT)

    width = 2 * HEAD_DIM
    ones = jnp.ones((tk, width), BF16)

    for m_scr, acc_scr in zip(m_scrs, acc_scrs):
        m_scr[...] = jnp.full(m_scr.shape, -jnp.inf, F32)
        acc_scr[...] = jnp.zeros(acc_scr.shape, F32)

    def kv_body(c, carry):
        tks = tk // KV_SUBCHUNKS
        kcs, v_exts = [], []
        for sub in range(KV_SUBCHUNKS):
            start = pl.multiple_of(c * tk + sub * tks, tks)
            kcs.append(k_ref[pl.ds(start, tks), :])
            v_exts.append(jnp.concatenate([v_ref[pl.ds(start, tks), :], ones[:tks]], axis=1))
        scores = [[lax.dot_general(q_maps[mp], kcs[sub], (((1,), (1,)), ((), ())),
                                   preferred_element_type=F32) for mp in range(2)]
                  for sub in range(KV_SUBCHUNKS)]
        m = [m_scrs[mp][...] for mp in range(2)]
        probs, alphas = [], []
        for sub in range(KV_SUBCHUNKS):
            probs.append([])
            alphas.append([])
            for mp in range(2):
                mn = jnp.maximum(m[mp], jnp.max(scores[sub][mp], axis=-1, keepdims=True))
                probs[sub].append(jnp.exp2(scores[sub][mp] - mn[:, 0:1]).astype(BF16))
                alpha = jnp.exp2(m[mp] - mn)
                alphas[sub].append(jnp.concatenate([alpha, alpha], axis=1))
                m[mp] = mn
        for mp in range(2):
            m_scrs[mp][...] = m[mp]
            acc = acc_scrs[mp][...]
            for sub in range(KV_SUBCHUNKS):
                acc = alphas[sub][mp] * acc + jnp.dot(probs[sub][mp], v_exts[sub],
                                                      preferred_element_type=F32)
            acc_scrs[mp][...] = acc
        return carry

    lax.fori_loop(0, seq // tk, kv_body, 0)
    a0, a1 = acc0_scr[...], acc1_scr[...]
    o = a0[:, :width] / a0[:, width:width + 1] - lam * (a1[:, :width] / a1[:, width:width + 1])
    ms = jnp.mean(o * o, axis=-1, keepdims=True)
    y = o * lax.rsqrt(ms + EPS) * sub_ref[...]
    o_ref[...] = (y * (1.0 - LAMBDA_INIT)).astype(BF16)


def _diff_attention(q, k, v, lam_params, subln_w, batch, seq):
    tq, tk = min(TQ, seq), min(TK, seq)
    nq = seq // tq
    width = 2 * HEAD_DIM
    qspec = pl.BlockSpec((tq, width), lambda b, h, i: (b * nq + i, h))
    kvspec = pl.BlockSpec((seq, width), lambda b, h, i: (b, h))
    return pl.pallas_call(
        functools.partial(_diff_kernel, seq=seq, tq=tq, tk=tk),
        grid=(batch, DIFF_HEADS, nq),
        in_specs=[qspec, kvspec, kvspec, _const_spec((4, HEAD_DIM)), _const_spec((1, width))],
        out_specs=qspec,
        out_shape=jax.ShapeDtypeStruct(q.shape, BF16),
        scratch_shapes=[pltpu.VMEM((tq, width), F32), pltpu.VMEM((tq, width), F32),
                        pltpu.VMEM((tq, 2 * width), F32), pltpu.VMEM((tq, 2 * width), F32)],
        compiler_params=_cparams(("parallel", "parallel", "arbitrary")),
    )(q, k, v, lam_params, subln_w.reshape(1, width))


def _top16(sc, payload=None):
    iota = lax.broadcasted_iota(jnp.int32, sc.shape, 0).astype(F32)
    big = float(sc.shape[0])
    vals, picks = [], []
    for _ in range(PEER_TOPK):
        m = jnp.max(sc, axis=0, keepdims=True)
        ix = jnp.min(jnp.where(sc == m, iota, big), axis=0, keepdims=True)
        sel = iota == ix
        if payload is None:
            picks.append(ix)
        else:
            picks.append(jnp.sum(jnp.where(sel, payload, 0), axis=0, keepdims=True))
        sc = jnp.where(sel, -jnp.inf, sc)
        vals.append(m)
    picks = jnp.concatenate(picks, axis=0)
    return jnp.concatenate(vals, axis=0), picks.astype(jnp.int32)


def _pair_candidates(r0, r1):
    half = SUBLANES // 2
    groups = [r0[0:1] + r1[0:8], r0[0:1] + r1[8:16]]
    groups += [r0[a:a + 1] + r1[0:8] for a in (1, 2, 3)]
    groups += [jnp.concatenate([r0[a:a + 1] + r1[0:half], r0[a + 1:a + 2] + r1[0:half]], axis=0)
               for a in (4, 6)]
    groups.append(r0[8:16] + r1[0:1])
    return jnp.concatenate(groups, axis=0)


def _route_kernel(na_ref, df_ref, x_ref, wo_ref, n2_ref, wq_ref, sk_ref,
                  x1_ref, xn_ref, eid_ref, g_ref, q_scr, eid_scr, g_scr):
    tm = x_ref.shape[0]
    x1 = (x_ref[...]
          + jnp.dot(na_ref[...], wo_ref[0:SECTION, :], preferred_element_type=F32)
          + jnp.dot(df_ref[...], wo_ref[SECTION:2 * SECTION, :], preferred_element_type=F32))
    ms = jnp.mean(x1 * x1, axis=-1, keepdims=True)
    xn = x1 * lax.rsqrt(ms + EPS) * n2_ref[...]
    for r in range(SUBLANES):
        x1_ref[:, r, :] = x1[:, r * LANES:(r + 1) * LANES]
        xn_ref[:, r, :] = xn[:, r * LANES:(r + 1) * LANES]
    xb = xn.astype(BF16)
    for c in range(2 * PEER_HEADS):
        q_scr[c] = jnp.dot(xb, wq_ref[:, c * LANES:(c + 1) * LANES],
                           preferred_element_type=F32).astype(BF16)

    nblk = tm // TB

    def route_block(h, blk):
        tok0 = blk * TB
        tops = []
        for p in range(2):
            sc = lax.dot_general(sk_ref[2 * h + p], q_scr[2 * h + p, pl.ds(tok0, TB), :],
                                 (((1,), (1,)), ((), ())), preferred_element_type=F32)
            tops.append(_top16(sc))
        (s0, i0), (s1, i1) = tops
        cand_s = _pair_candidates(s0, s1)
        cand_i = _pair_candidates(i0 * N_KEYS, i1)
        top_s, eid = _top16(cand_s, cand_i)
        e = jnp.exp(top_s - top_s[0:1])
        row = pl.multiple_of(h * PEER_TOPK, PEER_TOPK)
        eid_scr[blk, pl.ds(row, PEER_TOPK), :] = eid * ROW_SUB
        g_scr[blk, pl.ds(row, PEER_TOPK), :] = e / jnp.sum(e, axis=0, keepdims=True)

    def head_body(h, carry):
        for blk in range(nblk):
            route_block(h, blk)
        return carry

    lax.fori_loop(0, PEER_HEADS, head_body, 0)
    for blk in range(nblk):
        eid_ref[blk] = eid_scr[blk].T
        g_ref[blk * TB:(blk + 1) * TB, :] = g_scr[blk].T


def _out_and_route(na, df, x2d, w_out_bf, norm2_w, w_q_bf, sk_bf):
    n = x2d.shape[0]
    tm = TM_OUT
    tok = lambda w: pl.BlockSpec((tm, w), lambda i: (i, 0))
    tiles = pl.BlockSpec((tm, SUBLANES, LANES), lambda i: (i, 0, 0))
    return pl.pallas_call(
        _route_kernel,
        grid=(n // tm,),
        in_specs=[tok(SECTION), tok(SECTION), tok(D_MODEL),
                  _const_spec((D_MODEL, D_MODEL)), _const_spec((1, D_MODEL)),
                  _const_spec(w_q_bf.shape), _const_spec(sk_bf.shape)],
        out_specs=[tiles, tiles,
                   pl.BlockSpec((tm // TB, TB, HK), lambda i: (i, 0, 0)), tok(HK)],
        out_shape=[jax.ShapeDtypeStruct((n, SUBLANES, LANES), F32),
                   jax.ShapeDtypeStruct((n, SUBLANES, LANES), F32),
                   jax.ShapeDtypeStruct((n // TB, TB, HK), jnp.int32),
                   jax.ShapeDtypeStruct((n, HK), F32)],
        scratch_shapes=[pltpu.VMEM((2 * PEER_HEADS, tm, LANES), BF16),
                        pltpu.VMEM((tm // TB, HK, TB), jnp.int32), pltpu.VMEM((tm // TB, HK, TB), F32)],
        compiler_params=_cparams(("parallel",)),
    )(na, df, x2d, w_out_bf, norm2_w.reshape(1, D_MODEL), w_q_bf, sk_bf)


ROW_SUB = SUBLANES // 2
CHUNK_SLOTS = 32
N_CHUNKS = HK // CHUNK_SLOTS
CHUNK_ROWS = CHUNK_SLOTS * SUBLANES
TOKEN_UNROLL = 8
ROW_WORDS = ROW_SUB * LANES
SC_CORES, SC_SUBCORES, SC_LANES = 2, 16, 16
SC_WORKERS = SC_CORES * SC_SUBCORES


def _pack_table(tab):
    bits = lax.bitcast_convert_type(tab.astype(BF16), jnp.uint16).astype(jnp.uint32)
    bits = bits.reshape(tab.shape[0], ROW_SUB, 2, LANES)
    return (bits[:, :, 0, :] | (bits[:, :, 1, :] << 16)).reshape(tab.shape[0] * ROW_SUB, LANES)


def _split_bf16(x):
    hi = x.astype(BF16)
    return hi, (x - hi.astype(F32)).astype(BF16)


def _gather_chunk(tab_ref, eid_ref, t, chunk):
    ids = eid_ref.at[0, t]
    rows = [tab_ref[pl.ds(pl.multiple_of(ids[chunk * CHUNK_SLOTS + i], ROW_SUB), ROW_SUB), :]
            for i in range(CHUNK_SLOTS)]
    return pltpu.bitcast(jnp.concatenate(rows, axis=0), BF16)


def _diag_mask(width):
    r = lax.broadcasted_iota(jnp.int32, (SUBLANES, width), 0)
    n = lax.broadcasted_iota(jnp.int32, (SUBLANES, width), 1)
    return (n % SUBLANES) == r


def _peer_u_kernel(eid_ref, xn_ref, g_ref, tab_ref, rept_ref, rep_ref, rep16_ref, wrep_ref, w16_ref, d_scr):
    mask = _diag_mask(CHUNK_ROWS)
    zeros = jnp.zeros((SUBLANES, LANES), BF16)

    def one_token(t):
        xhi, xlo = _split_bf16(xn_ref[t])
        lhs = jnp.concatenate([jnp.concatenate([xhi, zeros], axis=1),
                               jnp.concatenate([xlo, zeros], axis=1),
                               jnp.concatenate([zeros, xhi], axis=1),
                               jnp.concatenate([zeros, xlo], axis=1)], axis=0)
        for pr in range(N_CHUNKS // 2):
            s = jnp.concatenate([_gather_chunk(tab_ref, eid_ref, t, 2 * pr),
                                 _gather_chunk(tab_ref, eid_ref, t, 2 * pr + 1)], axis=1)
            z = lax.dot_general(lhs, s, (((1,), (1,)), ((), ())), preferred_element_type=F32)
            for half in range(2):
                z8 = z[16 * half:16 * half + 8] + z[16 * half + 8:16 * half + 16]
                zs = jnp.sum(jnp.where(mask, z8, 0.0), axis=0, keepdims=True)
                d_scr[pl.ds(t, 1), pl.ds((2 * pr + half) * CHUNK_ROWS, CHUNK_ROWS)] = zs

    def tok_body(i, carry):
        for u in range(TOKEN_UNROLL):
            one_token(i * TOKEN_UNROLL + u)
        return carry

    lax.fori_loop(0, PB // TOKEN_UNROLL, tok_body, 0)
    _gate_outputs(d_scr[...], rept_ref, g_ref, rep_ref, rep16_ref, wrep_ref, w16_ref)


def _gate_outputs(partial, sum_ref, g_ref, rep_ref, rep16_ref, wrep_ref, w16_ref):
    dhi, dlo = _split_bf16(partial)
    a = (jnp.dot(dhi, sum_ref[...], preferred_element_type=F32)
         + jnp.dot(dlo, sum_ref[...], preferred_element_type=F32))
    w = g_ref[...] * (0.5 * a * (1.0 + lax.erf(a * (2.0 ** -0.5))))
    whi, wlo = _split_bf16(w)
    wrep_ref[...] = (jnp.dot(whi, rep_ref[...], preferred_element_type=F32)
                     + jnp.dot(wlo, rep_ref[...], preferred_element_type=F32))
    w16_ref[...] = (jnp.dot(whi, rep16_ref[...], preferred_element_type=F32)
                    + jnp.dot(wlo, rep16_ref[...], preferred_element_type=F32))


def _gate_outputs_kernel(part_ref, g_ref, sum_ref, rep_ref, rep16_ref, wrep_ref, w16_ref):
    _gate_outputs(part_ref[...], sum_ref, g_ref, rep_ref, rep16_ref, wrep_ref, w16_ref)


def _gate_kernel(part_ref, g_ref, sum_ref, rep_ref, rep16_ref, wrep_in, w16_in, wrep_ref, w16_ref):
    del wrep_in, w16_in
    _gate_outputs(part_ref[...], sum_ref, g_ref, rep_ref, rep16_ref, wrep_ref, w16_ref)


def _peer_v_kernel(eid_ref, wrep_ref, x1_ref, tab_ref, o_ref):
    mask = _diag_mask(CHUNK_ROWS)

    def one_token(t):
        wr = wrep_ref[pl.ds(t, 1), :]
        acc = x1_ref[t]
        for pr in range(N_CHUNKS // 2):
            parts = []
            for half in range(2):
                seg = wr[:, (2 * pr + half) * CHUNK_ROWS:(2 * pr + half + 1) * CHUNK_ROWS]
                parts.extend(_split_bf16(jnp.where(mask, jnp.broadcast_to(seg, mask.shape), 0.0)))
            lhs = jnp.concatenate(parts, axis=0)
            s = jnp.concatenate([_gather_chunk(tab_ref, eid_ref, t, 2 * pr),
                                 _gather_chunk(tab_ref, eid_ref, t, 2 * pr + 1)], axis=1)
            z = jnp.dot(lhs, s, preferred_element_type=F32)
            acc = acc + ((z[0:8, :LANES] + z[8:16, :LANES]) + (z[16:24, LANES:] + z[24:32, LANES:]))
        return acc

    def tok_body(i, carry):
        accs = [one_token(i * TOKEN_UNROLL + u) for u in range(TOKEN_UNROLL)]
        base = pl.multiple_of(i * TOKEN_UNROLL, TOKEN_UNROLL)
        for r in range(SUBLANES):
            o_ref[pl.ds(base, TOKEN_UNROLL), r * LANES:(r + 1) * LANES] = jnp.concatenate(
                [acc[r:r + 1, :] for acc in accs], axis=0)
        return carry

    lax.fori_loop(0, PB // TOKEN_UNROLL, tok_body, 0)


def _sc_scratch(mid_shape, last_shape):
    half = HK // 2
    per_parity = [pltpu.VMEM((2, half), jnp.int32), pltpu.VMEM(mid_shape, F32), pltpu.VMEM(last_shape, F32)]
    return per_parity + per_parity + [
        pltpu.VMEM((half, ROW_WORDS), jnp.uint32), pltpu.VMEM((half, ROW_WORDS), jnp.uint32),
        pltpu.SemaphoreType.DMA, pltpu.SemaphoreType.DMA,
        pltpu.SemaphoreType.DMA, pltpu.SemaphoreType.DMA,
        pltpu.SemaphoreType.DMA, pltpu.SemaphoreType.DMA,
    ]


def _sc_u_phase(n_tok, tok0):
    per_w = n_tok // SC_WORKERS
    assert per_w * SC_WORKERS == n_tok and per_w % 2 == 0
    half = HK // 2
    mesh = plsc.VectorSubcoreMesh(core_axis_name="c", subcore_axis_name="s")

    @functools.partial(
        pl.kernel, mesh=mesh,
        out_type=jax.ShapeDtypeStruct((n_tok, HK * SC_LANES), F32),
        scratch_types=_sc_scratch((SUBLANES, LANES), (HK * SC_LANES,)),
        compiler_params=pltpu.CompilerParams(needs_layout_passes=False),
    )
    def sc_kernel(tab_hbm, eid_hbm, xn_hbm, out_hbm,
                  idx_a, x_a, a_a, idx_b, x_b, a_b, g0, g1, s_a, s_b, sg0, sg1, so_a, so_b):
        wid = lax.axis_index("s") * SC_CORES + lax.axis_index("c")
        first = wid * per_w
        par = ((idx_a, x_a, a_a, s_a, so_a), (idx_b, x_b, a_b, s_b, so_b))
        gbuf, gsem = (g0, g1), (sg0, sg1)

        def input_copies(t, p):
            idx_v, x_v, _, sem, _ = par[p]
            return (pltpu.make_async_copy(eid_hbm.at[tok0 + t], idx_v, sem),
                    pltpu.make_async_copy(xn_hbm.at[tok0 + t], x_v, sem))

        def gather(p, h):
            return pltpu.make_async_copy(tab_hbm.at[par[p][0].at[h]], gbuf[h], gsem[h])

        def out_copy(t, p):
            return pltpu.make_async_copy(par[p][2], out_hbm.at[t], par[p][4])

        def dot_half(p, h):
            _, x_v, a_v, _, _ = par[p]
            g = gbuf[h]
            groups = LANES // SC_LANES
            block = SUBLANES
            for rb in range(half // block):
                def col_body(k, accs):
                    c, kk = k // groups, k % groups
                    lane0 = pl.multiple_of(kk * SC_LANES, SC_LANES)
                    xl = x_v[2 * c, pl.ds(lane0, SC_LANES)]
                    xh = x_v[2 * c + 1, pl.ds(lane0, SC_LANES)]
                    out = []
                    for r in range(block):
                        words = g[rb * block + r, pl.ds(pl.multiple_of(k * SC_LANES, SC_LANES), SC_LANES)]
                        lov = plsc.bitcast(lax.shift_left(words, jnp.uint32(16)), F32)
                        hiv = plsc.bitcast(words & jnp.uint32(0xFFFF0000), F32)
                        out.append(accs[r] + lov * xl + hiv * xh)
                    return tuple(out)

                zero = jnp.zeros((SC_LANES,), F32)
                accs = lax.fori_loop(0, ROW_WORDS // SC_LANES, col_body, (zero,) * block)
                for r in range(block):
                    a_v[pl.ds((h * half + rb * block + r) * SC_LANES, SC_LANES)] = accs[r]

        def token(t, p):
            q = 1 - p
            has_next = t + 1 < first + per_w
            gather(p, 0).wait()
            gather(p, 1).start()

            @pl.when(has_next)
            def _():
                for cp in input_copies(t + 1, q):
                    cp.start()

            @pl.when(t >= first + 2)
            def _():
                out_copy(t - 2, p).wait()

            dot_half(p, 0)
            gather(p, 1).wait()

            @pl.when(has_next)
            def _():
                for cp in input_copies(t + 1, q):
                    cp.wait()
                gather(q, 0).start()

            dot_half(p, 1)
            out_copy(t, p).start()

        for cp in input_copies(first, 0):
            cp.start()
        for cp in input_copies(first, 0):
            cp.wait()
        gather(0, 0).start()

        def pair_body(i, carry):
            token(first + 2 * i, 0)
            token(first + 2 * i + 1, 1)
            return carry

        lax.fori_loop(0, per_w // 2, pair_body, 0)
        out_copy(first + per_w - 2, 0).wait()
        out_copy(first + per_w - 1, 1).wait()

    return sc_kernel


def _sc_v_phase(n_tok, tok0):
    per_w = n_tok // SC_WORKERS
    assert per_w * SC_WORKERS == n_tok and per_w % 2 == 0
    half = HK // 2
    mesh = plsc.VectorSubcoreMesh(core_axis_name="c", subcore_axis_name="s")

    @functools.partial(
        pl.kernel, mesh=mesh,
        out_type=jax.ShapeDtypeStruct((n_tok, SUBLANES, LANES), F32),
        scratch_types=_sc_scratch((HK * SC_LANES,), (SUBLANES, LANES)),
        compiler_params=pltpu.CompilerParams(needs_layout_passes=False),
    )
    def sc_kernel(tab_hbm, eid_hbm, w16_hbm, x1_hbm, out_hbm,
                  idx_a, w_a, x_a, idx_b, w_b, x_b, g0, g1, s_a, s_b, sg0, sg1, so_a, so_b):
        wid = lax.axis_index("s") * SC_CORES + lax.axis_index("c")
        first = wid * per_w
        par = ((idx_a, w_a, x_a, s_a, so_a), (idx_b, w_b, x_b, s_b, so_b))
        gbuf, gsem = (g0, g1), (sg0, sg1)

        def input_copies(t, p):
            idx_v, w_v, x_v, sem, _ = par[p]
            return (pltpu.make_async_copy(eid_hbm.at[tok0 + t], idx_v, sem),
                    pltpu.make_async_copy(w16_hbm.at[tok0 + t], w_v, sem),
                    pltpu.make_async_copy(x1_hbm.at[tok0 + t], x_v, sem))

        def gather(p, h):
            return pltpu.make_async_copy(tab_hbm.at[par[p][0].at[h]], gbuf[h], gsem[h])

        def out_copy(t, p):
            return pltpu.make_async_copy(par[p][2], out_hbm.at[t], par[p][4])

        def accumulate_half(p, h):
            _, w_v, x_v, _, _ = par[p]
            g = gbuf[h]
            groups = LANES // SC_LANES
            for c in range(ROW_SUB):
                lo0 = tuple(x_v[2 * c, pl.ds(SC_LANES * kk, SC_LANES)] for kk in range(groups))
                hi0 = tuple(x_v[2 * c + 1, pl.ds(SC_LANES * kk, SC_LANES)] for kk in range(groups))

                def row_body(j, acc):
                    lo, hi = acc
                    wj = w_v[pl.ds(pl.multiple_of((h * half + j) * SC_LANES, SC_LANES), SC_LANES)]
                    nlo, nhi = [], []
                    for kk in range(groups):
                        words = g[j, pl.ds(LANES * c + SC_LANES * kk, SC_LANES)]
                        lov = plsc.bitcast(lax.shift_left(words, jnp.uint32(16)), F32)
                        hiv = plsc.bitcast(words & jnp.uint32(0xFFFF0000), F32)
                        nlo.append(lo[kk] + wj * lov)
                        nhi.append(hi[kk] + wj * hiv)
                    return tuple(nlo), tuple(nhi)

                lo, hi = lax.fori_loop(0, half, row_body, (lo0, hi0))
                for kk in range(groups):
                    x_v[2 * c, pl.ds(SC_LANES * kk, SC_LANES)] = lo[kk]
                    x_v[2 * c + 1, pl.ds(SC_LANES * kk, SC_LANES)] = hi[kk]

        def token(t, p):
            q = 1 - p
            has_next = t + 1 < first + per_w
            gather(p, 0).wait()
            gather(p, 1).start()

            @pl.when(has_next)
            def _():
                @pl.when(t >= first + 1)
                def _():
                    out_copy(t - 1, q).wait()
                for cp in input_copies(t + 1, q):
                    cp.start()

            accumulate_half(p, 0)
            gather(p, 1).wait()

            @pl.when(has_next)
            def _():
                for cp in input_copies(t + 1, q):
                    cp.wait()
                gather(q, 0).start()

            accumulate_half(p, 1)
            out_copy(t, p).start()

        for cp in input_copies(first, 0):
            cp.start()
        for cp in input_copies(first, 0):
            cp.wait()
        gather(0, 0).start()

        def pair_body(i, carry):
            token(first + 2 * i, 0)
            token(first + 2 * i + 1, 1)
            return carry

        lax.fori_loop(0, per_w // 2, pair_body, 0)
        out_copy(first + per_w - 2, 0).wait()
        out_copy(first + per_w - 1, 1).wait()

    return sc_kernel


def _peer_experts(eid, g, xn3, x13, u_pack, v_pack, n_sc_u, n_sc_v):
    n = xn3.shape[0]
    n_tc_u, n_tc_v = n - n_sc_u, n - n_sc_v
    assert n_tc_u % PB == 0 and n_tc_v % PB == 0
    eid = eid.reshape(n // PB, PB, HK)
    smem_slot = pl.BlockSpec((1, PB, HK), lambda i: (i, 0, 0), memory_space=pltpu.SMEM)
    tok = pl.BlockSpec((PB, SUBLANES, LANES), lambda i: (i, 0, 0))
    flat = lambda w, off=0: pl.BlockSpec((PB, w), lambda i: (i + off, 0))
    table = pl.BlockSpec(u_pack.shape, lambda i: (0, 0), pipeline_mode=pl.Buffered(1))

    def repeat_matrix(times):
        m = (np.arange(HK)[:, None] == (np.arange(HK * times)[None, :] // times)).astype(np.float32)
        return jnp.asarray(m, BF16)

    rep, rep16 = repeat_matrix(SUBLANES), repeat_matrix(SC_LANES)
    gate_shapes = [jax.ShapeDtypeStruct((n, D_MODEL), F32), jax.ShapeDtypeStruct((n, HK * SC_LANES), F32)]
    expert = eid.reshape(n, 2, HK // 2) // ROW_SUB
    if n_sc_u:
        u_rows = u_pack.reshape(N_EXPERTS, ROW_WORDS)
        partial = _sc_u_phase(n_sc_u, n_tc_u)(u_rows, expert, xn3)
    if n_tc_u:
        wrep, w16 = pl.pallas_call(
            _peer_u_kernel,
            grid=(n_tc_u // PB,),
            in_specs=[smem_slot, tok, flat(HK), table, _const_spec(rep.T.shape), _const_spec(rep.shape),
                      _const_spec(rep16.shape)],
            out_specs=[flat(D_MODEL), flat(HK * SC_LANES)],
            out_shape=gate_shapes,
            scratch_shapes=[pltpu.VMEM((PB, D_MODEL), F32)],
            compiler_params=_cparams(("arbitrary",)),
        )(eid, xn3, g, u_pack, rep.T, rep, rep16)
    if n_sc_u:
        off = n_tc_u // PB
        gate_in = [flat(HK * SC_LANES), flat(HK, off), _const_spec(rep16.T.shape), _const_spec(rep.shape),
                   _const_spec(rep16.shape)]
        gate_args = (partial, g, rep16.T, rep, rep16)
        if n_tc_u:
            anyspec = pl.BlockSpec(memory_space=pl.ANY)
            gate_fn, gate_in, gate_args = _gate_kernel, gate_in + [anyspec, anyspec], gate_args + (wrep, w16)
            aliases = {5: 0, 6: 1}
        else:
            gate_fn, aliases = _gate_outputs_kernel, {}
        wrep, w16 = pl.pallas_call(
            gate_fn,
            grid=(n_sc_u // PB,),
            in_specs=gate_in,
            out_specs=[flat(D_MODEL, off), flat(HK * SC_LANES, off)],
            out_shape=gate_shapes,
            input_output_aliases=aliases,
            compiler_params=_cparams(("arbitrary",)),
        )(*gate_args)
    parts = []
    if n_tc_v:
        parts.append(pl.pallas_call(
            _peer_v_kernel,
            grid=(n_tc_v // PB,),
            in_specs=[smem_slot, flat(D_MODEL), tok, table],
            out_specs=flat(D_MODEL),
            out_shape=jax.ShapeDtypeStruct((n_tc_v, D_MODEL), F32),
            compiler_params=_cparams(("arbitrary",)),
        )(eid, wrep, x13, v_pack))
    if n_sc_v:
        v_rows = v_pack.reshape(N_EXPERTS, ROW_WORDS)
        parts.append(_sc_v_phase(n_sc_v, n_tc_v)(v_rows, expert, w16, x13).reshape(n_sc_v, D_MODEL))
    y = parts[0] if len(parts) == 1 else jnp.concatenate(parts, axis=0)
    return y, w16


def _encoder_layer(x, p, after):
    if after is not None:
        x, _ = lax.optimization_barrier((x, after["route"]))
    batch, seq, _ = x.shape
    n = batch * seq
    x2d = x.reshape(n, D_MODEL)
    naq, nak, nav, dq, dk, dv = _in_projection(x2d, seq, p["norm1_w"], p["w_in"], p["qk_w"], p["rope"])
    na = _neighbourhood_attention(naq, nak, nav, p["na_bias"], batch, seq)
    df = _diff_attention(dq, dk, dv, p["lam"], p["subln_w"], batch, seq)
    if after is not None:
        na, _ = lax.optimization_barrier((na, after["gate"]))
    x1, xn, eid, g = _out_and_route(na, df, x2d, p["w_out"], p["norm2_w"], p["w_q"], p["sub_keys"])
    g_peer = g
    if after is not None:
        g_peer, _ = lax.optimization_barrier((g, after["out"]))
    y, w16 = _peer_experts(eid, g_peer, xn, x1, p["u_pack"], p["v_pack"], *SC_TOKENS.get(n, (0, 0)))
    return y.reshape(batch, seq, D_MODEL), {"route": g, "gate": w16, "out": y}


def kernel(x_prompt, x_sample, norm1_w, w_in, na_q_norm, na_k_norm, na_rpb, diff_q_norm, diff_k_norm,
           diff_lambda_q1, diff_lambda_k1, diff_lambda_q2, diff_lambda_k2, diff_subln_w, w_out, norm2_w,
           peer_w_q, peer_sub_keys, peer_u, peer_v):
    assert norm1_w.shape[0] == 1, "single-layer problem"
    reps = SECTION // HEAD_DIM
    params = {
        "norm1_w": norm1_w[0],
        "w_in": w_in[0].astype(BF16),
        "qk_w": jnp.stack([jnp.tile(w[0], reps) for w in (na_q_norm, na_k_norm, diff_q_norm, diff_k_norm)]),
        "lam": jnp.stack([diff_lambda_q1[0], diff_lambda_k1[0], diff_lambda_q2[0], diff_lambda_k2[0]]),
        "subln_w": diff_subln_w[0],
        "w_out": w_out[0].astype(BF16),
        "norm2_w": norm2_w[0],
        "w_q": peer_w_q[0].astype(BF16),
        "sub_keys": peer_sub_keys[0].reshape(2 * PEER_HEADS, N_KEYS, N_KEYS).astype(BF16),
        "na_bias": _na_bias(na_rpb[0]),
        "u_pack": _pack_table(peer_u[0]),
        "v_pack": _pack_table(peer_v[0]),
    }
    rope_by_seq = {}
    outs = [None, None]
    marker = None
    for slot, x in sorted(enumerate((x_prompt, x_sample)), key=lambda item: item[1].shape[0] * item[1].shape[1]):
        seq = x.shape[1]
        if seq not in rope_by_seq:
            rope_by_seq[seq] = _rope_tables(seq)
        outs[slot], marker = _encoder_layer(x, dict(params, rope=rope_by_seq[seq]), marker)
    return tuple(outs)
```

```python
import functools
import math

import jax
import jax.numpy as jnp
import numpy as np
from jax import lax
from jax.experimental import pallas as pl
from jax.experimental.pallas import tpu as pltpu
from jax.experimental.pallas import tpu_sc as plsc

F32 = jnp.float32
BF16 = jnp.bfloat16

D_MODEL = 1024
GRID_W = 64
HEAD_DIM = 64
NA_HEADS = 8
NA_KH = 8
NA_KW = 16
DIFF_HEADS = 4
SECTION = 512
ROT_DIM = HEAD_DIM // 4
ROPE_THETA = 500000.0
PEER_HEADS = 8
N_KEYS = 128
PEER_TOPK = 16
N_EXPERTS = N_KEYS * N_KEYS
HK = PEER_HEADS * PEER_TOPK
EPS = 1e-6
NEG_BIG = -1e30
LOG2E = math.log2(math.e)
LAMBDA_INIT = 0.8 - 0.6 * math.exp(-0.3 * 0)

LANES = 128
SUBLANES = 8
VMEM_LIMIT = 56 * 1024 * 1024

TM_IN = 512
NA_ROWS_PER_STEP = 8
NA_ROW_UNROLL = 2
TQ = 512
KV_SUBCHUNKS = 1
TK = 4096
TM_OUT = 256
TB = 128
PB = 256
SC_TOKENS = {8 * 4096: (14336, 8 * 4096), 4 * 4096: (0, 5632)}


def _cparams(sem):
    return pltpu.CompilerParams(dimension_semantics=sem, vmem_limit_bytes=VMEM_LIMIT)


def _const_spec(shape):
    nd = len(shape)
    return pl.BlockSpec(shape, lambda *_: (0,) * nd)


def _inproj_kernel(x_ref, n1_ref, w_ref, bd_ref, nw_ref, rc_ref, ra_ref, rb_ref,
                   naq_ref, nak_ref, nav_ref, dq_ref, dk_ref, dv_ref):
    x = x_ref[...]
    ms = jnp.mean(x * x, axis=-1, keepdims=True)
    h = (x * lax.rsqrt(ms + EPS) * n1_ref[...]).astype(BF16)
    bd = bd_ref[...]

    def proj(c):
        return jnp.dot(h, w_ref[:, c * SECTION:(c + 1) * SECTION], preferred_element_type=F32)

    def qknorm(y, row):
        sq = y * y
        hi = sq.astype(BF16)
        lo = (sq - hi.astype(F32)).astype(BF16)
        msq = (jnp.dot(hi, bd, preferred_element_type=F32)
               + jnp.dot(lo, bd, preferred_element_type=F32))
        return y * lax.rsqrt(msq + EPS) * nw_ref[row:row + 1, :]

    def rope(y):
        return (y * rc_ref[...]
                + pltpu.roll(y, SECTION - ROT_DIM // 2, axis=1) * ra_ref[...]
                + pltpu.roll(y, ROT_DIM // 2, axis=1) * rb_ref[...])

    scale = HEAD_DIM ** -0.5 * LOG2E
    naq_ref[...] = (qknorm(proj(0), 0) * scale).astype(BF16)
    nak_ref[...] = qknorm(proj(1), 1).astype(BF16)
    nav_ref[...] = proj(2).astype(BF16)
    dq_ref[...] = (rope(qknorm(proj(3), 2)) * scale).astype(BF16)
    dk_ref[...] = rope(qknorm(proj(4), 3)).astype(BF16)
    dv_ref[...] = proj(5).astype(BF16)


def _rope_tables(seq):
    pos = jnp.arange(seq, dtype=F32)
    inv = ROPE_THETA ** (-jnp.arange(0, ROT_DIM, 2, dtype=F32) / ROT_DIM)
    ang = pos[:, None] * inv[None, :]
    cos, sin = jnp.cos(ang), jnp.sin(ang)
    half = ROT_DIM // 2
    pad = HEAD_DIM - ROT_DIM
    c_head = jnp.concatenate([cos, cos, jnp.ones((seq, pad), F32)], -1)
    a_head = jnp.concatenate([-sin, jnp.zeros((seq, pad + half), F32)], -1)
    b_head = jnp.concatenate([jnp.zeros((seq, half), F32), sin, jnp.zeros((seq, pad), F32)], -1)
    reps = SECTION // HEAD_DIM
    return (jnp.tile(c_head, (1, reps)), jnp.tile(a_head, (1, reps)), jnp.tile(b_head, (1, reps)))


def _in_projection(x2d, seq, norm1_w, w_in_bf, qk_w, rope_tabs):
    n = x2d.shape[0]
    tm = min(TM_IN, seq)
    steps_per_seq = seq // tm
    gid = np.arange(SECTION) // HEAD_DIM
    bd = jnp.asarray((gid[:, None] == gid[None, :]).astype(np.float32) / HEAD_DIM, BF16)
    tok = pl.BlockSpec((tm, SECTION), lambda i: (i, 0))
    pos = pl.BlockSpec((tm, SECTION), lambda i: (i % steps_per_seq, 0))
    out = jax.ShapeDtypeStruct((n, SECTION), BF16)
    return pl.pallas_call(
        _inproj_kernel,
        grid=(n // tm,),
        in_specs=[pl.BlockSpec((tm, D_MODEL), lambda i: (i, 0)),
                  _const_spec((1, D_MODEL)),
                  _const_spec((D_MODEL, 6 * SECTION)),
                  _const_spec((SECTION, SECTION)),
                  _const_spec((4, SECTION)),
                  pos, pos, pos],
        out_specs=[tok] * 6,
        out_shape=[out] * 6,
        compiler_params=_cparams(("parallel",)),
    )(x2d, norm1_w.reshape(1, D_MODEL), w_in_bf, bd, qk_w, *rope_tabs)


def _na_bias(rpb):
    c = np.arange(GRID_W)[:, None]
    kc = np.arange(GRID_W)[None, :]
    cs = np.clip(c - NA_KW // 2, 0, GRID_W - NA_KW)
    valid = (kc >= cs) & (kc < cs + NA_KW)
    rel_c = kc - c + (NA_KW - 1)
    onehot = ((np.arange(2 * NA_KW - 1)[:, None, None] == rel_c[None]) & valid[None]).astype(np.float32)
    col = jnp.einsum("hrx,xck->hrck", rpb.astype(F32), onehot, precision=lax.Precision.HIGHEST)
    col = jnp.where(valid, col * LOG2E, NEG_BIG)
    per_off = [col[:, NA_KH - 1 - o:2 * NA_KH - 1 - o] for o in range(NA_KH)]
    b = jnp.stack(per_off).transpose(0, 1, 3, 2, 4)
    return b.reshape(NA_KH, NA_HEADS, GRID_W, NA_KH * GRID_W)


def _na_kernel(q_ref, k_ref, v_ref, b_ref, o_ref, *, rows):
    j = pl.program_id(1)
    win = NA_KH * GRID_W
    pair = 2 * HEAD_DIM
    lane = lax.broadcasted_iota(jnp.int32, (GRID_W, pair), 1)
    first = lane < HEAD_DIM
    ones = jnp.ones((win, pair), BF16)

    def one_row(rr):
        r = j * NA_ROWS_PER_STEP + rr
        rs = jnp.clip(r - NA_KH // 2, 0, rows - NA_KH)
        off = r - rs
        kstart = pl.multiple_of(rs * GRID_W, GRID_W)
        qstart = pl.multiple_of(rr * GRID_W, GRID_W)
        outs = []
        for hp in range(NA_HEADS // 2):
            sl = slice(hp * pair, (hp + 1) * pair)
            qp = q_ref[pl.ds(qstart, GRID_W), sl]
            kp = k_ref[pl.ds(kstart, win), sl]
            v_ext = jnp.concatenate([v_ref[pl.ds(kstart, win), sl], ones], axis=1)
            halves = []
            for hh in range(2):
                qm = jnp.where(first if hh == 0 else ~first, qp, jnp.zeros_like(qp))
                s = lax.dot_general(qm, kp, (((1,), (1,)), ((), ())), preferred_element_type=F32)
                s = s + b_ref[off, 2 * hp + hh]
                p = jnp.exp2(s - jnp.max(s, axis=-1, keepdims=True)).astype(BF16)
                z = jnp.dot(p, v_ext, preferred_element_type=F32)
                halves.append(z[:, :pair] / z[:, pair:pair + 1])
            outs.append(jnp.where(first, halves[0], halves[1]))
        o_ref[pl.ds(qstart, GRID_W), :] = jnp.concatenate(outs, axis=-1).astype(BF16)

    def row_body(i, carry):
        for u in range(NA_ROW_UNROLL):
            one_row(i * NA_ROW_UNROLL + u)
        return carry

    lax.fori_loop(0, NA_ROWS_PER_STEP // NA_ROW_UNROLL, row_body, 0)


def _neighbourhood_attention(q, k, v, bias, batch, seq):
    rows = seq // GRID_W
    assert rows >= NA_KH and rows % NA_ROWS_PER_STEP == 0
    nblk = rows // NA_ROWS_PER_STEP
    tq = NA_ROWS_PER_STEP * GRID_W
    qspec = pl.BlockSpec((tq, SECTION), lambda b, j: (b * nblk + j, 0))
    kvspec = pl.BlockSpec((seq, SECTION), lambda b, j: (b, 0))
    return pl.pallas_call(
        functools.partial(_na_kernel, rows=rows),
        grid=(batch, nblk),
        in_specs=[qspec, kvspec, kvspec, _const_spec(bias.shape)],
        out_specs=qspec,
        out_shape=jax.ShapeDtypeStruct(q.shape, BF16),
        compiler_params=_cparams(("parallel", "arbitrary")),
    )(q, k, v, bias)


def _diff_kernel(q_ref, k_ref, v_ref, lam_ref, sub_ref, o_ref, m0_scr, m1_scr, acc0_scr, acc1_scr,
                 *, seq, tq, tk):
    m_scrs, acc_scrs = (m0_scr, m1_scr), (acc0_scr, acc1_scr)
    q = q_ref[...]
    lane = lax.broadcasted_iota(jnp.int32, q.shape, 1)
    zero = jnp.zeros_like(q)
    q_maps = (jnp.where(lane < HEAD_DIM, q, zero), jnp.where(lane >= HEAD_DIM, q, zero))

    lp = lam_ref[...]
    lam = (jnp.exp(jnp.sum(lp[0:1] * lp[1:2], axis=-1, keepdims=True))
           - jnp.exp(jnp.sum(lp[2:3] * lp[3:4], axis=-1, keepdims=True)) + LAMBDA_INIT)

    width = 2 * HEAD_DIM
    ones = jnp.ones((tk, width), BF16)

    for m_scr, acc_scr in zip(m_scrs, acc_scrs):
        m_scr[...] = jnp.full(m_scr.shape, -jnp.inf, F32)
        acc_scr[...] = jnp.zeros(acc_scr.shape, F32)

    def kv_body(c, carry):
        tks = tk // KV_SUBCHUNKS
        kcs, v_exts = [], []
        for sub in range(KV_SUBCHUNKS):
            start = pl.multiple_of(c * tk + sub * tks, tks)
            kcs.append(k_ref[pl.ds(start, tks), :])
            v_exts.append(jnp.concatenate([v_ref[pl.ds(start, tks), :], ones[:tks]], axis=1))
        scores = [[lax.dot_general(q_maps[mp], kcs[sub], (((1,), (1,)), ((), ())),
                                   preferred_element_type=F32) for mp in range(2)]
                  for sub in range(KV_SUBCHUNKS)]
        m = [m_scrs[mp][...] for mp in range(2)]
        probs, alphas = [], []
        for sub in range(KV_SUBCHUNKS):
            probs.append([])
            alphas.append([])
            for mp in range(2):
                mn = jnp.maximum(m[mp], jnp.max(scores[sub][mp], axis=-1, keepdims=True))
                probs[sub].append(jnp.exp2(scores[sub][mp] - mn[:, 0:1]).astype(BF16))
                alpha = jnp.exp2(m[mp] - mn)
                alphas[sub].append(jnp.concatenate([alpha, alpha], axis=1))
                m[mp] = mn
        for mp in range(2):
            m_scrs[mp][...] = m[mp]
            acc = acc_scrs[mp][...]
            for sub in range(KV_SUBCHUNKS):
                acc = alphas[sub][mp] * acc + jnp.dot(probs[sub][mp], v_exts[sub],
                                                      preferred_element_type=F32)
            acc_scrs[mp][...] = acc
        return carry

    lax.fori_loop(0, seq // tk, kv_body, 0)
    a0, a1 = acc0_scr[...], acc1_scr[...]
    o = a0[:, :width] / a0[:, width:width + 1] - lam * (a1[:, :width] / a1[:, width:width + 1])
    ms = jnp.mean(o * o, axis=-1, keepdims=True)
    y = o * lax.rsqrt(ms + EPS) * sub_ref[...]
    o_ref[...] = (y * (1.0 - LAMBDA_INIT)).astype(BF16)


def _diff_attention(q, k, v, lam_params, subln_w, batch, seq):
    tq, tk = min(TQ, seq), min(TK, seq)
    nq = seq // tq
    width = 2 * HEAD_DIM
    qspec = pl.BlockSpec((tq, width), lambda b, h, i: (b * nq + i, h))
    kvspec = pl.BlockSpec((seq, width), lambda b, h, i: (b, h))
    return pl.pallas_call(
        functools.partial(_diff_kernel, seq=seq, tq=tq, tk=tk),
        grid=(batch, DIFF_HEADS, nq),
        in_specs=[qspec, kvspec, kvspec, _const_spec((4, HEAD_DIM)), _const_spec((1, width))],
        out_specs=qspec,
        out_shape=jax.ShapeDtypeStruct(q.shape, BF16),
        scratch_shapes=[pltpu.VMEM((tq, width), F32), pltpu.VMEM((tq, width), F32),
                        pltpu.VMEM((tq, 2 * width), F32), pltpu.VMEM((tq, 2 * width), F32)],
        compiler_params=_cparams(("parallel", "parallel", "arbitrary")),
    )(q, k, v, lam_params, subln_w.reshape(1, width))


def _top16(sc, payload=None):
    iota = lax.broadcasted_iota(jnp.int32, sc.shape, 0).astype(F32)
    big = float(sc.shape[0])
    vals, picks = [], []
    for _ in range(PEER_TOPK):
        m = jnp.max(sc, axis=0, keepdims=True)
        ix = jnp.min(jnp.where(sc == m, iota, big), axis=0, keepdims=True)
        sel = iota == ix
        if payload is None:
            picks.append(ix)
        else:
            picks.append(jnp.sum(jnp.where(sel, payload, 0), axis=0, keepdims=True))
        sc = jnp.where(sel, -jnp.inf, sc)
        vals.append(m)
    picks = jnp.concatenate(picks, axis=0)
    return jnp.concatenate(vals, axis=0), picks.astype(jnp.int32)


def _pair_candidates(r0, r1):
    half = SUBLANES // 2
    groups = [r0[0:1] + r1[0:8], r0[0:1] + r1[8:16]]
    groups += [r0[a:a + 1] + r1[0:8] for a in (1, 2, 3)]
    groups += [jnp.concatenate([r0[a:a + 1] + r1[0:half], r0[a + 1:a + 2] + r1[0:half]], axis=0)
               for a in (4, 6)]
    groups.append(r0[8:16] + r1[0:1])
    return jnp.concatenate(groups, axis=0)


def _route_kernel(na_ref, df_ref, x_ref, wo_ref, n2_ref, wq_ref, sk_ref,
                  x1_ref, xn_ref, eid_ref, g_ref, q_scr, eid_scr, g_scr):
    tm = x_ref.shape[0]
    x1 = (x_ref[...]
          + jnp.dot(na_ref[...], wo_ref[0:SECTION, :], preferred_element_type=F32)
          + jnp.dot(df_ref[...], wo_ref[SECTION:2 * SECTION, :], preferred_element_type=F32))
    ms = jnp.mean(x1 * x1, axis=-1, keepdims=True)
    xn = x1 * lax.rsqrt(ms + EPS) * n2_ref[...]
    for r in range(SUBLANES):
        x1_ref[:, r, :] = x1[:, r * LANES:(r + 1) * LANES]
        xn_ref[:, r, :] = xn[:, r * LANES:(r + 1) * LANES]
    xb = xn.astype(BF16)
    for c in range(2 * PEER_HEADS):
        q_scr[c] = jnp.dot(xb, wq_ref[:, c * LANES:(c + 1) * LANES],
                           preferred_element_type=F32).astype(BF16)

    nblk = tm // TB

    def route_block(h, blk):
        tok0 = blk * TB
        tops = []
        for p in range(2):
            sc = lax.dot_general(sk_ref[2 * h + p], q_scr[2 * h + p, pl.ds(tok0, TB), :],
                                 (((1,), (1,)), ((), ())), preferred_element_type=F32)
            tops.append(_top16(sc))
        (s0, i0), (s1, i1) = tops
        cand_s = _pair_candidates(s0, s1)
        cand_i = _pair_candidates(i0 * N_KEYS, i1)
        top_s, eid = _top16(cand_s, cand_i)
        e = jnp.exp(top_s - top_s[0:1])
        row = pl.multiple_of(h * PEER_TOPK, PEER_TOPK)
        eid_scr[blk, pl.ds(row, PEER_TOPK), :] = eid * ROW_SUB
        g_scr[blk, pl.ds(row, PEER_TOPK), :] = e / jnp.sum(e, axis=0, keepdims=True)

    def head_body(h, carry):
        for blk in range(nblk):
            route_block(h, blk)
        return carry

    lax.fori_loop(0, PEER_HEADS, head_body, 0)
    for blk in range(nblk):
        eid_ref[blk] = eid_scr[blk].T
        g_ref[blk * TB:(blk + 1) * TB, :] = g_scr[blk].T


def _out_and_route(na, df, x2d, w_out_bf, norm2_w, w_q_bf, sk_bf):
    n = x2d.shape[0]
    tm = TM_OUT
    tok = lambda w: pl.BlockSpec((tm, w), lambda i: (i, 0))
    tiles = pl.BlockSpec((tm, SUBLANES, LANES), lambda i: (i, 0, 0))
    return pl.pallas_call(
        _route_kernel,
        grid=(n // tm,),
        in_specs=[tok(SECTION), tok(SECTION), tok(D_MODEL),
                  _const_spec((D_MODEL, D_MODEL)), _const_spec((1, D_MODEL)),
                  _const_spec(w_q_bf.shape), _const_spec(sk_bf.shape)],
        out_specs=[tiles, tiles,
                   pl.BlockSpec((tm // TB, TB, HK), lambda i: (i, 0, 0)), tok(HK)],
        out_shape=[jax.ShapeDtypeStruct((n, SUBLANES, LANES), F32),
                   jax.ShapeDtypeStruct((n, SUBLANES, LANES), F32),
                   jax.ShapeDtypeStruct((n // TB, TB, HK), jnp.int32),
                   jax.ShapeDtypeStruct((n, HK), F32)],
        scratch_shapes=[pltpu.VMEM((2 * PEER_HEADS, tm, LANES), BF16),
                        pltpu.VMEM((tm // TB, HK, TB), jnp.int32), pltpu.VMEM((tm // TB, HK, TB), F32)],
        compiler_params=_cparams(("parallel",)),
    )(na, df, x2d, w_out_bf, norm2_w.reshape(1, D_MODEL), w_q_bf, sk_bf)


ROW_SUB = SUBLANES // 2
CHUNK_SLOTS = 32
N_CHUNKS = HK // CHUNK_SLOTS
CHUNK_ROWS = CHUNK_SLOTS * SUBLANES
TOKEN_UNROLL = 8
ROW_WORDS = ROW_SUB * LANES
SC_CORES, SC_SUBCORES, SC_LANES = 2, 16, 16
SC_WORKERS = SC_CORES * SC_SUBCORES


def _pack_table(tab):
    bits = lax.bitcast_convert_type(tab.astype(BF16), jnp.uint16).astype(jnp.uint32)
    bits = bits.reshape(tab.shape[0], ROW_SUB, 2, LANES)
    return (bits[:, :, 0, :] | (bits[:, :, 1, :] << 16)).reshape(tab.shape[0] * ROW_SUB, LANES)


def _split_bf16(x):
    hi = x.astype(BF16)
    return hi, (x - hi.astype(F32)).astype(BF16)


def _gather_chunk(tab_ref, eid_ref, t, chunk):
    ids = eid_ref.at[0, t]
    rows = [tab_ref[pl.ds(pl.multiple_of(ids[chunk * CHUNK_SLOTS + i], ROW_SUB), ROW_SUB), :]
            for i in range(CHUNK_SLOTS)]
    return pltpu.bitcast(jnp.concatenate(rows, axis=0), BF16)


def _diag_mask(width):
    r = lax.broadcasted_iota(jnp.int32, (SUBLANES, width), 0)
    n = lax.broadcasted_iota(jnp.int32, (SUBLANES, width), 1)
    return (n % SUBLANES) == r


def _peer_u_kernel(eid_ref, xn_ref, g_ref, tab_ref, rept_ref, rep_ref, rep16_ref, wrep_ref, w16_ref, d_scr):
    mask = _diag_mask(CHUNK_ROWS)
    zeros = jnp.zeros((SUBLANES, LANES), BF16)

    def one_token(t):
        xhi, xlo = _split_bf16(xn_ref[t])
        lhs = jnp.concatenate([jnp.concatenate([xhi, zeros], axis=1),
                               jnp.concatenate([xlo, zeros], axis=1),
                               jnp.concatenate([zeros, xhi], axis=1),
                               jnp.concatenate([zeros, xlo], axis=1)], axis=0)
        for pr in range(N_CHUNKS // 2):
            s = jnp.concatenate([_gather_chunk(tab_ref, eid_ref, t, 2 * pr),
                                 _gather_chunk(tab_ref, eid_ref, t, 2 * pr + 1)], axis=1)
            z = lax.dot_general(lhs, s, (((1,), (1,)), ((), ())), preferred_element_type=F32)
            for half in range(2):
                z8 = z[16 * half:16 * half + 8] + z[16 * half + 8:16 * half + 16]
                zs = jnp.sum(jnp.where(mask, z8, 0.0), axis=0, keepdims=True)
                d_scr[pl.ds(t, 1), pl.ds((2 * pr + half) * CHUNK_ROWS, CHUNK_ROWS)] = zs

    def tok_body(i, carry):
        for u in range(TOKEN_UNROLL):
            one_token(i * TOKEN_UNROLL + u)
        return carry

    lax.fori_loop(0, PB // TOKEN_UNROLL, tok_body, 0)
    _gate_outputs(d_scr[...], rept_ref, g_ref, rep_ref, rep16_ref, wrep_ref, w16_ref)


def _gate_outputs(partial, sum_ref, g_ref, rep_ref, rep16_ref, wrep_ref, w16_ref):
    dhi, dlo = _split_bf16(partial)
    a = (jnp.dot(dhi, sum_ref[...], preferred_element_type=F32)
         + jnp.dot(dlo, sum_ref[...], preferred_element_type=F32))
    w = g_ref[...] * (0.5 * a * (1.0 + lax.erf(a * (2.0 ** -0.5))))
    whi, wlo = _split_bf16(w)
    wrep_ref[...] = (jnp.dot(whi, rep_ref[...], preferred_element_type=F32)
                     + jnp.dot(wlo, rep_ref[...], preferred_element_type=F32))
    w16_ref[...] = (jnp.dot(whi, rep16_ref[...], preferred_element_type=F32)
                    + jnp.dot(wlo, rep16_ref[...], preferred_element_type=F32))


def _gate_outputs_kernel(part_ref, g_ref, sum_ref, rep_ref, rep16_ref, wrep_ref, w16_ref):
    _gate_outputs(part_ref[...], sum_ref, g_ref, rep_ref, rep16_ref, wrep_ref, w16_ref)


def _gate_kernel(part_ref, g_ref, sum_ref, rep_ref, rep16_ref, wrep_in, w16_in, wrep_ref, w16_ref):
    del wrep_in, w16_in
    _gate_outputs(part_ref[...], sum_ref, g_ref, rep_ref, rep16_ref, wrep_ref, w16_ref)


def _peer_v_kernel(eid_ref, wrep_ref, x1_ref, tab_ref, o_ref):
    mask = _diag_mask(CHUNK_ROWS)

    def one_token(t):
        wr = wrep_ref[pl.ds(t, 1), :]
        acc = x1_ref[t]
        for pr in range(N_CHUNKS // 2):
            parts = []
            for half in range(2):
                seg = wr[:, (2 * pr + half) * CHUNK_ROWS:(2 * pr + half + 1) * CHUNK_ROWS]
                parts.extend(_split_bf16(jnp.where(mask, jnp.broadcast_to(seg, mask.shape), 0.0)))
            lhs = jnp.concatenate(parts, axis=0)
            s = jnp.concatenate([_gather_chunk(tab_ref, eid_ref, t, 2 * pr),
                                 _gather_chunk(tab_ref, eid_ref, t, 2 * pr + 1)], axis=1)
            z = jnp.dot(lhs, s, preferred_element_type=F32)
            acc = acc + ((z[0:8, :LANES] + z[8:16, :LANES]) + (z[16:24, LANES:] + z[24:32, LANES:]))
        return acc

    def tok_body(i, carry):
        accs = [one_token(i * TOKEN_UNROLL + u) for u in range(TOKEN_UNROLL)]
        base = pl.multiple_of(i * TOKEN_UNROLL, TOKEN_UNROLL)
        for r in range(SUBLANES):
            o_ref[pl.ds(base, TOKEN_UNROLL), r * LANES:(r + 1) * LANES] = jnp.concatenate(
                [acc[r:r + 1, :] for acc in accs], axis=0)
        return carry

    lax.fori_loop(0, PB // TOKEN_UNROLL, tok_body, 0)


def _sc_scratch(mid_shape, last_shape):
    half = HK // 2
    per_parity = [pltpu.VMEM((2, half), jnp.int32), pltpu.VMEM(mid_shape, F32), pltpu.VMEM(last_shape, F32)]
    return per_parity + per_parity + [
        pltpu.VMEM((half, ROW_WORDS), jnp.uint32), pltpu.VMEM((half, ROW_WORDS), jnp.uint32),
        pltpu.SemaphoreType.DMA, pltpu.SemaphoreType.DMA,
        pltpu.SemaphoreType.DMA, pltpu.SemaphoreType.DMA,
        pltpu.SemaphoreType.DMA, pltpu.SemaphoreType.DMA,
    ]


def _sc_u_phase(n_tok, tok0):
    per_w = n_tok // SC_WORKERS
    assert per_w * SC_WORKERS == n_tok and per_w % 2 == 0
    half = HK // 2
    mesh = plsc.VectorSubcoreMesh(core_axis_name="c", subcore_axis_name="s")

    @functools.partial(
        pl.kernel, mesh=mesh,
        out_type=jax.ShapeDtypeStruct((n_tok, HK * SC_LANES), F32),
        scratch_types=_sc_scratch((SUBLANES, LANES), (HK * SC_LANES,)),
        compiler_params=pltpu.CompilerParams(needs_layout_passes=False),
    )
    def sc_kernel(tab_hbm, eid_hbm, xn_hbm, out_hbm,
                  idx_a, x_a, a_a, idx_b, x_b, a_b, g0, g1, s_a, s_b, sg0, sg1, so_a, so_b):
        wid = lax.axis_index("s") * SC_CORES + lax.axis_index("c")
        first = wid * per_w
        par = ((idx_a, x_a, a_a, s_a, so_a), (idx_b, x_b, a_b, s_b, so_b))
        gbuf, gsem = (g0, g1), (sg0, sg1)

        def input_copies(t, p):
            idx_v, x_v, _, sem, _ = par[p]
            return (pltpu.make_async_copy(eid_hbm.at[tok0 + t], idx_v, sem),
                    pltpu.make_async_copy(xn_hbm.at[tok0 + t], x_v, sem))

        def gather(p, h):
            return pltpu.make_async_copy(tab_hbm.at[par[p][0].at[h]], gbuf[h], gsem[h])

        def out_copy(t, p):
            return pltpu.make_async_copy(par[p][2], out_hbm.at[t], par[p][4])

        def dot_half(p, h):
            _, x_v, a_v, _, _ = par[p]
            g = gbuf[h]
            groups = LANES // SC_LANES
            block = SUBLANES
            for rb in range(half // block):
                def col_body(k, accs):
                    c, kk = k // groups, k % groups
                    lane0 = pl.multiple_of(kk * SC_LANES, SC_LANES)
                    xl = x_v[2 * c, pl.ds(lane0, SC_LANES)]
                    xh = x_v[2 * c + 1, pl.ds(lane0, SC_LANES)]
                    out = []
                    for r in range(block):
                        words = g[rb * block + r, pl.ds(pl.multiple_of(k * SC_LANES, SC_LANES), SC_LANES)]
                        lov = plsc.bitcast(lax.shift_left(words, jnp.uint32(16)), F32)
                        hiv = plsc.bitcast(words & jnp.uint32(0xFFFF0000), F32)
                        out.append(accs[r] + lov * xl + hiv * xh)
                    return tuple(out)

                zero = jnp.zeros((SC_LANES,), F32)
                accs = lax.fori_loop(0, ROW_WORDS // SC_LANES, col_body, (zero,) * block)
                for r in range(block):
                    a_v[pl.ds((h * half + rb * block + r) * SC_LANES, SC_LANES)] = accs[r]

        def token(t, p):
            q = 1 - p
            has_next = t + 1 < first + per_w
            gather(p, 0).wait()
            gather(p, 1).start()

            @pl.when(has_next)
            def _():
                for cp in input_copies(t + 1, q):
                    cp.start()

            @pl.when(t >= first + 2)
            def _():
                out_copy(t - 2, p).wait()

            dot_half(p, 0)
            gather(p, 1).wait()

            @pl.when(has_next)
            def _():
                for cp in input_copies(t + 1, q):
                    cp.wait()
                gather(q, 0).start()

            dot_half(p, 1)
            out_copy(t, p).start()

        for cp in input_copies(first, 0):
            cp.start()
        for cp in input_copies(first, 0):
            cp.wait()
        gather(0, 0).start()

        def pair_body(i, carry):
            token(first + 2 * i, 0)
            token(first + 2 * i + 1, 1)
            return carry

        lax.fori_loop(0, per_w // 2, pair_body, 0)
        out_copy(first + per_w - 2, 0).wait()
        out_copy(first + per_w - 1, 1).wait()

    return sc_kernel


def _sc_v_phase(n_tok, tok0):
    per_w = n_tok // SC_WORKERS
    assert per_w * SC_WORKERS == n_tok and per_w % 2 == 0
    half = HK // 2
    mesh = plsc.VectorSubcoreMesh(core_axis_name="c", subcore_axis_name="s")

    @functools.partial(
        pl.kernel, mesh=mesh,
        out_type=jax.ShapeDtypeStruct((n_tok, SUBLANES, LANES), F32),
        scratch_types=_sc_scratch((HK * SC_LANES,), (SUBLANES, LANES)),
        compiler_params=pltpu.CompilerParams(needs_layout_passes=False),
    )
    def sc_kernel(tab_hbm, eid_hbm, w16_hbm, x1_hbm, out_hbm,
                  idx_a, w_a, x_a, idx_b, w_b, x_b, g0, g1, s_a, s_b, sg0, sg1, so_a, so_b):
        wid = lax.axis_index("s") * SC_CORES + lax.axis_index("c")
        first = wid * per_w
        par = ((idx_a, w_a, x_a, s_a, so_a), (idx_b, w_b, x_b, s_b, so_b))
        gbuf, gsem = (g0, g1), (sg0, sg1)

        def input_copies(t, p):
            idx_v, w_v, x_v, sem, _ = par[p]
            return (pltpu.make_async_copy(eid_hbm.at[tok0 + t], idx_v, sem),
                    pltpu.make_async_copy(w16_hbm.at[tok0 + t], w_v, sem),
                    pltpu.make_async_copy(x1_hbm.at[tok0 + t], x_v, sem))

        def gather(p, h):
            return pltpu.make_async_copy(tab_hbm.at[par[p][0].at[h]], gbuf[h], gsem[h])

        def out_copy(t, p):
            return pltpu.make_async_copy(par[p][2], out_hbm.at[t], par[p][4])

        def accumulate_half(p, h):
            _, w_v, x_v, _, _ = par[p]
            g = gbuf[h]
            groups = LANES // SC_LANES
            for c in range(ROW_SUB):
                lo0 = tuple(x_v[2 * c, pl.ds(SC_LANES * kk, SC_LANES)] for kk in range(groups))
                hi0 = tuple(x_v[2 * c + 1, pl.ds(SC_LANES * kk, SC_LANES)] for kk in range(groups))

                def row_body(j, acc):
                    lo, hi = acc
                    wj = w_v[pl.ds(pl.multiple_of((h * half + j) * SC_LANES, SC_LANES), SC_LANES)]
                    nlo, nhi = [], []
                    for kk in range(groups):
                        words = g[j, pl.ds(LANES * c + SC_LANES * kk, SC_LANES)]
                        lov = plsc.bitcast(lax.shift_left(words, jnp.uint32(16)), F32)
                        hiv = plsc.bitcast(words & jnp.uint32(0xFFFF0000), F32)
                        nlo.append(lo[kk] + wj * lov)
                        nhi.append(hi[kk] + wj * hiv)
                    return tuple(nlo), tuple(nhi)

                lo, hi = lax.fori_loop(0, half, row_body, (lo0, hi0))
                for kk in range(groups):
                    x_v[2 * c, pl.ds(SC_LANES * kk, SC_LANES)] = lo[kk]
                    x_v[2 * c + 1, pl.ds(SC_LANES * kk, SC_LANES)] = hi[kk]

        def token(t, p):
            q = 1 - p
            has_next = t + 1 < first + per_w
            gather(p, 0).wait()
            gather(p, 1).start()

            @pl.when(has_next)
            def _():
                @pl.when(t >= first + 1)
                def _():
                    out_copy(t - 1, q).wait()
                for cp in input_copies(t + 1, q):
                    cp.start()

            accumulate_half(p, 0)
            gather(p, 1).wait()

            @pl.when(has_next)
            def _():
                for cp in input_copies(t + 1, q):
                    cp.wait()
                gather(q, 0).start()

            accumulate_half(p, 1)
            out_copy(t, p).start()

        for cp in input_copies(first, 0):
            cp.start()
        for cp in input_copies(first, 0):
            cp.wait()
        gather(0, 0).start()

        def pair_body(i, carry):
            token(first + 2 * i, 0)
            token(first + 2 * i + 1, 1)
            return carry

        lax.fori_loop(0, per_w // 2, pair_body, 0)
        out_copy(first + per_w - 2, 0).wait()
        out_copy(first + per_w - 1, 1).wait()

    return sc_kernel


def _peer_experts(eid, g, xn3, x13, u_pack, v_pack, n_sc_u, n_sc_v, sc_after=None):
    n = xn3.shape[0]
    n_tc_u, n_tc_v = n - n_sc_u, n - n_sc_v
    assert n_tc_u % PB == 0 and n_tc_v % PB == 0
    eid = eid.reshape(n // PB, PB, HK)
    smem_slot = pl.BlockSpec((1, PB, HK), lambda i: (i, 0, 0), memory_space=pltpu.SMEM)
    tok = pl.BlockSpec((PB, SUBLANES, LANES), lambda i: (i, 0, 0))
    flat = lambda w, off=0: pl.BlockSpec((PB, w), lambda i: (i + off, 0))
    table = pl.BlockSpec(u_pack.shape, lambda i: (0, 0), pipeline_mode=pl.Buffered(1))

    def repeat_matrix(times):
        m = (np.arange(HK)[:, None] == (np.arange(HK * times)[None, :] // times)).astype(np.float32)
        return jnp.asarray(m, BF16)

    rep, rep16 = repeat_matrix(SUBLANES), repeat_matrix(SC_LANES)
    gate_shapes = [jax.ShapeDtypeStruct((n, D_MODEL), F32), jax.ShapeDtypeStruct((n, HK * SC_LANES), F32)]
    expert = eid.reshape(n, 2, HK // 2) // ROW_SUB
    if n_sc_u:
        u_rows = u_pack.reshape(N_EXPERTS, ROW_WORDS)
        partial = _sc_u_phase(n_sc_u, n_tc_u)(u_rows, expert, xn3)
    if n_tc_u:
        wrep, w16 = pl.pallas_call(
            _peer_u_kernel,
            grid=(n_tc_u // PB,),
            in_specs=[smem_slot, tok, flat(HK), table, _const_spec(rep.T.shape), _const_spec(rep.shape),
                      _const_spec(rep16.shape)],
            out_specs=[flat(D_MODEL), flat(HK * SC_LANES)],
            out_shape=gate_shapes,
            scratch_shapes=[pltpu.VMEM((PB, D_MODEL), F32)],
            compiler_params=_cparams(("arbitrary",)),
        )(eid, xn3, g, u_pack, rep.T, rep, rep16)
    if n_sc_u:
        off = n_tc_u // PB
        gate_in = [flat(HK * SC_LANES), flat(HK, off), _const_spec(rep16.T.shape), _const_spec(rep.shape),
                   _const_spec(rep16.shape)]
        gate_args = (partial, g, rep16.T, rep, rep16)
        if n_tc_u:
            anyspec = pl.BlockSpec(memory_space=pl.ANY)
            gate_fn, gate_in, gate_args = _gate_kernel, gate_in + [anyspec, anyspec], gate_args + (wrep, w16)
            aliases = {5: 0, 6: 1}
        else:
            gate_fn, aliases = _gate_outputs_kernel, {}
        wrep, w16 = pl.pallas_call(
            gate_fn,
            grid=(n_sc_u // PB,),
            in_specs=gate_in,
            out_specs=[flat(D_MODEL, off), flat(HK * SC_LANES, off)],
            out_shape=gate_shapes,
            input_output_aliases=aliases,
            compiler_params=_cparams(("arbitrary",)),
        )(*gate_args)
    parts = []
    if n_tc_v:
        parts.append(pl.pallas_call(
            _peer_v_kernel,
            grid=(n_tc_v // PB,),
            in_specs=[smem_slot, flat(D_MODEL), tok, table],
            out_specs=flat(D_MODEL),
            out_shape=jax.ShapeDtypeStruct((n_tc_v, D_MODEL), F32),
            compiler_params=_cparams(("arbitrary",)),
        )(eid, wrep, x13, v_pack))
    if n_sc_v:
        v_rows = v_pack.reshape(N_EXPERTS, ROW_WORDS)
        w16_sc = w16
        if sc_after is not None:
            w16_sc, _ = lax.optimization_barrier((w16, sc_after))
        parts.append(_sc_v_phase(n_sc_v, n_tc_v)(v_rows, expert, w16_sc, x13).reshape(n_sc_v, D_MODEL))
    y = parts[0] if len(parts) == 1 else jnp.concatenate(parts, axis=0)
    return y, w16


def _encoder_layer(x, p, after):
    if after is not None:
        x, _ = lax.optimization_barrier((x, after["gate"]))
    batch, seq, _ = x.shape
    n = batch * seq
    x2d = x.reshape(n, D_MODEL)
    naq, nak, nav, dq, dk, dv = _in_projection(x2d, seq, p["norm1_w"], p["w_in"], p["qk_w"], p["rope"])
    na = _neighbourhood_attention(naq, nak, nav, p["na_bias"], batch, seq)
    df = _diff_attention(dq, dk, dv, p["lam"], p["subln_w"], batch, seq)
    x1, xn, eid, g = _out_and_route(na, df, x2d, p["w_out"], p["norm2_w"], p["w_q"], p["sub_keys"])
    y, w16 = _peer_experts(eid, g, xn, x1, p["u_pack"], p["v_pack"], *SC_TOKENS.get(n, (0, 0)),
                           sc_after=None if after is None else after["out"])
    return y.reshape(batch, seq, D_MODEL), {"gate": w16, "out": y}


def kernel(x_prompt, x_sample, norm1_w, w_in, na_q_norm, na_k_norm, na_rpb, diff_q_norm, diff_k_norm,
           diff_lambda_q1, diff_lambda_k1, diff_lambda_q2, diff_lambda_k2, diff_subln_w, w_out, norm2_w,
           peer_w_q, peer_sub_keys, peer_u, peer_v):
    assert norm1_w.shape[0] == 1, "single-layer problem"
    reps = SECTION // HEAD_DIM
    params = {
        "norm1_w": norm1_w[0],
        "w_in": w_in[0].astype(BF16),
        "qk_w": jnp.stack([jnp.tile(w[0], reps) for w in (na_q_norm, na_k_norm, diff_q_norm, diff_k_norm)]),
        "lam": jnp.stack([diff_lambda_q1[0], diff_lambda_k1[0], diff_lambda_q2[0], diff_lambda_k2[0]]),
        "subln_w": diff_subln_w[0],
        "w_out": w_out[0].astype(BF16),
        "norm2_w": norm2_w[0],
        "w_q": peer_w_q[0].astype(BF16),
        "sub_keys": peer_sub_keys[0].reshape(2 * PEER_HEADS, N_KEYS, N_KEYS).astype(BF16),
        "na_bias": _na_bias(na_rpb[0]),
        "u_pack": _pack_table(peer_u[0]),
        "v_pack": _pack_table(peer_v[0]),
    }
    rope_by_seq = {}
    outs = [None, None]
    marker = None
    for slot, x in sorted(enumerate((x_prompt, x_sample)), key=lambda item: -item[1].shape[0] * item[1].shape[1]):
        seq = x.shape[1]
        if seq not in rope_by_seq:
            rope_by_seq[seq] = _rope_tables(seq)
        outs[slot], marker = _encoder_layer(x, dict(params, rope=rope_by_seq[seq]), marker)
    return tuple(outs)
```

```python
import functools
import math

import jax
import jax.numpy as jnp
import numpy as np
from jax import lax
from jax.experimental import pallas as pl
from jax.experimental.pallas import tpu as pltpu
from jax.experimental.pallas import tpu_sc as plsc

F32 = jnp.float32
BF16 = jnp.bfloat16

D_MODEL = 1024
GRID_W = 64
HEAD_DIM = 64
NA_HEADS = 8
NA_KH = 8
NA_KW = 16
DIFF_HEADS = 4
SECTION = 512
ROT_DIM = HEAD_DIM // 4
ROPE_THETA = 500000.0
PEER_HEADS = 8
N_KEYS = 128
PEER_TOPK = 16
N_EXPERTS = N_KEYS * N_KEYS
HK = PEER_HEADS * PEER_TOPK
EPS = 1e-6
NEG_BIG = -1e30
LOG2E = math.log2(math.e)
LAMBDA_INIT = 0.8 - 0.6 * math.exp(-0.3 * 0)

LANES = 128
SUBLANES = 8
VMEM_LIMIT = 56 * 1024 * 1024

TM_IN = 512
NA_ROWS_PER_STEP = 8
NA_ROW_UNROLL = 2
TQ = 512
KV_SUBCHUNKS = 1
TK = 4096
TM_OUT = 256
TB = 128
PB = 256
SC_TOKENS = {8 * 4096: (14336, 8 * 4096)}


def _cparams(sem):
    return pltpu.CompilerParams(dimension_semantics=sem, vmem_limit_bytes=VMEM_LIMIT)


def _const_spec(shape):
    nd = len(shape)
    return pl.BlockSpec(shape, lambda *_: (0,) * nd)


def _inproj_kernel(x_ref, n1_ref, w_ref, bd_ref, nw_ref, rc_ref, ra_ref, rb_ref,
                   naq_ref, nak_ref, nav_ref, dq_ref, dk_ref, dv_ref):
    x = x_ref[...]
    ms = jnp.mean(x * x, axis=-1, keepdims=True)
    h = (x * lax.rsqrt(ms + EPS) * n1_ref[...]).astype(BF16)
    bd = bd_ref[...]

    def proj(c):
        return jnp.dot(h, w_ref[:, c * SECTION:(c + 1) * SECTION], preferred_element_type=F32)

    def qknorm(y, row):
        sq = y * y
        hi = sq.astype(BF16)
        lo = (sq - hi.astype(F32)).astype(BF16)
        msq = (jnp.dot(hi, bd, preferred_element_type=F32)
               + jnp.dot(lo, bd, preferred_element_type=F32))
        return y * lax.rsqrt(msq + EPS) * nw_ref[row:row + 1, :]

    def rope(y):
        return (y * rc_ref[...]
                + pltpu.roll(y, SECTION - ROT_DIM // 2, axis=1) * ra_ref[...]
                + pltpu.roll(y, ROT_DIM // 2, axis=1) * rb_ref[...])

    scale = HEAD_DIM ** -0.5 * LOG2E
    naq_ref[...] = (qknorm(proj(0), 0) * scale).astype(BF16)
    nak_ref[...] = qknorm(proj(1), 1).astype(BF16)
    nav_ref[...] = proj(2).astype(BF16)
    dq_ref[...] = (rope(qknorm(proj(3), 2)) * scale).astype(BF16)
    dk_ref[...] = rope(qknorm(proj(4), 3)).astype(BF16)
    dv_ref[...] = proj(5).astype(BF16)


def _rope_tables(seq):
    pos = jnp.arange(seq, dtype=F32)
    inv = ROPE_THETA ** (-jnp.arange(0, ROT_DIM, 2, dtype=F32) / ROT_DIM)
    ang = pos[:, None] * inv[None, :]
    cos, sin = jnp.cos(ang), jnp.sin(ang)
    half = ROT_DIM // 2
    pad = HEAD_DIM - ROT_DIM
    c_head = jnp.concatenate([cos, cos, jnp.ones((seq, pad), F32)], -1)
    a_head = jnp.concatenate([-sin, jnp.zeros((seq, pad + half), F32)], -1)
    b_head = jnp.concatenate([jnp.zeros((seq, half), F32), sin, jnp.zeros((seq, pad), F32)], -1)
    reps = SECTION // HEAD_DIM
    return (jnp.tile(c_head, (1, reps)), jnp.tile(a_head, (1, reps)), jnp.tile(b_head, (1, reps)))


def _in_projection(x2d, seq, norm1_w, w_in_bf, qk_w, rope_tabs):
    n = x2d.shape[0]
    tm = min(TM_IN, seq)
    steps_per_seq = seq // tm
    gid = np.arange(SECTION) // HEAD_DIM
    bd = jnp.asarray((gid[:, None] == gid[None, :]).astype(np.float32) / HEAD_DIM, BF16)
    tok = pl.BlockSpec((tm, SECTION), lambda i: (i, 0))
    pos = pl.BlockSpec((tm, SECTION), lambda i: (i % steps_per_seq, 0))
    out = jax.ShapeDtypeStruct((n, SECTION), BF16)
    return pl.pallas_call(
        _inproj_kernel,
        grid=(n // tm,),
        in_specs=[pl.BlockSpec((tm, D_MODEL), lambda i: (i, 0)),
                  _const_spec((1, D_MODEL)),
                  _const_spec((D_MODEL, 6 * SECTION)),
                  _const_spec((SECTION, SECTION)),
                  _const_spec((4, SECTION)),
                  pos, pos, pos],
        out_specs=[tok] * 6,
        out_shape=[out] * 6,
        compiler_params=_cparams(("parallel",)),
    )(x2d, norm1_w.reshape(1, D_MODEL), w_in_bf, bd, qk_w, *rope_tabs)


def _na_bias(rpb):
    c = np.arange(GRID_W)[:, None]
    kc = np.arange(GRID_W)[None, :]
    cs = np.clip(c - NA_KW // 2, 0, GRID_W - NA_KW)
    valid = (kc >= cs) & (kc < cs + NA_KW)
    rel_c = kc - c + (NA_KW - 1)
    onehot = ((np.arange(2 * NA_KW - 1)[:, None, None] == rel_c[None]) & valid[None]).astype(np.float32)
    col = jnp.einsum("hrx,xck->hrck", rpb.astype(F32), onehot, precision=lax.Precision.HIGHEST)
    col = jnp.where(valid, col * LOG2E, NEG_BIG)
    per_off = [col[:, NA_KH - 1 - o:2 * NA_KH - 1 - o] for o in range(NA_KH)]
    b = jnp.stack(per_off).transpose(0, 1, 3, 2, 4)
    return b.reshape(NA_KH, NA_HEADS, GRID_W, NA_KH * GRID_W)


def _na_kernel(q_ref, k_ref, v_ref, b_ref, o_ref, *, rows):
    j = pl.program_id(1)
    win = NA_KH * GRID_W
    pair = 2 * HEAD_DIM
    lane = lax.broadcasted_iota(jnp.int32, (GRID_W, pair), 1)
    first = lane < HEAD_DIM
    ones = jnp.ones((win, pair), BF16)

    def one_row(rr):
        r = j * NA_ROWS_PER_STEP + rr
        rs = jnp.clip(r - NA_KH // 2, 0, rows - NA_KH)
        off = r - rs
        kstart = pl.multiple_of(rs * GRID_W, GRID_W)
        qstart = pl.multiple_of(rr * GRID_W, GRID_W)
        outs = []
        for hp in range(NA_HEADS // 2):
            sl = slice(hp * pair, (hp + 1) * pair)
            qp = q_ref[pl.ds(qstart, GRID_W), sl]
            kp = k_ref[pl.ds(kstart, win), sl]
            v_ext = jnp.concatenate([v_ref[pl.ds(kstart, win), sl], ones], axis=1)
            halves = []
            for hh in range(2):
                qm = jnp.where(first if hh == 0 else ~first, qp, jnp.zeros_like(qp))
                s = lax.dot_general(qm, kp, (((1,), (1,)), ((), ())), preferred_element_type=F32)
                s = s + b_ref[off, 2 * hp + hh]
                p = jnp.exp2(s - jnp.max(s, axis=-1, keepdims=True)).astype(BF16)
                z = jnp.dot(p, v_ext, preferred_element_type=F32)
                halves.append(z[:, :pair] / z[:, pair:pair + 1])
            outs.append(jnp.where(first, halves[0], halves[1]))
        o_ref[pl.ds(qstart, GRID_W), :] = jnp.concatenate(outs, axis=-1).astype(BF16)

    def row_body(i, carry):
        for u in range(NA_ROW_UNROLL):
            one_row(i * NA_ROW_UNROLL + u)
        return carry

    lax.fori_loop(0, NA_ROWS_PER_STEP // NA_ROW_UNROLL, row_body, 0)


def _neighbourhood_attention(q, k, v, bias, batch, seq):
    rows = seq // GRID_W
    assert rows >= NA_KH and rows % NA_ROWS_PER_STEP == 0
    nblk = rows // NA_ROWS_PER_STEP
    tq = NA_ROWS_PER_STEP * GRID_W
    qspec = pl.BlockSpec((tq, SECTION), lambda b, j: (b * nblk + j, 0))
    kvspec = pl.BlockSpec((seq, SECTION), lambda b, j: (b, 0))
    return pl.pallas_call(
        functools.partial(_na_kernel, rows=rows),
        grid=(batch, nblk),
        in_specs=[qspec, kvspec, kvspec, _const_spec(bias.shape)],
        out_specs=qspec,
        out_shape=jax.ShapeDtypeStruct(q.shape, BF16),
        compiler_params=_cparams(("parallel", "arbitrary")),
    )(q, k, v, bias)


def _diff_kernel(q_ref, k_ref, v_ref, lam_ref, sub_ref, o_ref, m0_scr, m1_scr, acc0_scr, acc1_scr,
                 *, seq, tq, tk):
    m_scrs, acc_scrs = (m0_scr, m1_scr), (acc0_scr, acc1_scr)
    q = q_ref[...]
    lane = lax.broadcasted_iota(jnp.int32, q.shape, 1)
    zero = jnp.zeros_like(q)
    q_maps = (jnp.where(lane < HEAD_DIM, q, zero), jnp.where(lane >= HEAD_DIM, q, zero))

    lp = lam_ref[...]
    lam = (jnp.exp(jnp.sum(lp[0:1] * lp[1:2], axis=-1, keepdims=True))
           - jnp.exp(jnp.sum(lp[2:3] * lp[3:4], axis=-1, keepdims=True)) + LAMBDA_INIT)

    width = 2 * HEAD_DIM
    ones = jnp.ones((tk, width), BF16)

    for m_scr, acc_scr in zip(m_scrs, acc_scrs):
        m_scr[...] = jnp.full(m_scr.shape, -jnp.inf, F32)
        acc_scr[...] = jnp.zeros(acc_scr.shape, F32)

    def kv_body(c, carry):
        tks = tk // KV_SUBCHUNKS
        kcs, v_exts = [], []
        for sub in range(KV_SUBCHUNKS):
            start = pl.multiple_of(c * tk + sub * tks, tks)
            kcs.append(k_ref[pl.ds(start, tks), :])
            v_exts.append(jnp.concatenate([v_ref[pl.ds(start, tks), :], ones[:tks]], axis=1))
        scores = [[lax.dot_general(q_maps[mp], kcs[sub], (((1,), (1,)), ((), ())),
                                   preferred_element_type=F32) for mp in range(2)]
                  for sub in range(KV_SUBCHUNKS)]
        m = [m_scrs[mp][...] for mp in range(2)]
        probs, alphas = [], []
        for sub in range(KV_SUBCHUNKS):
            probs.append([])
            alphas.append([])
            for mp in range(2):
                mn = jnp.maximum(m[mp], jnp.max(scores[sub][mp], axis=-1, keepdims=True))
                probs[sub].append(jnp.exp2(scores[sub][mp] - mn[:, 0:1]).astype(BF16))
                alpha = jnp.exp2(m[mp] - mn)
                alphas[sub].append(jnp.concatenate([alpha, alpha], axis=1))
                m[mp] = mn
        for mp in range(2):
            m_scrs[mp][...] = m[mp]
            acc = acc_scrs[mp][...]
            for sub in range(KV_SUBCHUNKS):
                acc = alphas[sub][mp] * acc + jnp.dot(probs[sub][mp], v_exts[sub],
                                                      preferred_element_type=F32)
            acc_scrs[mp][...] = acc
        return carry

    lax.fori_loop(0, seq // tk, kv_body, 0)
    a0, a1 = acc0_scr[...], acc1_scr[...]
    o = a0[:, :width] / a0[:, width:width + 1] - lam * (a1[:, :width] / a1[:, width:width + 1])
    ms = jnp.mean(o * o, axis=-1, keepdims=True)
    y = o * lax.rsqrt(ms + EPS) * sub_ref[...]
    o_ref[...] = (y * (1.0 - LAMBDA_INIT)).astype(BF16)


def _diff_attention(q, k, v, lam_params, subln_w, batch, seq):
    tq, tk = min(TQ, seq), min(TK, seq)
    nq = seq // tq
    width = 2 * HEAD_DIM
    qspec = pl.BlockSpec((tq, width), lambda b, h, i: (b * nq + i, h))
    kvspec = pl.BlockSpec((seq, width), lambda b, h, i: (b, h))
    return pl.pallas_call(
        functools.partial(_diff_kernel, seq=seq, tq=tq, tk=tk),
        grid=(batch, DIFF_HEADS, nq),
        in_specs=[qspec, kvspec, kvspec, _const_spec((4, HEAD_DIM)), _const_spec((1, width))],
        out_specs=qspec,
        out_shape=jax.ShapeDtypeStruct(q.shape, BF16),
        scratch_shapes=[pltpu.VMEM((tq, width), F32), pltpu.VMEM((tq, width), F32),
                        pltpu.VMEM((tq, 2 * width), F32), pltpu.VMEM((tq, 2 * width), F32)],
        compiler_params=_cparams(("parallel", "parallel", "arbitrary")),
    )(q, k, v, lam_params, subln_w.reshape(1, width))


def _top16(sc, payload=None):
    iota = lax.broadcasted_iota(jnp.int32, sc.shape, 0).astype(F32)
    big = float(sc.shape[0])
    vals, picks = [], []
    for _ in range(PEER_TOPK):
        m = jnp.max(sc, axis=0, keepdims=True)
        ix = jnp.min(jnp.where(sc == m, iota, big), axis=0, keepdims=True)
        sel = iota == ix
        if payload is None:
            picks.append(ix)
        else:
            picks.append(jnp.sum(jnp.where(sel, payload, 0), axis=0, keepdims=True))
        sc = jnp.where(sel, -jnp.inf, sc)
        vals.append(m)
    picks = jnp.concatenate(picks, axis=0)
    return jnp.concatenate(vals, axis=0), picks.astype(jnp.int32)


def _pair_candidates(r0, r1):
    half = SUBLANES // 2
    groups = [r0[0:1] + r1[0:8], r0[0:1] + r1[8:16]]
    groups += [r0[a:a + 1] + r1[0:8] for a in (1, 2, 3)]
    groups += [jnp.concatenate([r0[a:a + 1] + r1[0:half], r0[a + 1:a + 2] + r1[0:half]], axis=0)
               for a in (4, 6)]
    groups.append(r0[8:16] + r1[0:1])
    return jnp.concatenate(groups, axis=0)


def _route_kernel(na_ref, df_ref, x_ref, wo_ref, n2_ref, wq_ref, sk_ref,
                  x1_ref, xn_ref, eid_ref, g_ref, q_scr, eid_scr, g_scr):
    tm = x_ref.shape[0]
    x1 = (x_ref[...]
          + jnp.dot(na_ref[...], wo_ref[0:SECTION, :], preferred_element_type=F32)
          + jnp.dot(df_ref[...], wo_ref[SECTION:2 * SECTION, :], preferred_element_type=F32))
    ms = jnp.mean(x1 * x1, axis=-1, keepdims=True)
    xn = x1 * lax.rsqrt(ms + EPS) * n2_ref[...]
    for r in range(SUBLANES):
        x1_ref[:, r, :] = x1[:, r * LANES:(r + 1) * LANES]
        xn_ref[:, r, :] = xn[:, r * LANES:(r + 1) * LANES]
    xb = xn.astype(BF16)
    for c in range(2 * PEER_HEADS):
        q_scr[c] = jnp.dot(xb, wq_ref[:, c * LANES:(c + 1) * LANES],
                           preferred_element_type=F32).astype(BF16)

    nblk = tm // TB

    def route_block(h, blk):
        tok0 = blk * TB
        tops = []
        for p in range(2):
            sc = lax.dot_general(sk_ref[2 * h + p], q_scr[2 * h + p, pl.ds(tok0, TB), :],
                                 (((1,), (1,)), ((), ())), preferred_element_type=F32)
            tops.append(_top16(sc))
        (s0, i0), (s1, i1) = tops
        cand_s = _pair_candidates(s0, s1)
        cand_i = _pair_candidates(i0 * N_KEYS, i1)
        top_s, eid = _top16(cand_s, cand_i)
        e = jnp.exp(top_s - top_s[0:1])
        row = pl.multiple_of(h * PEER_TOPK, PEER_TOPK)
        eid_scr[blk, pl.ds(row, PEER_TOPK), :] = eid * ROW_SUB
        g_scr[blk, pl.ds(row, PEER_TOPK), :] = e / jnp.sum(e, axis=0, keepdims=True)

    def head_body(h, carry):
        for blk in range(nblk):
            route_block(h, blk)
        return carry

    lax.fori_loop(0, PEER_HEADS, head_body, 0)
    for blk in range(nblk):
        eid_ref[blk] = eid_scr[blk].T
        g_ref[blk * TB:(blk + 1) * TB, :] = g_scr[blk].T


def _out_and_route(na, df, x2d, w_out_bf, norm2_w, w_q_bf, sk_bf):
    n = x2d.shape[0]
    tm = TM_OUT
    tok = lambda w: pl.BlockSpec((tm, w), lambda i: (i, 0))
    tiles = pl.BlockSpec((tm, SUBLANES, LANES), lambda i: (i, 0, 0))
    return pl.pallas_call(
        _route_kernel,
        grid=(n // tm,),
        in_specs=[tok(SECTION), tok(SECTION), tok(D_MODEL),
                  _const_spec((D_MODEL, D_MODEL)), _const_spec((1, D_MODEL)),
                  _const_spec(w_q_bf.shape), _const_spec(sk_bf.shape)],
        out_specs=[tiles, tiles,
                   pl.BlockSpec((tm // TB, TB, HK), lambda i: (i, 0, 0)), tok(HK)],
        out_shape=[jax.ShapeDtypeStruct((n, SUBLANES, LANES), F32),
                   jax.ShapeDtypeStruct((n, SUBLANES, LANES), F32),
                   jax.ShapeDtypeStruct((n // TB, TB, HK), jnp.int32),
                   jax.ShapeDtypeStruct((n, HK), F32)],
        scratch_shapes=[pltpu.VMEM((2 * PEER_HEADS, tm, LANES), BF16),
                        pltpu.VMEM((tm // TB, HK, TB), jnp.int32), pltpu.VMEM((tm // TB, HK, TB), F32)],
        compiler_params=_cparams(("parallel",)),
    )(na, df, x2d, w_out_bf, norm2_w.reshape(1, D_MODEL), w_q_bf, sk_bf)


ROW_SUB = SUBLANES // 2
CHUNK_SLOTS = 32
N_CHUNKS = HK // CHUNK_SLOTS
CHUNK_ROWS = CHUNK_SLOTS * SUBLANES
TOKEN_UNROLL = 8
ROW_WORDS = ROW_SUB * LANES
SC_CORES, SC_SUBCORES, SC_LANES = 2, 16, 16
SC_WORKERS = SC_CORES * SC_SUBCORES


PACK_EXPERTS = 512


def _pack_kernel(x_ref, o_ref):
    x = x_ref[...]
    for c in range(ROW_SUB):
        lo = x[:, (2 * c) * LANES:(2 * c + 1) * LANES].astype(BF16).astype(F32)
        hi = x[:, (2 * c + 1) * LANES:(2 * c + 2) * LANES].astype(BF16).astype(F32)
        words = (pltpu.bitcast(lo, jnp.uint32) >> 16) | (pltpu.bitcast(hi, jnp.uint32) & jnp.uint32(0xFFFF0000))
        o_ref[pl.ds(c, PACK_EXPERTS, stride=ROW_SUB), :] = words


def _pack_table(tab):
    experts = tab.shape[0]
    return pl.pallas_call(
        _pack_kernel,
        grid=(experts // PACK_EXPERTS,),
        in_specs=[pl.BlockSpec((PACK_EXPERTS, D_MODEL), lambda i: (i, 0))],
        out_specs=pl.BlockSpec((PACK_EXPERTS * ROW_SUB, LANES), lambda i: (i, 0)),
        out_shape=jax.ShapeDtypeStruct((experts * ROW_SUB, LANES), jnp.uint32),
        compiler_params=_cparams(("parallel",)),
    )(tab)


def _split_bf16(x):
    hi = x.astype(BF16)
    return hi, (x - hi.astype(F32)).astype(BF16)


def _gather_chunk(tab_ref, eid_ref, t, chunk):
    ids = eid_ref.at[0, t]
    rows = [tab_ref[pl.ds(pl.multiple_of(ids[chunk * CHUNK_SLOTS + i], ROW_SUB), ROW_SUB), :]
            for i in range(CHUNK_SLOTS)]
    return pltpu.bitcast(jnp.concatenate(rows, axis=0), BF16)


def _diag_mask(width):
    r = lax.broadcasted_iota(jnp.int32, (SUBLANES, width), 0)
    n = lax.broadcasted_iota(jnp.int32, (SUBLANES, width), 1)
    return (n % SUBLANES) == r


def _peer_u_kernel(eid_ref, xn_ref, g_ref, tab_ref, rept_ref, rep_ref, rep16_ref, wrep_ref, w16_ref, d_scr):
    mask = _diag_mask(CHUNK_ROWS)
    zeros = jnp.zeros((SUBLANES, LANES), BF16)

    def one_token(t):
        xhi, xlo = _split_bf16(xn_ref[t])
        lhs = jnp.concatenate([jnp.concatenate([xhi, zeros], axis=1),
                               jnp.concatenate([xlo, zeros], axis=1),
                               jnp.concatenate([zeros, xhi], axis=1),
                               jnp.concatenate([zeros, xlo], axis=1)], axis=0)
        for pr in range(N_CHUNKS // 2):
            s = jnp.concatenate([_gather_chunk(tab_ref, eid_ref, t, 2 * pr),
                                 _gather_chunk(tab_ref, eid_ref, t, 2 * pr + 1)], axis=1)
            z = lax.dot_general(lhs, s, (((1,), (1,)), ((), ())), preferred_element_type=F32)
            for half in range(2):
                z8 = z[16 * half:16 * half + 8] + z[16 * half + 8:16 * half + 16]
                zs = jnp.sum(jnp.where(mask, z8, 0.0), axis=0, keepdims=True)
                d_scr[pl.ds(t, 1), pl.ds((2 * pr + half) * CHUNK_ROWS, CHUNK_ROWS)] = zs

    def tok_body(i, carry):
        for u in range(TOKEN_UNROLL):
            one_token(i * TOKEN_UNROLL + u)
        return carry

    lax.fori_loop(0, PB // TOKEN_UNROLL, tok_body, 0)
    _gate_outputs(d_scr[...], rept_ref, g_ref, rep_ref, rep16_ref, wrep_ref, w16_ref)


def _gate_outputs(partial, sum_ref, g_ref, rep_ref, rep16_ref, wrep_ref, w16_ref):
    dhi, dlo = _split_bf16(partial)
    a = (jnp.dot(dhi, sum_ref[...], preferred_element_type=F32)
         + jnp.dot(dlo, sum_ref[...], preferred_element_type=F32))
    w = g_ref[...] * (0.5 * a * (1.0 + lax.erf(a * (2.0 ** -0.5))))
    whi, wlo = _split_bf16(w)
    wrep_ref[...] = (jnp.dot(whi, rep_ref[...], preferred_element_type=F32)
                     + jnp.dot(wlo, rep_ref[...], preferred_element_type=F32))
    w16_ref[...] = (jnp.dot(whi, rep16_ref[...], preferred_element_type=F32)
                    + jnp.dot(wlo, rep16_ref[...], preferred_element_type=F32))


def _gate_outputs_kernel(part_ref, g_ref, sum_ref, rep_ref, rep16_ref, wrep_ref, w16_ref):
    _gate_outputs(part_ref[...], sum_ref, g_ref, rep_ref, rep16_ref, wrep_ref, w16_ref)


def _gate_kernel(part_ref, g_ref, sum_ref, rep_ref, rep16_ref, wrep_in, w16_in, wrep_ref, w16_ref):
    del wrep_in, w16_in
    _gate_outputs(part_ref[...], sum_ref, g_ref, rep_ref, rep16_ref, wrep_ref, w16_ref)


def _peer_v_kernel(eid_ref, wrep_ref, x1_ref, tab_ref, o_ref):
    mask = _diag_mask(CHUNK_ROWS)

    def one_token(t):
        wr = wrep_ref[pl.ds(t, 1), :]
        acc = x1_ref[t]
        for pr in range(N_CHUNKS // 2):
            parts = []
            for half in range(2):
                seg = wr[:, (2 * pr + half) * CHUNK_ROWS:(2 * pr + half + 1) * CHUNK_ROWS]
                parts.extend(_split_bf16(jnp.where(mask, jnp.broadcast_to(seg, mask.shape), 0.0)))
            lhs = jnp.concatenate(parts, axis=0)
            s = jnp.concatenate([_gather_chunk(tab_ref, eid_ref, t, 2 * pr),
                                 _gather_chunk(tab_ref, eid_ref, t, 2 * pr + 1)], axis=1)
            z = jnp.dot(lhs, s, preferred_element_type=F32)
            acc = acc + ((z[0:8, :LANES] + z[8:16, :LANES]) + (z[16:24, LANES:] + z[24:32, LANES:]))
        return acc

    def tok_body(i, carry):
        accs = [one_token(i * TOKEN_UNROLL + u) for u in range(TOKEN_UNROLL)]
        base = pl.multiple_of(i * TOKEN_UNROLL, TOKEN_UNROLL)
        for r in range(SUBLANES):
            o_ref[pl.ds(base, TOKEN_UNROLL), r * LANES:(r + 1) * LANES] = jnp.concatenate(
                [acc[r:r + 1, :] for acc in accs], axis=0)
        return carry

    lax.fori_loop(0, PB // TOKEN_UNROLL, tok_body, 0)


def _sc_scratch(mid_shape, last_shape):
    half = HK // 2
    per_parity = [pltpu.VMEM((2, half), jnp.int32), pltpu.VMEM(mid_shape, F32), pltpu.VMEM(last_shape, F32)]
    return per_parity + per_parity + [
        pltpu.VMEM((half, ROW_WORDS), jnp.uint32), pltpu.VMEM((half, ROW_WORDS), jnp.uint32),
        pltpu.SemaphoreType.DMA, pltpu.SemaphoreType.DMA,
        pltpu.SemaphoreType.DMA, pltpu.SemaphoreType.DMA,
        pltpu.SemaphoreType.DMA, pltpu.SemaphoreType.DMA,
    ]


def _sc_u_phase(n_tok, tok0):
    per_w = n_tok // SC_WORKERS
    assert per_w * SC_WORKERS == n_tok and per_w % 2 == 0
    half = HK // 2
    mesh = plsc.VectorSubcoreMesh(core_axis_name="c", subcore_axis_name="s")

    @functools.partial(
        pl.kernel, mesh=mesh,
        out_type=jax.ShapeDtypeStruct((n_tok, HK * SC_LANES), F32),
        scratch_types=_sc_scratch((SUBLANES, LANES), (HK * SC_LANES,)),
        compiler_params=pltpu.CompilerParams(needs_layout_passes=False),
    )
    def sc_kernel(tab_hbm, eid_hbm, xn_hbm, out_hbm,
                  idx_a, x_a, a_a, idx_b, x_b, a_b, g0, g1, s_a, s_b, sg0, sg1, so_a, so_b):
        wid = lax.axis_index("s") * SC_CORES + lax.axis_index("c")
        first = wid * per_w
        par = ((idx_a, x_a, a_a, s_a, so_a), (idx_b, x_b, a_b, s_b, so_b))
        gbuf, gsem = (g0, g1), (sg0, sg1)

        def input_copies(t, p):
            idx_v, x_v, _, sem, _ = par[p]
            return (pltpu.make_async_copy(eid_hbm.at[tok0 + t], idx_v, sem),
                    pltpu.make_async_copy(xn_hbm.at[tok0 + t], x_v, sem))

        def gather(p, h):
            return pltpu.make_async_copy(tab_hbm.at[par[p][0].at[h]], gbuf[h], gsem[h])

        def out_copy(t, p):
            return pltpu.make_async_copy(par[p][2], out_hbm.at[t], par[p][4])

        def dot_half(p, h):
            _, x_v, a_v, _, _ = par[p]
            g = gbuf[h]
            groups = LANES // SC_LANES
            block = SUBLANES
            for rb in range(half // block):
                def col_body(k, accs):
                    c, kk = k // groups, k % groups
                    lane0 = pl.multiple_of(kk * SC_LANES, SC_LANES)
                    xl = x_v[2 * c, pl.ds(lane0, SC_LANES)]
                    xh = x_v[2 * c + 1, pl.ds(lane0, SC_LANES)]
                    out = []
                    for r in range(block):
                        words = g[rb * block + r, pl.ds(pl.multiple_of(k * SC_LANES, SC_LANES), SC_LANES)]
                        lov = plsc.bitcast(lax.shift_left(words, jnp.uint32(16)), F32)
                        hiv = plsc.bitcast(words & jnp.uint32(0xFFFF0000), F32)
                        out.append(accs[r] + lov * xl + hiv * xh)
                    return tuple(out)

                zero = jnp.zeros((SC_LANES,), F32)
                accs = lax.fori_loop(0, ROW_WORDS // SC_LANES, col_body, (zero,) * block)
                for r in range(block):
                    a_v[pl.ds((h * half + rb * block + r) * SC_LANES, SC_LANES)] = accs[r]

        def token(t, p):
            q = 1 - p
            has_next = t + 1 < first + per_w
            gather(p, 0).wait()
            gather(p, 1).start()

            @pl.when(has_next)
            def _():
                for cp in input_copies(t + 1, q):
                    cp.start()

            @pl.when(t >= first + 2)
            def _():
                out_copy(t - 2, p).wait()

            dot_half(p, 0)
            gather(p, 1).wait()

            @pl.when(has_next)
            def _():
                for cp in input_copies(t + 1, q):
                    cp.wait()
                gather(q, 0).start()

            dot_half(p, 1)
            out_copy(t, p).start()

        for cp in input_copies(first, 0):
            cp.start()
        for cp in input_copies(first, 0):
            cp.wait()
        gather(0, 0).start()

        def pair_body(i, carry):
            token(first + 2 * i, 0)
            token(first + 2 * i + 1, 1)
            return carry

        lax.fori_loop(0, per_w // 2, pair_body, 0)
        out_copy(first + per_w - 2, 0).wait()
        out_copy(first + per_w - 1, 1).wait()

    return sc_kernel


def _sc_v_phase(n_tok, tok0):
    per_w = n_tok // SC_WORKERS
    assert per_w * SC_WORKERS == n_tok and per_w % 2 == 0
    half = HK // 2
    mesh = plsc.VectorSubcoreMesh(core_axis_name="c", subcore_axis_name="s")

    @functools.partial(
        pl.kernel, mesh=mesh,
        out_type=jax.ShapeDtypeStruct((n_tok, SUBLANES, LANES), F32),
        scratch_types=_sc_scratch((HK * SC_LANES,), (SUBLANES, LANES)),
        compiler_params=pltpu.CompilerParams(needs_layout_passes=False),
    )
    def sc_kernel(tab_hbm, eid_hbm, w16_hbm, x1_hbm, out_hbm,
                  idx_a, w_a, x_a, idx_b, w_b, x_b, g0, g1, s_a, s_b, sg0, sg1, so_a, so_b):
        wid = lax.axis_index("s") * SC_CORES + lax.axis_index("c")
        first = wid * per_w
        par = ((idx_a, w_a, x_a, s_a, so_a), (idx_b, w_b, x_b, s_b, so_b))
        gbuf, gsem = (g0, g1), (sg0, sg1)

        def input_copies(t, p):
            idx_v, w_v, x_v, sem, _ = par[p]
            return (pltpu.make_async_copy(eid_hbm.at[tok0 + t], idx_v, sem),
                    pltpu.make_async_copy(w16_hbm.at[tok0 + t], w_v, sem),
                    pltpu.make_async_copy(x1_hbm.at[tok0 + t], x_v, sem))

        def gather(p, h):
            return pltpu.make_async_copy(tab_hbm.at[par[p][0].at[h]], gbuf[h], gsem[h])

        def out_copy(t, p):
            return pltpu.make_async_copy(par[p][2], out_hbm.at[t], par[p][4])

        def accumulate_half(p, h):
            _, w_v, x_v, _, _ = par[p]
            g = gbuf[h]
            groups = LANES // SC_LANES
            for c in range(ROW_SUB):
                lo0 = tuple(x_v[2 * c, pl.ds(SC_LANES * kk, SC_LANES)] for kk in range(groups))
                hi0 = tuple(x_v[2 * c + 1, pl.ds(SC_LANES * kk, SC_LANES)] for kk in range(groups))

                def row_body(j, acc):
                    lo, hi = acc
                    wj = w_v[pl.ds(pl.multiple_of((h * half + j) * SC_LANES, SC_LANES), SC_LANES)]
                    nlo, nhi = [], []
                    for kk in range(groups):
                        words = g[j, pl.ds(LANES * c + SC_LANES * kk, SC_LANES)]
                        lov = plsc.bitcast(lax.shift_left(words, jnp.uint32(16)), F32)
                        hiv = plsc.bitcast(words & jnp.uint32(0xFFFF0000), F32)
                        nlo.append(lo[kk] + wj * lov)
                        nhi.append(hi[kk] + wj * hiv)
                    return tuple(nlo), tuple(nhi)

                lo, hi = lax.fori_loop(0, half, row_body, (lo0, hi0))
                for kk in range(groups):
                    x_v[2 * c, pl.ds(SC_LANES * kk, SC_LANES)] = lo[kk]
                    x_v[2 * c + 1, pl.ds(SC_LANES * kk, SC_LANES)] = hi[kk]

        def token(t, p):
            q = 1 - p
            has_next = t + 1 < first + per_w
            gather(p, 0).wait()
            gather(p, 1).start()

            @pl.when(has_next)
            def _():
                @pl.when(t >= first + 1)
                def _():
                    out_copy(t - 1, q).wait()
                for cp in input_copies(t + 1, q):
                    cp.start()

            accumulate_half(p, 0)
            gather(p, 1).wait()

            @pl.when(has_next)
            def _():
                for cp in input_copies(t + 1, q):
                    cp.wait()
                gather(q, 0).start()

            accumulate_half(p, 1)
            out_copy(t, p).start()

        for cp in input_copies(first, 0):
            cp.start()
        for cp in input_copies(first, 0):
            cp.wait()
        gather(0, 0).start()

        def pair_body(i, carry):
            token(first + 2 * i, 0)
            token(first + 2 * i + 1, 1)
            return carry

        lax.fori_loop(0, per_w // 2, pair_body, 0)
        out_copy(first + per_w - 2, 0).wait()
        out_copy(first + per_w - 1, 1).wait()

    return sc_kernel


def _peer_experts(eid, g, xn3, x13, u_pack, v_pack, n_sc_u, n_sc_v):
    n = xn3.shape[0]
    n_tc_u, n_tc_v = n - n_sc_u, n - n_sc_v
    assert n_tc_u % PB == 0 and n_tc_v % PB == 0
    eid = eid.reshape(n // PB, PB, HK)
    smem_slot = pl.BlockSpec((1, PB, HK), lambda i: (i, 0, 0), memory_space=pltpu.SMEM)
    tok = pl.BlockSpec((PB, SUBLANES, LANES), lambda i: (i, 0, 0))
    flat = lambda w, off=0: pl.BlockSpec((PB, w), lambda i: (i + off, 0))
    table = pl.BlockSpec(u_pack.shape, lambda i: (0, 0), pipeline_mode=pl.Buffered(1))

    def repeat_matrix(times):
        m = (np.arange(HK)[:, None] == (np.arange(HK * times)[None, :] // times)).astype(np.float32)
        return jnp.asarray(m, BF16)

    rep, rep16 = repeat_matrix(SUBLANES), repeat_matrix(SC_LANES)
    gate_shapes = [jax.ShapeDtypeStruct((n, D_MODEL), F32), jax.ShapeDtypeStruct((n, HK * SC_LANES), F32)]
    expert = eid.reshape(n, 2, HK // 2) >> (ROW_SUB.bit_length() - 1)
    if n_sc_u:
        u_rows = u_pack.reshape(N_EXPERTS, ROW_WORDS)
        partial = _sc_u_phase(n_sc_u, n_tc_u)(u_rows, expert, xn3)
    if n_tc_u:
        wrep, w16 = pl.pallas_call(
            _peer_u_kernel,
            grid=(n_tc_u // PB,),
            in_specs=[smem_slot, tok, flat(HK), table, _const_spec(rep.T.shape), _const_spec(rep.shape),
                      _const_spec(rep16.shape)],
            out_specs=[flat(D_MODEL), flat(HK * SC_LANES)],
            out_shape=gate_shapes,
            scratch_shapes=[pltpu.VMEM((PB, D_MODEL), F32)],
            compiler_params=_cparams(("arbitrary",)),
        )(eid, xn3, g, u_pack, rep.T, rep, rep16)
    if n_sc_u:
        off = n_tc_u // PB
        gate_in = [flat(HK * SC_LANES), flat(HK, off), _const_spec(rep16.T.shape), _const_spec(rep.shape),
                   _const_spec(rep16.shape)]
        gate_args = (partial, g, rep16.T, rep, rep16)
        if n_tc_u:
            anyspec = pl.BlockSpec(memory_space=pl.ANY)
            gate_fn, gate_in, gate_args = _gate_kernel, gate_in + [anyspec, anyspec], gate_args + (wrep, w16)
            aliases = {5: 0, 6: 1}
        else:
            gate_fn, aliases = _gate_outputs_kernel, {}
        wrep, w16 = pl.pallas_call(
            gate_fn,
            grid=(n_sc_u // PB,),
            in_specs=gate_in,
            out_specs=[flat(D_MODEL, off), flat(HK * SC_LANES, off)],
            out_shape=gate_shapes,
            input_output_aliases=aliases,
            compiler_params=_cparams(("arbitrary",)),
        )(*gate_args)
    parts = []
    if n_tc_v:
        parts.append(pl.pallas_call(
            _peer_v_kernel,
            grid=(n_tc_v // PB,),
            in_specs=[smem_slot, flat(D_MODEL), tok, table],
            out_specs=flat(D_MODEL),
            out_shape=jax.ShapeDtypeStruct((n_tc_v, D_MODEL), F32),
            compiler_params=_cparams(("arbitrary",)),
        )(eid, wrep, x13, v_pack))
    if n_sc_v:
        v_rows = v_pack.reshape(N_EXPERTS, ROW_WORDS)
        parts.append(_sc_v_phase(n_sc_v, n_tc_v)(v_rows, expert, w16, x13).reshape(n_sc_v, D_MODEL))
    y = parts[0] if len(parts) == 1 else jnp.concatenate(parts, axis=0)
    return y, w16


def _encoder_layer(x, p, after):
    if after is not None:
        x, _ = lax.optimization_barrier((x, after))
    batch, seq, _ = x.shape
    n = batch * seq
    x2d = x.reshape(n, D_MODEL)
    naq, nak, nav, dq, dk, dv = _in_projection(x2d, seq, p["norm1_w"], p["w_in"], p["qk_w"], p["rope"])
    na = _neighbourhood_attention(naq, nak, nav, p["na_bias"], batch, seq)
    df = _diff_attention(dq, dk, dv, p["lam"], p["subln_w"], batch, seq)
    x1, xn, eid, g = _out_and_route(na, df, x2d, p["w_out"], p["norm2_w"], p["w_q"], p["sub_keys"])
    y, w16 = _peer_experts(eid, g, xn, x1, p["u_pack"], p["v_pack"], *SC_TOKENS.get(n, (0, 0)))
    return y.reshape(batch, seq, D_MODEL), w16


def kernel(x_prompt, x_sample, norm1_w, w_in, na_q_norm, na_k_norm, na_rpb, diff_q_norm, diff_k_norm,
           diff_lambda_q1, diff_lambda_k1, diff_lambda_q2, diff_lambda_k2, diff_subln_w, w_out, norm2_w,
           peer_w_q, peer_sub_keys, peer_u, peer_v):
    assert norm1_w.shape[0] == 1, "single-layer problem"
    reps = SECTION // HEAD_DIM
    params = {
        "norm1_w": norm1_w[0],
        "w_in": w_in[0].astype(BF16),
        "qk_w": jnp.stack([jnp.tile(w[0], reps) for w in (na_q_norm, na_k_norm, diff_q_norm, diff_k_norm)]),
        "lam": jnp.stack([diff_lambda_q1[0], diff_lambda_k1[0], diff_lambda_q2[0], diff_lambda_k2[0]]),
        "subln_w": diff_subln_w[0],
        "w_out": w_out[0].astype(BF16),
        "norm2_w": norm2_w[0],
        "w_q": peer_w_q[0].astype(BF16),
        "sub_keys": peer_sub_keys[0].reshape(2 * PEER_HEADS, N_KEYS, N_KEYS).astype(BF16),
        "na_bias": _na_bias(na_rpb[0]),
        "u_pack": _pack_table(peer_u[0]),
        "v_pack": _pack_table(peer_v[0]),
    }
    rope_by_seq = {}
    outs = [None, None]
    marker = None
    for slot, x in sorted(enumerate((x_prompt, x_sample)), key=lambda item: -item[1].shape[0] * item[1].shape[1]):
        seq = x.shape[1]
        if seq not in rope_by_seq:
            rope_by_seq[seq] = _rope_tables(seq)
        outs[slot], marker = _encoder_layer(x, dict(params, rope=rope_by_seq[seq]), marker)
    return tuple(outs)
```

```python
import functools
import math

import jax
import jax.numpy as jnp
import numpy as np
from jax import lax
from jax.experimental import pallas as pl
from jax.experimental.pallas import tpu as pltpu
from jax.experimental.pallas import tpu_sc as plsc

F32 = jnp.float32
BF16 = jnp.bfloat16

D_MODEL = 1024
GRID_W = 64
HEAD_DIM = 64
NA_HEADS = 8
NA_KH = 8
NA_KW = 16
DIFF_HEADS = 4
SECTION = 512
ROT_DIM = HEAD_DIM // 4
ROPE_THETA = 500000.0
PEER_HEADS = 8
N_KEYS = 128
PEER_TOPK = 16
N_EXPERTS = N_KEYS * N_KEYS
HK = PEER_HEADS * PEER_TOPK
EPS = 1e-6
NEG_BIG = -1e30
LOG2E = math.log2(math.e)
LAMBDA_INIT = 0.8 - 0.6 * math.exp(-0.3 * 0)

LANES = 128
SUBLANES = 8
VMEM_LIMIT = 56 * 1024 * 1024

TM_IN = 512
NA_ROWS_PER_STEP = 8
NA_ROW_UNROLL = 2
TQ = 512
KV_SUBCHUNKS = 1
TK = 4096
TM_OUT = 512
TB = 128
PB = 256
SC_TOKENS = {8 * 4096: (14336, 8 * 4096)}


def _cparams(sem):
    return pltpu.CompilerParams(dimension_semantics=sem, vmem_limit_bytes=VMEM_LIMIT)


def _const_spec(shape):
    nd = len(shape)
    return pl.BlockSpec(shape, lambda *_: (0,) * nd)


def _inproj_kernel(x_ref, n1_ref, w_ref, bd_ref, nw_ref, rc_ref, ra_ref, rb_ref,
                   naq_ref, nak_ref, nav_ref, dq_ref, dk_ref, dv_ref):
    x = x_ref[...]
    ms = jnp.mean(x * x, axis=-1, keepdims=True)
    h = (x * lax.rsqrt(ms + EPS) * n1_ref[...]).astype(BF16)
    bd = bd_ref[...]

    def proj(c):
        return jnp.dot(h, w_ref[:, c * SECTION:(c + 1) * SECTION], preferred_element_type=F32)

    def qknorm(y, row):
        sq = y * y
        hi = sq.astype(BF16)
        lo = (sq - hi.astype(F32)).astype(BF16)
        msq = (jnp.dot(hi, bd, preferred_element_type=F32)
               + jnp.dot(lo, bd, preferred_element_type=F32))
        return y * lax.rsqrt(msq + EPS) * nw_ref[row:row + 1, :]

    def rope(y):
        return (y * rc_ref[...]
                + pltpu.roll(y, SECTION - ROT_DIM // 2, axis=1) * ra_ref[...]
                + pltpu.roll(y, ROT_DIM // 2, axis=1) * rb_ref[...])

    scale = HEAD_DIM ** -0.5 * LOG2E
    naq_ref[...] = (qknorm(proj(0), 0) * scale).astype(BF16)
    nak_ref[...] = qknorm(proj(1), 1).astype(BF16)
    nav_ref[...] = proj(2).astype(BF16)
    dq_ref[...] = (rope(qknorm(proj(3), 2)) * scale).astype(BF16)
    dk_ref[...] = rope(qknorm(proj(4), 3)).astype(BF16)
    dv_ref[...] = proj(5).astype(BF16)


def _rope_tables(seq):
    pos = jnp.arange(seq, dtype=F32)
    inv = ROPE_THETA ** (-jnp.arange(0, ROT_DIM, 2, dtype=F32) / ROT_DIM)
    ang = pos[:, None] * inv[None, :]
    cos, sin = jnp.cos(ang), jnp.sin(ang)
    half = ROT_DIM // 2
    pad = HEAD_DIM - ROT_DIM
    c_head = jnp.concatenate([cos, cos, jnp.ones((seq, pad), F32)], -1)
    a_head = jnp.concatenate([-sin, jnp.zeros((seq, pad + half), F32)], -1)
    b_head = jnp.concatenate([jnp.zeros((seq, half), F32), sin, jnp.zeros((seq, pad), F32)], -1)
    reps = SECTION // HEAD_DIM
    return (jnp.tile(c_head, (1, reps)), jnp.tile(a_head, (1, reps)), jnp.tile(b_head, (1, reps)))


def _in_projection(x2d, seq, norm1_w, w_in_bf, qk_w, rope_tabs):
    n = x2d.shape[0]
    tm = min(TM_IN, seq)
    steps_per_seq = seq // tm
    gid = np.arange(SECTION) // HEAD_DIM
    bd = jnp.asarray((gid[:, None] == gid[None, :]).astype(np.float32) / HEAD_DIM, BF16)
    tok = pl.BlockSpec((tm, SECTION), lambda i: (i, 0))
    pos = pl.BlockSpec((tm, SECTION), lambda i: (i % steps_per_seq, 0))
    out = jax.ShapeDtypeStruct((n, SECTION), BF16)
    return pl.pallas_call(
        _inproj_kernel,
        grid=(n // tm,),
        in_specs=[pl.BlockSpec((tm, D_MODEL), lambda i: (i, 0)),
                  _const_spec((1, D_MODEL)),
                  _const_spec((D_MODEL, 6 * SECTION)),
                  _const_spec((SECTION, SECTION)),
                  _const_spec((4, SECTION)),
                  pos, pos, pos],
        out_specs=[tok] * 6,
        out_shape=[out] * 6,
        compiler_params=_cparams(("parallel",)),
    )(x2d, norm1_w.reshape(1, D_MODEL), w_in_bf, bd, qk_w, *rope_tabs)


def _na_bias(rpb):
    c = np.arange(GRID_W)[:, None]
    kc = np.arange(GRID_W)[None, :]
    cs = np.clip(c - NA_KW // 2, 0, GRID_W - NA_KW)
    valid = (kc >= cs) & (kc < cs + NA_KW)
    rel_c = kc - c + (NA_KW - 1)
    onehot = ((np.arange(2 * NA_KW - 1)[:, None, None] == rel_c[None]) & valid[None]).astype(np.float32)
    col = jnp.einsum("hrx,xck->hrck", rpb.astype(F32), onehot, precision=lax.Precision.HIGHEST)
    col = jnp.where(valid, col * LOG2E, NEG_BIG)
    per_off = [col[:, NA_KH - 1 - o:2 * NA_KH - 1 - o] for o in range(NA_KH)]
    b = jnp.stack(per_off).transpose(0, 1, 3, 2, 4)
    return b.reshape(NA_KH, NA_HEADS, GRID_W, NA_KH * GRID_W)


def _na_kernel(q_ref, k_ref, v_ref, b_ref, o_ref, *, rows):
    j = pl.program_id(1)
    win = NA_KH * GRID_W
    pair = 2 * HEAD_DIM
    lane = lax.broadcasted_iota(jnp.int32, (GRID_W, pair), 1)
    first = lane < HEAD_DIM
    ones = jnp.ones((win, pair), BF16)

    def one_row(rr):
        r = j * NA_ROWS_PER_STEP + rr
        rs = jnp.clip(r - NA_KH // 2, 0, rows - NA_KH)
        off = r - rs
        kstart = pl.multiple_of(rs * GRID_W, GRID_W)
        qstart = pl.multiple_of(rr * GRID_W, GRID_W)
        outs = []
        for hp in range(NA_HEADS // 2):
            sl = slice(hp * pair, (hp + 1) * pair)
            qp = q_ref[pl.ds(qstart, GRID_W), sl]
            kp = k_ref[pl.ds(kstart, win), sl]
            v_ext = jnp.concatenate([v_ref[pl.ds(kstart, win), sl], ones], axis=1)
            halves = []
            for hh in range(2):
                qm = jnp.where(first if hh == 0 else ~first, qp, jnp.zeros_like(qp))
                s = lax.dot_general(qm, kp, (((1,), (1,)), ((), ())), preferred_element_type=F32)
                s = s + b_ref[off, 2 * hp + hh]
                p = jnp.exp2(s - jnp.max(s, axis=-1, keepdims=True)).astype(BF16)
                z = jnp.dot(p, v_ext, preferred_element_type=F32)
                halves.append(z[:, :pair] / z[:, pair:pair + 1])
            outs.append(jnp.where(first, halves[0], halves[1]))
        o_ref[pl.ds(qstart, GRID_W), :] = jnp.concatenate(outs, axis=-1).astype(BF16)

    def row_body(i, carry):
        for u in range(NA_ROW_UNROLL):
            one_row(i * NA_ROW_UNROLL + u)
        return carry

    lax.fori_loop(0, NA_ROWS_PER_STEP // NA_ROW_UNROLL, row_body, 0)


def _neighbourhood_attention(q, k, v, bias, batch, seq):
    rows = seq // GRID_W
    assert rows >= NA_KH and rows % NA_ROWS_PER_STEP == 0
    nblk = rows // NA_ROWS_PER_STEP
    tq = NA_ROWS_PER_STEP * GRID_W
    qspec = pl.BlockSpec((tq, SECTION), lambda b, j: (b * nblk + j, 0))
    kvspec = pl.BlockSpec((seq, SECTION), lambda b, j: (b, 0))
    return pl.pallas_call(
        functools.partial(_na_kernel, rows=rows),
        grid=(batch, nblk),
        in_specs=[qspec, kvspec, kvspec, _const_spec(bias.shape)],
        out_specs=qspec,
        out_shape=jax.ShapeDtypeStruct(q.shape, BF16),
        compiler_params=_cparams(("parallel", "arbitrary")),
    )(q, k, v, bias)


def _diff_kernel(q_ref, k_ref, v_ref, lam_ref, sub_ref, o_ref, m0_scr, m1_scr, acc0_scr, acc1_scr,
                 *, seq, tq, tk):
    m_scrs, acc_scrs = (m0_scr, m1_scr), (acc0_scr, acc1_scr)
    q = q_ref[...]
    lane = lax.broadcasted_iota(jnp.int32, q.shape, 1)
    zero = jnp.zeros_like(q)
    q_maps = (jnp.where(lane < HEAD_DIM, q, zero), jnp.where(lane >= HEAD_DIM, q, zero))

    lp = lam_ref[...]
    lam = (jnp.exp(jnp.sum(lp[0:1] * lp[1:2], axis=-1, keepdims=True))
           - jnp.exp(jnp.sum(lp[2:3] * lp[3:4], axis=-1, keepdims=True)) + LAMBDA_INIT)

    width = 2 * HEAD_DIM
    ones = jnp.ones((tk, width), BF16)

    for m_scr, acc_scr in zip(m_scrs, acc_scrs):
        m_scr[...] = jnp.full(m_scr.shape, -jnp.inf, F32)
        acc_scr[...] = jnp.zeros(acc_scr.shape, F32)

    def kv_body(c, carry):
        tks = tk // KV_SUBCHUNKS
        kcs, v_exts = [], []
        for sub in range(KV_SUBCHUNKS):
            start = pl.multiple_of(c * tk + sub * tks, tks)
            kcs.append(k_ref[pl.ds(start, tks), :])
            v_exts.append(jnp.concatenate([v_ref[pl.ds(start, tks), :], ones[:tks]], axis=1))
        scores = [[lax.dot_general(q_maps[mp], kcs[sub], (((1,), (1,)), ((), ())),
                                   preferred_element_type=F32) for mp in range(2)]
                  for sub in range(KV_SUBCHUNKS)]
        m = [m_scrs[mp][...] for mp in range(2)]
        probs, alphas = [], []
        for sub in range(KV_SUBCHUNKS):
            probs.append([])
            alphas.append([])
            for mp in range(2):
                mn = jnp.maximum(m[mp], jnp.max(scores[sub][mp], axis=-1, keepdims=True))
                probs[sub].append(jnp.exp2(scores[sub][mp] - mn[:, 0:1]).astype(BF16))
                alpha = jnp.exp2(m[mp] - mn)
                alphas[sub].append(jnp.concatenate([alpha, alpha], axis=1))
                m[mp] = mn
        for mp in range(2):
            m_scrs[mp][...] = m[mp]
            acc = acc_scrs[mp][...]
            for sub in range(KV_SUBCHUNKS):
                acc = alphas[sub][mp] * acc + jnp.dot(probs[sub][mp], v_exts[sub],
                                                      preferred_element_type=F32)
            acc_scrs[mp][...] = acc
        return carry

    lax.fori_loop(0, seq // tk, kv_body, 0)
    a0, a1 = acc0_scr[...], acc1_scr[...]
    o = a0[:, :width] / a0[:, width:width + 1] - lam * (a1[:, :width] / a1[:, width:width + 1])
    ms = jnp.mean(o * o, axis=-1, keepdims=True)
    y = o * lax.rsqrt(ms + EPS) * sub_ref[...]
    o_ref[...] = (y * (1.0 - LAMBDA_INIT)).astype(BF16)


def _diff_attention(q, k, v, lam_params, subln_w, batch, seq):
    tq, tk = min(TQ, seq), min(TK, seq)
    nq = seq // tq
    width = 2 * HEAD_DIM
    qspec = pl.BlockSpec((tq, width), lambda b, h, i: (b * nq + i, h))
    kvspec = pl.BlockSpec((seq, width), lambda b, h, i: (b, h))
    return pl.pallas_call(
        functools.partial(_diff_kernel, seq=seq, tq=tq, tk=tk),
        grid=(batch, DIFF_HEADS, nq),
        in_specs=[qspec, kvspec, kvspec, _const_spec((4, HEAD_DIM)), _const_spec((1, width))],
        out_specs=qspec,
        out_shape=jax.ShapeDtypeStruct(q.shape, BF16),
        scratch_shapes=[pltpu.VMEM((tq, width), F32), pltpu.VMEM((tq, width), F32),
                        pltpu.VMEM((tq, 2 * width), F32), pltpu.VMEM((tq, 2 * width), F32)],
        compiler_params=_cparams(("parallel", "parallel", "arbitrary")),
    )(q, k, v, lam_params, subln_w.reshape(1, width))


def _top16(sc, payload=None):
    iota = lax.broadcasted_iota(jnp.int32, sc.shape, 0).astype(F32)
    big = float(sc.shape[0])
    vals, picks = [], []
    for _ in range(PEER_TOPK):
        m = jnp.max(sc, axis=0, keepdims=True)
        ix = jnp.min(jnp.where(sc == m, iota, big), axis=0, keepdims=True)
        sel = iota == ix
        if payload is None:
            picks.append(ix)
        else:
            picks.append(jnp.sum(jnp.where(sel, payload, 0), axis=0, keepdims=True))
        sc = jnp.where(sel, -jnp.inf, sc)
        vals.append(m)
    picks = jnp.concatenate(picks, axis=0)
    return jnp.concatenate(vals, axis=0), picks.astype(jnp.int32)


def _pair_candidates(r0, r1):
    half = SUBLANES // 2
    groups = [r0[0:1] + r1[0:8], r0[0:1] + r1[8:16]]
    groups += [r0[a:a + 1] + r1[0:8] for a in (1, 2, 3)]
    groups += [jnp.concatenate([r0[a:a + 1] + r1[0:half], r0[a + 1:a + 2] + r1[0:half]], axis=0)
               for a in (4, 6)]
    groups.append(r0[8:16] + r1[0:1])
    return jnp.concatenate(groups, axis=0)


def _route_kernel(na_ref, df_ref, x_ref, wo_ref, n2_ref, wq_ref, sk_ref,
                  x1_ref, xn_ref, eid_ref, g_ref, q_scr, eid_scr, g_scr):
    tm = x_ref.shape[0]
    x1 = (x_ref[...]
          + jnp.dot(na_ref[...], wo_ref[0:SECTION, :], preferred_element_type=F32)
          + jnp.dot(df_ref[...], wo_ref[SECTION:2 * SECTION, :], preferred_element_type=F32))
    ms = jnp.mean(x1 * x1, axis=-1, keepdims=True)
    xn = x1 * lax.rsqrt(ms + EPS) * n2_ref[...]
    for r in range(SUBLANES):
        x1_ref[:, r, :] = x1[:, r * LANES:(r + 1) * LANES]
        xn_ref[:, r, :] = xn[:, r * LANES:(r + 1) * LANES]
    xb = xn.astype(BF16)
    for c in range(2 * PEER_HEADS):
        q_scr[c] = jnp.dot(xb, wq_ref[:, c * LANES:(c + 1) * LANES],
                           preferred_element_type=F32).astype(BF16)

    nblk = tm // TB

    def route_block(h, blk):
        tok0 = blk * TB
        tops = []
        for p in range(2):
            sc = lax.dot_general(sk_ref[2 * h + p], q_scr[2 * h + p, pl.ds(tok0, TB), :],
                                 (((1,), (1,)), ((), ())), preferred_element_type=F32)
            tops.append(_top16(sc))
        (s0, i0), (s1, i1) = tops
        cand_s = _pair_candidates(s0, s1)
        cand_i = _pair_candidates(i0 * N_KEYS, i1)
        top_s, eid = _top16(cand_s, cand_i)
        e = jnp.exp(top_s - top_s[0:1])
        row = pl.multiple_of(h * PEER_TOPK, PEER_TOPK)
        eid_scr[blk, pl.ds(row, PEER_TOPK), :] = eid * ROW_SUB
        g_scr[blk, pl.ds(row, PEER_TOPK), :] = e / jnp.sum(e, axis=0, keepdims=True)

    def head_body(h, carry):
        for blk in range(nblk):
            route_block(h, blk)
        return carry

    lax.fori_loop(0, PEER_HEADS, head_body, 0)
    for blk in range(nblk):
        eid_ref[blk] = eid_scr[blk].T
        g_ref[blk * TB:(blk + 1) * TB, :] = g_scr[blk].T


def _out_and_route(na, df, x2d, w_out_bf, norm2_w, w_q_bf, sk_bf):
    n = x2d.shape[0]
    tm = TM_OUT
    tok = lambda w: pl.BlockSpec((tm, w), lambda i: (i, 0))
    tiles = pl.BlockSpec((tm, SUBLANES, LANES), lambda i: (i, 0, 0))
    return pl.pallas_call(
        _route_kernel,
        grid=(n // tm,),
        in_specs=[tok(SECTION), tok(SECTION), tok(D_MODEL),
                  _const_spec((D_MODEL, D_MODEL)), _const_spec((1, D_MODEL)),
                  _const_spec(w_q_bf.shape), _const_spec(sk_bf.shape)],
        out_specs=[tiles, tiles,
                   pl.BlockSpec((tm // TB, TB, HK), lambda i: (i, 0, 0)), tok(HK)],
        out_shape=[jax.ShapeDtypeStruct((n, SUBLANES, LANES), F32),
                   jax.ShapeDtypeStruct((n, SUBLANES, LANES), F32),
                   jax.ShapeDtypeStruct((n // TB, TB, HK), jnp.int32),
                   jax.ShapeDtypeStruct((n, HK), F32)],
        scratch_shapes=[pltpu.VMEM((2 * PEER_HEADS, tm, LANES), BF16),
                        pltpu.VMEM((tm // TB, HK, TB), jnp.int32), pltpu.VMEM((tm // TB, HK, TB), F32)],
        compiler_params=_cparams(("parallel",)),
    )(na, df, x2d, w_out_bf, norm2_w.reshape(1, D_MODEL), w_q_bf, sk_bf)


ROW_SUB = SUBLANES // 2
CHUNK_SLOTS = 32
N_CHUNKS = HK // CHUNK_SLOTS
CHUNK_ROWS = CHUNK_SLOTS * SUBLANES
TOKEN_UNROLL = 8
ROW_WORDS = ROW_SUB * LANES
SC_CORES, SC_SUBCORES, SC_LANES = 2, 16, 16
SC_WORKERS = SC_CORES * SC_SUBCORES


PACK_EXPERTS = 512


def _pack_kernel(x_ref, o_ref):
    x = x_ref[...]
    for c in range(ROW_SUB):
        lo = x[:, (2 * c) * LANES:(2 * c + 1) * LANES].astype(BF16).astype(F32)
        hi = x[:, (2 * c + 1) * LANES:(2 * c + 2) * LANES].astype(BF16).astype(F32)
        words = (pltpu.bitcast(lo, jnp.uint32) >> 16) | (pltpu.bitcast(hi, jnp.uint32) & jnp.uint32(0xFFFF0000))
        o_ref[pl.ds(c, PACK_EXPERTS, stride=ROW_SUB), :] = words


def _pack_table(tab):
    experts = tab.shape[0]
    return pl.pallas_call(
        _pack_kernel,
        grid=(experts // PACK_EXPERTS,),
        in_specs=[pl.BlockSpec((PACK_EXPERTS, D_MODEL), lambda i: (i, 0))],
        out_specs=pl.BlockSpec((PACK_EXPERTS * ROW_SUB, LANES), lambda i: (i, 0)),
        out_shape=jax.ShapeDtypeStruct((experts * ROW_SUB, LANES), jnp.uint32),
        compiler_params=_cparams(("parallel",)),
    )(tab)


def _split_bf16(x):
    hi = x.astype(BF16)
    return hi, (x - hi.astype(F32)).astype(BF16)


def _gather_chunk(tab_ref, eid_ref, t, chunk):
    ids = eid_ref.at[0, t]
    rows = [tab_ref[pl.ds(pl.multiple_of(ids[chunk * CHUNK_SLOTS + i], ROW_SUB), ROW_SUB), :]
            for i in range(CHUNK_SLOTS)]
    return pltpu.bitcast(jnp.concatenate(rows, axis=0), BF16)


def _diag_mask(width):
    r = lax.broadcasted_iota(jnp.int32, (SUBLANES, width), 0)
    n = lax.broadcasted_iota(jnp.int32, (SUBLANES, width), 1)
    return (n % SUBLANES) == r


def _peer_u_kernel(eid_ref, xn_ref, g_ref, tab_ref, rept_ref, rep_ref, rep16_ref, wrep_ref, w16_ref, d_scr):
    mask = _diag_mask(CHUNK_ROWS)
    zeros = jnp.zeros((SUBLANES, LANES), BF16)

    def one_token(t):
        xhi, xlo = _split_bf16(xn_ref[t])
        lhs = jnp.concatenate([jnp.concatenate([xhi, zeros], axis=1),
                               jnp.concatenate([xlo, zeros], axis=1),
                               jnp.concatenate([zeros, xhi], axis=1),
                               jnp.concatenate([zeros, xlo], axis=1)], axis=0)
        for pr in range(N_CHUNKS // 2):
            s = jnp.concatenate([_gather_chunk(tab_ref, eid_ref, t, 2 * pr),
                                 _gather_chunk(tab_ref, eid_ref, t, 2 * pr + 1)], axis=1)
            z = lax.dot_general(lhs, s, (((1,), (1,)), ((), ())), preferred_element_type=F32)
            for half in range(2):
                z8 = z[16 * half:16 * half + 8] + z[16 * half + 8:16 * half + 16]
                zs = jnp.sum(jnp.where(mask, z8, 0.0), axis=0, keepdims=True)
                d_scr[pl.ds(t, 1), pl.ds((2 * pr + half) * CHUNK_ROWS, CHUNK_ROWS)] = zs

    def tok_body(i, carry):
        for u in range(TOKEN_UNROLL):
            one_token(i * TOKEN_UNROLL + u)
        return carry

    lax.fori_loop(0, PB // TOKEN_UNROLL, tok_body, 0)
    _gate_outputs(d_scr[...], rept_ref, g_ref, rep_ref, rep16_ref, wrep_ref, w16_ref)


def _gate_outputs(partial, sum_ref, g_ref, rep_ref, rep16_ref, wrep_ref, w16_ref):
    dhi, dlo = _split_bf16(partial)
    a = (jnp.dot(dhi, sum_ref[...], preferred_element_type=F32)
         + jnp.dot(dlo, sum_ref[...], preferred_element_type=F32))
    w = g_ref[...] * (0.5 * a * (1.0 + lax.erf(a * (2.0 ** -0.5))))
    whi, wlo = _split_bf16(w)
    wrep_ref[...] = (jnp.dot(whi, rep_ref[...], preferred_element_type=F32)
                     + jnp.dot(wlo, rep_ref[...], preferred_element_type=F32))
    w16_ref[...] = (jnp.dot(whi, rep16_ref[...], preferred_element_type=F32)
                    + jnp.dot(wlo, rep16_ref[...], preferred_element_type=F32))


def _gate_outputs_kernel(part_ref, g_ref, sum_ref, rep_ref, rep16_ref, wrep_ref, w16_ref):
    _gate_outputs(part_ref[...], sum_ref, g_ref, rep_ref, rep16_ref, wrep_ref, w16_ref)


def _gate_kernel(part_ref, g_ref, sum_ref, rep_ref, rep16_ref, wrep_in, w16_in, wrep_ref, w16_ref):
    del wrep_in, w16_in
    _gate_outputs(part_ref[...], sum_ref, g_ref, rep_ref, rep16_ref, wrep_ref, w16_ref)


def _peer_v_kernel(eid_ref, wrep_ref, x1_ref, tab_ref, o_ref):
    mask = _diag_mask(CHUNK_ROWS)

    def one_token(t):
        wr = wrep_ref[pl.ds(t, 1), :]
        acc = x1_ref[t]
        for pr in range(N_CHUNKS // 2):
            parts = []
            for half in range(2):
                seg = wr[:, (2 * pr + half) * CHUNK_ROWS:(2 * pr + half + 1) * CHUNK_ROWS]
                parts.extend(_split_bf16(jnp.where(mask, jnp.broadcast_to(seg, mask.shape), 0.0)))
            lhs = jnp.concatenate(parts, axis=0)
            s = jnp.concatenate([_gather_chunk(tab_ref, eid_ref, t, 2 * pr),
                                 _gather_chunk(tab_ref, eid_ref, t, 2 * pr + 1)], axis=1)
            z = jnp.dot(lhs, s, preferred_element_type=F32)
            acc = acc + ((z[0:8, :LANES] + z[8:16, :LANES]) + (z[16:24, LANES:] + z[24:32, LANES:]))
        return acc

    def tok_body(i, carry):
        accs = [one_token(i * TOKEN_UNROLL + u) for u in range(TOKEN_UNROLL)]
        base = pl.multiple_of(i * TOKEN_UNROLL, TOKEN_UNROLL)
        for r in range(SUBLANES):
            o_ref[pl.ds(base, TOKEN_UNROLL), r * LANES:(r + 1) * LANES] = jnp.concatenate(
                [acc[r:r + 1, :] for acc in accs], axis=0)
        return carry

    lax.fori_loop(0, PB // TOKEN_UNROLL, tok_body, 0)


def _sc_scratch(mid_shape, last_shape):
    half = HK // 2
    per_parity = [pltpu.VMEM((2, half), jnp.int32), pltpu.VMEM(mid_shape, F32), pltpu.VMEM(last_shape, F32)]
    return per_parity + per_parity + [
        pltpu.VMEM((half, ROW_WORDS), jnp.uint32), pltpu.VMEM((half, ROW_WORDS), jnp.uint32),
        pltpu.SemaphoreType.DMA, pltpu.SemaphoreType.DMA,
        pltpu.SemaphoreType.DMA, pltpu.SemaphoreType.DMA,
        pltpu.SemaphoreType.DMA, pltpu.SemaphoreType.DMA,
    ]


def _sc_u_phase(n_tok, tok0):
    per_w = n_tok // SC_WORKERS
    assert per_w * SC_WORKERS == n_tok and per_w % 2 == 0
    half = HK // 2
    mesh = plsc.VectorSubcoreMesh(core_axis_name="c", subcore_axis_name="s")

    @functools.partial(
        pl.kernel, mesh=mesh,
        out_type=jax.ShapeDtypeStruct((n_tok, HK * SC_LANES), F32),
        scratch_types=_sc_scratch((SUBLANES, LANES), (HK * SC_LANES,)),
        compiler_params=pltpu.CompilerParams(needs_layout_passes=False),
    )
    def sc_kernel(tab_hbm, eid_hbm, xn_hbm, out_hbm,
                  idx_a, x_a, a_a, idx_b, x_b, a_b, g0, g1, s_a, s_b, sg0, sg1, so_a, so_b):
        wid = lax.axis_index("s") * SC_CORES + lax.axis_index("c")
        first = wid * per_w
        par = ((idx_a, x_a, a_a, s_a, so_a), (idx_b, x_b, a_b, s_b, so_b))
        gbuf, gsem = (g0, g1), (sg0, sg1)

        def input_copies(t, p):
            idx_v, x_v, _, sem, _ = par[p]
            return (pltpu.make_async_copy(eid_hbm.at[tok0 + t], idx_v, sem),
                    pltpu.make_async_copy(xn_hbm.at[tok0 + t], x_v, sem))

        def gather(p, h):
            return pltpu.make_async_copy(tab_hbm.at[par[p][0].at[h]], gbuf[h], gsem[h])

        def out_copy(t, p):
            return pltpu.make_async_copy(par[p][2], out_hbm.at[t], par[p][4])

        def dot_half(p, h):
            _, x_v, a_v, _, _ = par[p]
            g = gbuf[h]
            groups = LANES // SC_LANES
            block = SUBLANES
            for rb in range(half // block):
                def col_body(k, accs):
                    c, kk = k // groups, k % groups
                    lane0 = pl.multiple_of(kk * SC_LANES, SC_LANES)
                    xl = x_v[2 * c, pl.ds(lane0, SC_LANES)]
                    xh = x_v[2 * c + 1, pl.ds(lane0, SC_LANES)]
                    out = []
                    for r in range(block):
                        words = g[rb * block + r, pl.ds(pl.multiple_of(k * SC_LANES, SC_LANES), SC_LANES)]
                        lov = plsc.bitcast(lax.shift_left(words, jnp.uint32(16)), F32)
                        hiv = plsc.bitcast(words & jnp.uint32(0xFFFF0000), F32)
                        out.append(accs[r] + lov * xl + hiv * xh)
                    return tuple(out)

                zero = jnp.zeros((SC_LANES,), F32)
                accs = lax.fori_loop(0, ROW_WORDS // SC_LANES, col_body, (zero,) * block)
                for r in range(block):
                    a_v[pl.ds((h * half + rb * block + r) * SC_LANES, SC_LANES)] = accs[r]

        def token(t, p):
            q = 1 - p
            has_next = t + 1 < first + per_w
            gather(p, 0).wait()
            gather(p, 1).start()

            @pl.when(has_next)
            def _():
                for cp in input_copies(t + 1, q):
                    cp.start()

            @pl.when(t >= first + 2)
            def _():
                out_copy(t - 2, p).wait()

            dot_half(p, 0)
            gather(p, 1).wait()

            @pl.when(has_next)
            def _():
                for cp in input_copies(t + 1, q):
                    cp.wait()
                gather(q, 0).start()

            dot_half(p, 1)
            out_copy(t, p).start()

        for cp in input_copies(first, 0):
            cp.start()
        for cp in input_copies(first, 0):
            cp.wait()
        gather(0, 0).start()

        def pair_body(i, carry):
            token(first + 2 * i, 0)
            token(first + 2 * i + 1, 1)
            return carry

        lax.fori_loop(0, per_w // 2, pair_body, 0)
        out_copy(first + per_w - 2, 0).wait()
        out_copy(first + per_w - 1, 1).wait()

    return sc_kernel


def _sc_v_phase(n_tok, tok0):
    per_w = n_tok // SC_WORKERS
    assert per_w * SC_WORKERS == n_tok and per_w % 2 == 0
    half = HK // 2
    mesh = plsc.VectorSubcoreMesh(core_axis_name="c", subcore_axis_name="s")

    @functools.partial(
        pl.kernel, mesh=mesh,
        out_type=jax.ShapeDtypeStruct((n_tok, SUBLANES, LANES), F32),
        scratch_types=_sc_scratch((HK * SC_LANES,), (SUBLANES, LANES)),
        compiler_params=pltpu.CompilerParams(needs_layout_passes=False),
    )
    def sc_kernel(tab_hbm, eid_hbm, w16_hbm, x1_hbm, out_hbm,
                  idx_a, w_a, x_a, idx_b, w_b, x_b, g0, g1, s_a, s_b, sg0, sg1, so_a, so_b):
        wid = lax.axis_index("s") * SC_CORES + lax.axis_index("c")
        first = wid * per_w
        par = ((idx_a, w_a, x_a, s_a, so_a), (idx_b, w_b, x_b, s_b, so_b))
        gbuf, gsem = (g0, g1), (sg0, sg1)

        def input_copies(t, p):
            idx_v, w_v, x_v, sem, _ = par[p]
            return (pltpu.make_async_copy(eid_hbm.at[tok0 + t], idx_v, sem),
                    pltpu.make_async_copy(w16_hbm.at[tok0 + t], w_v, sem),
                    pltpu.make_async_copy(x1_hbm.at[tok0 + t], x_v, sem))

        def gather(p, h):
            return pltpu.make_async_copy(tab_hbm.at[par[p][0].at[h]], gbuf[h], gsem[h])

        def out_copy(t, p):
            return pltpu.make_async_copy(par[p][2], out_hbm.at[t], par[p][4])

        def accumulate_half(p, h):
            _, w_v, x_v, _, _ = par[p]
            g = gbuf[h]
            groups = LANES // SC_LANES
            for c in range(ROW_SUB):
                lo0 = tuple(x_v[2 * c, pl.ds(SC_LANES * kk, SC_LANES)] for kk in range(groups))
                hi0 = tuple(x_v[2 * c + 1, pl.ds(SC_LANES * kk, SC_LANES)] for kk in range(groups))

                def row_body(j, acc):
                    lo, hi = acc
                    wj = w_v[pl.ds(pl.multiple_of((h * half + j) * SC_LANES, SC_LANES), SC_LANES)]
                    nlo, nhi = [], []
                    for kk in range(groups):
                        words = g[j, pl.ds(LANES * c + SC_LANES * kk, SC_LANES)]
                        lov = plsc.bitcast(lax.shift_left(words, jnp.uint32(16)), F32)
                        hiv = plsc.bitcast(words & jnp.uint32(0xFFFF0000), F32)
                        nlo.append(lo[kk] + wj * lov)
                        nhi.append(hi[kk] + wj * hiv)
                    return tuple(nlo), tuple(nhi)

                lo, hi = lax.fori_loop(0, half, row_body, (lo0, hi0))
                for kk in range(groups):
                    x_v[2 * c, pl.ds(SC_LANES * kk, SC_LANES)] = lo[kk]
                    x_v[2 * c + 1, pl.ds(SC_LANES * kk, SC_LANES)] = hi[kk]

        def token(t, p):
            q = 1 - p
            has_next = t + 1 < first + per_w
            gather(p, 0).wait()
            gather(p, 1).start()

            @pl.when(has_next)
            def _():
                @pl.when(t >= first + 1)
                def _():
                    out_copy(t - 1, q).wait()
                for cp in input_copies(t + 1, q):
                    cp.start()

            accumulate_half(p, 0)
            gather(p, 1).wait()

            @pl.when(has_next)
            def _():
                for cp in input_copies(t + 1, q):
                    cp.wait()
                gather(q, 0).start()

            accumulate_half(p, 1)
            out_copy(t, p).start()

        for cp in input_copies(first, 0):
            cp.start()
        for cp in input_copies(first, 0):
            cp.wait()
        gather(0, 0).start()

        def pair_body(i, carry):
            token(first + 2 * i, 0)
            token(first + 2 * i + 1, 1)
            return carry

        lax.fori_loop(0, per_w // 2, pair_body, 0)
        out_copy(first + per_w - 2, 0).wait()
        out_copy(first + per_w - 1, 1).wait()

    return sc_kernel


def _peer_experts(eid, g, xn3, x13, u_pack, v_pack, n_sc_u, n_sc_v):
    n = xn3.shape[0]
    n_tc_u, n_tc_v = n - n_sc_u, n - n_sc_v
    assert n_tc_u % PB == 0 and n_tc_v % PB == 0
    eid = eid.reshape(n // PB, PB, HK)
    smem_slot = pl.BlockSpec((1, PB, HK), lambda i: (i, 0, 0), memory_space=pltpu.SMEM)
    tok = pl.BlockSpec((PB, SUBLANES, LANES), lambda i: (i, 0, 0))
    flat = lambda w, off=0: pl.BlockSpec((PB, w), lambda i: (i + off, 0))
    table = pl.BlockSpec(u_pack.shape, lambda i: (0, 0), pipeline_mode=pl.Buffered(1))

    def repeat_matrix(times):
        m = (np.arange(HK)[:, None] == (np.arange(HK * times)[None, :] // times)).astype(np.float32)
        return jnp.asarray(m, BF16)

    rep, rep16 = repeat_matrix(SUBLANES), repeat_matrix(SC_LANES)
    gate_shapes = [jax.ShapeDtypeStruct((n, D_MODEL), F32), jax.ShapeDtypeStruct((n, HK * SC_LANES), F32)]
    expert = eid.reshape(n, 2, HK // 2) >> (ROW_SUB.bit_length() - 1)
    if n_sc_u:
        u_rows = u_pack.reshape(N_EXPERTS, ROW_WORDS)
        partial = _sc_u_phase(n_sc_u, n_tc_u)(u_rows, expert, xn3)
    if n_tc_u:
        wrep, w16 = pl.pallas_call(
            _peer_u_kernel,
            grid=(n_tc_u // PB,),
            in_specs=[smem_slot, tok, flat(HK), table, _const_spec(rep.T.shape), _const_spec(rep.shape),
                      _const_spec(rep16.shape)],
            out_specs=[flat(D_MODEL), flat(HK * SC_LANES)],
            out_shape=gate_shapes,
            scratch_shapes=[pltpu.VMEM((PB, D_MODEL), F32)],
            compiler_params=_cparams(("arbitrary",)),
        )(eid, xn3, g, u_pack, rep.T, rep, rep16)
    if n_sc_u:
        off = n_tc_u // PB
        gate_in = [flat(HK * SC_LANES), flat(HK, off), _const_spec(rep16.T.shape), _const_spec(rep.shape),
                   _const_spec(rep16.shape)]
        gate_args = (partial, g, rep16.T, rep, rep16)
        if n_tc_u:
            anyspec = pl.BlockSpec(memory_space=pl.ANY)
            gate_fn, gate_in, gate_args = _gate_kernel, gate_in + [anyspec, anyspec], gate_args + (wrep, w16)
            aliases = {5: 0, 6: 1}
        else:
            gate_fn, aliases = _gate_outputs_kernel, {}
        wrep, w16 = pl.pallas_call(
            gate_fn,
            grid=(n_sc_u // PB,),
            in_specs=gate_in,
            out_specs=[flat(D_MODEL, off), flat(HK * SC_LANES, off)],
            out_shape=gate_shapes,
            input_output_aliases=aliases,
            compiler_params=_cparams(("arbitrary",)),
        )(*gate_args)
    parts = []
    if n_tc_v:
        parts.append(pl.pallas_call(
            _peer_v_kernel,
            grid=(n_tc_v // PB,),
            in_specs=[smem_slot, flat(D_MODEL), tok, table],
            out_specs=flat(D_MODEL),
            out_shape=jax.ShapeDtypeStruct((n_tc_v, D_MODEL), F32),
            compiler_params=_cparams(("arbitrary",)),
        )(eid, wrep, x13, v_pack))
    if n_sc_v:
        v_rows = v_pack.reshape(N_EXPERTS, ROW_WORDS)
        parts.append(_sc_v_phase(n_sc_v, n_tc_v)(v_rows, expert, w16, x13).reshape(n_sc_v, D_MODEL))
    y = parts[0] if len(parts) == 1 else jnp.concatenate(parts, axis=0)
    return y, w16


def _encoder_layer(x, p, after):
    if after is not None:
        x, _ = lax.optimization_barrier((x, after))
    batch, seq, _ = x.shape
    n = batch * seq
    x2d = x.reshape(n, D_MODEL)
    naq, nak, nav, dq, dk, dv = _in_projection(x2d, seq, p["norm1_w"], p["w_in"], p["qk_w"], p["rope"])
    na = _neighbourhood_attention(naq, nak, nav, p["na_bias"], batch, seq)
    df = _diff_attention(dq, dk, dv, p["lam"], p["subln_w"], batch, seq)
    x1, xn, eid, g = _out_and_route(na, df, x2d, p["w_out"], p["norm2_w"], p["w_q"], p["sub_keys"])
    y, w16 = _peer_experts(eid, g, xn, x1, p["u_pack"], p["v_pack"], *SC_TOKENS.get(n, (0, 0)))
    return y.reshape(batch, seq, D_MODEL), w16


def kernel(x_prompt, x_sample, norm1_w, w_in, na_q_norm, na_k_norm, na_rpb, diff_q_norm, diff_k_norm,
           diff_lambda_q1, diff_lambda_k1, diff_lambda_q2, diff_lambda_k2, diff_subln_w, w_out, norm2_w,
           peer_w_q, peer_sub_keys, peer_u, peer_v):
    assert norm1_w.shape[0] == 1, "single-layer problem"
    reps = SECTION // HEAD_DIM
    params = {
        "norm1_w": norm1_w[0],
        "w_in": w_in[0].astype(BF16),
        "qk_w": jnp.stack([jnp.tile(w[0], reps) for w in (na_q_norm, na_k_norm, diff_q_norm, diff_k_norm)]),
        "lam": jnp.stack([diff_lambda_q1[0], diff_lambda_k1[0], diff_lambda_q2[0], diff_lambda_k2[0]]),
        "subln_w": diff_subln_w[0],
        "w_out": w_out[0].astype(BF16),
        "norm2_w": norm2_w[0],
        "w_q": peer_w_q[0].astype(BF16),
        "sub_keys": peer_sub_keys[0].reshape(2 * PEER_HEADS, N_KEYS, N_KEYS).astype(BF16),
        "na_bias": _na_bias(na_rpb[0]),
        "u_pack": _pack_table(peer_u[0]),
        "v_pack": _pack_table(peer_v[0]),
    }
    rope_by_seq = {}
    outs = [None, None]
    marker = None
    for slot, x in sorted(enumerate((x_prompt, x_sample)), key=lambda item: -item[1].shape[0] * item[1].shape[1]):
        seq = x.shape[1]
        if seq not in rope_by_seq:
            rope_by_seq[seq] = _rope_tables(seq)
        outs[slot], marker = _encoder_layer(x, dict(params, rope=rope_by_seq[seq]), marker)
    return tuple(outs)
```

```python
import functools
import math

import jax
import jax.numpy as jnp
import numpy as np
from jax import lax
from jax.experimental import pallas as pl
from jax.experimental.pallas import tpu as pltpu
from jax.experimental.pallas import tpu_sc as plsc

F32 = jnp.float32
BF16 = jnp.bfloat16

D_MODEL = 1024
GRID_W = 64
HEAD_DIM = 64
NA_HEADS = 8
NA_KH = 8
NA_KW = 16
DIFF_HEADS = 4
SECTION = 512
ROT_DIM = HEAD_DIM // 4
ROPE_THETA = 500000.0
PEER_HEADS = 8
N_KEYS = 128
PEER_TOPK = 16
N_EXPERTS = N_KEYS * N_KEYS
HK = PEER_HEADS * PEER_TOPK
EPS = 1e-6
NEG_BIG = -1e30
LOG2E = math.log2(math.e)
LAMBDA_INIT = 0.8 - 0.6 * math.exp(-0.3 * 0)

LANES = 128
SUBLANES = 8
VMEM_LIMIT = 56 * 1024 * 1024

TM_IN = 512
NA_ROWS_PER_STEP = 8
NA_ROW_UNROLL = 2
TQ = 512
KV_SUBCHUNKS = 1
TK = 4096
TM_OUT = 512
TB = 128
PB = 256
SC_TOKENS = {8 * 4096: (17920, 8 * 4096)}


def _cparams(sem):
    return pltpu.CompilerParams(dimension_semantics=sem, vmem_limit_bytes=VMEM_LIMIT)


def _const_spec(shape):
    nd = len(shape)
    return pl.BlockSpec(shape, lambda *_: (0,) * nd)


def _inproj_kernel(x_ref, n1_ref, w_ref, bd_ref, nw_ref, rc_ref, ra_ref, rb_ref,
                   naq_ref, nak_ref, nav_ref, dq_ref, dk_ref, dv_ref):
    x = x_ref[...]
    ms = jnp.mean(x * x, axis=-1, keepdims=True)
    h = (x * lax.rsqrt(ms + EPS) * n1_ref[...]).astype(BF16)
    bd = bd_ref[...]

    def proj(c):
        return jnp.dot(h, w_ref[:, c * SECTION:(c + 1) * SECTION], preferred_element_type=F32)

    def qknorm(y, row):
        sq = y * y
        hi = sq.astype(BF16)
        lo = (sq - hi.astype(F32)).astype(BF16)
        msq = (jnp.dot(hi, bd, preferred_element_type=F32)
               + jnp.dot(lo, bd, preferred_element_type=F32))
        return y * lax.rsqrt(msq + EPS) * nw_ref[row:row + 1, :]

    def rope(y):
        return (y * rc_ref[...]
                + pltpu.roll(y, SECTION - ROT_DIM // 2, axis=1) * ra_ref[...]
                + pltpu.roll(y, ROT_DIM // 2, axis=1) * rb_ref[...])

    scale = HEAD_DIM ** -0.5 * LOG2E
    naq_ref[...] = (qknorm(proj(0), 0) * scale).astype(BF16)
    nak_ref[...] = qknorm(proj(1), 1).astype(BF16)
    nav_ref[...] = proj(2).astype(BF16)
    dq_ref[...] = (rope(qknorm(proj(3), 2)) * scale).astype(BF16)
    dk_ref[...] = rope(qknorm(proj(4), 3)).astype(BF16)
    dv_ref[...] = proj(5).astype(BF16)


def _rope_tables(seq):
    pos = jnp.arange(seq, dtype=F32)
    inv = ROPE_THETA ** (-jnp.arange(0, ROT_DIM, 2, dtype=F32) / ROT_DIM)
    ang = pos[:, None] * inv[None, :]
    cos, sin = jnp.cos(ang), jnp.sin(ang)
    half = ROT_DIM // 2
    pad = HEAD_DIM - ROT_DIM
    c_head = jnp.concatenate([cos, cos, jnp.ones((seq, pad), F32)], -1)
    a_head = jnp.concatenate([-sin, jnp.zeros((seq, pad + half), F32)], -1)
    b_head = jnp.concatenate([jnp.zeros((seq, half), F32), sin, jnp.zeros((seq, pad), F32)], -1)
    reps = SECTION // HEAD_DIM
    return (jnp.tile(c_head, (1, reps)), jnp.tile(a_head, (1, reps)), jnp.tile(b_head, (1, reps)))


def _in_projection(x2d, seq, norm1_w, w_in_bf, qk_w, rope_tabs):
    n = x2d.shape[0]
    tm = min(TM_IN, seq)
    steps_per_seq = seq // tm
    gid = np.arange(SECTION) // HEAD_DIM
    bd = jnp.asarray((gid[:, None] == gid[None, :]).astype(np.float32) / HEAD_DIM, BF16)
    tok = pl.BlockSpec((tm, SECTION), lambda i: (i, 0))
    pos = pl.BlockSpec((tm, SECTION), lambda i: (i % steps_per_seq, 0))
    out = jax.ShapeDtypeStruct((n, SECTION), BF16)
    return pl.pallas_call(
        _inproj_kernel,
        grid=(n // tm,),
        in_specs=[pl.BlockSpec((tm, D_MODEL), lambda i: (i, 0)),
                  _const_spec((1, D_MODEL)),
                  _const_spec((D_MODEL, 6 * SECTION)),
                  _const_spec((SECTION, SECTION)),
                  _const_spec((4, SECTION)),
                  pos, pos, pos],
        out_specs=[tok] * 6,
        out_shape=[out] * 6,
        compiler_params=_cparams(("parallel",)),
    )(x2d, norm1_w.reshape(1, D_MODEL), w_in_bf, bd, qk_w, *rope_tabs)


def _na_bias(rpb):
    c = np.arange(GRID_W)[:, None]
    kc = np.arange(GRID_W)[None, :]
    cs = np.clip(c - NA_KW // 2, 0, GRID_W - NA_KW)
    valid = (kc >= cs) & (kc < cs + NA_KW)
    rel_c = kc - c + (NA_KW - 1)
    onehot = ((np.arange(2 * NA_KW - 1)[:, None, None] == rel_c[None]) & valid[None]).astype(np.float32)
    col = jnp.einsum("hrx,xck->hrck", rpb.astype(F32), onehot, precision=lax.Precision.HIGHEST)
    col = jnp.where(valid, col * LOG2E, NEG_BIG)
    per_off = [col[:, NA_KH - 1 - o:2 * NA_KH - 1 - o] for o in range(NA_KH)]
    b = jnp.stack(per_off).transpose(0, 1, 3, 2, 4)
    return b.reshape(NA_KH, NA_HEADS, GRID_W, NA_KH * GRID_W)


def _na_kernel(q_ref, k_ref, v_ref, b_ref, o_ref, *, rows):
    j = pl.program_id(1)
    win = NA_KH * GRID_W
    pair = 2 * HEAD_DIM
    lane = lax.broadcasted_iota(jnp.int32, (GRID_W, pair), 1)
    first = lane < HEAD_DIM
    ones = jnp.ones((win, pair), BF16)

    def one_row(rr):
        r = j * NA_ROWS_PER_STEP + rr
        rs = jnp.clip(r - NA_KH // 2, 0, rows - NA_KH)
        off = r - rs
        kstart = pl.multiple_of(rs * GRID_W, GRID_W)
        qstart = pl.multiple_of(rr * GRID_W, GRID_W)
        outs = []
        for hp in range(NA_HEADS // 2):
            sl = slice(hp * pair, (hp + 1) * pair)
            qp = q_ref[pl.ds(qstart, GRID_W), sl]
            kp = k_ref[pl.ds(kstart, win), sl]
            v_ext = jnp.concatenate([v_ref[pl.ds(kstart, win), sl], ones], axis=1)
            halves = []
            for hh in range(2):
                qm = jnp.where(first if hh == 0 else ~first, qp, jnp.zeros_like(qp))
                s = lax.dot_general(qm, kp, (((1,), (1,)), ((), ())), preferred_element_type=F32)
                s = s + b_ref[off, 2 * hp + hh]
                p = jnp.exp2(s - jnp.max(s, axis=-1, keepdims=True)).astype(BF16)
                z = jnp.dot(p, v_ext, preferred_element_type=F32)
                halves.append(z[:, :pair] / z[:, pair:pair + 1])
            outs.append(jnp.where(first, halves[0], halves[1]))
        o_ref[pl.ds(qstart, GRID_W), :] = jnp.concatenate(outs, axis=-1).astype(BF16)

    def row_body(i, carry):
        for u in range(NA_ROW_UNROLL):
            one_row(i * NA_ROW_UNROLL + u)
        return carry

    lax.fori_loop(0, NA_ROWS_PER_STEP // NA_ROW_UNROLL, row_body, 0)


def _neighbourhood_attention(q, k, v, bias, batch, seq):
    rows = seq // GRID_W
    assert rows >= NA_KH and rows % NA_ROWS_PER_STEP == 0
    nblk = rows // NA_ROWS_PER_STEP
    tq = NA_ROWS_PER_STEP * GRID_W
    qspec = pl.BlockSpec((tq, SECTION), lambda b, j: (b * nblk + j, 0))
    kvspec = pl.BlockSpec((seq, SECTION), lambda b, j: (b, 0))
    return pl.pallas_call(
        functools.partial(_na_kernel, rows=rows),
        grid=(batch, nblk),
        in_specs=[qspec, kvspec, kvspec, _const_spec(bias.shape)],
        out_specs=qspec,
        out_shape=jax.ShapeDtypeStruct(q.shape, BF16),
        compiler_params=_cparams(("parallel", "arbitrary")),
    )(q, k, v, bias)


def _diff_kernel(q_ref, k_ref, v_ref, lam_ref, sub_ref, o_ref, m0_scr, m1_scr, acc0_scr, acc1_scr,
                 *, seq, tq, tk):
    m_scrs, acc_scrs = (m0_scr, m1_scr), (acc0_scr, acc1_scr)
    q = q_ref[...]
    lane = lax.broadcasted_iota(jnp.int32, q.shape, 1)
    zero = jnp.zeros_like(q)
    q_maps = (jnp.where(lane < HEAD_DIM, q, zero), jnp.where(lane >= HEAD_DIM, q, zero))

    lp = lam_ref[...]
    lam = (jnp.exp(jnp.sum(lp[0:1] * lp[1:2], axis=-1, keepdims=True))
           - jnp.exp(jnp.sum(lp[2:3] * lp[3:4], axis=-1, keepdims=True)) + LAMBDA_INIT)

    width = 2 * HEAD_DIM
    ones = jnp.ones((tk, width), BF16)

    for m_scr, acc_scr in zip(m_scrs, acc_scrs):
        m_scr[...] = jnp.full(m_scr.shape, -jnp.inf, F32)
        acc_scr[...] = jnp.zeros(acc_scr.shape, F32)

    def kv_body(c, carry):
        tks = tk // KV_SUBCHUNKS
        kcs, v_exts = [], []
        for sub in range(KV_SUBCHUNKS):
            start = pl.multiple_of(c * tk + sub * tks, tks)
            kcs.append(k_ref[pl.ds(start, tks), :])
            v_exts.append(jnp.concatenate([v_ref[pl.ds(start, tks), :], ones[:tks]], axis=1))
        scores = [[lax.dot_general(q_maps[mp], kcs[sub], (((1,), (1,)), ((), ())),
                                   preferred_element_type=F32) for mp in range(2)]
                  for sub in range(KV_SUBCHUNKS)]
        m = [m_scrs[mp][...] for mp in range(2)]
        probs, alphas = [], []
        for sub in range(KV_SUBCHUNKS):
            probs.append([])
            alphas.append([])
            for mp in range(2):
                mn = jnp.maximum(m[mp], jnp.max(scores[sub][mp], axis=-1, keepdims=True))
                probs[sub].append(jnp.exp2(scores[sub][mp] - mn[:, 0:1]).astype(BF16))
                alpha = jnp.exp2(m[mp] - mn)
                alphas[sub].append(jnp.concatenate([alpha, alpha], axis=1))
                m[mp] = mn
        for mp in range(2):
            m_scrs[mp][...] = m[mp]
            acc = acc_scrs[mp][...]
            for sub in range(KV_SUBCHUNKS):
                acc = alphas[sub][mp] * acc + jnp.dot(probs[sub][mp], v_exts[sub],
                                                      preferred_element_type=F32)
            acc_scrs[mp][...] = acc
        return carry

    lax.fori_loop(0, seq // tk, kv_body, 0)
    a0, a1 = acc0_scr[...], acc1_scr[...]
    o = a0[:, :width] / a0[:, width:width + 1] - lam * (a1[:, :width] / a1[:, width:width + 1])
    ms = jnp.mean(o * o, axis=-1, keepdims=True)
    y = o * lax.rsqrt(ms + EPS) * sub_ref[...]
    o_ref[...] = (y * (1.0 - LAMBDA_INIT)).astype(BF16)


def _diff_attention(q, k, v, lam_params, subln_w, batch, seq):
    tq, tk = min(TQ, seq), min(TK, seq)
    nq = seq // tq
    width = 2 * HEAD_DIM
    qspec = pl.BlockSpec((tq, width), lambda b, h, i: (b * nq + i, h))
    kvspec = pl.BlockSpec((seq, width), lambda b, h, i: (b, h))
    return pl.pallas_call(
        functools.partial(_diff_kernel, seq=seq, tq=tq, tk=tk),
        grid=(batch, DIFF_HEADS, nq),
        in_specs=[qspec, kvspec, kvspec, _const_spec((4, HEAD_DIM)), _const_spec((1, width))],
        out_specs=qspec,
        out_shape=jax.ShapeDtypeStruct(q.shape, BF16),
        scratch_shapes=[pltpu.VMEM((tq, width), F32), pltpu.VMEM((tq, width), F32),
                        pltpu.VMEM((tq, 2 * width), F32), pltpu.VMEM((tq, 2 * width), F32)],
        compiler_params=_cparams(("parallel", "parallel", "arbitrary")),
    )(q, k, v, lam_params, subln_w.reshape(1, width))


def _top16(sc, payload=None):
    iota = lax.broadcasted_iota(jnp.int32, sc.shape, 0).astype(F32)
    big = float(sc.shape[0])
    vals, picks = [], []
    for _ in range(PEER_TOPK):
        m = jnp.max(sc, axis=0, keepdims=True)
        ix = jnp.min(jnp.where(sc == m, iota, big), axis=0, keepdims=True)
        sel = iota == ix
        if payload is None:
            picks.append(ix)
        else:
            picks.append(jnp.sum(jnp.where(sel, payload, 0), axis=0, keepdims=True))
        sc = jnp.where(sel, -jnp.inf, sc)
        vals.append(m)
    picks = jnp.concatenate(picks, axis=0)
    return jnp.concatenate(vals, axis=0), picks.astype(jnp.int32)


def _pair_candidates(r0, r1):
    half = SUBLANES // 2
    groups = [r0[0:1] + r1[0:8], r0[0:1] + r1[8:16]]
    groups += [r0[a:a + 1] + r1[0:8] for a in (1, 2, 3)]
    groups += [jnp.concatenate([r0[a:a + 1] + r1[0:half], r0[a + 1:a + 2] + r1[0:half]], axis=0)
               for a in (4, 6)]
    groups.append(r0[8:16] + r1[0:1])
    return jnp.concatenate(groups, axis=0)


def _route_kernel(na_ref, df_ref, x_ref, wo_ref, n2_ref, wq_ref, sk_ref,
                  x1_ref, xn_ref, eid_ref, g_ref, q_scr, eid_scr, g_scr):
    tm = x_ref.shape[0]
    x1 = (x_ref[...]
          + jnp.dot(na_ref[...], wo_ref[0:SECTION, :], preferred_element_type=F32)
          + jnp.dot(df_ref[...], wo_ref[SECTION:2 * SECTION, :], preferred_element_type=F32))
    ms = jnp.mean(x1 * x1, axis=-1, keepdims=True)
    xn = x1 * lax.rsqrt(ms + EPS) * n2_ref[...]
    for r in range(SUBLANES):
        x1_ref[:, r, :] = x1[:, r * LANES:(r + 1) * LANES]
        xn_ref[:, r, :] = xn[:, r * LANES:(r + 1) * LANES]
    xb = xn.astype(BF16)
    for c in range(2 * PEER_HEADS):
        q_scr[c] = jnp.dot(xb, wq_ref[:, c * LANES:(c + 1) * LANES],
                           preferred_element_type=F32).astype(BF16)

    nblk = tm // TB

    def route_block(h, blk):
        tok0 = blk * TB
        tops = []
        for p in range(2):
            sc = lax.dot_general(sk_ref[2 * h + p], q_scr[2 * h + p, pl.ds(tok0, TB), :],
                                 (((1,), (1,)), ((), ())), preferred_element_type=F32)
            tops.append(_top16(sc))
        (s0, i0), (s1, i1) = tops
        cand_s = _pair_candidates(s0, s1)
        cand_i = _pair_candidates(i0 * N_KEYS, i1)
        top_s, eid = _top16(cand_s, cand_i)
        e = jnp.exp(top_s - top_s[0:1])
        row = pl.multiple_of(h * PEER_TOPK, PEER_TOPK)
        eid_scr[blk, pl.ds(row, PEER_TOPK), :] = eid * ROW_SUB
        g_scr[blk, pl.ds(row, PEER_TOPK), :] = e / jnp.sum(e, axis=0, keepdims=True)

    def head_body(h, carry):
        for blk in range(nblk):
            route_block(h, blk)
        return carry

    lax.fori_loop(0, PEER_HEADS, head_body, 0)
    for blk in range(nblk):
        eid_ref[blk] = eid_scr[blk].T
        g_ref[blk * TB:(blk + 1) * TB, :] = g_scr[blk].T


def _out_and_route(na, df, x2d, w_out_bf, norm2_w, w_q_bf, sk_bf):
    n = x2d.shape[0]
    tm = TM_OUT
    tok = lambda w: pl.BlockSpec((tm, w), lambda i: (i, 0))
    tiles = pl.BlockSpec((tm, SUBLANES, LANES), lambda i: (i, 0, 0))
    return pl.pallas_call(
        _route_kernel,
        grid=(n // tm,),
        in_specs=[tok(SECTION), tok(SECTION), tok(D_MODEL),
                  _const_spec((D_MODEL, D_MODEL)), _const_spec((1, D_MODEL)),
                  _const_spec(w_q_bf.shape), _const_spec(sk_bf.shape)],
        out_specs=[tiles, tiles,
                   pl.BlockSpec((tm // TB, TB, HK), lambda i: (i, 0, 0)), tok(HK)],
        out_shape=[jax.ShapeDtypeStruct((n, SUBLANES, LANES), F32),
                   jax.ShapeDtypeStruct((n, SUBLANES, LANES), F32),
                   jax.ShapeDtypeStruct((n // TB, TB, HK), jnp.int32),
                   jax.ShapeDtypeStruct((n, HK), F32)],
        scratch_shapes=[pltpu.VMEM((2 * PEER_HEADS, tm, LANES), BF16),
                        pltpu.VMEM((tm // TB, HK, TB), jnp.int32), pltpu.VMEM((tm // TB, HK, TB), F32)],
        compiler_params=_cparams(("parallel",)),
    )(na, df, x2d, w_out_bf, norm2_w.reshape(1, D_MODEL), w_q_bf, sk_bf)


ROW_SUB = SUBLANES // 2
CHUNK_SLOTS = 32
N_CHUNKS = HK // CHUNK_SLOTS
CHUNK_ROWS = CHUNK_SLOTS * SUBLANES
TOKEN_UNROLL = 8
ROW_WORDS = ROW_SUB * LANES
SC_CORES, SC_SUBCORES, SC_LANES = 2, 16, 16
SC_WORKERS = SC_CORES * SC_SUBCORES


PACK_EXPERTS = 512


def _pack_kernel(x_ref, o_ref):
    x = x_ref[...]
    for c in range(ROW_SUB):
        lo = x[:, (2 * c) * LANES:(2 * c + 1) * LANES].astype(BF16).astype(F32)
        hi = x[:, (2 * c + 1) * LANES:(2 * c + 2) * LANES].astype(BF16).astype(F32)
        words = (pltpu.bitcast(lo, jnp.uint32) >> 16) | (pltpu.bitcast(hi, jnp.uint32) & jnp.uint32(0xFFFF0000))
        o_ref[pl.ds(c, PACK_EXPERTS, stride=ROW_SUB), :] = words


def _pack_table(tab):
    experts = tab.shape[0]
    return pl.pallas_call(
        _pack_kernel,
        grid=(experts // PACK_EXPERTS,),
        in_specs=[pl.BlockSpec((PACK_EXPERTS, D_MODEL), lambda i: (i, 0))],
        out_specs=pl.BlockSpec((PACK_EXPERTS * ROW_SUB, LANES), lambda i: (i, 0)),
        out_shape=jax.ShapeDtypeStruct((experts * ROW_SUB, LANES), jnp.uint32),
        compiler_params=_cparams(("parallel",)),
    )(tab)


def _split_bf16(x):
    hi = x.astype(BF16)
    return hi, (x - hi.astype(F32)).astype(BF16)


def _gather_chunk(tab_ref, eid_ref, t, chunk):
    ids = eid_ref.at[0, t]
    rows = [tab_ref[pl.ds(pl.multiple_of(ids[chunk * CHUNK_SLOTS + i], ROW_SUB), ROW_SUB), :]
            for i in range(CHUNK_SLOTS)]
    return pltpu.bitcast(jnp.concatenate(rows, axis=0), BF16)


def _diag_mask(width):
    r = lax.broadcasted_iota(jnp.int32, (SUBLANES, width), 0)
    n = lax.broadcasted_iota(jnp.int32, (SUBLANES, width), 1)
    return (n % SUBLANES) == r


def _peer_u_kernel(eid_ref, xn_ref, g_ref, tab_ref, rept_ref, rep_ref, rep16_ref, wrep_ref, w16_ref, d_scr):
    mask = _diag_mask(CHUNK_ROWS)
    zeros = jnp.zeros((SUBLANES, LANES), BF16)

    def one_token(t):
        xhi, xlo = _split_bf16(xn_ref[t])
        lhs = jnp.concatenate([jnp.concatenate([xhi, zeros], axis=1),
                               jnp.concatenate([xlo, zeros], axis=1),
                               jnp.concatenate([zeros, xhi], axis=1),
                               jnp.concatenate([zeros, xlo], axis=1)], axis=0)
        for pr in range(N_CHUNKS // 2):
            s = jnp.concatenate([_gather_chunk(tab_ref, eid_ref, t, 2 * pr),
                                 _gather_chunk(tab_ref, eid_ref, t, 2 * pr + 1)], axis=1)
            z = lax.dot_general(lhs, s, (((1,), (1,)), ((), ())), preferred_element_type=F32)
            for half in range(2):
                z8 = z[16 * half:16 * half + 8] + z[16 * half + 8:16 * half + 16]
                zs = jnp.sum(jnp.where(mask, z8, 0.0), axis=0, keepdims=True)
                d_scr[pl.ds(t, 1), pl.ds((2 * pr + half) * CHUNK_ROWS, CHUNK_ROWS)] = zs

    def tok_body(i, carry):
        for u in range(TOKEN_UNROLL):
            one_token(i * TOKEN_UNROLL + u)
        return carry

    lax.fori_loop(0, PB // TOKEN_UNROLL, tok_body, 0)
    _gate_outputs(d_scr[...], rept_ref, g_ref, rep_ref, rep16_ref, wrep_ref, w16_ref)


def _gate_outputs(partial, sum_ref, g_ref, rep_ref, rep16_ref, wrep_ref, w16_ref):
    dhi, dlo = _split_bf16(partial)
    a = (jnp.dot(dhi, sum_ref[...], preferred_element_type=F32)
         + jnp.dot(dlo, sum_ref[...], preferred_element_type=F32))
    w = g_ref[...] * (0.5 * a * (1.0 + lax.erf(a * (2.0 ** -0.5))))
    whi, wlo = _split_bf16(w)
    wrep_ref[...] = (jnp.dot(whi, rep_ref[...], preferred_element_type=F32)
                     + jnp.dot(wlo, rep_ref[...], preferred_element_type=F32))
    w16_ref[...] = (jnp.dot(whi, rep16_ref[...], preferred_element_type=F32)
                    + jnp.dot(wlo, rep16_ref[...], preferred_element_type=F32))


def _gate_outputs_kernel(part_ref, g_ref, sum_ref, rep_ref, rep16_ref, wrep_ref, w16_ref):
    _gate_outputs(part_ref[...], sum_ref, g_ref, rep_ref, rep16_ref, wrep_ref, w16_ref)


def _gate_kernel(part_ref, g_ref, sum_ref, rep_ref, rep16_ref, wrep_in, w16_in, wrep_ref, w16_ref):
    del wrep_in, w16_in
    _gate_outputs(part_ref[...], sum_ref, g_ref, rep_ref, rep16_ref, wrep_ref, w16_ref)


def _peer_v_kernel(eid_ref, wrep_ref, x1_ref, tab_ref, o_ref):
    mask = _diag_mask(CHUNK_ROWS)

    def one_token(t):
        wr = wrep_ref[pl.ds(t, 1), :]
        acc = x1_ref[t]
        for pr in range(N_CHUNKS // 2):
            parts = []
            for half in range(2):
                seg = wr[:, (2 * pr + half) * CHUNK_ROWS:(2 * pr + half + 1) * CHUNK_ROWS]
                parts.extend(_split_bf16(jnp.where(mask, jnp.broadcast_to(seg, mask.shape), 0.0)))
            lhs = jnp.concatenate(parts, axis=0)
            s = jnp.concatenate([_gather_chunk(tab_ref, eid_ref, t, 2 * pr),
                                 _gather_chunk(tab_ref, eid_ref, t, 2 * pr + 1)], axis=1)
            z = jnp.dot(lhs, s, preferred_element_type=F32)
            acc = acc + ((z[0:8, :LANES] + z[8:16, :LANES]) + (z[16:24, LANES:] + z[24:32, LANES:]))
        return acc

    def tok_body(i, carry):
        accs = [one_token(i * TOKEN_UNROLL + u) for u in range(TOKEN_UNROLL)]
        base = pl.multiple_of(i * TOKEN_UNROLL, TOKEN_UNROLL)
        for r in range(SUBLANES):
            o_ref[pl.ds(base, TOKEN_UNROLL), r * LANES:(r + 1) * LANES] = jnp.concatenate(
                [acc[r:r + 1, :] for acc in accs], axis=0)
        return carry

    lax.fori_loop(0, PB // TOKEN_UNROLL, tok_body, 0)


def _sc_scratch(mid_shape, last_shape):
    half = HK // 2
    per_parity = [pltpu.VMEM((2, half), jnp.int32), pltpu.VMEM(mid_shape, F32), pltpu.VMEM(last_shape, F32)]
    return per_parity + per_parity + [
        pltpu.VMEM((half, ROW_WORDS), jnp.uint32), pltpu.VMEM((half, ROW_WORDS), jnp.uint32),
        pltpu.SemaphoreType.DMA, pltpu.SemaphoreType.DMA,
        pltpu.SemaphoreType.DMA, pltpu.SemaphoreType.DMA,
        pltpu.SemaphoreType.DMA, pltpu.SemaphoreType.DMA,
    ]


def _sc_u_phase(n_tok, tok0):
    per_w = n_tok // SC_WORKERS
    assert per_w * SC_WORKERS == n_tok and per_w % 2 == 0
    half = HK // 2
    mesh = plsc.VectorSubcoreMesh(core_axis_name="c", subcore_axis_name="s")

    @functools.partial(
        pl.kernel, mesh=mesh,
        out_type=jax.ShapeDtypeStruct((n_tok, HK * SC_LANES), F32),
        scratch_types=_sc_scratch((SUBLANES, LANES), (HK * SC_LANES,)),
        compiler_params=pltpu.CompilerParams(needs_layout_passes=False),
    )
    def sc_kernel(tab_hbm, eid_hbm, xn_hbm, out_hbm,
                  idx_a, x_a, a_a, idx_b, x_b, a_b, g0, g1, s_a, s_b, sg0, sg1, so_a, so_b):
        wid = lax.axis_index("s") * SC_CORES + lax.axis_index("c")
        first = wid * per_w
        par = ((idx_a, x_a, a_a, s_a, so_a), (idx_b, x_b, a_b, s_b, so_b))
        gbuf, gsem = (g0, g1), (sg0, sg1)

        def input_copies(t, p):
            idx_v, x_v, _, sem, _ = par[p]
            return (pltpu.make_async_copy(eid_hbm.at[tok0 + t], idx_v, sem),
                    pltpu.make_async_copy(xn_hbm.at[tok0 + t], x_v, sem))

        def gather(p, h):
            return pltpu.make_async_copy(tab_hbm.at[par[p][0].at[h]], gbuf[h], gsem[h])

        def out_copy(t, p):
            return pltpu.make_async_copy(par[p][2], out_hbm.at[t], par[p][4])

        def dot_half(p, h):
            _, x_v, a_v, _, _ = par[p]
            g = gbuf[h]
            groups = LANES // SC_LANES
            block = SUBLANES
            for rb in range(half // block):
                def col_body(k, accs):
                    c, kk = k // groups, k % groups
                    lane0 = pl.multiple_of(kk * SC_LANES, SC_LANES)
                    xl = x_v[2 * c, pl.ds(lane0, SC_LANES)]
                    xh = x_v[2 * c + 1, pl.ds(lane0, SC_LANES)]
                    out = []
                    for r in range(block):
                        words = g[rb * block + r, pl.ds(pl.multiple_of(k * SC_LANES, SC_LANES), SC_LANES)]
                        lov = plsc.bitcast(lax.shift_left(words, jnp.uint32(16)), F32)
                        hiv = plsc.bitcast(words & jnp.uint32(0xFFFF0000), F32)
                        out.append(accs[r] + lov * xl + hiv * xh)
                    return tuple(out)

                zero = jnp.zeros((SC_LANES,), F32)
                accs = lax.fori_loop(0, ROW_WORDS // SC_LANES, col_body, (zero,) * block)
                for r in range(block):
                    a_v[pl.ds((h * half + rb * block + r) * SC_LANES, SC_LANES)] = accs[r]

        def token(t, p):
            q = 1 - p
            has_next = t + 1 < first + per_w
            gather(p, 0).wait()
            gather(p, 1).start()

            @pl.when(has_next)
            def _():
                for cp in input_copies(t + 1, q):
                    cp.start()

            @pl.when(t >= first + 2)
            def _():
                out_copy(t - 2, p).wait()

            dot_half(p, 0)
            gather(p, 1).wait()

            @pl.when(has_next)
            def _():
                for cp in input_copies(t + 1, q):
                    cp.wait()
                gather(q, 0).start()

            dot_half(p, 1)
            out_copy(t, p).start()

        for cp in input_copies(first, 0):
            cp.start()
        for cp in input_copies(first, 0):
            cp.wait()
        gather(0, 0).start()

        def pair_body(i, carry):
            token(first + 2 * i, 0)
            token(first + 2 * i + 1, 1)
            return carry

        lax.fori_loop(0, per_w // 2, pair_body, 0)
        out_copy(first + per_w - 2, 0).wait()
        out_copy(first + per_w - 1, 1).wait()

    return sc_kernel


def _sc_v_phase(n_tok, tok0):
    per_w = n_tok // SC_WORKERS
    assert per_w * SC_WORKERS == n_tok and per_w % 2 == 0
    half = HK // 2
    mesh = plsc.VectorSubcoreMesh(core_axis_name="c", subcore_axis_name="s")

    @functools.partial(
        pl.kernel, mesh=mesh,
        out_type=jax.ShapeDtypeStruct((n_tok, SUBLANES, LANES), F32),
        scratch_types=_sc_scratch((HK * SC_LANES,), (SUBLANES, LANES)),
        compiler_params=pltpu.CompilerParams(needs_layout_passes=False),
    )
    def sc_kernel(tab_hbm, eid_hbm, w16_hbm, x1_hbm, out_hbm,
                  idx_a, w_a, x_a, idx_b, w_b, x_b, g0, g1, s_a, s_b, sg0, sg1, so_a, so_b):
        wid = lax.axis_index("s") * SC_CORES + lax.axis_index("c")
        first = wid * per_w
        par = ((idx_a, w_a, x_a, s_a, so_a), (idx_b, w_b, x_b, s_b, so_b))
        gbuf, gsem = (g0, g1), (sg0, sg1)

        def input_copies(t, p):
            idx_v, w_v, x_v, sem, _ = par[p]
            return (pltpu.make_async_copy(eid_hbm.at[tok0 + t], idx_v, sem),
                    pltpu.make_async_copy(w16_hbm.at[tok0 + t], w_v, sem),
                    pltpu.make_async_copy(x1_hbm.at[tok0 + t], x_v, sem))

        def gather(p, h):
            return pltpu.make_async_copy(tab_hbm.at[par[p][0].at[h]], gbuf[h], gsem[h])

        def out_copy(t, p):
            return pltpu.make_async_copy(par[p][2], out_hbm.at[t], par[p][4])

        def accumulate_half(p, h):
            _, w_v, x_v, _, _ = par[p]
            g = gbuf[h]
            groups = LANES // SC_LANES
            for c in range(ROW_SUB):
                lo0 = tuple(x_v[2 * c, pl.ds(SC_LANES * kk, SC_LANES)] for kk in range(groups))
                hi0 = tuple(x_v[2 * c + 1, pl.ds(SC_LANES * kk, SC_LANES)] for kk in range(groups))

                def row_body(j, acc):
                    lo, hi = acc
                    wj = w_v[pl.ds(pl.multiple_of((h * half + j) * SC_LANES, SC_LANES), SC_LANES)]
                    nlo, nhi = [], []
                    for kk in range(groups):
                        words = g[j, pl.ds(LANES * c + SC_LANES * kk, SC_LANES)]
                        lov = plsc.bitcast(lax.shift_left(words, jnp.uint32(16)), F32)
                        hiv = plsc.bitcast(words & jnp.uint32(0xFFFF0000), F32)
                        nlo.append(lo[kk] + wj * lov)
                        nhi.append(hi[kk] + wj * hiv)
                    return tuple(nlo), tuple(nhi)

                lo, hi = lax.fori_loop(0, half, row_body, (lo0, hi0))
                for kk in range(groups):
                    x_v[2 * c, pl.ds(SC_LANES * kk, SC_LANES)] = lo[kk]
                    x_v[2 * c + 1, pl.ds(SC_LANES * kk, SC_LANES)] = hi[kk]

        def token(t, p):
            q = 1 - p
            has_next = t + 1 < first + per_w
            gather(p, 0).wait()
            gather(p, 1).start()

            @pl.when(has_next)
            def _():
                @pl.when(t >= first + 1)
                def _():
                    out_copy(t - 1, q).wait()
                for cp in input_copies(t + 1, q):
                    cp.start()

            accumulate_half(p, 0)
            gather(p, 1).wait()

            @pl.when(has_next)
            def _():
                for cp in input_copies(t + 1, q):
                    cp.wait()
                gather(q, 0).start()

            accumulate_half(p, 1)
            out_copy(t, p).start()

        for cp in input_copies(first, 0):
            cp.start()
        for cp in input_copies(first, 0):
            cp.wait()
        gather(0, 0).start()

        def pair_body(i, carry):
            token(first + 2 * i, 0)
            token(first + 2 * i + 1, 1)
            return carry

        lax.fori_loop(0, per_w // 2, pair_body, 0)
        out_copy(first + per_w - 2, 0).wait()
        out_copy(first + per_w - 1, 1).wait()

    return sc_kernel


def _peer_experts(eid, g, xn3, x13, u_pack, v_pack, n_sc_u, n_sc_v):
    n = xn3.shape[0]
    n_tc_u, n_tc_v = n - n_sc_u, n - n_sc_v
    assert n_tc_u % PB == 0 and n_tc_v % PB == 0
    eid = eid.reshape(n // PB, PB, HK)
    smem_slot = pl.BlockSpec((1, PB, HK), lambda i: (i, 0, 0), memory_space=pltpu.SMEM)
    tok = pl.BlockSpec((PB, SUBLANES, LANES), lambda i: (i, 0, 0))
    flat = lambda w, off=0: pl.BlockSpec((PB, w), lambda i: (i + off, 0))
    table = pl.BlockSpec(u_pack.shape, lambda i: (0, 0), pipeline_mode=pl.Buffered(1))

    def repeat_matrix(times):
        m = (np.arange(HK)[:, None] == (np.arange(HK * times)[None, :] // times)).astype(np.float32)
        return jnp.asarray(m, BF16)

    rep, rep16 = repeat_matrix(SUBLANES), repeat_matrix(SC_LANES)
    gate_shapes = [jax.ShapeDtypeStruct((n, D_MODEL), F32), jax.ShapeDtypeStruct((n, HK * SC_LANES), F32)]
    expert = eid.reshape(n, 2, HK // 2) >> (ROW_SUB.bit_length() - 1)
    if n_sc_u:
        u_rows = u_pack.reshape(N_EXPERTS, ROW_WORDS)
        partial = _sc_u_phase(n_sc_u, n_tc_u)(u_rows, expert, xn3)
    if n_tc_u:
        wrep, w16 = pl.pallas_call(
            _peer_u_kernel,
            grid=(n_tc_u // PB,),
            in_specs=[smem_slot, tok, flat(HK), table, _const_spec(rep.T.shape), _const_spec(rep.shape),
                      _const_spec(rep16.shape)],
            out_specs=[flat(D_MODEL), flat(HK * SC_LANES)],
            out_shape=gate_shapes,
            scratch_shapes=[pltpu.VMEM((PB, D_MODEL), F32)],
            compiler_params=_cparams(("arbitrary",)),
        )(eid, xn3, g, u_pack, rep.T, rep, rep16)
    if n_sc_u:
        off = n_tc_u // PB
        gate_in = [flat(HK * SC_LANES), flat(HK, off), _const_spec(rep16.T.shape), _const_spec(rep.shape),
                   _const_spec(rep16.shape)]
        gate_args = (partial, g, rep16.T, rep, rep16)
        if n_tc_u:
            anyspec = pl.BlockSpec(memory_space=pl.ANY)
            gate_fn, gate_in, gate_args = _gate_kernel, gate_in + [anyspec, anyspec], gate_args + (wrep, w16)
            aliases = {5: 0, 6: 1}
        else:
            gate_fn, aliases = _gate_outputs_kernel, {}
        wrep, w16 = pl.pallas_call(
            gate_fn,
            grid=(n_sc_u // PB,),
            in_specs=gate_in,
            out_specs=[flat(D_MODEL, off), flat(HK * SC_LANES, off)],
            out_shape=gate_shapes,
            input_output_aliases=aliases,
            compiler_params=_cparams(("arbitrary",)),
        )(*gate_args)
    parts = []
    if n_tc_v:
        parts.append(pl.pallas_call(
            _peer_v_kernel,
            grid=(n_tc_v // PB,),
            in_specs=[smem_slot, flat(D_MODEL), tok, table],
            out_specs=flat(D_MODEL),
            out_shape=jax.ShapeDtypeStruct((n_tc_v, D_MODEL), F32),
            compiler_params=_cparams(("arbitrary",)),
        )(eid, wrep, x13, v_pack))
    if n_sc_v:
        v_rows = v_pack.reshape(N_EXPERTS, ROW_WORDS)
        parts.append(_sc_v_phase(n_sc_v, n_tc_v)(v_rows, expert, w16, x13).reshape(n_sc_v, D_MODEL))
    y = parts[0] if len(parts) == 1 else jnp.concatenate(parts, axis=0)
    return y, w16


def _encoder_layer(x, p, after):
    if after is not None:
        x, _ = lax.optimization_barrier((x, after["route"]))
    batch, seq, _ = x.shape
    n = batch * seq
    x2d = x.reshape(n, D_MODEL)
    naq, nak, nav, dq, dk, dv = _in_projection(x2d, seq, p["norm1_w"], p["w_in"], p["qk_w"], p["rope"])
    na = _neighbourhood_attention(naq, nak, nav, p["na_bias"], batch, seq)
    df = _diff_attention(dq, dk, dv, p["lam"], p["subln_w"], batch, seq)
    if after is not None:
        na, _ = lax.optimization_barrier((na, after["gate"]))
    x1, xn, eid, g = _out_and_route(na, df, x2d, p["w_out"], p["norm2_w"], p["w_q"], p["sub_keys"])
    y, w16 = _peer_experts(eid, g, xn, x1, p["u_pack"], p["v_pack"], *SC_TOKENS.get(n, (0, 0)))
    return y.reshape(batch, seq, D_MODEL), {"route": g, "gate": w16}


def kernel(x_prompt, x_sample, norm1_w, w_in, na_q_norm, na_k_norm, na_rpb, diff_q_norm, diff_k_norm,
           diff_lambda_q1, diff_lambda_k1, diff_lambda_q2, diff_lambda_k2, diff_subln_w, w_out, norm2_w,
           peer_w_q, peer_sub_keys, peer_u, peer_v):
    assert norm1_w.shape[0] == 1, "single-layer problem"
    reps = SECTION // HEAD_DIM
    params = {
        "norm1_w": norm1_w[0],
        "w_in": w_in[0].astype(BF16),
        "qk_w": jnp.stack([jnp.tile(w[0], reps) for w in (na_q_norm, na_k_norm, diff_q_norm, diff_k_norm)]),
        "lam": jnp.stack([diff_lambda_q1[0], diff_lambda_k1[0], diff_lambda_q2[0], diff_lambda_k2[0]]),
        "subln_w": diff_subln_w[0],
        "w_out": w_out[0].astype(BF16),
        "norm2_w": norm2_w[0],
        "w_q": peer_w_q[0].astype(BF16),
        "sub_keys": peer_sub_keys[0].reshape(2 * PEER_HEADS, N_KEYS, N_KEYS).astype(BF16),
        "na_bias": _na_bias(na_rpb[0]),
        "u_pack": _pack_table(peer_u[0]),
        "v_pack": _pack_table(peer_v[0]),
    }
    rope_by_seq = {}
    outs = [None, None]
    marker = None
    for slot, x in sorted(enumerate((x_prompt, x_sample)), key=lambda item: -item[1].shape[0] * item[1].shape[1]):
        seq = x.shape[1]
        if seq not in rope_by_seq:
            rope_by_seq[seq] = _rope_tables(seq)
        outs[slot], marker = _encoder_layer(x, dict(params, rope=rope_by_seq[seq]), marker)
    return tuple(outs)
```

```python
import functools
import math

import jax
import jax.numpy as jnp
import numpy as np
from jax import lax
from jax.experimental import pallas as pl
from jax.experimental.pallas import tpu as pltpu
from jax.experimental.pallas import tpu_sc as plsc

F32 = jnp.float32
BF16 = jnp.bfloat16

D_MODEL = 1024
GRID_W = 64
HEAD_DIM = 64
NA_HEADS = 8
NA_KH = 8
NA_KW = 16
DIFF_HEADS = 4
SECTION = 512
ROT_DIM = HEAD_DIM // 4
ROPE_THETA = 500000.0
PEER_HEADS = 8
N_KEYS = 128
PEER_TOPK = 16
N_EXPERTS = N_KEYS * N_KEYS
HK = PEER_HEADS * PEER_TOPK
EPS = 1e-6
NEG_BIG = -1e30
LOG2E = math.log2(math.e)
LAMBDA_INIT = 0.8 - 0.6 * math.exp(-0.3 * 0)

LANES = 128
SUBLANES = 8
VMEM_LIMIT = 56 * 1024 * 1024

TM_IN = 512
NA_ROWS_PER_STEP = 8
NA_ROW_UNROLL = 2
TQ = 512
KV_SUBCHUNKS = 1
TK = 4096
TM_OUT = 1024
TB = 128
PB = 256
SC_TOKENS = {8 * 4096: (14336, 8 * 4096)}


def _cparams(sem):
    return pltpu.CompilerParams(dimension_semantics=sem, vmem_limit_bytes=VMEM_LIMIT)


def _const_spec(shape):
    nd = len(shape)
    return pl.BlockSpec(shape, lambda *_: (0,) * nd)


def _inproj_kernel(x_ref, n1_ref, w_ref, bd_ref, nw_ref, rc_ref, ra_ref, rb_ref,
                   naq_ref, nak_ref, nav_ref, dq_ref, dk_ref, dv_ref):
    x = x_ref[...]
    ms = jnp.mean(x * x, axis=-1, keepdims=True)
    h = (x * lax.rsqrt(ms + EPS) * n1_ref[...]).astype(BF16)
    bd = bd_ref[...]

    def proj(c):
        return jnp.dot(h, w_ref[:, c * SECTION:(c + 1) * SECTION], preferred_element_type=F32)

    def qknorm(y, row):
        sq = y * y
        hi = sq.astype(BF16)
        lo = (sq - hi.astype(F32)).astype(BF16)
        msq = (jnp.dot(hi, bd, preferred_element_type=F32)
               + jnp.dot(lo, bd, preferred_element_type=F32))
        return y * lax.rsqrt(msq + EPS) * nw_ref[row:row + 1, :]

    def rope(y):
        return (y * rc_ref[...]
                + pltpu.roll(y, SECTION - ROT_DIM // 2, axis=1) * ra_ref[...]
                + pltpu.roll(y, ROT_DIM // 2, axis=1) * rb_ref[...])

    scale = HEAD_DIM ** -0.5 * LOG2E
    naq_ref[...] = (qknorm(proj(0), 0) * scale).astype(BF16)
    nak_ref[...] = qknorm(proj(1), 1).astype(BF16)
    nav_ref[...] = proj(2).astype(BF16)
    dq_ref[...] = (rope(qknorm(proj(3), 2)) * scale).astype(BF16)
    dk_ref[...] = rope(qknorm(proj(4), 3)).astype(BF16)
    dv_ref[...] = proj(5).astype(BF16)


def _rope_tables(seq):
    pos = jnp.arange(seq, dtype=F32)
    inv = ROPE_THETA ** (-jnp.arange(0, ROT_DIM, 2, dtype=F32) / ROT_DIM)
    ang = pos[:, None] * inv[None, :]
    cos, sin = jnp.cos(ang), jnp.sin(ang)
    half = ROT_DIM // 2
    pad = HEAD_DIM - ROT_DIM
    c_head = jnp.concatenate([cos, cos, jnp.ones((seq, pad), F32)], -1)
    a_head = jnp.concatenate([-sin, jnp.zeros((seq, pad + half), F32)], -1)
    b_head = jnp.concatenate([jnp.zeros((seq, half), F32), sin, jnp.zeros((seq, pad), F32)], -1)
    reps = SECTION // HEAD_DIM
    return (jnp.tile(c_head, (1, reps)), jnp.tile(a_head, (1, reps)), jnp.tile(b_head, (1, reps)))


def _in_projection(x2d, seq, norm1_w, w_in_bf, qk_w, rope_tabs):
    n = x2d.shape[0]
    tm = min(TM_IN, seq)
    steps_per_seq = seq // tm
    gid = np.arange(SECTION) // HEAD_DIM
    bd = jnp.asarray((gid[:, None] == gid[None, :]).astype(np.float32) / HEAD_DIM, BF16)
    tok = pl.BlockSpec((tm, SECTION), lambda i: (i, 0))
    pos = pl.BlockSpec((tm, SECTION), lambda i: (i % steps_per_seq, 0))
    out = jax.ShapeDtypeStruct((n, SECTION), BF16)
    return pl.pallas_call(
        _inproj_kernel,
        grid=(n // tm,),
        in_specs=[pl.BlockSpec((tm, D_MODEL), lambda i: (i, 0)),
                  _const_spec((1, D_MODEL)),
                  _const_spec((D_MODEL, 6 * SECTION)),
                  _const_spec((SECTION, SECTION)),
                  _const_spec((4, SECTION)),
                  pos, pos, pos],
        out_specs=[tok] * 6,
        out_shape=[out] * 6,
        compiler_params=_cparams(("parallel",)),
    )(x2d, norm1_w.reshape(1, D_MODEL), w_in_bf, bd, qk_w, *rope_tabs)


def _na_bias(rpb):
    c = np.arange(GRID_W)[:, None]
    kc = np.arange(GRID_W)[None, :]
    cs = np.clip(c - NA_KW // 2, 0, GRID_W - NA_KW)
    valid = (kc >= cs) & (kc < cs + NA_KW)
    rel_c = kc - c + (NA_KW - 1)
    onehot = ((np.arange(2 * NA_KW - 1)[:, None, None] == rel_c[None]) & valid[None]).astype(np.float32)
    col = jnp.einsum("hrx,xck->hrck", rpb.astype(F32), onehot, precision=lax.Precision.HIGHEST)
    col = jnp.where(valid, col * LOG2E, NEG_BIG)
    per_off = [col[:, NA_KH - 1 - o:2 * NA_KH - 1 - o] for o in range(NA_KH)]
    b = jnp.stack(per_off).transpose(0, 1, 3, 2, 4)
    return b.reshape(NA_KH, NA_HEADS, GRID_W, NA_KH * GRID_W)


def _na_kernel(q_ref, k_ref, v_ref, b_ref, o_ref, *, rows):
    j = pl.program_id(1)
    win = NA_KH * GRID_W
    pair = 2 * HEAD_DIM
    lane = lax.broadcasted_iota(jnp.int32, (GRID_W, pair), 1)
    first = lane < HEAD_DIM
    ones = jnp.ones((win, pair), BF16)

    def one_row(rr):
        r = j * NA_ROWS_PER_STEP + rr
        rs = jnp.clip(r - NA_KH // 2, 0, rows - NA_KH)
        off = r - rs
        kstart = pl.multiple_of(rs * GRID_W, GRID_W)
        qstart = pl.multiple_of(rr * GRID_W, GRID_W)
        outs = []
        for hp in range(NA_HEADS // 2):
            sl = slice(hp * pair, (hp + 1) * pair)
            qp = q_ref[pl.ds(qstart, GRID_W), sl]
            kp = k_ref[pl.ds(kstart, win), sl]
            v_ext = jnp.concatenate([v_ref[pl.ds(kstart, win), sl], ones], axis=1)
            halves = []
            for hh in range(2):
                qm = jnp.where(first if hh == 0 else ~first, qp, jnp.zeros_like(qp))
                s = lax.dot_general(qm, kp, (((1,), (1,)), ((), ())), preferred_element_type=F32)
                s = s + b_ref[off, 2 * hp + hh]
                p = jnp.exp2(s - jnp.max(s, axis=-1, keepdims=True)).astype(BF16)
                z = jnp.dot(p, v_ext, preferred_element_type=F32)
                halves.append(z[:, :pair] / z[:, pair:pair + 1])
            outs.append(jnp.where(first, halves[0], halves[1]))
        o_ref[pl.ds(qstart, GRID_W), :] = jnp.concatenate(outs, axis=-1).astype(BF16)

    def row_body(i, carry):
        for u in range(NA_ROW_UNROLL):
            one_row(i * NA_ROW_UNROLL + u)
        return carry

    lax.fori_loop(0, NA_ROWS_PER_STEP // NA_ROW_UNROLL, row_body, 0)


def _neighbourhood_attention(q, k, v, bias, batch, seq):
    rows = seq // GRID_W
    assert rows >= NA_KH and rows % NA_ROWS_PER_STEP == 0
    nblk = rows // NA_ROWS_PER_STEP
    tq = NA_ROWS_PER_STEP * GRID_W
    qspec = pl.BlockSpec((tq, SECTION), lambda b, j: (b * nblk + j, 0))
    kvspec = pl.BlockSpec((seq, SECTION), lambda b, j: (b, 0))
    return pl.pallas_call(
        functools.partial(_na_kernel, rows=rows),
        grid=(batch, nblk),
        in_specs=[qspec, kvspec, kvspec, _const_spec(bias.shape)],
        out_specs=qspec,
        out_shape=jax.ShapeDtypeStruct(q.shape, BF16),
        compiler_params=_cparams(("parallel", "arbitrary")),
    )(q, k, v, bias)


def _diff_kernel(q_ref, k_ref, v_ref, lam_ref, sub_ref, o_ref, m0_scr, m1_scr, acc0_scr, acc1_scr,
                 *, seq, tq, tk):
    m_scrs, acc_scrs = (m0_scr, m1_scr), (acc0_scr, acc1_scr)
    q = q_ref[...]
    lane = lax.broadcasted_iota(jnp.int32, q.shape, 1)
    zero = jnp.zeros_like(q)
    q_maps = (jnp.where(lane < HEAD_DIM, q, zero), jnp.where(lane >= HEAD_DIM, q, zero))

    lp = lam_ref[...]
    lam = (jnp.exp(jnp.sum(lp[0:1] * lp[1:2], axis=-1, keepdims=True))
           - jnp.exp(jnp.sum(lp[2:3] * lp[3:4], axis=-1, keepdims=True)) + LAMBDA_INIT)

    width = 2 * HEAD_DIM
    ones = jnp.ones((tk, width), BF16)

    for m_scr, acc_scr in zip(m_scrs, acc_scrs):
        m_scr[...] = jnp.full(m_scr.shape, -jnp.inf, F32)
        acc_scr[...] = jnp.zeros(acc_scr.shape, F32)

    def kv_body(c, carry):
        tks = tk // KV_SUBCHUNKS
        kcs, v_exts = [], []
        for sub in range(KV_SUBCHUNKS):
            start = pl.multiple_of(c * tk + sub * tks, tks)
            kcs.append(k_ref[pl.ds(start, tks), :])
            v_exts.append(jnp.concatenate([v_ref[pl.ds(start, tks), :], ones[:tks]], axis=1))
        scores = [[lax.dot_general(q_maps[mp], kcs[sub], (((1,), (1,)), ((), ())),
                                   preferred_element_type=F32) for mp in range(2)]
                  for sub in range(KV_SUBCHUNKS)]
        m = [m_scrs[mp][...] for mp in range(2)]
        probs, alphas = [], []
        for sub in range(KV_SUBCHUNKS):
            probs.append([])
            alphas.append([])
            for mp in range(2):
                mn = jnp.maximum(m[mp], jnp.max(scores[sub][mp], axis=-1, keepdims=True))
                probs[sub].append(jnp.exp2(scores[sub][mp] - mn[:, 0:1]).astype(BF16))
                alpha = jnp.exp2(m[mp] - mn)
                alphas[sub].append(jnp.concatenate([alpha, alpha], axis=1))
                m[mp] = mn
        for mp in range(2):
            m_scrs[mp][...] = m[mp]
            acc = acc_scrs[mp][...]
            for sub in range(KV_SUBCHUNKS):
                acc = alphas[sub][mp] * acc + jnp.dot(probs[sub][mp], v_exts[sub],
                                                      preferred_element_type=F32)
            acc_scrs[mp][...] = acc
        return carry

    lax.fori_loop(0, seq // tk, kv_body, 0)
    a0, a1 = acc0_scr[...], acc1_scr[...]
    o = a0[:, :width] / a0[:, width:width + 1] - lam * (a1[:, :width] / a1[:, width:width + 1])
    ms = jnp.mean(o * o, axis=-1, keepdims=True)
    y = o * lax.rsqrt(ms + EPS) * sub_ref[...]
    o_ref[...] = (y * (1.0 - LAMBDA_INIT)).astype(BF16)


def _diff_attention(q, k, v, lam_params, subln_w, batch, seq):
    tq, tk = min(TQ, seq), min(TK, seq)
    nq = seq // tq
    width = 2 * HEAD_DIM
    qspec = pl.BlockSpec((tq, width), lambda b, h, i: (b * nq + i, h))
    kvspec = pl.BlockSpec((seq, width), lambda b, h, i: (b, h))
    return pl.pallas_call(
        functools.partial(_diff_kernel, seq=seq, tq=tq, tk=tk),
        grid=(batch, DIFF_HEADS, nq),
        in_specs=[qspec, kvspec, kvspec, _const_spec((4, HEAD_DIM)), _const_spec((1, width))],
        out_specs=qspec,
        out_shape=jax.ShapeDtypeStruct(q.shape, BF16),
        scratch_shapes=[pltpu.VMEM((tq, width), F32), pltpu.VMEM((tq, width), F32),
                        pltpu.VMEM((tq, 2 * width), F32), pltpu.VMEM((tq, 2 * width), F32)],
        compiler_params=_cparams(("parallel", "parallel", "arbitrary")),
    )(q, k, v, lam_params, subln_w.reshape(1, width))


def _top16(sc, payload=None):
    iota = lax.broadcasted_iota(jnp.int32, sc.shape, 0).astype(F32)
    big = float(sc.shape[0])
    vals, picks = [], []
    for _ in range(PEER_TOPK):
        m = jnp.max(sc, axis=0, keepdims=True)
        ix = jnp.min(jnp.where(sc == m, iota, big), axis=0, keepdims=True)
        sel = iota == ix
        if payload is None:
            picks.append(ix)
        else:
            picks.append(jnp.sum(jnp.where(sel, payload, 0), axis=0, keepdims=True))
        sc = jnp.where(sel, -jnp.inf, sc)
        vals.append(m)
    picks = jnp.concatenate(picks, axis=0)
    return jnp.concatenate(vals, axis=0), picks.astype(jnp.int32)


def _pair_candidates(r0, r1):
    half = SUBLANES // 2
    groups = [r0[0:1] + r1[0:8], r0[0:1] + r1[8:16]]
    groups += [r0[a:a + 1] + r1[0:8] for a in (1, 2, 3)]
    groups += [jnp.concatenate([r0[a:a + 1] + r1[0:half], r0[a + 1:a + 2] + r1[0:half]], axis=0)
               for a in (4, 6)]
    groups.append(r0[8:16] + r1[0:1])
    return jnp.concatenate(groups, axis=0)


def _route_kernel(na_ref, df_ref, x_ref, wo_ref, n2_ref, wq_ref, sk_ref,
                  x1_ref, xn_ref, eid_ref, g_ref, q_scr, eid_scr, g_scr):
    tm = x_ref.shape[0]
    x1 = (x_ref[...]
          + jnp.dot(na_ref[...], wo_ref[0:SECTION, :], preferred_element_type=F32)
          + jnp.dot(df_ref[...], wo_ref[SECTION:2 * SECTION, :], preferred_element_type=F32))
    ms = jnp.mean(x1 * x1, axis=-1, keepdims=True)
    xn = x1 * lax.rsqrt(ms + EPS) * n2_ref[...]
    for r in range(SUBLANES):
        x1_ref[:, r, :] = x1[:, r * LANES:(r + 1) * LANES]
        xn_ref[:, r, :] = xn[:, r * LANES:(r + 1) * LANES]
    xb = xn.astype(BF16)
    for c in range(2 * PEER_HEADS):
        q_scr[c] = jnp.dot(xb, wq_ref[:, c * LANES:(c + 1) * LANES],
                           preferred_element_type=F32).astype(BF16)

    nblk = tm // TB

    def route_block(h, blk):
        tok0 = blk * TB
        tops = []
        for p in range(2):
            sc = lax.dot_general(sk_ref[2 * h + p], q_scr[2 * h + p, pl.ds(tok0, TB), :],
                                 (((1,), (1,)), ((), ())), preferred_element_type=F32)
            tops.append(_top16(sc))
        (s0, i0), (s1, i1) = tops
        cand_s = _pair_candidates(s0, s1)
        cand_i = _pair_candidates(i0 * N_KEYS, i1)
        top_s, eid = _top16(cand_s, cand_i)
        e = jnp.exp(top_s - top_s[0:1])
        row = pl.multiple_of(h * PEER_TOPK, PEER_TOPK)
        eid_scr[blk, pl.ds(row, PEER_TOPK), :] = eid * ROW_SUB
        g_scr[blk, pl.ds(row, PEER_TOPK), :] = e / jnp.sum(e, axis=0, keepdims=True)

    def head_body(h, carry):
        for blk in range(nblk):
            route_block(h, blk)
        return carry

    lax.fori_loop(0, PEER_HEADS, head_body, 0)
    for blk in range(nblk):
        eid_ref[blk] = eid_scr[blk].T
        g_ref[blk * TB:(blk + 1) * TB, :] = g_scr[blk].T


def _out_and_route(na, df, x2d, w_out_bf, norm2_w, w_q_bf, sk_bf):
    n = x2d.shape[0]
    tm = TM_OUT
    tok = lambda w: pl.BlockSpec((tm, w), lambda i: (i, 0))
    tiles = pl.BlockSpec((tm, SUBLANES, LANES), lambda i: (i, 0, 0))
    return pl.pallas_call(
        _route_kernel,
        grid=(n // tm,),
        in_specs=[tok(SECTION), tok(SECTION), tok(D_MODEL),
                  _const_spec((D_MODEL, D_MODEL)), _const_spec((1, D_MODEL)),
                  _const_spec(w_q_bf.shape), _const_spec(sk_bf.shape)],
        out_specs=[tiles, tiles,
                   pl.BlockSpec((tm // TB, TB, HK), lambda i: (i, 0, 0)), tok(HK)],
        out_shape=[jax.ShapeDtypeStruct((n, SUBLANES, LANES), F32),
                   jax.ShapeDtypeStruct((n, SUBLANES, LANES), F32),
                   jax.ShapeDtypeStruct((n // TB, TB, HK), jnp.int32),
                   jax.ShapeDtypeStruct((n, HK), F32)],
        scratch_shapes=[pltpu.VMEM((2 * PEER_HEADS, tm, LANES), BF16),
                        pltpu.VMEM((tm // TB, HK, TB), jnp.int32), pltpu.VMEM((tm // TB, HK, TB), F32)],
        compiler_params=_cparams(("parallel",)),
    )(na, df, x2d, w_out_bf, norm2_w.reshape(1, D_MODEL), w_q_bf, sk_bf)


ROW_SUB = SUBLANES // 2
CHUNK_SLOTS = 32
N_CHUNKS = HK // CHUNK_SLOTS
CHUNK_ROWS = CHUNK_SLOTS * SUBLANES
TOKEN_UNROLL = 8
ROW_WORDS = ROW_SUB * LANES
SC_CORES, SC_SUBCORES, SC_LANES = 2, 16, 16
SC_WORKERS = SC_CORES * SC_SUBCORES


PACK_EXPERTS = 512


def _pack_kernel(x_ref, o_ref):
    x = x_ref[...]
    for c in range(ROW_SUB):
        lo = x[:, (2 * c) * LANES:(2 * c + 1) * LANES].astype(BF16).astype(F32)
        hi = x[:, (2 * c + 1) * LANES:(2 * c + 2) * LANES].astype(BF16).astype(F32)
        words = (pltpu.bitcast(lo, jnp.uint32) >> 16) | (pltpu.bitcast(hi, jnp.uint32) & jnp.uint32(0xFFFF0000))
        o_ref[pl.ds(c, PACK_EXPERTS, stride=ROW_SUB), :] = words


def _pack_table(tab):
    experts = tab.shape[0]
    return pl.pallas_call(
        _pack_kernel,
        grid=(experts // PACK_EXPERTS,),
        in_specs=[pl.BlockSpec((PACK_EXPERTS, D_MODEL), lambda i: (i, 0))],
        out_specs=pl.BlockSpec((PACK_EXPERTS * ROW_SUB, LANES), lambda i: (i, 0)),
        out_shape=jax.ShapeDtypeStruct((experts * ROW_SUB, LANES), jnp.uint32),
        compiler_params=_cparams(("parallel",)),
    )(tab)


def _split_bf16(x):
    hi = x.astype(BF16)
    return hi, (x - hi.astype(F32)).astype(BF16)


def _gather_chunk(tab_ref, eid_ref, t, chunk):
    ids = eid_ref.at[0, t]
    rows = [tab_ref[pl.ds(pl.multiple_of(ids[chunk * CHUNK_SLOTS + i], ROW_SUB), ROW_SUB), :]
            for i in range(CHUNK_SLOTS)]
    return pltpu.bitcast(jnp.concatenate(rows, axis=0), BF16)


def _diag_mask(width):
    r = lax.broadcasted_iota(jnp.int32, (SUBLANES, width), 0)
    n = lax.broadcasted_iota(jnp.int32, (SUBLANES, width), 1)
    return (n % SUBLANES) == r


def _peer_u_kernel(eid_ref, xn_ref, g_ref, tab_ref, rept_ref, rep_ref, rep16_ref, wrep_ref, w16_ref, d_scr):
    mask = _diag_mask(CHUNK_ROWS)
    zeros = jnp.zeros((SUBLANES, LANES), BF16)

    def one_token(t):
        xhi, xlo = _split_bf16(xn_ref[t])
        lhs = jnp.concatenate([jnp.concatenate([xhi, zeros], axis=1),
                               jnp.concatenate([xlo, zeros], axis=1),
                               jnp.concatenate([zeros, xhi], axis=1),
                               jnp.concatenate([zeros, xlo], axis=1)], axis=0)
        for pr in range(N_CHUNKS // 2):
            s = jnp.concatenate([_gather_chunk(tab_ref, eid_ref, t, 2 * pr),
                                 _gather_chunk(tab_ref, eid_ref, t, 2 * pr + 1)], axis=1)
            z = lax.dot_general(lhs, s, (((1,), (1,)), ((), ())), preferred_element_type=F32)
            for half in range(2):
                z8 = z[16 * half:16 * half + 8] + z[16 * half + 8:16 * half + 16]
                zs = jnp.sum(jnp.where(mask, z8, 0.0), axis=0, keepdims=True)
                d_scr[pl.ds(t, 1), pl.ds((2 * pr + half) * CHUNK_ROWS, CHUNK_ROWS)] = zs

    def tok_body(i, carry):
        for u in range(TOKEN_UNROLL):
            one_token(i * TOKEN_UNROLL + u)
        return carry

    lax.fori_loop(0, PB // TOKEN_UNROLL, tok_body, 0)
    _gate_outputs(d_scr[...], rept_ref, g_ref, rep_ref, rep16_ref, wrep_ref, w16_ref)


def _gate_outputs(partial, sum_ref, g_ref, rep_ref, rep16_ref, wrep_ref, w16_ref):
    dhi, dlo = _split_bf16(partial)
    a = (jnp.dot(dhi, sum_ref[...], preferred_element_type=F32)
         + jnp.dot(dlo, sum_ref[...], preferred_element_type=F32))
    w = g_ref[...] * (0.5 * a * (1.0 + lax.erf(a * (2.0 ** -0.5))))
    whi, wlo = _split_bf16(w)
    wrep_ref[...] = (jnp.dot(whi, rep_ref[...], preferred_element_type=F32)
                     + jnp.dot(wlo, rep_ref[...], preferred_element_type=F32))
    w16_ref[...] = (jnp.dot(whi, rep16_ref[...], preferred_element_type=F32)
                    + jnp.dot(wlo, rep16_ref[...], preferred_element_type=F32))


def _gate_outputs_kernel(part_ref, g_ref, sum_ref, rep_ref, rep16_ref, wrep_ref, w16_ref):
    _gate_outputs(part_ref[...], sum_ref, g_ref, rep_ref, rep16_ref, wrep_ref, w16_ref)


def _gate_kernel(part_ref, g_ref, sum_ref, rep_ref, rep16_ref, wrep_in, w16_in, wrep_ref, w16_ref):
    del wrep_in, w16_in
    _gate_outputs(part_ref[...], sum_ref, g_ref, rep_ref, rep16_ref, wrep_ref, w16_ref)


def _peer_v_kernel(eid_ref, wrep_ref, x1_ref, tab_ref, o_ref):
    mask = _diag_mask(CHUNK_ROWS)

    def one_token(t):
        wr = wrep_ref[pl.ds(t, 1), :]
        acc = x1_ref[t]
        for pr in range(N_CHUNKS // 2):
            parts = []
            for half in range(2):
                seg = wr[:, (2 * pr + half) * CHUNK_ROWS:(2 * pr + half + 1) * CHUNK_ROWS]
                parts.extend(_split_bf16(jnp.where(mask, jnp.broadcast_to(seg, mask.shape), 0.0)))
            lhs = jnp.concatenate(parts, axis=0)
            s = jnp.concatenate([_gather_chunk(tab_ref, eid_ref, t, 2 * pr),
                                 _gather_chunk(tab_ref, eid_ref, t, 2 * pr + 1)], axis=1)
            z = jnp.dot(lhs, s, preferred_element_type=F32)
            acc = acc + ((z[0:8, :LANES] + z[8:16, :LANES]) + (z[16:24, LANES:] + z[24:32, LANES:]))
        return acc

    def tok_body(i, carry):
        accs = [one_token(i * TOKEN_UNROLL + u) for u in range(TOKEN_UNROLL)]
        base = pl.multiple_of(i * TOKEN_UNROLL, TOKEN_UNROLL)
        for r in range(SUBLANES):
            o_ref[pl.ds(base, TOKEN_UNROLL), r * LANES:(r + 1) * LANES] = jnp.concatenate(
                [acc[r:r + 1, :] for acc in accs], axis=0)
        return carry

    lax.fori_loop(0, PB // TOKEN_UNROLL, tok_body, 0)


def _sc_scratch(mid_shape, last_shape):
    half = HK // 2
    per_parity = [pltpu.VMEM((2, half), jnp.int32), pltpu.VMEM(mid_shape, F32), pltpu.VMEM(last_shape, F32)]
    return per_parity + per_parity + [
        pltpu.VMEM((half, ROW_WORDS), jnp.uint32), pltpu.VMEM((half, ROW_WORDS), jnp.uint32),
        pltpu.SemaphoreType.DMA, pltpu.SemaphoreType.DMA,
        pltpu.SemaphoreType.DMA, pltpu.SemaphoreType.DMA,
        pltpu.SemaphoreType.DMA, pltpu.SemaphoreType.DMA,
    ]


def _sc_u_phase(n_tok, tok0):
    per_w = n_tok // SC_WORKERS
    assert per_w * SC_WORKERS == n_tok and per_w % 2 == 0
    half = HK // 2
    mesh = plsc.VectorSubcoreMesh(core_axis_name="c", subcore_axis_name="s")

    @functools.partial(
        pl.kernel, mesh=mesh,
        out_type=jax.ShapeDtypeStruct((n_tok, HK * SC_LANES), F32),
        scratch_types=_sc_scratch((SUBLANES, LANES), (HK * SC_LANES,)),
        compiler_params=pltpu.CompilerParams(needs_layout_passes=False),
    )
    def sc_kernel(tab_hbm, eid_hbm, xn_hbm, out_hbm,
                  idx_a, x_a, a_a, idx_b, x_b, a_b, g0, g1, s_a, s_b, sg0, sg1, so_a, so_b):
        wid = lax.axis_index("s") * SC_CORES + lax.axis_index("c")
        first = wid * per_w
        par = ((idx_a, x_a, a_a, s_a, so_a), (idx_b, x_b, a_b, s_b, so_b))
        gbuf, gsem = (g0, g1), (sg0, sg1)

        def input_copies(t, p):
            idx_v, x_v, _, sem, _ = par[p]
            return (pltpu.make_async_copy(eid_hbm.at[tok0 + t], idx_v, sem),
                    pltpu.make_async_copy(xn_hbm.at[tok0 + t], x_v, sem))

        def gather(p, h):
            return pltpu.make_async_copy(tab_hbm.at[par[p][0].at[h]], gbuf[h], gsem[h])

        def out_copy(t, p):
            return pltpu.make_async_copy(par[p][2], out_hbm.at[t], par[p][4])

        def dot_half(p, h):
            _, x_v, a_v, _, _ = par[p]
            g = gbuf[h]
            groups = LANES // SC_LANES
            block = SUBLANES
            for rb in range(half // block):
                def col_body(k, accs):
                    c, kk = k // groups, k % groups
                    lane0 = pl.multiple_of(kk * SC_LANES, SC_LANES)
                    xl = x_v[2 * c, pl.ds(lane0, SC_LANES)]
                    xh = x_v[2 * c + 1, pl.ds(lane0, SC_LANES)]
                    out = []
                    for r in range(block):
                        words = g[rb * block + r, pl.ds(pl.multiple_of(k * SC_LANES, SC_LANES), SC_LANES)]
                        lov = plsc.bitcast(lax.shift_left(words, jnp.uint32(16)), F32)
                        hiv = plsc.bitcast(words & jnp.uint32(0xFFFF0000), F32)
                        out.append(accs[r] + lov * xl + hiv * xh)
                    return tuple(out)

                zero = jnp.zeros((SC_LANES,), F32)
                accs = lax.fori_loop(0, ROW_WORDS // SC_LANES, col_body, (zero,) * block)
                for r in range(block):
                    a_v[pl.ds((h * half + rb * block + r) * SC_LANES, SC_LANES)] = accs[r]

        def token(t, p):
            q = 1 - p
            has_next = t + 1 < first + per_w
            gather(p, 0).wait()
            gather(p, 1).start()

            @pl.when(has_next)
            def _():
                for cp in input_copies(t + 1, q):
                    cp.start()

            @pl.when(t >= first + 2)
            def _():
                out_copy(t - 2, p).wait()

            dot_half(p, 0)
            gather(p, 1).wait()

            @pl.when(has_next)
            def _():
                for cp in input_copies(t + 1, q):
                    cp.wait()
                gather(q, 0).start()

            dot_half(p, 1)
            out_copy(t, p).start()

        for cp in input_copies(first, 0):
            cp.start()
        for cp in input_copies(first, 0):
            cp.wait()
        gather(0, 0).start()

        def pair_body(i, carry):
            token(first + 2 * i, 0)
            token(first + 2 * i + 1, 1)
            return carry

        lax.fori_loop(0, per_w // 2, pair_body, 0)
        out_copy(first + per_w - 2, 0).wait()
        out_copy(first + per_w - 1, 1).wait()

    return sc_kernel


def _sc_v_phase(n_tok, tok0):
    per_w = n_tok // SC_WORKERS
    assert per_w * SC_WORKERS == n_tok and per_w % 2 == 0
    half = HK // 2
    mesh = plsc.VectorSubcoreMesh(core_axis_name="c", subcore_axis_name="s")

    @functools.partial(
        pl.kernel, mesh=mesh,
        out_type=jax.ShapeDtypeStruct((n_tok, SUBLANES, LANES), F32),
        scratch_types=_sc_scratch((HK * SC_LANES,), (SUBLANES, LANES)),
        compiler_params=pltpu.CompilerParams(needs_layout_passes=False),
    )
    def sc_kernel(tab_hbm, eid_hbm, w16_hbm, x1_hbm, out_hbm,
                  idx_a, w_a, x_a, idx_b, w_b, x_b, g0, g1, s_a, s_b, sg0, sg1, so_a, so_b):
        wid = lax.axis_index("s") * SC_CORES + lax.axis_index("c")
        first = wid * per_w
        par = ((idx_a, w_a, x_a, s_a, so_a), (idx_b, w_b, x_b, s_b, so_b))
        gbuf, gsem = (g0, g1), (sg0, sg1)

        def input_copies(t, p):
            idx_v, w_v, x_v, sem, _ = par[p]
            return (pltpu.make_async_copy(eid_hbm.at[tok0 + t], idx_v, sem),
                    pltpu.make_async_copy(w16_hbm.at[tok0 + t], w_v, sem),
                    pltpu.make_async_copy(x1_hbm.at[tok0 + t], x_v, sem))

        def gather(p, h):
            return pltpu.make_async_copy(tab_hbm.at[par[p][0].at[h]], gbuf[h], gsem[h])

        def out_copy(t, p):
            return pltpu.make_async_copy(par[p][2], out_hbm.at[t], par[p][4])

        def accumulate_half(p, h):
            _, w_v, x_v, _, _ = par[p]
            g = gbuf[h]
            groups = LANES // SC_LANES
            for c in range(ROW_SUB):
                lo0 = tuple(x_v[2 * c, pl.ds(SC_LANES * kk, SC_LANES)] for kk in range(groups))
                hi0 = tuple(x_v[2 * c + 1, pl.ds(SC_LANES * kk, SC_LANES)] for kk in range(groups))

                def row_body(j, acc):
                    lo, hi = acc
                    wj = w_v[pl.ds(pl.multiple_of((h * half + j) * SC_LANES, SC_LANES), SC_LANES)]
                    nlo, nhi = [], []
                    for kk in range(groups):
                        words = g[j, pl.ds(LANES * c + SC_LANES * kk, SC_LANES)]
                        lov = plsc.bitcast(lax.shift_left(words, jnp.uint32(16)), F32)
                        hiv = plsc.bitcast(words & jnp.uint32(0xFFFF0000), F32)
                        nlo.append(lo[kk] + wj * lov)
                        nhi.append(hi[kk] + wj * hiv)
                    return tuple(nlo), tuple(nhi)

                lo, hi = lax.fori_loop(0, half, row_body, (lo0, hi0))
                for kk in range(groups):
                    x_v[2 * c, pl.ds(SC_LANES * kk, SC_LANES)] = lo[kk]
                    x_v[2 * c + 1, pl.ds(SC_LANES * kk, SC_LANES)] = hi[kk]

        def token(t, p):
            q = 1 - p
            has_next = t + 1 < first + per_w
            gather(p, 0).wait()
            gather(p, 1).start()

            @pl.when(has_next)
            def _():
                @pl.when(t >= first + 1)
                def _():
                    out_copy(t - 1, q).wait()
                for cp in input_copies(t + 1, q):
                    cp.start()

            accumulate_half(p, 0)
            gather(p, 1).wait()

            @pl.when(has_next)
            def _():
                for cp in input_copies(t + 1, q):
                    cp.wait()
                gather(q, 0).start()

            accumulate_half(p, 1)
            out_copy(t, p).start()

        for cp in input_copies(first, 0):
            cp.start()
        for cp in input_copies(first, 0):
            cp.wait()
        gather(0, 0).start()

        def pair_body(i, carry):
            token(first + 2 * i, 0)
            token(first + 2 * i + 1, 1)
            return carry

        lax.fori_loop(0, per_w // 2, pair_body, 0)
        out_copy(first + per_w - 2, 0).wait()
        out_copy(first + per_w - 1, 1).wait()

    return sc_kernel


def _peer_experts(eid, g, xn3, x13, u_pack, v_pack, n_sc_u, n_sc_v):
    n = xn3.shape[0]
    n_tc_u, n_tc_v = n - n_sc_u, n - n_sc_v
    assert n_tc_u % PB == 0 and n_tc_v % PB == 0
    eid = eid.reshape(n // PB, PB, HK)
    smem_slot = pl.BlockSpec((1, PB, HK), lambda i: (i, 0, 0), memory_space=pltpu.SMEM)
    tok = pl.BlockSpec((PB, SUBLANES, LANES), lambda i: (i, 0, 0))
    flat = lambda w, off=0: pl.BlockSpec((PB, w), lambda i: (i + off, 0))
    table = pl.BlockSpec(u_pack.shape, lambda i: (0, 0), pipeline_mode=pl.Buffered(1))

    def repeat_matrix(times):
        m = (np.arange(HK)[:, None] == (np.arange(HK * times)[None, :] // times)).astype(np.float32)
        return jnp.asarray(m, BF16)

    rep, rep16 = repeat_matrix(SUBLANES), repeat_matrix(SC_LANES)
    gate_shapes = [jax.ShapeDtypeStruct((n, D_MODEL), F32), jax.ShapeDtypeStruct((n, HK * SC_LANES), F32)]
    expert = eid.reshape(n, 2, HK // 2) >> (ROW_SUB.bit_length() - 1)
    if n_sc_u:
        u_rows = u_pack.reshape(N_EXPERTS, ROW_WORDS)
        partial = _sc_u_phase(n_sc_u, n_tc_u)(u_rows, expert, xn3)
    if n_tc_u:
        wrep, w16 = pl.pallas_call(
            _peer_u_kernel,
            grid=(n_tc_u // PB,),
            in_specs=[smem_slot, tok, flat(HK), table, _const_spec(rep.T.shape), _const_spec(rep.shape),
                      _const_spec(rep16.shape)],
            out_specs=[flat(D_MODEL), flat(HK * SC_LANES)],
            out_shape=gate_shapes,
            scratch_shapes=[pltpu.VMEM((PB, D_MODEL), F32)],
            compiler_params=_cparams(("arbitrary",)),
        )(eid, xn3, g, u_pack, rep.T, rep, rep16)
    if n_sc_u:
        off = n_tc_u // PB
        gate_in = [flat(HK * SC_LANES), flat(HK, off), _const_spec(rep16.T.shape), _const_spec(rep.shape),
                   _const_spec(rep16.shape)]
        gate_args = (partial, g, rep16.T, rep, rep16)
        if n_tc_u:
            anyspec = pl.BlockSpec(memory_space=pl.ANY)
            gate_fn, gate_in, gate_args = _gate_kernel, gate_in + [anyspec, anyspec], gate_args + (wrep, w16)
            aliases = {5: 0, 6: 1}
        else:
            gate_fn, aliases = _gate_outputs_kernel, {}
        wrep, w16 = pl.pallas_call(
            gate_fn,
            grid=(n_sc_u // PB,),
            in_specs=gate_in,
            out_specs=[flat(D_MODEL, off), flat(HK * SC_LANES, off)],
            out_shape=gate_shapes,
            input_output_aliases=aliases,
            compiler_params=_cparams(("arbitrary",)),
        )(*gate_args)
    parts = []
    if n_tc_v:
        parts.append(pl.pallas_call(
            _peer_v_kernel,
            grid=(n_tc_v // PB,),
            in_specs=[smem_slot, flat(D_MODEL), tok, table],
            out_specs=flat(D_MODEL),
            out_shape=jax.ShapeDtypeStruct((n_tc_v, D_MODEL), F32),
            compiler_params=_cparams(("arbitrary",)),
        )(eid, wrep, x13, v_pack))
    if n_sc_v:
        v_rows = v_pack.reshape(N_EXPERTS, ROW_WORDS)
        parts.append(_sc_v_phase(n_sc_v, n_tc_v)(v_rows, expert, w16, x13).reshape(n_sc_v, D_MODEL))
    y = parts[0] if len(parts) == 1 else jnp.concatenate(parts, axis=0)
    return y, w16


def _encoder_layer(x, p, after):
    if after is not None:
        x, _ = lax.optimization_barrier((x, after))
    batch, seq, _ = x.shape
    n = batch * seq
    x2d = x.reshape(n, D_MODEL)
    naq, nak, nav, dq, dk, dv = _in_projection(x2d, seq, p["norm1_w"], p["w_in"], p["qk_w"], p["rope"])
    na = _neighbourhood_attention(naq, nak, nav, p["na_bias"], batch, seq)
    df = _diff_attention(dq, dk, dv, p["lam"], p["subln_w"], batch, seq)
    x1, xn, eid, g = _out_and_route(na, df, x2d, p["w_out"], p["norm2_w"], p["w_q"], p["sub_keys"])
    y, w16 = _peer_experts(eid, g, xn, x1, p["u_pack"], p["v_pack"], *SC_TOKENS.get(n, (0, 0)))
    return y.reshape(batch, seq, D_MODEL), w16


def kernel(x_prompt, x_sample, norm1_w, w_in, na_q_norm, na_k_norm, na_rpb, diff_q_norm, diff_k_norm,
           diff_lambda_q1, diff_lambda_k1, diff_lambda_q2, diff_lambda_k2, diff_subln_w, w_out, norm2_w,
           peer_w_q, peer_sub_keys, peer_u, peer_v):
    assert norm1_w.shape[0] == 1, "single-layer problem"
    reps = SECTION // HEAD_DIM
    params = {
        "norm1_w": norm1_w[0],
        "w_in": w_in[0].astype(BF16),
        "qk_w": jnp.stack([jnp.tile(w[0], reps) for w in (na_q_norm, na_k_norm, diff_q_norm, diff_k_norm)]),
        "lam": jnp.stack([diff_lambda_q1[0], diff_lambda_k1[0], diff_lambda_q2[0], diff_lambda_k2[0]]),
        "subln_w": diff_subln_w[0],
        "w_out": w_out[0].astype(BF16),
        "norm2_w": norm2_w[0],
        "w_q": peer_w_q[0].astype(BF16),
        "sub_keys": peer_sub_keys[0].reshape(2 * PEER_HEADS, N_KEYS, N_KEYS).astype(BF16),
        "na_bias": _na_bias(na_rpb[0]),
        "u_pack": _pack_table(peer_u[0]),
        "v_pack": _pack_table(peer_v[0]),
    }
    rope_by_seq = {}
    outs = [None, None]
    marker = None
    for slot, x in sorted(enumerate((x_prompt, x_sample)), key=lambda item: -item[1].shape[0] * item[1].shape[1]):
        seq = x.shape[1]
        if seq not in rope_by_seq:
            rope_by_seq[seq] = _rope_tables(seq)
        outs[slot], marker = _encoder_layer(x, dict(params, rope=rope_by_seq[seq]), marker)
    return tuple(outs)
```

```python
import functools
import math

import jax
import jax.numpy as jnp
import numpy as np
from jax import lax
from jax.experimental import pallas as pl
from jax.experimental.pallas import tpu as pltpu
from jax.experimental.pallas import tpu_sc as plsc

F32 = jnp.float32
BF16 = jnp.bfloat16

D_MODEL = 1024
GRID_W = 64
HEAD_DIM = 64
NA_HEADS = 8
NA_KH = 8
NA_KW = 16
DIFF_HEADS = 4
SECTION = 512
ROT_DIM = HEAD_DIM // 4
ROPE_THETA = 500000.0
PEER_HEADS = 8
N_KEYS = 128
PEER_TOPK = 16
N_EXPERTS = N_KEYS * N_KEYS
HK = PEER_HEADS * PEER_TOPK
EPS = 1e-6
NEG_BIG = -1e30
LOG2E = math.log2(math.e)
LAMBDA_INIT = 0.8 - 0.6 * math.exp(-0.3 * 0)

LANES = 128
SUBLANES = 8
VMEM_LIMIT = 56 * 1024 * 1024

TM_IN = 512
NA_ROWS_PER_STEP = 8
NA_ROW_UNROLL = 2
TQ = 512
KV_SUBCHUNKS = 1
TK = 4096
TM_OUT = 1024
TB = 128
PB = 256
SC_TOKENS = {8 * 4096: (14336, 8 * 4096)}


def _cparams(sem):
    return pltpu.CompilerParams(dimension_semantics=sem, vmem_limit_bytes=VMEM_LIMIT)


def _const_spec(shape):
    nd = len(shape)
    return pl.BlockSpec(shape, lambda *_: (0,) * nd)


def _inproj_kernel(x_ref, n1_ref, w_ref, bd_ref, nw_ref, rc_ref, ra_ref, rb_ref,
                   naq_ref, nak_ref, nav_ref, dq_ref, dk_ref, dv_ref):
    x = x_ref[...]
    ms = jnp.mean(x * x, axis=-1, keepdims=True)
    h = (x * lax.rsqrt(ms + EPS) * n1_ref[...]).astype(BF16)
    bd = bd_ref[...]

    def proj(c):
        return jnp.dot(h, w_ref[:, c * SECTION:(c + 1) * SECTION], preferred_element_type=F32)

    def qknorm(y, row):
        sq = y * y
        hi = sq.astype(BF16)
        lo = (sq - hi.astype(F32)).astype(BF16)
        msq = (jnp.dot(hi, bd, preferred_element_type=F32)
               + jnp.dot(lo, bd, preferred_element_type=F32))
        return y * lax.rsqrt(msq + EPS) * nw_ref[row:row + 1, :]

    def rope(y):
        return (y * rc_ref[...]
                + pltpu.roll(y, SECTION - ROT_DIM // 2, axis=1) * ra_ref[...]
                + pltpu.roll(y, ROT_DIM // 2, axis=1) * rb_ref[...])

    scale = HEAD_DIM ** -0.5 * LOG2E
    naq_ref[...] = (qknorm(proj(0), 0) * scale).astype(BF16)
    nak_ref[...] = qknorm(proj(1), 1).astype(BF16)
    nav_ref[...] = proj(2).astype(BF16)
    dq_ref[...] = (rope(qknorm(proj(3), 2)) * scale).astype(BF16)
    dk_ref[...] = rope(qknorm(proj(4), 3)).astype(BF16)
    dv_ref[...] = proj(5).astype(BF16)


def _rope_tables(seq):
    pos = jnp.arange(seq, dtype=F32)
    inv = ROPE_THETA ** (-jnp.arange(0, ROT_DIM, 2, dtype=F32) / ROT_DIM)
    ang = pos[:, None] * inv[None, :]
    cos, sin = jnp.cos(ang), jnp.sin(ang)
    half = ROT_DIM // 2
    pad = HEAD_DIM - ROT_DIM
    c_head = jnp.concatenate([cos, cos, jnp.ones((seq, pad), F32)], -1)
    a_head = jnp.concatenate([-sin, jnp.zeros((seq, pad + half), F32)], -1)
    b_head = jnp.concatenate([jnp.zeros((seq, half), F32), sin, jnp.zeros((seq, pad), F32)], -1)
    reps = SECTION // HEAD_DIM
    return (jnp.tile(c_head, (1, reps)), jnp.tile(a_head, (1, reps)), jnp.tile(b_head, (1, reps)))


def _in_projection(x2d, seq, norm1_w, w_in_bf, qk_w, rope_tabs):
    n = x2d.shape[0]
    tm = min(TM_IN, seq)
    steps_per_seq = seq // tm
    gid = np.arange(SECTION) // HEAD_DIM
    bd = jnp.asarray((gid[:, None] == gid[None, :]).astype(np.float32) / HEAD_DIM, BF16)
    tok = pl.BlockSpec((tm, SECTION), lambda i: (i, 0))
    pos = pl.BlockSpec((tm, SECTION), lambda i: (i % steps_per_seq, 0))
    out = jax.ShapeDtypeStruct((n, SECTION), BF16)
    return pl.pallas_call(
        _inproj_kernel,
        grid=(n // tm,),
        in_specs=[pl.BlockSpec((tm, D_MODEL), lambda i: (i, 0)),
                  _const_spec((1, D_MODEL)),
                  _const_spec((D_MODEL, 6 * SECTION)),
                  _const_spec((SECTION, SECTION)),
                  _const_spec((4, SECTION)),
                  pos, pos, pos],
        out_specs=[tok] * 6,
        out_shape=[out] * 6,
        compiler_params=_cparams(("parallel",)),
    )(x2d, norm1_w.reshape(1, D_MODEL), w_in_bf, bd, qk_w, *rope_tabs)


def _na_bias(rpb):
    c = np.arange(GRID_W)[:, None]
    kc = np.arange(GRID_W)[None, :]
    cs = np.clip(c - NA_KW // 2, 0, GRID_W - NA_KW)
    valid = (kc >= cs) & (kc < cs + NA_KW)
    rel_c = kc - c + (NA_KW - 1)
    onehot = ((np.arange(2 * NA_KW - 1)[:, None, None] == rel_c[None]) & valid[None]).astype(np.float32)
    col = jnp.einsum("hrx,xck->hrck", rpb.astype(F32), onehot, precision=lax.Precision.HIGHEST)
    col = jnp.where(valid, col * LOG2E, NEG_BIG)
    per_off = [col[:, NA_KH - 1 - o:2 * NA_KH - 1 - o] for o in range(NA_KH)]
    b = jnp.stack(per_off).transpose(0, 1, 3, 2, 4)
    return b.reshape(NA_KH, NA_HEADS, GRID_W, NA_KH * GRID_W)


def _na_kernel(q_ref, k_ref, v_ref, b_ref, o_ref, *, rows):
    j = pl.program_id(1)
    win = NA_KH * GRID_W
    pair = 2 * HEAD_DIM
    lane = lax.broadcasted_iota(jnp.int32, (GRID_W, pair), 1)
    first = lane < HEAD_DIM
    ones = jnp.ones((win, pair), BF16)

    def one_row(rr):
        r = j * NA_ROWS_PER_STEP + rr
        rs = jnp.clip(r - NA_KH // 2, 0, rows - NA_KH)
        off = r - rs
        kstart = pl.multiple_of(rs * GRID_W, GRID_W)
        qstart = pl.multiple_of(rr * GRID_W, GRID_W)
        outs = []
        for hp in range(NA_HEADS // 2):
            sl = slice(hp * pair, (hp + 1) * pair)
            qp = q_ref[pl.ds(qstart, GRID_W), sl]
            kp = k_ref[pl.ds(kstart, win), sl]
            v_ext = jnp.concatenate([v_ref[pl.ds(kstart, win), sl], ones], axis=1)
            halves = []
            for hh in range(2):
                qm = jnp.where(first if hh == 0 else ~first, qp, jnp.zeros_like(qp))
                s = lax.dot_general(qm, kp, (((1,), (1,)), ((), ())), preferred_element_type=F32)
                s = s + b_ref[off, 2 * hp + hh]
                p = jnp.exp2(s - jnp.max(s, axis=-1, keepdims=True)).astype(BF16)
                z = jnp.dot(p, v_ext, preferred_element_type=F32)
                halves.append(z[:, :pair] / z[:, pair:pair + 1])
            outs.append(jnp.where(first, halves[0], halves[1]))
        o_ref[pl.ds(qstart, GRID_W), :] = jnp.concatenate(outs, axis=-1).astype(BF16)

    def row_body(i, carry):
        for u in range(NA_ROW_UNROLL):
            one_row(i * NA_ROW_UNROLL + u)
        return carry

    lax.fori_loop(0, NA_ROWS_PER_STEP // NA_ROW_UNROLL, row_body, 0)


def _neighbourhood_attention(q, k, v, bias, batch, seq):
    rows = seq // GRID_W
    assert rows >= NA_KH and rows % NA_ROWS_PER_STEP == 0
    nblk = rows // NA_ROWS_PER_STEP
    tq = NA_ROWS_PER_STEP * GRID_W
    qspec = pl.BlockSpec((tq, SECTION), lambda b, j: (b * nblk + j, 0))
    kvspec = pl.BlockSpec((seq, SECTION), lambda b, j: (b, 0))
    return pl.pallas_call(
        functools.partial(_na_kernel, rows=rows),
        grid=(batch, nblk),
        in_specs=[qspec, kvspec, kvspec, _const_spec(bias.shape)],
        out_specs=qspec,
        out_shape=jax.ShapeDtypeStruct(q.shape, BF16),
        compiler_params=_cparams(("parallel", "arbitrary")),
    )(q, k, v, bias)


def _diff_kernel(q_ref, k_ref, v_ref, lam_ref, sub_ref, o_ref, m0_scr, m1_scr, acc0_scr, acc1_scr,
                 *, seq, tq, tk):
    m_scrs, acc_scrs = (m0_scr, m1_scr), (acc0_scr, acc1_scr)
    q = q_ref[...]
    lane = lax.broadcasted_iota(jnp.int32, q.shape, 1)
    zero = jnp.zeros_like(q)
    q_maps = (jnp.where(lane < HEAD_DIM, q, zero), jnp.where(lane >= HEAD_DIM, q, zero))

    lp = lam_ref[...]
    lam = (jnp.exp(jnp.sum(lp[0:1] * lp[1:2], axis=-1, keepdims=True))
           - jnp.exp(jnp.sum(lp[2:3] * lp[3:4], axis=-1, keepdims=True)) + LAMBDA_INIT)

    width = 2 * HEAD_DIM
    ones = jnp.ones((tk, width), BF16)

    for m_scr, acc_scr in zip(m_scrs, acc_scrs):
        m_scr[...] = jnp.full(m_scr.shape, -jnp.inf, F32)
        acc_scr[...] = jnp.zeros(acc_scr.shape, F32)

    def kv_body(c, carry):
        tks = tk // KV_SUBCHUNKS
        kcs, v_exts = [], []
        for sub in range(KV_SUBCHUNKS):
            start = pl.multiple_of(c * tk + sub * tks, tks)
            kcs.append(k_ref[pl.ds(start, tks), :])
            v_exts.append(jnp.concatenate([v_ref[pl.ds(start, tks), :], ones[:tks]], axis=1))
        scores = [[lax.dot_general(q_maps[mp], kcs[sub], (((1,), (1,)), ((), ())),
                                   preferred_element_type=F32) for mp in range(2)]
                  for sub in range(KV_SUBCHUNKS)]
        m = [m_scrs[mp][...] for mp in range(2)]
        probs, alphas = [], []
        for sub in range(KV_SUBCHUNKS):
            probs.append([])
            alphas.append([])
            for mp in range(2):
                mn = jnp.maximum(m[mp], jnp.max(scores[sub][mp], axis=-1, keepdims=True))
                probs[sub].append(jnp.exp2(scores[sub][mp] - mn[:, 0:1]).astype(BF16))
                alpha = jnp.exp2(m[mp] - mn)
                alphas[sub].append(jnp.concatenate([alpha, alpha], axis=1))
                m[mp] = mn
        for mp in range(2):
            m_scrs[mp][...] = m[mp]
            acc = acc_scrs[mp][...]
            for sub in range(KV_SUBCHUNKS):
                acc = alphas[sub][mp] * acc + jnp.dot(probs[sub][mp], v_exts[sub],
                                                      preferred_element_type=F32)
            acc_scrs[mp][...] = acc
        return carry

    lax.fori_loop(0, seq // tk, kv_body, 0)
    a0, a1 = acc0_scr[...], acc1_scr[...]
    o = a0[:, :width] / a0[:, width:width + 1] - lam * (a1[:, :width] / a1[:, width:width + 1])
    ms = jnp.mean(o * o, axis=-1, keepdims=True)
    y = o * lax.rsqrt(ms + EPS) * sub_ref[...]
    o_ref[...] = (y * (1.0 - LAMBDA_INIT)).astype(BF16)


def _diff_attention(q, k, v, lam_params, subln_w, batch, seq):
    tq, tk = min(TQ, seq), min(TK, seq)
    nq = seq // tq
    width = 2 * HEAD_DIM
    qspec = pl.BlockSpec((tq, width), lambda b, h, i: (b * nq + i, h))
    kvspec = pl.BlockSpec((seq, width), lambda b, h, i: (b, h))
    return pl.pallas_call(
        functools.partial(_diff_kernel, seq=seq, tq=tq, tk=tk),
        grid=(batch, DIFF_HEADS, nq),
        in_specs=[qspec, kvspec, kvspec, _const_spec((4, HEAD_DIM)), _const_spec((1, width))],
        out_specs=qspec,
        out_shape=jax.ShapeDtypeStruct(q.shape, BF16),
        scratch_shapes=[pltpu.VMEM((tq, width), F32), pltpu.VMEM((tq, width), F32),
                        pltpu.VMEM((tq, 2 * width), F32), pltpu.VMEM((tq, 2 * width), F32)],
        compiler_params=_cparams(("parallel", "parallel", "arbitrary")),
    )(q, k, v, lam_params, subln_w.reshape(1, width))


def _top16(sc, payload=None):
    iota = lax.broadcasted_iota(jnp.int32, sc.shape, 0).astype(F32)
    big = float(sc.shape[0])
    vals, picks = [], []
    for _ in range(PEER_TOPK):
        m = jnp.max(sc, axis=0, keepdims=True)
        ix = jnp.min(jnp.where(sc == m, iota, big), axis=0, keepdims=True)
        sel = iota == ix
        if payload is None:
            picks.append(ix)
        else:
            picks.append(jnp.sum(jnp.where(sel, payload, 0), axis=0, keepdims=True))
        sc = jnp.where(sel, -jnp.inf, sc)
        vals.append(m)
    picks = jnp.concatenate(picks, axis=0)
    return jnp.concatenate(vals, axis=0), picks.astype(jnp.int32)


def _pair_candidates(r0, r1):
    half = SUBLANES // 2
    groups = [r0[0:1] + r1[0:8], r0[0:1] + r1[8:16]]
    groups += [r0[a:a + 1] + r1[0:8] for a in (1, 2, 3)]
    groups += [jnp.concatenate([r0[a:a + 1] + r1[0:half], r0[a + 1:a + 2] + r1[0:half]], axis=0)
               for a in (4, 6)]
    groups.append(r0[8:16] + r1[0:1])
    return jnp.concatenate(groups, axis=0)


def _route_kernel(na_ref, df_ref, x_ref, wo_ref, n2_ref, wq_ref, sk_ref,
                  x1_ref, xn_ref, eid_ref, g_ref, q_scr, eid_scr, g_scr):
    tm = x_ref.shape[0]
    x1 = (x_ref[...]
          + jnp.dot(na_ref[...], wo_ref[0:SECTION, :], preferred_element_type=F32)
          + jnp.dot(df_ref[...], wo_ref[SECTION:2 * SECTION, :], preferred_element_type=F32))
    ms = jnp.mean(x1 * x1, axis=-1, keepdims=True)
    xn = x1 * lax.rsqrt(ms + EPS) * n2_ref[...]
    for r in range(SUBLANES):
        x1_ref[:, r, :] = x1[:, r * LANES:(r + 1) * LANES]
        xn_ref[:, r, :] = xn[:, r * LANES:(r + 1) * LANES]
    xb = xn.astype(BF16)
    for c in range(2 * PEER_HEADS):
        q_scr[c] = jnp.dot(xb, wq_ref[:, c * LANES:(c + 1) * LANES],
                           preferred_element_type=F32).astype(BF16)

    nblk = tm // TB

    def route_block(h, blk):
        tok0 = blk * TB
        tops = []
        for p in range(2):
            sc = lax.dot_general(sk_ref[2 * h + p], q_scr[2 * h + p, pl.ds(tok0, TB), :],
                                 (((1,), (1,)), ((), ())), preferred_element_type=F32)
            tops.append(_top16(sc))
        (s0, i0), (s1, i1) = tops
        cand_s = _pair_candidates(s0, s1)
        cand_i = _pair_candidates(i0 * N_KEYS, i1)
        top_s, eid = _top16(cand_s, cand_i)
        e = jnp.exp(top_s - top_s[0:1])
        row = pl.multiple_of(h * PEER_TOPK, PEER_TOPK)
        eid_scr[blk, pl.ds(row, PEER_TOPK), :] = eid * ROW_SUB
        g_scr[blk, pl.ds(row, PEER_TOPK), :] = e / jnp.sum(e, axis=0, keepdims=True)

    def head_body(h, carry):
        for blk in range(nblk):
            route_block(h, blk)
        return carry

    lax.fori_loop(0, PEER_HEADS, head_body, 0)
    for blk in range(nblk):
        eid_ref[blk] = eid_scr[blk].T
        g_ref[blk * TB:(blk + 1) * TB, :] = g_scr[blk].T


def _out_and_route(na, df, x2d, w_out_bf, norm2_w, w_q_bf, sk_bf):
    n = x2d.shape[0]
    tm = TM_OUT
    tok = lambda w: pl.BlockSpec((tm, w), lambda i: (i, 0))
    tiles = pl.BlockSpec((tm, SUBLANES, LANES), lambda i: (i, 0, 0))
    return pl.pallas_call(
        _route_kernel,
        grid=(n // tm,),
        in_specs=[tok(SECTION), tok(SECTION), tok(D_MODEL),
                  _const_spec((D_MODEL, D_MODEL)), _const_spec((1, D_MODEL)),
                  _const_spec(w_q_bf.shape), _const_spec(sk_bf.shape)],
        out_specs=[tiles, tiles,
                   pl.BlockSpec((tm // TB, TB, HK), lambda i: (i, 0, 0)), tok(HK)],
        out_shape=[jax.ShapeDtypeStruct((n, SUBLANES, LANES), F32),
                   jax.ShapeDtypeStruct((n, SUBLANES, LANES), F32),
                   jax.ShapeDtypeStruct((n // TB, TB, HK), jnp.int32),
                   jax.ShapeDtypeStruct((n, HK), F32)],
        scratch_shapes=[pltpu.VMEM((2 * PEER_HEADS, tm, LANES), BF16),
                        pltpu.VMEM((tm // TB, HK, TB), jnp.int32), pltpu.VMEM((tm // TB, HK, TB), F32)],
        compiler_params=_cparams(("parallel",)),
    )(na, df, x2d, w_out_bf, norm2_w.reshape(1, D_MODEL), w_q_bf, sk_bf)


ROW_SUB = SUBLANES // 2
CHUNK_SLOTS = 32
N_CHUNKS = HK // CHUNK_SLOTS
CHUNK_ROWS = CHUNK_SLOTS * SUBLANES
TOKEN_UNROLL = 16
ROW_WORDS = ROW_SUB * LANES
SC_CORES, SC_SUBCORES, SC_LANES = 2, 16, 16
SC_WORKERS = SC_CORES * SC_SUBCORES


PACK_EXPERTS = 512


def _pack_kernel(x_ref, o_ref):
    x = x_ref[...]
    for c in range(ROW_SUB):
        lo = x[:, (2 * c) * LANES:(2 * c + 1) * LANES].astype(BF16).astype(F32)
        hi = x[:, (2 * c + 1) * LANES:(2 * c + 2) * LANES].astype(BF16).astype(F32)
        words = (pltpu.bitcast(lo, jnp.uint32) >> 16) | (pltpu.bitcast(hi, jnp.uint32) & jnp.uint32(0xFFFF0000))
        o_ref[pl.ds(c, PACK_EXPERTS, stride=ROW_SUB), :] = words


def _pack_table(tab):
    experts = tab.shape[0]
    return pl.pallas_call(
        _pack_kernel,
        grid=(experts // PACK_EXPERTS,),
        in_specs=[pl.BlockSpec((PACK_EXPERTS, D_MODEL), lambda i: (i, 0))],
        out_specs=pl.BlockSpec((PACK_EXPERTS * ROW_SUB, LANES), lambda i: (i, 0)),
        out_shape=jax.ShapeDtypeStruct((experts * ROW_SUB, LANES), jnp.uint32),
        compiler_params=_cparams(("parallel",)),
    )(tab)


def _split_bf16(x):
    hi = x.astype(BF16)
    return hi, (x - hi.astype(F32)).astype(BF16)


def _gather_chunk(tab_ref, eid_ref, t, chunk):
    ids = eid_ref.at[0, t]
    rows = [tab_ref[pl.ds(pl.multiple_of(ids[chunk * CHUNK_SLOTS + i], ROW_SUB), ROW_SUB), :]
            for i in range(CHUNK_SLOTS)]
    return pltpu.bitcast(jnp.concatenate(rows, axis=0), BF16)


def _diag_mask(width):
    r = lax.broadcasted_iota(jnp.int32, (SUBLANES, width), 0)
    n = lax.broadcasted_iota(jnp.int32, (SUBLANES, width), 1)
    return (n % SUBLANES) == r


def _peer_u_kernel(eid_ref, xn_ref, g_ref, tab_ref, rept_ref, rep_ref, rep16_ref, wrep_ref, w16_ref, d_scr):
    mask = _diag_mask(CHUNK_ROWS)
    zeros = jnp.zeros((SUBLANES, LANES), BF16)

    def one_token(t):
        xhi, xlo = _split_bf16(xn_ref[t])
        lhs = jnp.concatenate([jnp.concatenate([xhi, zeros], axis=1),
                               jnp.concatenate([xlo, zeros], axis=1),
                               jnp.concatenate([zeros, xhi], axis=1),
                               jnp.concatenate([zeros, xlo], axis=1)], axis=0)
        for pr in range(N_CHUNKS // 2):
            s = jnp.concatenate([_gather_chunk(tab_ref, eid_ref, t, 2 * pr),
                                 _gather_chunk(tab_ref, eid_ref, t, 2 * pr + 1)], axis=1)
            z = lax.dot_general(lhs, s, (((1,), (1,)), ((), ())), preferred_element_type=F32)
            for half in range(2):
                z8 = z[16 * half:16 * half + 8] + z[16 * half + 8:16 * half + 16]
                zs = jnp.sum(jnp.where(mask, z8, 0.0), axis=0, keepdims=True)
                d_scr[pl.ds(t, 1), pl.ds((2 * pr + half) * CHUNK_ROWS, CHUNK_ROWS)] = zs

    def tok_body(i, carry):
        for u in range(TOKEN_UNROLL):
            one_token(i * TOKEN_UNROLL + u)
        return carry

    lax.fori_loop(0, PB // TOKEN_UNROLL, tok_body, 0)
    _gate_outputs(d_scr[...], rept_ref, g_ref, rep_ref, rep16_ref, wrep_ref, w16_ref)


def _gate_outputs(partial, sum_ref, g_ref, rep_ref, rep16_ref, wrep_ref, w16_ref):
    dhi, dlo = _split_bf16(partial)
    a = (jnp.dot(dhi, sum_ref[...], preferred_element_type=F32)
         + jnp.dot(dlo, sum_ref[...], preferred_element_type=F32))
    w = g_ref[...] * (0.5 * a * (1.0 + lax.erf(a * (2.0 ** -0.5))))
    whi, wlo = _split_bf16(w)
    wrep_ref[...] = (jnp.dot(whi, rep_ref[...], preferred_element_type=F32)
                     + jnp.dot(wlo, rep_ref[...], preferred_element_type=F32))
    w16_ref[...] = (jnp.dot(whi, rep16_ref[...], preferred_element_type=F32)
                    + jnp.dot(wlo, rep16_ref[...], preferred_element_type=F32))


def _gate_outputs_kernel(part_ref, g_ref, sum_ref, rep_ref, rep16_ref, wrep_ref, w16_ref):
    _gate_outputs(part_ref[...], sum_ref, g_ref, rep_ref, rep16_ref, wrep_ref, w16_ref)


def _gate_kernel(part_ref, g_ref, sum_ref, rep_ref, rep16_ref, wrep_in, w16_in, wrep_ref, w16_ref):
    del wrep_in, w16_in
    _gate_outputs(part_ref[...], sum_ref, g_ref, rep_ref, rep16_ref, wrep_ref, w16_ref)


def _peer_v_kernel(eid_ref, wrep_ref, x1_ref, tab_ref, o_ref):
    mask = _diag_mask(CHUNK_ROWS)

    def one_token(t):
        wr = wrep_ref[pl.ds(t, 1), :]
        acc = x1_ref[t]
        for pr in range(N_CHUNKS // 2):
            parts = []
            for half in range(2):
                seg = wr[:, (2 * pr + half) * CHUNK_ROWS:(2 * pr + half + 1) * CHUNK_ROWS]
                parts.extend(_split_bf16(jnp.where(mask, jnp.broadcast_to(seg, mask.shape), 0.0)))
            lhs = jnp.concatenate(parts, axis=0)
            s = jnp.concatenate([_gather_chunk(tab_ref, eid_ref, t, 2 * pr),
                                 _gather_chunk(tab_ref, eid_ref, t, 2 * pr + 1)], axis=1)
            z = jnp.dot(lhs, s, preferred_element_type=F32)
            acc = acc + ((z[0:8, :LANES] + z[8:16, :LANES]) + (z[16:24, LANES:] + z[24:32, LANES:]))
        return acc

    def tok_body(i, carry):
        accs = [one_token(i * TOKEN_UNROLL + u) for u in range(TOKEN_UNROLL)]
        base = pl.multiple_of(i * TOKEN_UNROLL, TOKEN_UNROLL)
        for r in range(SUBLANES):
            o_ref[pl.ds(base, TOKEN_UNROLL), r * LANES:(r + 1) * LANES] = jnp.concatenate(
                [acc[r:r + 1, :] for acc in accs], axis=0)
        return carry

    lax.fori_loop(0, PB // TOKEN_UNROLL, tok_body, 0)


def _sc_scratch(mid_shape, last_shape):
    half = HK // 2
    per_parity = [pltpu.VMEM((2, half), jnp.int32), pltpu.VMEM(mid_shape, F32), pltpu.VMEM(last_shape, F32)]
    return per_parity + per_parity + [
        pltpu.VMEM((half, ROW_WORDS), jnp.uint32), pltpu.VMEM((half, ROW_WORDS), jnp.uint32),
        pltpu.SemaphoreType.DMA, pltpu.SemaphoreType.DMA,
        pltpu.SemaphoreType.DMA, pltpu.SemaphoreType.DMA,
        pltpu.SemaphoreType.DMA, pltpu.SemaphoreType.DMA,
    ]


def _sc_u_phase(n_tok, tok0):
    per_w = n_tok // SC_WORKERS
    assert per_w * SC_WORKERS == n_tok and per_w % 2 == 0
    half = HK // 2
    mesh = plsc.VectorSubcoreMesh(core_axis_name="c", subcore_axis_name="s")

    @functools.partial(
        pl.kernel, mesh=mesh,
        out_type=jax.ShapeDtypeStruct((n_tok, HK * SC_LANES), F32),
        scratch_types=_sc_scratch((SUBLANES, LANES), (HK * SC_LANES,)),
        compiler_params=pltpu.CompilerParams(needs_layout_passes=False),
    )
    def sc_kernel(tab_hbm, eid_hbm, xn_hbm, out_hbm,
                  idx_a, x_a, a_a, idx_b, x_b, a_b, g0, g1, s_a, s_b, sg0, sg1, so_a, so_b):
        wid = lax.axis_index("s") * SC_CORES + lax.axis_index("c")
        first = wid * per_w
        par = ((idx_a, x_a, a_a, s_a, so_a), (idx_b, x_b, a_b, s_b, so_b))
        gbuf, gsem = (g0, g1), (sg0, sg1)

        def input_copies(t, p):
            idx_v, x_v, _, sem, _ = par[p]
            return (pltpu.make_async_copy(eid_hbm.at[tok0 + t], idx_v, sem),
                    pltpu.make_async_copy(xn_hbm.at[tok0 + t], x_v, sem))

        def gather(p, h):
            return pltpu.make_async_copy(tab_hbm.at[par[p][0].at[h]], gbuf[h], gsem[h])

        def out_copy(t, p):
            return pltpu.make_async_copy(par[p][2], out_hbm.at[t], par[p][4])

        def dot_half(p, h):
            _, x_v, a_v, _, _ = par[p]
            g = gbuf[h]
            groups = LANES // SC_LANES
            block = SUBLANES
            for rb in range(half // block):
                def col_body(k, accs):
                    c, kk = k // groups, k % groups
                    lane0 = pl.multiple_of(kk * SC_LANES, SC_LANES)
                    xl = x_v[2 * c, pl.ds(lane0, SC_LANES)]
                    xh = x_v[2 * c + 1, pl.ds(lane0, SC_LANES)]
                    out = []
                    for r in range(block):
                        words = g[rb * block + r, pl.ds(pl.multiple_of(k * SC_LANES, SC_LANES), SC_LANES)]
                        lov = plsc.bitcast(lax.shift_left(words, jnp.uint32(16)), F32)
                        hiv = plsc.bitcast(words & jnp.uint32(0xFFFF0000), F32)
                        out.append(accs[r] + lov * xl + hiv * xh)
                    return tuple(out)

                zero = jnp.zeros((SC_LANES,), F32)
                accs = lax.fori_loop(0, ROW_WORDS // SC_LANES, col_body, (zero,) * block)
                for r in range(block):
                    a_v[pl.ds((h * half + rb * block + r) * SC_LANES, SC_LANES)] = accs[r]

        def token(t, p):
            q = 1 - p
            has_next = t + 1 < first + per_w
            gather(p, 0).wait()
            gather(p, 1).start()

            @pl.when(has_next)
            def _():
                for cp in input_copies(t + 1, q):
                    cp.start()

            @pl.when(t >= first + 2)
            def _():
                out_copy(t - 2, p).wait()

            dot_half(p, 0)
            gather(p, 1).wait()

            @pl.when(has_next)
            def _():
                for cp in input_copies(t + 1, q):
                    cp.wait()
                gather(q, 0).start()

            dot_half(p, 1)
            out_copy(t, p).start()

        for cp in input_copies(first, 0):
            cp.start()
        for cp in input_copies(first, 0):
            cp.wait()
        gather(0, 0).start()

        def pair_body(i, carry):
            token(first + 2 * i, 0)
            token(first + 2 * i + 1, 1)
            return carry

        lax.fori_loop(0, per_w // 2, pair_body, 0)
        out_copy(first + per_w - 2, 0).wait()
        out_copy(first + per_w - 1, 1).wait()

    return sc_kernel


def _sc_v_phase(n_tok, tok0):
    per_w = n_tok // SC_WORKERS
    assert per_w * SC_WORKERS == n_tok and per_w % 2 == 0
    half = HK // 2
    mesh = plsc.VectorSubcoreMesh(core_axis_name="c", subcore_axis_name="s")

    @functools.partial(
        pl.kernel, mesh=mesh,
        out_type=jax.ShapeDtypeStruct((n_tok, SUBLANES, LANES), F32),
        scratch_types=_sc_scratch((HK * SC_LANES,), (SUBLANES, LANES)),
        compiler_params=pltpu.CompilerParams(needs_layout_passes=False),
    )
    def sc_kernel(tab_hbm, eid_hbm, w16_hbm, x1_hbm, out_hbm,
                  idx_a, w_a, x_a, idx_b, w_b, x_b, g0, g1, s_a, s_b, sg0, sg1, so_a, so_b):
        wid = lax.axis_index("s") * SC_CORES + lax.axis_index("c")
        first = wid * per_w
        par = ((idx_a, w_a, x_a, s_a, so_a), (idx_b, w_b, x_b, s_b, so_b))
        gbuf, gsem = (g0, g1), (sg0, sg1)

        def input_copies(t, p):
            idx_v, w_v, x_v, sem, _ = par[p]
            return (pltpu.make_async_copy(eid_hbm.at[tok0 + t], idx_v, sem),
                    pltpu.make_async_copy(w16_hbm.at[tok0 + t], w_v, sem),
                    pltpu.make_async_copy(x1_hbm.at[tok0 + t], x_v, sem))

        def gather(p, h):
            return pltpu.make_async_copy(tab_hbm.at[par[p][0].at[h]], gbuf[h], gsem[h])

        def out_copy(t, p):
            return pltpu.make_async_copy(par[p][2], out_hbm.at[t], par[p][4])

        def accumulate_half(p, h):
            _, w_v, x_v, _, _ = par[p]
            g = gbuf[h]
            groups = LANES // SC_LANES
            for c in range(ROW_SUB):
                lo0 = tuple(x_v[2 * c, pl.ds(SC_LANES * kk, SC_LANES)] for kk in range(groups))
                hi0 = tuple(x_v[2 * c + 1, pl.ds(SC_LANES * kk, SC_LANES)] for kk in range(groups))

                def row_body(j, acc):
                    lo, hi = acc
                    wj = w_v[pl.ds(pl.multiple_of((h * half + j) * SC_LANES, SC_LANES), SC_LANES)]
                    nlo, nhi = [], []
                    for kk in range(groups):
                        words = g[j, pl.ds(LANES * c + SC_LANES * kk, SC_LANES)]
                        lov = plsc.bitcast(lax.shift_left(words, jnp.uint32(16)), F32)
                        hiv = plsc.bitcast(words & jnp.uint32(0xFFFF0000), F32)
                        nlo.append(lo[kk] + wj * lov)
                        nhi.append(hi[kk] + wj * hiv)
                    return tuple(nlo), tuple(nhi)

                lo, hi = lax.fori_loop(0, half, row_body, (lo0, hi0))
                for kk in range(groups):
                    x_v[2 * c, pl.ds(SC_LANES * kk, SC_LANES)] = lo[kk]
                    x_v[2 * c + 1, pl.ds(SC_LANES * kk, SC_LANES)] = hi[kk]

        def token(t, p):
            q = 1 - p
            has_next = t + 1 < first + per_w
            gather(p, 0).wait()
            gather(p, 1).start()

            @pl.when(has_next)
            def _():
                @pl.when(t >= first + 1)
                def _():
                    out_copy(t - 1, q).wait()
                for cp in input_copies(t + 1, q):
                    cp.start()

            accumulate_half(p, 0)
            gather(p, 1).wait()

            @pl.when(has_next)
            def _():
                for cp in input_copies(t + 1, q):
                    cp.wait()
                gather(q, 0).start()

            accumulate_half(p, 1)
            out_copy(t, p).start()

        for cp in input_copies(first, 0):
            cp.start()
        for cp in input_copies(first, 0):
            cp.wait()
        gather(0, 0).start()

        def pair_body(i, carry):
            token(first + 2 * i, 0)
            token(first + 2 * i + 1, 1)
            return carry

        lax.fori_loop(0, per_w // 2, pair_body, 0)
        out_copy(first + per_w - 2, 0).wait()
        out_copy(first + per_w - 1, 1).wait()

    return sc_kernel


def _peer_experts(eid, g, xn3, x13, u_pack, v_pack, n_sc_u, n_sc_v):
    n = xn3.shape[0]
    n_tc_u, n_tc_v = n - n_sc_u, n - n_sc_v
    assert n_tc_u % PB == 0 and n_tc_v % PB == 0
    eid = eid.reshape(n // PB, PB, HK)
    smem_slot = pl.BlockSpec((1, PB, HK), lambda i: (i, 0, 0), memory_space=pltpu.SMEM)
    tok = pl.BlockSpec((PB, SUBLANES, LANES), lambda i: (i, 0, 0))
    flat = lambda w, off=0: pl.BlockSpec((PB, w), lambda i: (i + off, 0))
    table = pl.BlockSpec(u_pack.shape, lambda i: (0, 0), pipeline_mode=pl.Buffered(1))

    def repeat_matrix(times):
        m = (np.arange(HK)[:, None] == (np.arange(HK * times)[None, :] // times)).astype(np.float32)
        return jnp.asarray(m, BF16)

    rep, rep16 = repeat_matrix(SUBLANES), repeat_matrix(SC_LANES)
    gate_shapes = [jax.ShapeDtypeStruct((n, D_MODEL), F32), jax.ShapeDtypeStruct((n, HK * SC_LANES), F32)]
    expert = eid.reshape(n, 2, HK // 2) >> (ROW_SUB.bit_length() - 1)
    if n_sc_u:
        u_rows = u_pack.reshape(N_EXPERTS, ROW_WORDS)
        partial = _sc_u_phase(n_sc_u, n_tc_u)(u_rows, expert, xn3)
    if n_tc_u:
        wrep, w16 = pl.pallas_call(
            _peer_u_kernel,
            grid=(n_tc_u // PB,),
            in_specs=[smem_slot, tok, flat(HK), table, _const_spec(rep.T.shape), _const_spec(rep.shape),
                      _const_spec(rep16.shape)],
            out_specs=[flat(D_MODEL), flat(HK * SC_LANES)],
            out_shape=gate_shapes,
            scratch_shapes=[pltpu.VMEM((PB, D_MODEL), F32)],
            compiler_params=_cparams(("arbitrary",)),
        )(eid, xn3, g, u_pack, rep.T, rep, rep16)
    if n_sc_u:
        off = n_tc_u // PB
        gate_in = [flat(HK * SC_LANES), flat(HK, off), _const_spec(rep16.T.shape), _const_spec(rep.shape),
                   _const_spec(rep16.shape)]
        gate_args = (partial, g, rep16.T, rep, rep16)
        if n_tc_u:
            anyspec = pl.BlockSpec(memory_space=pl.ANY)
            gate_fn, gate_in, gate_args = _gate_kernel, gate_in + [anyspec, anyspec], gate_args + (wrep, w16)
            aliases = {5: 0, 6: 1}
        else:
            gate_fn, aliases = _gate_outputs_kernel, {}
        wrep, w16 = pl.pallas_call(
            gate_fn,
            grid=(n_sc_u // PB,),
            in_specs=gate_in,
            out_specs=[flat(D_MODEL, off), flat(HK * SC_LANES, off)],
            out_shape=gate_shapes,
            input_output_aliases=aliases,
            compiler_params=_cparams(("arbitrary",)),
        )(*gate_args)
    parts = []
    if n_tc_v:
        parts.append(pl.pallas_call(
            _peer_v_kernel,
            grid=(n_tc_v // PB,),
            in_specs=[smem_slot, flat(D_MODEL), tok, table],
            out_specs=flat(D_MODEL),
            out_shape=jax.ShapeDtypeStruct((n_tc_v, D_MODEL), F32),
            compiler_params=_cparams(("arbitrary",)),
        )(eid, wrep, x13, v_pack))
    if n_sc_v:
        v_rows = v_pack.reshape(N_EXPERTS, ROW_WORDS)
        parts.append(_sc_v_phase(n_sc_v, n_tc_v)(v_rows, expert, w16, x13).reshape(n_sc_v, D_MODEL))
    y = parts[0] if len(parts) == 1 else jnp.concatenate(parts, axis=0)
    return y, w16


def _encoder_layer(x, p, after):
    if after is not None:
        x, _ = lax.optimization_barrier((x, after))
    batch, seq, _ = x.shape
    n = batch * seq
    x2d = x.reshape(n, D_MODEL)
    naq, nak, nav, dq, dk, dv = _in_projection(x2d, seq, p["norm1_w"], p["w_in"], p["qk_w"], p["rope"])
    na = _neighbourhood_attention(naq, nak, nav, p["na_bias"], batch, seq)
    df = _diff_attention(dq, dk, dv, p["lam"], p["subln_w"], batch, seq)
    x1, xn, eid, g = _out_and_route(na, df, x2d, p["w_out"], p["norm2_w"], p["w_q"], p["sub_keys"])
    y, w16 = _peer_experts(eid, g, xn, x1, p["u_pack"], p["v_pack"], *SC_TOKENS.get(n, (0, 0)))
    return y.reshape(batch, seq, D_MODEL), w16


def kernel(x_prompt, x_sample, norm1_w, w_in, na_q_norm, na_k_norm, na_rpb, diff_q_norm, diff_k_norm,
           diff_lambda_q1, diff_lambda_k1, diff_lambda_q2, diff_lambda_k2, diff_subln_w, w_out, norm2_w,
           peer_w_q, peer_sub_keys, peer_u, peer_v):
    assert norm1_w.shape[0] == 1, "single-layer problem"
    reps = SECTION // HEAD_DIM
    params = {
        "norm1_w": norm1_w[0],
        "w_in": w_in[0].astype(BF16),
        "qk_w": jnp.stack([jnp.tile(w[0], reps) for w in (na_q_norm, na_k_norm, diff_q_norm, diff_k_norm)]),
        "lam": jnp.stack([diff_lambda_q1[0], diff_lambda_k1[0], diff_lambda_q2[0], diff_lambda_k2[0]]),
        "subln_w": diff_subln_w[0],
        "w_out": w_out[0].astype(BF16),
        "norm2_w": norm2_w[0],
        "w_q": peer_w_q[0].astype(BF16),
        "sub_keys": peer_sub_keys[0].reshape(2 * PEER_HEADS, N_KEYS, N_KEYS).astype(BF16),
        "na_bias": _na_bias(na_rpb[0]),
        "u_pack": _pack_table(peer_u[0]),
        "v_pack": _pack_table(peer_v[0]),
    }
    rope_by_seq = {}
    outs = [None, None]
    marker = None
    for slot, x in sorted(enumerate((x_prompt, x_sample)), key=lambda item: -item[1].shape[0] * item[1].shape[1]):
        seq = x.shape[1]
        if seq not in rope_by_seq:
            rope_by_seq[seq] = _rope_tables(seq)
        outs[slot], marker = _encoder_layer(x, dict(params, rope=rope_by_seq[seq]), marker)
    return tuple(outs)
```

```python
import functools
import math

import jax
import jax.numpy as jnp
import numpy as np
from jax import lax
from jax.experimental import pallas as pl
from jax.experimental.pallas import tpu as pltpu
from jax.experimental.pallas import tpu_sc as plsc

F32 = jnp.float32
BF16 = jnp.bfloat16

D_MODEL = 1024
GRID_W = 64
HEAD_DIM = 64
NA_HEADS = 8
NA_KH = 8
NA_KW = 16
DIFF_HEADS = 4
SECTION = 512
ROT_DIM = HEAD_DIM // 4
ROPE_THETA = 500000.0
PEER_HEADS = 8
N_KEYS = 128
PEER_TOPK = 16
N_EXPERTS = N_KEYS * N_KEYS
HK = PEER_HEADS * PEER_TOPK
EPS = 1e-6
NEG_BIG = -1e30
LOG2E = math.log2(math.e)
LAMBDA_INIT = 0.8 - 0.6 * math.exp(-0.3 * 0)

LANES = 128
SUBLANES = 8
VMEM_LIMIT = 56 * 1024 * 1024

TM_IN = 512
NA_ROWS_PER_STEP = 8
NA_ROW_UNROLL = 2
TQ = 512
KV_SUBCHUNKS = 1
TK = 4096
TM_OUT = 1024
TB = 128
PB = 256
SC_TOKENS = {8 * 4096: (14336, 8 * 4096)}


def _cparams(sem):
    return pltpu.CompilerParams(dimension_semantics=sem, vmem_limit_bytes=VMEM_LIMIT)


def _const_spec(shape):
    nd = len(shape)
    return pl.BlockSpec(shape, lambda *_: (0,) * nd)


def _inproj_kernel(x_ref, n1_ref, w_ref, bd_ref, nw_ref, rc_ref, ra_ref, rb_ref,
                   naq_ref, nak_ref, nav_ref, dq_ref, dk_ref, dv_ref):
    x = x_ref[...]
    ms = jnp.mean(x * x, axis=-1, keepdims=True)
    h = (x * lax.rsqrt(ms + EPS) * n1_ref[...]).astype(BF16)
    bd = bd_ref[...]

    def proj(c):
        return jnp.dot(h, w_ref[:, c * SECTION:(c + 1) * SECTION], preferred_element_type=F32)

    def qknorm(y, row):
        sq = y * y
        hi = sq.astype(BF16)
        lo = (sq - hi.astype(F32)).astype(BF16)
        msq = (jnp.dot(hi, bd, preferred_element_type=F32)
               + jnp.dot(lo, bd, preferred_element_type=F32))
        return y * lax.rsqrt(msq + EPS) * nw_ref[row:row + 1, :]

    def rope(y):
        return (y * rc_ref[...]
                + pltpu.roll(y, SECTION - ROT_DIM // 2, axis=1) * ra_ref[...]
                + pltpu.roll(y, ROT_DIM // 2, axis=1) * rb_ref[...])

    scale = HEAD_DIM ** -0.5 * LOG2E
    naq_ref[...] = (qknorm(proj(0), 0) * scale).astype(BF16)
    nak_ref[...] = qknorm(proj(1), 1).astype(BF16)
    nav_ref[...] = proj(2).astype(BF16)
    dq_ref[...] = (rope(qknorm(proj(3), 2)) * scale).astype(BF16)
    dk_ref[...] = rope(qknorm(proj(4), 3)).astype(BF16)
    dv_ref[...] = proj(5).astype(BF16)


def _rope_tables(seq):
    pos = jnp.arange(seq, dtype=F32)
    inv = ROPE_THETA ** (-jnp.arange(0, ROT_DIM, 2, dtype=F32) / ROT_DIM)
    ang = pos[:, None] * inv[None, :]
    cos, sin = jnp.cos(ang), jnp.sin(ang)
    half = ROT_DIM // 2
    pad = HEAD_DIM - ROT_DIM
    c_head = jnp.concatenate([cos, cos, jnp.ones((seq, pad), F32)], -1)
    a_head = jnp.concatenate([-sin, jnp.zeros((seq, pad + half), F32)], -1)
    b_head = jnp.concatenate([jnp.zeros((seq, half), F32), sin, jnp.zeros((seq, pad), F32)], -1)
    reps = SECTION // HEAD_DIM
    return (jnp.tile(c_head, (1, reps)), jnp.tile(a_head, (1, reps)), jnp.tile(b_head, (1, reps)))


def _in_projection(x2d, seq, norm1_w, w_in_bf, qk_w, rope_tabs):
    n = x2d.shape[0]
    tm = min(TM_IN, seq)
    steps_per_seq = seq // tm
    gid = np.arange(SECTION) // HEAD_DIM
    bd = jnp.asarray((gid[:, None] == gid[None, :]).astype(np.float32) / HEAD_DIM, BF16)
    tok = pl.BlockSpec((tm, SECTION), lambda i: (i, 0))
    pos = pl.BlockSpec((tm, SECTION), lambda i: (i % steps_per_seq, 0))
    out = jax.ShapeDtypeStruct((n, SECTION), BF16)
    return pl.pallas_call(
        _inproj_kernel,
        grid=(n // tm,),
        in_specs=[pl.BlockSpec((tm, D_MODEL), lambda i: (i, 0)),
                  _const_spec((1, D_MODEL)),
                  _const_spec((D_MODEL, 6 * SECTION)),
                  _const_spec((SECTION, SECTION)),
                  _const_spec((4, SECTION)),
                  pos, pos, pos],
        out_specs=[tok] * 6,
        out_shape=[out] * 6,
        compiler_params=_cparams(("parallel",)),
    )(x2d, norm1_w.reshape(1, D_MODEL), w_in_bf, bd, qk_w, *rope_tabs)


def _na_bias(rpb):
    c = np.arange(GRID_W)[:, None]
    kc = np.arange(GRID_W)[None, :]
    cs = np.clip(c - NA_KW // 2, 0, GRID_W - NA_KW)
    valid = (kc >= cs) & (kc < cs + NA_KW)
    rel_c = kc - c + (NA_KW - 1)
    onehot = ((np.arange(2 * NA_KW - 1)[:, None, None] == rel_c[None]) & valid[None]).astype(np.float32)
    col = jnp.einsum("hrx,xck->hrck", rpb.astype(F32), onehot, precision=lax.Precision.HIGHEST)
    col = jnp.where(valid, col * LOG2E, NEG_BIG)
    per_off = [col[:, NA_KH - 1 - o:2 * NA_KH - 1 - o] for o in range(NA_KH)]
    b = jnp.stack(per_off).transpose(0, 1, 3, 2, 4)
    return b.reshape(NA_KH, NA_HEADS, GRID_W, NA_KH * GRID_W)


def _na_kernel(q_ref, k_ref, v_ref, b_ref, o_ref, *, rows):
    j = pl.program_id(1)
    win = NA_KH * GRID_W
    pair = 2 * HEAD_DIM
    lane = lax.broadcasted_iota(jnp.int32, (GRID_W, pair), 1)
    first = lane < HEAD_DIM
    ones = jnp.ones((win, pair), BF16)

    def one_row(rr):
        r = j * NA_ROWS_PER_STEP + rr
        rs = jnp.clip(r - NA_KH // 2, 0, rows - NA_KH)
        off = r - rs
        kstart = pl.multiple_of(rs * GRID_W, GRID_W)
        qstart = pl.multiple_of(rr * GRID_W, GRID_W)
        outs = []
        for hp in range(NA_HEADS // 2):
            sl = slice(hp * pair, (hp + 1) * pair)
            qp = q_ref[pl.ds(qstart, GRID_W), sl]
            kp = k_ref[pl.ds(kstart, win), sl]
            v_ext = jnp.concatenate([v_ref[pl.ds(kstart, win), sl], ones], axis=1)
            halves = []
            for hh in range(2):
                qm = jnp.where(first if hh == 0 else ~first, qp, jnp.zeros_like(qp))
                s = lax.dot_general(qm, kp, (((1,), (1,)), ((), ())), preferred_element_type=F32)
                s = s + b_ref[off, 2 * hp + hh]
                p = jnp.exp2(s - jnp.max(s, axis=-1, keepdims=True)).astype(BF16)
                z = jnp.dot(p, v_ext, preferred_element_type=F32)
                halves.append(z[:, :pair] / z[:, pair:pair + 1])
            outs.append(jnp.where(first, halves[0], halves[1]))
        o_ref[pl.ds(qstart, GRID_W), :] = jnp.concatenate(outs, axis=-1).astype(BF16)

    def row_body(i, carry):
        for u in range(NA_ROW_UNROLL):
            one_row(i * NA_ROW_UNROLL + u)
        return carry

    lax.fori_loop(0, NA_ROWS_PER_STEP // NA_ROW_UNROLL, row_body, 0)


def _neighbourhood_attention(q, k, v, bias, batch, seq):
    rows = seq // GRID_W
    assert rows >= NA_KH and rows % NA_ROWS_PER_STEP == 0
    nblk = rows // NA_ROWS_PER_STEP
    tq = NA_ROWS_PER_STEP * GRID_W
    qspec = pl.BlockSpec((tq, SECTION), lambda b, j: (b * nblk + j, 0))
    kvspec = pl.BlockSpec((seq, SECTION), lambda b, j: (b, 0))
    return pl.pallas_call(
        functools.partial(_na_kernel, rows=rows),
        grid=(batch, nblk),
        in_specs=[qspec, kvspec, kvspec, _const_spec(bias.shape)],
        out_specs=qspec,
        out_shape=jax.ShapeDtypeStruct(q.shape, BF16),
        compiler_params=_cparams(("parallel", "arbitrary")),
    )(q, k, v, bias)


def _diff_kernel(q_ref, k_ref, v_ref, lam_ref, sub_ref, o_ref, m0_scr, m1_scr, acc0_scr, acc1_scr,
                 *, seq, tq, tk):
    m_scrs, acc_scrs = (m0_scr, m1_scr), (acc0_scr, acc1_scr)
    q = q_ref[...]
    lane = lax.broadcasted_iota(jnp.int32, q.shape, 1)
    zero = jnp.zeros_like(q)
    q_maps = (jnp.where(lane < HEAD_DIM, q, zero), jnp.where(lane >= HEAD_DIM, q, zero))

    lp = lam_ref[...]
    lam = (jnp.exp(jnp.sum(lp[0:1] * lp[1:2], axis=-1, keepdims=True))
           - jnp.exp(jnp.sum(lp[2:3] * lp[3:4], axis=-1, keepdims=True)) + LAMBDA_INIT)

    width = 2 * HEAD_DIM
    ones = jnp.ones((tk, width), BF16)

    for m_scr, acc_scr in zip(m_scrs, acc_scrs):
        m_scr[...] = jnp.full(m_scr.shape, -jnp.inf, F32)
        acc_scr[...] = jnp.zeros(acc_scr.shape, F32)

    def kv_body(c, carry):
        tks = tk // KV_SUBCHUNKS
        kcs, v_exts = [], []
        for sub in range(KV_SUBCHUNKS):
            start = pl.multiple_of(c * tk + sub * tks, tks)
            kcs.append(k_ref[pl.ds(start, tks), :])
            v_exts.append(jnp.concatenate([v_ref[pl.ds(start, tks), :], ones[:tks]], axis=1))
        scores = [[lax.dot_general(q_maps[mp], kcs[sub], (((1,), (1,)), ((), ())),
                                   preferred_element_type=F32) for mp in range(2)]
                  for sub in range(KV_SUBCHUNKS)]
        m = [m_scrs[mp][...] for mp in range(2)]
        probs, alphas = [], []
        for sub in range(KV_SUBCHUNKS):
            probs.append([])
            alphas.append([])
            for mp in range(2):
                mn = jnp.maximum(m[mp], jnp.max(scores[sub][mp], axis=-1, keepdims=True))
                probs[sub].append(jnp.exp2(scores[sub][mp] - mn[:, 0:1]).astype(BF16))
                alpha = jnp.exp2(m[mp] - mn)
                alphas[sub].append(jnp.concatenate([alpha, alpha], axis=1))
                m[mp] = mn
        for mp in range(2):
            m_scrs[mp][...] = m[mp]
            acc = acc_scrs[mp][...]
            for sub in range(KV_SUBCHUNKS):
                acc = alphas[sub][mp] * acc + jnp.dot(probs[sub][mp], v_exts[sub],
                                                      preferred_element_type=F32)
            acc_scrs[mp][...] = acc
        return carry

    lax.fori_loop(0, seq // tk, kv_body, 0)
    a0, a1 = acc0_scr[...], acc1_scr[...]
    o = a0[:, :width] / a0[:, width:width + 1] - lam * (a1[:, :width] / a1[:, width:width + 1])
    ms = jnp.mean(o * o, axis=-1, keepdims=True)
    y = o * lax.rsqrt(ms + EPS) * sub_ref[...]
    o_ref[...] = (y * (1.0 - LAMBDA_INIT)).astype(BF16)


def _diff_attention(q, k, v, lam_params, subln_w, batch, seq):
    tq, tk = min(TQ, seq), min(TK, seq)
    nq = seq // tq
    width = 2 * HEAD_DIM
    qspec = pl.BlockSpec((tq, width), lambda b, h, i: (b * nq + i, h))
    kvspec = pl.BlockSpec((seq, width), lambda b, h, i: (b, h))
    return pl.pallas_call(
        functools.partial(_diff_kernel, seq=seq, tq=tq, tk=tk),
        grid=(batch, DIFF_HEADS, nq),
        in_specs=[qspec, kvspec, kvspec, _const_spec((4, HEAD_DIM)), _const_spec((1, width))],
        out_specs=qspec,
        out_shape=jax.ShapeDtypeStruct(q.shape, BF16),
        scratch_shapes=[pltpu.VMEM((tq, width), F32), pltpu.VMEM((tq, width), F32),
                        pltpu.VMEM((tq, 2 * width), F32), pltpu.VMEM((tq, 2 * width), F32)],
        compiler_params=_cparams(("parallel", "parallel", "arbitrary")),
    )(q, k, v, lam_params, subln_w.reshape(1, width))


def _top16(sc, payload=None):
    iota = lax.broadcasted_iota(jnp.int32, sc.shape, 0).astype(F32)
    big = float(sc.shape[0])
    vals, picks = [], []
    for _ in range(PEER_TOPK):
        m = jnp.max(sc, axis=0, keepdims=True)
        ix = jnp.min(jnp.where(sc == m, iota, big), axis=0, keepdims=True)
        sel = iota == ix
        if payload is None:
            picks.append(ix)
        else:
            picks.append(jnp.sum(jnp.where(sel, payload, 0), axis=0, keepdims=True))
        sc = jnp.where(sel, -jnp.inf, sc)
        vals.append(m)
    picks = jnp.concatenate(picks, axis=0)
    return jnp.concatenate(vals, axis=0), picks.astype(jnp.int32)


def _pair_candidates(r0, r1):
    half = SUBLANES // 2
    groups = [r0[0:1] + r1[0:8], r0[0:1] + r1[8:16]]
    groups += [r0[a:a + 1] + r1[0:8] for a in (1, 2, 3)]
    groups += [jnp.concatenate([r0[a:a + 1] + r1[0:half], r0[a + 1:a + 2] + r1[0:half]], axis=0)
               for a in (4, 6)]
    groups.append(r0[8:16] + r1[0:1])
    return jnp.concatenate(groups, axis=0)


def _route_kernel(na_ref, df_ref, x_ref, wo_ref, n2_ref, wq_ref, sk_ref,
                  x1_ref, xn_ref, eid_ref, g_ref, q_scr, eid_scr, g_scr):
    tm = x_ref.shape[0]
    x1 = (x_ref[...]
          + jnp.dot(na_ref[...], wo_ref[0:SECTION, :], preferred_element_type=F32)
          + jnp.dot(df_ref[...], wo_ref[SECTION:2 * SECTION, :], preferred_element_type=F32))
    ms = jnp.mean(x1 * x1, axis=-1, keepdims=True)
    xn = x1 * lax.rsqrt(ms + EPS) * n2_ref[...]
    for r in range(SUBLANES):
        x1_ref[:, r, :] = x1[:, r * LANES:(r + 1) * LANES]
        xn_ref[:, r, :] = xn[:, r * LANES:(r + 1) * LANES]
    xb = xn.astype(BF16)
    for c in range(2 * PEER_HEADS):
        q_scr[c] = jnp.dot(xb, wq_ref[:, c * LANES:(c + 1) * LANES],
                           preferred_element_type=F32).astype(BF16)

    nblk = tm // TB

    def route_block(h, blk):
        tok0 = blk * TB
        tops = []
        for p in range(2):
            sc = lax.dot_general(sk_ref[2 * h + p], q_scr[2 * h + p, pl.ds(tok0, TB), :],
                                 (((1,), (1,)), ((), ())), preferred_element_type=F32)
            tops.append(_top16(sc))
        (s0, i0), (s1, i1) = tops
        cand_s = _pair_candidates(s0, s1)
        cand_i = _pair_candidates(i0 * N_KEYS, i1)
        top_s, eid = _top16(cand_s, cand_i)
        e = jnp.exp(top_s - top_s[0:1])
        row = pl.multiple_of(h * PEER_TOPK, PEER_TOPK)
        eid_scr[blk, pl.ds(row, PEER_TOPK), :] = eid * ROW_SUB
        g_scr[blk, pl.ds(row, PEER_TOPK), :] = e / jnp.sum(e, axis=0, keepdims=True)

    def head_body(h, carry):
        for blk in range(nblk):
            route_block(h, blk)
        return carry

    lax.fori_loop(0, PEER_HEADS, head_body, 0)
    for blk in range(nblk):
        eid_ref[blk] = eid_scr[blk].T
        g_ref[blk * TB:(blk + 1) * TB, :] = g_scr[blk].T


def _out_and_route(na, df, x2d, w_out_bf, norm2_w, w_q_bf, sk_bf):
    n = x2d.shape[0]
    tm = TM_OUT
    tok = lambda w: pl.BlockSpec((tm, w), lambda i: (i, 0))
    tiles = pl.BlockSpec((tm, SUBLANES, LANES), lambda i: (i, 0, 0))
    return pl.pallas_call(
        _route_kernel,
        grid=(n // tm,),
        in_specs=[tok(SECTION), tok(SECTION), tok(D_MODEL),
                  _const_spec((D_MODEL, D_MODEL)), _const_spec((1, D_MODEL)),
                  _const_spec(w_q_bf.shape), _const_spec(sk_bf.shape)],
        out_specs=[tiles, tiles,
                   pl.BlockSpec((tm // TB, TB, HK), lambda i: (i, 0, 0)), tok(HK)],
        out_shape=[jax.ShapeDtypeStruct((n, SUBLANES, LANES), F32),
                   jax.ShapeDtypeStruct((n, SUBLANES, LANES), F32),
                   jax.ShapeDtypeStruct((n // TB, TB, HK), jnp.int32),
                   jax.ShapeDtypeStruct((n, HK), F32)],
        scratch_shapes=[pltpu.VMEM((2 * PEER_HEADS, tm, LANES), BF16),
                        pltpu.VMEM((tm // TB, HK, TB), jnp.int32), pltpu.VMEM((tm // TB, HK, TB), F32)],
        compiler_params=_cparams(("parallel",)),
    )(na, df, x2d, w_out_bf, norm2_w.reshape(1, D_MODEL), w_q_bf, sk_bf)


ROW_SUB = SUBLANES // 2
CHUNK_SLOTS = 32
N_CHUNKS = HK // CHUNK_SLOTS
CHUNK_ROWS = CHUNK_SLOTS * SUBLANES
TOKEN_UNROLL = 32
ROW_WORDS = ROW_SUB * LANES
SC_CORES, SC_SUBCORES, SC_LANES = 2, 16, 16
SC_WORKERS = SC_CORES * SC_SUBCORES


PACK_EXPERTS = 512


def _pack_kernel(x_ref, o_ref):
    x = x_ref[...]
    for c in range(ROW_SUB):
        lo = x[:, (2 * c) * LANES:(2 * c + 1) * LANES].astype(BF16).astype(F32)
        hi = x[:, (2 * c + 1) * LANES:(2 * c + 2) * LANES].astype(BF16).astype(F32)
        words = (pltpu.bitcast(lo, jnp.uint32) >> 16) | (pltpu.bitcast(hi, jnp.uint32) & jnp.uint32(0xFFFF0000))
        o_ref[pl.ds(c, PACK_EXPERTS, stride=ROW_SUB), :] = words


def _pack_table(tab):
    experts = tab.shape[0]
    return pl.pallas_call(
        _pack_kernel,
        grid=(experts // PACK_EXPERTS,),
        in_specs=[pl.BlockSpec((PACK_EXPERTS, D_MODEL), lambda i: (i, 0))],
        out_specs=pl.BlockSpec((PACK_EXPERTS * ROW_SUB, LANES), lambda i: (i, 0)),
        out_shape=jax.ShapeDtypeStruct((experts * ROW_SUB, LANES), jnp.uint32),
        compiler_params=_cparams(("parallel",)),
    )(tab)


def _split_bf16(x):
    hi = x.astype(BF16)
    return hi, (x - hi.astype(F32)).astype(BF16)


def _gather_chunk(tab_ref, eid_ref, t, chunk):
    ids = eid_ref.at[0, t]
    rows = [tab_ref[pl.ds(pl.multiple_of(ids[chunk * CHUNK_SLOTS + i], ROW_SUB), ROW_SUB), :]
            for i in range(CHUNK_SLOTS)]
    return pltpu.bitcast(jnp.concatenate(rows, axis=0), BF16)


def _diag_mask(width):
    r = lax.broadcasted_iota(jnp.int32, (SUBLANES, width), 0)
    n = lax.broadcasted_iota(jnp.int32, (SUBLANES, width), 1)
    return (n % SUBLANES) == r


def _peer_u_kernel(eid_ref, xn_ref, g_ref, tab_ref, rept_ref, rep_ref, rep16_ref, wrep_ref, w16_ref, d_scr):
    mask = _diag_mask(CHUNK_ROWS)
    zeros = jnp.zeros((SUBLANES, LANES), BF16)

    def one_token(t):
        xhi, xlo = _split_bf16(xn_ref[t])
        lhs = jnp.concatenate([jnp.concatenate([xhi, zeros], axis=1),
                               jnp.concatenate([xlo, zeros], axis=1),
                               jnp.concatenate([zeros, xhi], axis=1),
                               jnp.concatenate([zeros, xlo], axis=1)], axis=0)
        for pr in range(N_CHUNKS // 2):
            s = jnp.concatenate([_gather_chunk(tab_ref, eid_ref, t, 2 * pr),
                                 _gather_chunk(tab_ref, eid_ref, t, 2 * pr + 1)], axis=1)
            z = lax.dot_general(lhs, s, (((1,), (1,)), ((), ())), preferred_element_type=F32)
            for half in range(2):
                z8 = z[16 * half:16 * half + 8] + z[16 * half + 8:16 * half + 16]
                zs = jnp.sum(jnp.where(mask, z8, 0.0), axis=0, keepdims=True)
                d_scr[pl.ds(t, 1), pl.ds((2 * pr + half) * CHUNK_ROWS, CHUNK_ROWS)] = zs

    def tok_body(i, carry):
        for u in range(TOKEN_UNROLL):
            one_token(i * TOKEN_UNROLL + u)
        return carry

    lax.fori_loop(0, PB // TOKEN_UNROLL, tok_body, 0)
    _gate_outputs(d_scr[...], rept_ref, g_ref, rep_ref, rep16_ref, wrep_ref, w16_ref)


def _gate_outputs(partial, sum_ref, g_ref, rep_ref, rep16_ref, wrep_ref, w16_ref):
    dhi, dlo = _split_bf16(partial)
    a = (jnp.dot(dhi, sum_ref[...], preferred_element_type=F32)
         + jnp.dot(dlo, sum_ref[...], preferred_element_type=F32))
    w = g_ref[...] * (0.5 * a * (1.0 + lax.erf(a * (2.0 ** -0.5))))
    whi, wlo = _split_bf16(w)
    wrep_ref[...] = (jnp.dot(whi, rep_ref[...], preferred_element_type=F32)
                     + jnp.dot(wlo, rep_ref[...], preferred_element_type=F32))
    w16_ref[...] = (jnp.dot(whi, rep16_ref[...], preferred_element_type=F32)
                    + jnp.dot(wlo, rep16_ref[...], preferred_element_type=F32))


def _gate_outputs_kernel(part_ref, g_ref, sum_ref, rep_ref, rep16_ref, wrep_ref, w16_ref):
    _gate_outputs(part_ref[...], sum_ref, g_ref, rep_ref, rep16_ref, wrep_ref, w16_ref)


def _gate_kernel(part_ref, g_ref, sum_ref, rep_ref, rep16_ref, wrep_in, w16_in, wrep_ref, w16_ref):
    del wrep_in, w16_in
    _gate_outputs(part_ref[...], sum_ref, g_ref, rep_ref, rep16_ref, wrep_ref, w16_ref)


def _peer_v_kernel(eid_ref, wrep_ref, x1_ref, tab_ref, o_ref):
    mask = _diag_mask(CHUNK_ROWS)

    def one_token(t):
        wr = wrep_ref[pl.ds(t, 1), :]
        acc = x1_ref[t]
        for pr in range(N_CHUNKS // 2):
            parts = []
            for half in range(2):
                seg = wr[:, (2 * pr + half) * CHUNK_ROWS:(2 * pr + half + 1) * CHUNK_ROWS]
                parts.extend(_split_bf16(jnp.where(mask, jnp.broadcast_to(seg, mask.shape), 0.0)))
            lhs = jnp.concatenate(parts, axis=0)
            s = jnp.concatenate([_gather_chunk(tab_ref, eid_ref, t, 2 * pr),
                                 _gather_chunk(tab_ref, eid_ref, t, 2 * pr + 1)], axis=1)
            z = jnp.dot(lhs, s, preferred_element_type=F32)
            acc = acc + ((z[0:8, :LANES] + z[8:16, :LANES]) + (z[16:24, LANES:] + z[24:32, LANES:]))
        return acc

    def tok_body(i, carry):
        accs = [one_token(i * TOKEN_UNROLL + u) for u in range(TOKEN_UNROLL)]
        base = pl.multiple_of(i * TOKEN_UNROLL, TOKEN_UNROLL)
        for r in range(SUBLANES):
            o_ref[pl.ds(base, TOKEN_UNROLL), r * LANES:(r + 1) * LANES] = jnp.concatenate(
                [acc[r:r + 1, :] for acc in accs], axis=0)
        return carry

    lax.fori_loop(0, PB // TOKEN_UNROLL, tok_body, 0)


def _sc_scratch(mid_shape, last_shape):
    half = HK // 2
    per_parity = [pltpu.VMEM((2, half), jnp.int32), pltpu.VMEM(mid_shape, F32), pltpu.VMEM(last_shape, F32)]
    return per_parity + per_parity + [
        pltpu.VMEM((half, ROW_WORDS), jnp.uint32), pltpu.VMEM((half, ROW_WORDS), jnp.uint32),
        pltpu.SemaphoreType.DMA, pltpu.SemaphoreType.DMA,
        pltpu.SemaphoreType.DMA, pltpu.SemaphoreType.DMA,
        pltpu.SemaphoreType.DMA, pltpu.SemaphoreType.DMA,
    ]


def _sc_u_phase(n_tok, tok0):
    per_w = n_tok // SC_WORKERS
    assert per_w * SC_WORKERS == n_tok and per_w % 2 == 0
    half = HK // 2
    mesh = plsc.VectorSubcoreMesh(core_axis_name="c", subcore_axis_name="s")

    @functools.partial(
        pl.kernel, mesh=mesh,
        out_type=jax.ShapeDtypeStruct((n_tok, HK * SC_LANES), F32),
        scratch_types=_sc_scratch((SUBLANES, LANES), (HK * SC_LANES,)),
        compiler_params=pltpu.CompilerParams(needs_layout_passes=False),
    )
    def sc_kernel(tab_hbm, eid_hbm, xn_hbm, out_hbm,
                  idx_a, x_a, a_a, idx_b, x_b, a_b, g0, g1, s_a, s_b, sg0, sg1, so_a, so_b):
        wid = lax.axis_index("s") * SC_CORES + lax.axis_index("c")
        first = wid * per_w
        par = ((idx_a, x_a, a_a, s_a, so_a), (idx_b, x_b, a_b, s_b, so_b))
        gbuf, gsem = (g0, g1), (sg0, sg1)

        def input_copies(t, p):
            idx_v, x_v, _, sem, _ = par[p]
            return (pltpu.make_async_copy(eid_hbm.at[tok0 + t], idx_v, sem),
                    pltpu.make_async_copy(xn_hbm.at[tok0 + t], x_v, sem))

        def gather(p, h):
            return pltpu.make_async_copy(tab_hbm.at[par[p][0].at[h]], gbuf[h], gsem[h])

        def out_copy(t, p):
            return pltpu.make_async_copy(par[p][2], out_hbm.at[t], par[p][4])

        def dot_half(p, h):
            _, x_v, a_v, _, _ = par[p]
            g = gbuf[h]
            groups = LANES // SC_LANES
            block = SUBLANES
            for rb in range(half // block):
                def col_body(k, accs):
                    c, kk = k // groups, k % groups
                    lane0 = pl.multiple_of(kk * SC_LANES, SC_LANES)
                    xl = x_v[2 * c, pl.ds(lane0, SC_LANES)]
                    xh = x_v[2 * c + 1, pl.ds(lane0, SC_LANES)]
                    out = []
                    for r in range(block):
                        words = g[rb * block + r, pl.ds(pl.multiple_of(k * SC_LANES, SC_LANES), SC_LANES)]
                        lov = plsc.bitcast(lax.shift_left(words, jnp.uint32(16)), F32)
                        hiv = plsc.bitcast(words & jnp.uint32(0xFFFF0000), F32)
                        out.append(accs[r] + lov * xl + hiv * xh)
                    return tuple(out)

                zero = jnp.zeros((SC_LANES,), F32)
                accs = lax.fori_loop(0, ROW_WORDS // SC_LANES, col_body, (zero,) * block)
                for r in range(block):
                    a_v[pl.ds((h * half + rb * block + r) * SC_LANES, SC_LANES)] = accs[r]

        def token(t, p):
            q = 1 - p
            has_next = t + 1 < first + per_w
            gather(p, 0).wait()
            gather(p, 1).start()

            @pl.when(has_next)
            def _():
                for cp in input_copies(t + 1, q):
                    cp.start()

            @pl.when(t >= first + 2)
            def _():
                out_copy(t - 2, p).wait()

            dot_half(p, 0)
            gather(p, 1).wait()

            @pl.when(has_next)
            def _():
                for cp in input_copies(t + 1, q):
                    cp.wait()
                gather(q, 0).start()

            dot_half(p, 1)
            out_copy(t, p).start()

        for cp in input_copies(first, 0):
            cp.start()
        for cp in input_copies(first, 0):
            cp.wait()
        gather(0, 0).start()

        def pair_body(i, carry):
            token(first + 2 * i, 0)
            token(first + 2 * i + 1, 1)
            return carry

        lax.fori_loop(0, per_w // 2, pair_body, 0)
        out_copy(first + per_w - 2, 0).wait()
        out_copy(first + per_w - 1, 1).wait()

    return sc_kernel


def _sc_v_phase(n_tok, tok0):
    per_w = n_tok // SC_WORKERS
    assert per_w * SC_WORKERS == n_tok and per_w % 2 == 0
    half = HK // 2
    mesh = plsc.VectorSubcoreMesh(core_axis_name="c", subcore_axis_name="s")

    @functools.partial(
        pl.kernel, mesh=mesh,
        out_type=jax.ShapeDtypeStruct((n_tok, SUBLANES, LANES), F32),
        scratch_types=_sc_scratch((HK * SC_LANES,), (SUBLANES, LANES)),
        compiler_params=pltpu.CompilerParams(needs_layout_passes=False),
    )
    def sc_kernel(tab_hbm, eid_hbm, w16_hbm, x1_hbm, out_hbm,
                  idx_a, w_a, x_a, idx_b, w_b, x_b, g0, g1, s_a, s_b, sg0, sg1, so_a, so_b):
        wid = lax.axis_index("s") * SC_CORES + lax.axis_index("c")
        first = wid * per_w
        par = ((idx_a, w_a, x_a, s_a, so_a), (idx_b, w_b, x_b, s_b, so_b))
        gbuf, gsem = (g0, g1), (sg0, sg1)

        def input_copies(t, p):
            idx_v, w_v, x_v, sem, _ = par[p]
            return (pltpu.make_async_copy(eid_hbm.at[tok0 + t], idx_v, sem),
                    pltpu.make_async_copy(w16_hbm.at[tok0 + t], w_v, sem),
                    pltpu.make_async_copy(x1_hbm.at[tok0 + t], x_v, sem))

        def gather(p, h):
            return pltpu.make_async_copy(tab_hbm.at[par[p][0].at[h]], gbuf[h], gsem[h])

        def out_copy(t, p):
            return pltpu.make_async_copy(par[p][2], out_hbm.at[t], par[p][4])

        def accumulate_half(p, h):
            _, w_v, x_v, _, _ = par[p]
            g = gbuf[h]
            groups = LANES // SC_LANES
            for c in range(ROW_SUB):
                lo0 = tuple(x_v[2 * c, pl.ds(SC_LANES * kk, SC_LANES)] for kk in range(groups))
                hi0 = tuple(x_v[2 * c + 1, pl.ds(SC_LANES * kk, SC_LANES)] for kk in range(groups))

                def row_body(j, acc):
                    lo, hi = acc
                    wj = w_v[pl.ds(pl.multiple_of((h * half + j) * SC_LANES, SC_LANES), SC_LANES)]
                    nlo, nhi = [], []
                    for kk in range(groups):
                        words = g[j, pl.ds(LANES * c + SC_LANES * kk, SC_LANES)]
                        lov = plsc.bitcast(lax.shift_left(words, jnp.uint32(16)), F32)
                        hiv = plsc.bitcast(words & jnp.uint32(0xFFFF0000), F32)
                        nlo.append(lo[kk] + wj * lov)
                        nhi.append(hi[kk] + wj * hiv)
                    return tuple(nlo), tuple(nhi)

                lo, hi = lax.fori_loop(0, half, row_body, (lo0, hi0))
                for kk in range(groups):
                    x_v[2 * c, pl.ds(SC_LANES * kk, SC_LANES)] = lo[kk]
                    x_v[2 * c + 1, pl.ds(SC_LANES * kk, SC_LANES)] = hi[kk]

        def token(t, p):
            q = 1 - p
            has_next = t + 1 < first + per_w
            gather(p, 0).wait()
            gather(p, 1).start()

            @pl.when(has_next)
            def _():
                @pl.when(t >= first + 1)
                def _():
                    out_copy(t - 1, q).wait()
                for cp in input_copies(t + 1, q):
                    cp.start()

            accumulate_half(p, 0)
            gather(p, 1).wait()

            @pl.when(has_next)
            def _():
                for cp in input_copies(t + 1, q):
                    cp.wait()
                gather(q, 0).start()

            accumulate_half(p, 1)
            out_copy(t, p).start()

        for cp in input_copies(first, 0):
            cp.start()
        for cp in input_copies(first, 0):
            cp.wait()
        gather(0, 0).start()

        def pair_body(i, carry):
            token(first + 2 * i, 0)
            token(first + 2 * i + 1, 1)
            return carry

        lax.fori_loop(0, per_w // 2, pair_body, 0)
        out_copy(first + per_w - 2, 0).wait()
        out_copy(first + per_w - 1, 1).wait()

    return sc_kernel


def _peer_experts(eid, g, xn3, x13, u_pack, v_pack, n_sc_u, n_sc_v):
    n = xn3.shape[0]
    n_tc_u, n_tc_v = n - n_sc_u, n - n_sc_v
    assert n_tc_u % PB == 0 and n_tc_v % PB == 0
    eid = eid.reshape(n // PB, PB, HK)
    smem_slot = pl.BlockSpec((1, PB, HK), lambda i: (i, 0, 0), memory_space=pltpu.SMEM)
    tok = pl.BlockSpec((PB, SUBLANES, LANES), lambda i: (i, 0, 0))
    flat = lambda w, off=0: pl.BlockSpec((PB, w), lambda i: (i + off, 0))
    table = pl.BlockSpec(u_pack.shape, lambda i: (0, 0), pipeline_mode=pl.Buffered(1))

    def repeat_matrix(times):
        m = (np.arange(HK)[:, None] == (np.arange(HK * times)[None, :] // times)).astype(np.float32)
        return jnp.asarray(m, BF16)

    rep, rep16 = repeat_matrix(SUBLANES), repeat_matrix(SC_LANES)
    gate_shapes = [jax.ShapeDtypeStruct((n, D_MODEL), F32), jax.ShapeDtypeStruct((n, HK * SC_LANES), F32)]
    expert = eid.reshape(n, 2, HK // 2) >> (ROW_SUB.bit_length() - 1)
    if n_sc_u:
        u_rows = u_pack.reshape(N_EXPERTS, ROW_WORDS)
        partial = _sc_u_phase(n_sc_u, n_tc_u)(u_rows, expert, xn3)
    if n_tc_u:
        wrep, w16 = pl.pallas_call(
            _peer_u_kernel,
            grid=(n_tc_u // PB,),
            in_specs=[smem_slot, tok, flat(HK), table, _const_spec(rep.T.shape), _const_spec(rep.shape),
                      _const_spec(rep16.shape)],
            out_specs=[flat(D_MODEL), flat(HK * SC_LANES)],
            out_shape=gate_shapes,
            scratch_shapes=[pltpu.VMEM((PB, D_MODEL), F32)],
            compiler_params=_cparams(("arbitrary",)),
        )(eid, xn3, g, u_pack, rep.T, rep, rep16)
    if n_sc_u:
        off = n_tc_u // PB
        gate_in = [flat(HK * SC_LANES), flat(HK, off), _const_spec(rep16.T.shape), _const_spec(rep.shape),
                   _const_spec(rep16.shape)]
        gate_args = (partial, g, rep16.T, rep, rep16)
        if n_tc_u:
            anyspec = pl.BlockSpec(memory_space=pl.ANY)
            gate_fn, gate_in, gate_args = _gate_kernel, gate_in + [anyspec, anyspec], gate_args + (wrep, w16)
            aliases = {5: 0, 6: 1}
        else:
            gate_fn, aliases = _gate_outputs_kernel, {}
        wrep, w16 = pl.pallas_call(
            gate_fn,
            grid=(n_sc_u // PB,),
            in_specs=gate_in,
            out_specs=[flat(D_MODEL, off), flat(HK * SC_LANES, off)],
            out_shape=gate_shapes,
            input_output_aliases=aliases,
            compiler_params=_cparams(("arbitrary",)),
        )(*gate_args)
    parts = []
    if n_tc_v:
        parts.append(pl.pallas_call(
            _peer_v_kernel,
            grid=(n_tc_v // PB,),
            in_specs=[smem_slot, flat(D_MODEL), tok, table],
            out_specs=flat(D_MODEL),
            out_shape=jax.ShapeDtypeStruct((n_tc_v, D_MODEL), F32),
            compiler_params=_cparams(("arbitrary",)),
        )(eid, wrep, x13, v_pack))
    if n_sc_v:
        v_rows = v_pack.reshape(N_EXPERTS, ROW_WORDS)
        parts.append(_sc_v_phase(n_sc_v, n_tc_v)(v_rows, expert, w16, x13).reshape(n_sc_v, D_MODEL))
    y = parts[0] if len(parts) == 1 else jnp.concatenate(parts, axis=0)
    return y, w16


def _encoder_layer(x, p, after):
    if after is not None:
        x, _ = lax.optimization_barrier((x, after))
    batch, seq, _ = x.shape
    n = batch * seq
    x2d = x.reshape(n, D_MODEL)
    naq, nak, nav, dq, dk, dv = _in_projection(x2d, seq, p["norm1_w"], p["w_in"], p["qk_w"], p["rope"])
    na = _neighbourhood_attention(naq, nak, nav, p["na_bias"], batch, seq)
    df = _diff_attention(dq, dk, dv, p["lam"], p["subln_w"], batch, seq)
    x1, xn, eid, g = _out_and_route(na, df, x2d, p["w_out"], p["norm2_w"], p["w_q"], p["sub_keys"])
    y, w16 = _peer_experts(eid, g, xn, x1, p["u_pack"], p["v_pack"], *SC_TOKENS.get(n, (0, 0)))
    return y.reshape(batch, seq, D_MODEL), w16


def kernel(x_prompt, x_sample, norm1_w, w_in, na_q_norm, na_k_norm, na_rpb, diff_q_norm, diff_k_norm,
           diff_lambda_q1, diff_lambda_k1, diff_lambda_q2, diff_lambda_k2, diff_subln_w, w_out, norm2_w,
           peer_w_q, peer_sub_keys, peer_u, peer_v):
    assert norm1_w.shape[0] == 1, "single-layer problem"
    reps = SECTION // HEAD_DIM
    params = {
        "norm1_w": norm1_w[0],
        "w_in": w_in[0].astype(BF16),
        "qk_w": jnp.stack([jnp.tile(w[0], reps) for w in (na_q_norm, na_k_norm, diff_q_norm, diff_k_norm)]),
        "lam": jnp.stack([diff_lambda_q1[0], diff_lambda_k1[0], diff_lambda_q2[0], diff_lambda_k2[0]]),
        "subln_w": diff_subln_w[0],
        "w_out": w_out[0].astype(BF16),
        "norm2_w": norm2_w[0],
        "w_q": peer_w_q[0].astype(BF16),
        "sub_keys": peer_sub_keys[0].reshape(2 * PEER_HEADS, N_KEYS, N_KEYS).astype(BF16),
        "na_bias": _na_bias(na_rpb[0]),
        "u_pack": _pack_table(peer_u[0]),
        "v_pack": _pack_table(peer_v[0]),
    }
    rope_by_seq = {}
    outs = [None, None]
    marker = None
    for slot, x in sorted(enumerate((x_prompt, x_sample)), key=lambda item: -item[1].shape[0] * item[1].shape[1]):
        seq = x.shape[1]
        if seq not in rope_by_seq:
            rope_by_seq[seq] = _rope_tables(seq)
        outs[slot], marker = _encoder_layer(x, dict(params, rope=rope_by_seq[seq]), marker)
    return tuple(outs)
```

```python
import functools
import math

import jax
import jax.numpy as jnp
import numpy as np
from jax import lax
from jax.experimental import pallas as pl
from jax.experimental.pallas import tpu as pltpu
from jax.experimental.pallas import tpu_sc as plsc

F32 = jnp.float32
BF16 = jnp.bfloat16

D_MODEL = 1024
GRID_W = 64
HEAD_DIM = 64
NA_HEADS = 8
NA_KH = 8
NA_KW = 16
DIFF_HEADS = 4
SECTION = 512
ROT_DIM = HEAD_DIM // 4
ROPE_THETA = 500000.0
PEER_HEADS = 8
N_KEYS = 128
PEER_TOPK = 16
N_EXPERTS = N_KEYS * N_KEYS
HK = PEER_HEADS * PEER_TOPK
EPS = 1e-6
NEG_BIG = -1e30
LOG2E = math.log2(math.e)
LAMBDA_INIT = 0.8 - 0.6 * math.exp(-0.3 * 0)

LANES = 128
SUBLANES = 8
VMEM_LIMIT = 56 * 1024 * 1024

TM_IN = 512
NA_ROWS_PER_STEP = 8
NA_ROW_UNROLL = 4
TQ = 512
KV_SUBCHUNKS = 1
TK = 4096
TM_OUT = 1024
TB = 128
PB = 256
SC_TOKENS = {8 * 4096: (14080, 8 * 4096)}


def _cparams(sem):
    return pltpu.CompilerParams(dimension_semantics=sem, vmem_limit_bytes=VMEM_LIMIT)


def _const_spec(shape):
    nd = len(shape)
    return pl.BlockSpec(shape, lambda *_: (0,) * nd)


def _inproj_kernel(x_ref, n1_ref, w_ref, bd_ref, nw_ref, rc_ref, ra_ref, rb_ref,
                   naq_ref, nak_ref, nav_ref, dq_ref, dk_ref, dv_ref):
    x = x_ref[...]
    ms = jnp.mean(x * x, axis=-1, keepdims=True)
    h = (x * lax.rsqrt(ms + EPS) * n1_ref[...]).astype(BF16)
    bd = bd_ref[...]

    def proj(c):
        return jnp.dot(h, w_ref[:, c * SECTION:(c + 1) * SECTION], preferred_element_type=F32)

    def qknorm(y, row):
        sq = y * y
        hi = sq.astype(BF16)
        lo = (sq - hi.astype(F32)).astype(BF16)
        msq = (jnp.dot(hi, bd, preferred_element_type=F32)
               + jnp.dot(lo, bd, preferred_element_type=F32))
        return y * lax.rsqrt(msq + EPS) * nw_ref[row:row + 1, :]

    def rope(y):
        return (y * rc_ref[...]
                + pltpu.roll(y, SECTION - ROT_DIM // 2, axis=1) * ra_ref[...]
                + pltpu.roll(y, ROT_DIM // 2, axis=1) * rb_ref[...])

    scale = HEAD_DIM ** -0.5 * LOG2E
    naq_ref[...] = (qknorm(proj(0), 0) * scale).astype(BF16)
    nak_ref[...] = qknorm(proj(1), 1).astype(BF16)
    nav_ref[...] = proj(2).astype(BF16)
    dq_ref[...] = (rope(qknorm(proj(3), 2)) * scale).astype(BF16)
    dk_ref[...] = rope(qknorm(proj(4), 3)).astype(BF16)
    dv_ref[...] = proj(5).astype(BF16)


def _rope_tables(seq):
    pos = jnp.arange(seq, dtype=F32)
    inv = ROPE_THETA ** (-jnp.arange(0, ROT_DIM, 2, dtype=F32) / ROT_DIM)
    ang = pos[:, None] * inv[None, :]
    cos, sin = jnp.cos(ang), jnp.sin(ang)
    half = ROT_DIM // 2
    pad = HEAD_DIM - ROT_DIM
    c_head = jnp.concatenate([cos, cos, jnp.ones((seq, pad), F32)], -1)
    a_head = jnp.concatenate([-sin, jnp.zeros((seq, pad + half), F32)], -1)
    b_head = jnp.concatenate([jnp.zeros((seq, half), F32), sin, jnp.zeros((seq, pad), F32)], -1)
    reps = SECTION // HEAD_DIM
    return (jnp.tile(c_head, (1, reps)), jnp.tile(a_head, (1, reps)), jnp.tile(b_head, (1, reps)))


def _in_projection(x2d, seq, norm1_w, w_in_bf, qk_w, rope_tabs):
    n = x2d.shape[0]
    tm = min(TM_IN, seq)
    steps_per_seq = seq // tm
    gid = np.arange(SECTION) // HEAD_DIM
    bd = jnp.asarray((gid[:, None] == gid[None, :]).astype(np.float32) / HEAD_DIM, BF16)
    tok = pl.BlockSpec((tm, SECTION), lambda i: (i, 0))
    pos = pl.BlockSpec((tm, SECTION), lambda i: (i % steps_per_seq, 0))
    out = jax.ShapeDtypeStruct((n, SECTION), BF16)
    return pl.pallas_call(
        _inproj_kernel,
        grid=(n // tm,),
        in_specs=[pl.BlockSpec((tm, D_MODEL), lambda i: (i, 0)),
                  _const_spec((1, D_MODEL)),
                  _const_spec((D_MODEL, 6 * SECTION)),
                  _const_spec((SECTION, SECTION)),
                  _const_spec((4, SECTION)),
                  pos, pos, pos],
        out_specs=[tok] * 6,
        out_shape=[out] * 6,
        compiler_params=_cparams(("parallel",)),
    )(x2d, norm1_w.reshape(1, D_MODEL), w_in_bf, bd, qk_w, *rope_tabs)


def _na_bias(rpb):
    c = np.arange(GRID_W)[:, None]
    kc = np.arange(GRID_W)[None, :]
    cs = np.clip(c - NA_KW // 2, 0, GRID_W - NA_KW)
    valid = (kc >= cs) & (kc < cs + NA_KW)
    rel_c = kc - c + (NA_KW - 1)
    onehot = ((np.arange(2 * NA_KW - 1)[:, None, None] == rel_c[None]) & valid[None]).astype(np.float32)
    col = jnp.einsum("hrx,xck->hrck", rpb.astype(F32), onehot, precision=lax.Precision.HIGHEST)
    col = jnp.where(valid, col * LOG2E, NEG_BIG)
    per_off = [col[:, NA_KH - 1 - o:2 * NA_KH - 1 - o] for o in range(NA_KH)]
    b = jnp.stack(per_off).transpose(0, 1, 3, 2, 4)
    return b.reshape(NA_KH, NA_HEADS, GRID_W, NA_KH * GRID_W)


def _na_kernel(q_ref, k_ref, v_ref, b_ref, o_ref, *, rows):
    j = pl.program_id(1)
    win = NA_KH * GRID_W
    pair = 2 * HEAD_DIM
    lane = lax.broadcasted_iota(jnp.int32, (GRID_W, pair), 1)
    first = lane < HEAD_DIM
    ones = jnp.ones((win, pair), BF16)

    def one_row(rr):
        r = j * NA_ROWS_PER_STEP + rr
        rs = jnp.clip(r - NA_KH // 2, 0, rows - NA_KH)
        off = r - rs
        kstart = pl.multiple_of(rs * GRID_W, GRID_W)
        qstart = pl.multiple_of(rr * GRID_W, GRID_W)
        outs = []
        for hp in range(NA_HEADS // 2):
            sl = slice(hp * pair, (hp + 1) * pair)
            qp = q_ref[pl.ds(qstart, GRID_W), sl]
            kp = k_ref[pl.ds(kstart, win), sl]
            v_ext = jnp.concatenate([v_ref[pl.ds(kstart, win), sl], ones], axis=1)
            halves = []
            for hh in range(2):
                qm = jnp.where(first if hh == 0 else ~first, qp, jnp.zeros_like(qp))
                s = lax.dot_general(qm, kp, (((1,), (1,)), ((), ())), preferred_element_type=F32)
                s = s + b_ref[off, 2 * hp + hh]
                p = jnp.exp2(s - jnp.max(s, axis=-1, keepdims=True)).astype(BF16)
                z = jnp.dot(p, v_ext, preferred_element_type=F32)
                halves.append(z[:, :pair] / z[:, pair:pair + 1])
            outs.append(jnp.where(first, halves[0], halves[1]))
        o_ref[pl.ds(qstart, GRID_W), :] = jnp.concatenate(outs, axis=-1).astype(BF16)

    def row_body(i, carry):
        for u in range(NA_ROW_UNROLL):
            one_row(i * NA_ROW_UNROLL + u)
        return carry

    lax.fori_loop(0, NA_ROWS_PER_STEP // NA_ROW_UNROLL, row_body, 0)


def _neighbourhood_attention(q, k, v, bias, batch, seq):
    rows = seq // GRID_W
    assert rows >= NA_KH and rows % NA_ROWS_PER_STEP == 0
    nblk = rows // NA_ROWS_PER_STEP
    tq = NA_ROWS_PER_STEP * GRID_W
    qspec = pl.BlockSpec((tq, SECTION), lambda b, j: (b * nblk + j, 0))
    kvspec = pl.BlockSpec((seq, SECTION), lambda b, j: (b, 0))
    return pl.pallas_call(
        functools.partial(_na_kernel, rows=rows),
        grid=(batch, nblk),
        in_specs=[qspec, kvspec, kvspec, _const_spec(bias.shape)],
        out_specs=qspec,
        out_shape=jax.ShapeDtypeStruct(q.shape, BF16),
        compiler_params=_cparams(("parallel", "arbitrary")),
    )(q, k, v, bias)


def _diff_kernel(q_ref, k_ref, v_ref, lam_ref, sub_ref, o_ref, m0_scr, m1_scr, acc0_scr, acc1_scr,
                 *, seq, tq, tk):
    m_scrs, acc_scrs = (m0_scr, m1_scr), (acc0_scr, acc1_scr)
    q = q_ref[...]
    lane = lax.broadcasted_iota(jnp.int32, q.shape, 1)
    zero = jnp.zeros_like(q)
    q_maps = (jnp.where(lane < HEAD_DIM, q, zero), jnp.where(lane >= HEAD_DIM, q, zero))

    lp = lam_ref[...]
    lam = (jnp.exp(jnp.sum(lp[0:1] * lp[1:2], axis=-1, keepdims=True))
           - jnp.exp(jnp.sum(lp[2:3] * lp[3:4], axis=-1, keepdims=True)) + LAMBDA_INIT)

    width = 2 * HEAD_DIM
    ones = jnp.ones((tk, width), BF16)

    for m_scr, acc_scr in zip(m_scrs, acc_scrs):
        m_scr[...] = jnp.full(m_scr.shape, -jnp.inf, F32)
        acc_scr[...] = jnp.zeros(acc_scr.shape, F32)

    def kv_body(c, carry):
        tks = tk // KV_SUBCHUNKS
        kcs, v_exts = [], []
        for sub in range(KV_SUBCHUNKS):
            start = pl.multiple_of(c * tk + sub * tks, tks)
            kcs.append(k_ref[pl.ds(start, tks), :])
            v_exts.append(jnp.concatenate([v_ref[pl.ds(start, tks), :], ones[:tks]], axis=1))
        scores = [[lax.dot_general(q_maps[mp], kcs[sub], (((1,), (1,)), ((), ())),
                                   preferred_element_type=F32) for mp in range(2)]
                  for sub in range(KV_SUBCHUNKS)]
        m = [m_scrs[mp][...] for mp in range(2)]
        probs, alphas = [], []
        for sub in range(KV_SUBCHUNKS):
            probs.append([])
            alphas.append([])
            for mp in range(2):
                mn = jnp.maximum(m[mp], jnp.max(scores[sub][mp], axis=-1, keepdims=True))
                probs[sub].append(jnp.exp2(scores[sub][mp] - mn[:, 0:1]).astype(BF16))
                alpha = jnp.exp2(m[mp] - mn)
                alphas[sub].append(jnp.concatenate([alpha, alpha], axis=1))
                m[mp] = mn
        for mp in range(2):
            m_scrs[mp][...] = m[mp]
            acc = acc_scrs[mp][...]
            for sub in range(KV_SUBCHUNKS):
                acc = alphas[sub][mp] * acc + jnp.dot(probs[sub][mp], v_exts[sub],
                                                      preferred_element_type=F32)
            acc_scrs[mp][...] = acc
        return carry

    lax.fori_loop(0, seq // tk, kv_body, 0)
    a0, a1 = acc0_scr[...], acc1_scr[...]
    o = a0[:, :width] / a0[:, width:width + 1] - lam * (a1[:, :width] / a1[:, width:width + 1])
    ms = jnp.mean(o * o, axis=-1, keepdims=True)
    y = o * lax.rsqrt(ms + EPS) * sub_ref[...]
    o_ref[...] = (y * (1.0 - LAMBDA_INIT)).astype(BF16)


def _diff_attention(q, k, v, lam_params, subln_w, batch, seq):
    tq, tk = min(TQ, seq), min(TK, seq)
    nq = seq // tq
    width = 2 * HEAD_DIM
    qspec = pl.BlockSpec((tq, width), lambda b, h, i: (b * nq + i, h))
    kvspec = pl.BlockSpec((seq, width), lambda b, h, i: (b, h))
    return pl.pallas_call(
        functools.partial(_diff_kernel, seq=seq, tq=tq, tk=tk),
        grid=(batch, DIFF_HEADS, nq),
        in_specs=[qspec, kvspec, kvspec, _const_spec((4, HEAD_DIM)), _const_spec((1, width))],
        out_specs=qspec,
        out_shape=jax.ShapeDtypeStruct(q.shape, BF16),
        scratch_shapes=[pltpu.VMEM((tq, width), F32), pltpu.VMEM((tq, width), F32),
                        pltpu.VMEM((tq, 2 * width), F32), pltpu.VMEM((tq, 2 * width), F32)],
        compiler_params=_cparams(("parallel", "parallel", "arbitrary")),
    )(q, k, v, lam_params, subln_w.reshape(1, width))


def _top16(sc, payload=None):
    iota = lax.broadcasted_iota(jnp.int32, sc.shape, 0).astype(F32)
    big = float(sc.shape[0])
    vals, picks = [], []
    for _ in range(PEER_TOPK):
        m = jnp.max(sc, axis=0, keepdims=True)
        ix = jnp.min(jnp.where(sc == m, iota, big), axis=0, keepdims=True)
        sel = iota == ix
        if payload is None:
            picks.append(ix)
        else:
            picks.append(jnp.sum(jnp.where(sel, payload, 0), axis=0, keepdims=True))
        sc = jnp.where(sel, -jnp.inf, sc)
        vals.append(m)
    picks = jnp.concatenate(picks, axis=0)
    return jnp.concatenate(vals, axis=0), picks.astype(jnp.int32)


def _pair_candidates(r0, r1):
    half = SUBLANES // 2
    groups = [r0[0:1] + r1[0:8], r0[0:1] + r1[8:16]]
    groups += [r0[a:a + 1] + r1[0:8] for a in (1, 2, 3)]
    groups += [jnp.concatenate([r0[a:a + 1] + r1[0:half], r0[a + 1:a + 2] + r1[0:half]], axis=0)
               for a in (4, 6)]
    groups.append(r0[8:16] + r1[0:1])
    return jnp.concatenate(groups, axis=0)


def _route_kernel(na_ref, df_ref, x_ref, wo_ref, n2_ref, wq_ref, sk_ref,
                  x1_ref, xn_ref, eid_ref, g_ref, q_scr, eid_scr, g_scr):
    tm = x_ref.shape[0]
    x1 = (x_ref[...]
          + jnp.dot(na_ref[...], wo_ref[0:SECTION, :], preferred_element_type=F32)
          + jnp.dot(df_ref[...], wo_ref[SECTION:2 * SECTION, :], preferred_element_type=F32))
    ms = jnp.mean(x1 * x1, axis=-1, keepdims=True)
    xn = x1 * lax.rsqrt(ms + EPS) * n2_ref[...]
    for r in range(SUBLANES):
        x1_ref[:, r, :] = x1[:, r * LANES:(r + 1) * LANES]
        xn_ref[:, r, :] = xn[:, r * LANES:(r + 1) * LANES]
    xb = xn.astype(BF16)
    for c in range(2 * PEER_HEADS):
        q_scr[c] = jnp.dot(xb, wq_ref[:, c * LANES:(c + 1) * LANES],
                           preferred_element_type=F32).astype(BF16)

    nblk = tm // TB

    def route_block(h, blk):
        tok0 = blk * TB
        tops = []
        for p in range(2):
            sc = lax.dot_general(sk_ref[2 * h + p], q_scr[2 * h + p, pl.ds(tok0, TB), :],
                                 (((1,), (1,)), ((), ())), preferred_element_type=F32)
            tops.append(_top16(sc))
        (s0, i0), (s1, i1) = tops
        cand_s = _pair_candidates(s0, s1)
        cand_i = _pair_candidates(i0 * N_KEYS, i1)
        top_s, eid = _top16(cand_s, cand_i)
        e = jnp.exp(top_s - top_s[0:1])
        row = pl.multiple_of(h * PEER_TOPK, PEER_TOPK)
        eid_scr[blk, pl.ds(row, PEER_TOPK), :] = eid * ROW_SUB
        g_scr[blk, pl.ds(row, PEER_TOPK), :] = e / jnp.sum(e, axis=0, keepdims=True)

    def head_body(h, carry):
        for blk in range(nblk):
            route_block(h, blk)
        return carry

    lax.fori_loop(0, PEER_HEADS, head_body, 0)
    for blk in range(nblk):
        eid_ref[blk] = eid_scr[blk].T
        g_ref[blk * TB:(blk + 1) * TB, :] = g_scr[blk].T


def _out_and_route(na, df, x2d, w_out_bf, norm2_w, w_q_bf, sk_bf):
    n = x2d.shape[0]
    tm = TM_OUT
    tok = lambda w: pl.BlockSpec((tm, w), lambda i: (i, 0))
    tiles = pl.BlockSpec((tm, SUBLANES, LANES), lambda i: (i, 0, 0))
    return pl.pallas_call(
        _route_kernel,
        grid=(n // tm,),
        in_specs=[tok(SECTION), tok(SECTION), tok(D_MODEL),
                  _const_spec((D_MODEL, D_MODEL)), _const_spec((1, D_MODEL)),
                  _const_spec(w_q_bf.shape), _const_spec(sk_bf.shape)],
        out_specs=[tiles, tiles,
                   pl.BlockSpec((tm // TB, TB, HK), lambda i: (i, 0, 0)), tok(HK)],
        out_shape=[jax.ShapeDtypeStruct((n, SUBLANES, LANES), F32),
                   jax.ShapeDtypeStruct((n, SUBLANES, LANES), F32),
                   jax.ShapeDtypeStruct((n // TB, TB, HK), jnp.int32),
                   jax.ShapeDtypeStruct((n, HK), F32)],
        scratch_shapes=[pltpu.VMEM((2 * PEER_HEADS, tm, LANES), BF16),
                        pltpu.VMEM((tm // TB, HK, TB), jnp.int32), pltpu.VMEM((tm // TB, HK, TB), F32)],
        compiler_params=_cparams(("parallel",)),
    )(na, df, x2d, w_out_bf, norm2_w.reshape(1, D_MODEL), w_q_bf, sk_bf)


ROW_SUB = SUBLANES // 2
CHUNK_SLOTS = 32
N_CHUNKS = HK // CHUNK_SLOTS
CHUNK_ROWS = CHUNK_SLOTS * SUBLANES
TOKEN_UNROLL = 32
ROW_WORDS = ROW_SUB * LANES
SC_CORES, SC_SUBCORES, SC_LANES = 2, 16, 16
SC_WORKERS = SC_CORES * SC_SUBCORES


PACK_EXPERTS = 512


def _pack_kernel(x_ref, o_ref):
    x = x_ref[...]
    for c in range(ROW_SUB):
        lo = x[:, (2 * c) * LANES:(2 * c + 1) * LANES].astype(BF16).astype(F32)
        hi = x[:, (2 * c + 1) * LANES:(2 * c + 2) * LANES].astype(BF16).astype(F32)
        words = (pltpu.bitcast(lo, jnp.uint32) >> 16) | (pltpu.bitcast(hi, jnp.uint32) & jnp.uint32(0xFFFF0000))
        o_ref[pl.ds(c, PACK_EXPERTS, stride=ROW_SUB), :] = words


def _pack_table(tab):
    experts = tab.shape[0]
    return pl.pallas_call(
        _pack_kernel,
        grid=(experts // PACK_EXPERTS,),
        in_specs=[pl.BlockSpec((PACK_EXPERTS, D_MODEL), lambda i: (i, 0))],
        out_specs=pl.BlockSpec((PACK_EXPERTS * ROW_SUB, LANES), lambda i: (i, 0)),
        out_shape=jax.ShapeDtypeStruct((experts * ROW_SUB, LANES), jnp.uint32),
        compiler_params=_cparams(("parallel",)),
    )(tab)


def _split_bf16(x):
    hi = x.astype(BF16)
    return hi, (x - hi.astype(F32)).astype(BF16)


def _gather_chunk(tab_ref, eid_ref, t, chunk):
    ids = eid_ref.at[0, t]
    rows = [tab_ref[pl.ds(pl.multiple_of(ids[chunk * CHUNK_SLOTS + i], ROW_SUB), ROW_SUB), :]
            for i in range(CHUNK_SLOTS)]
    return pltpu.bitcast(jnp.concatenate(rows, axis=0), BF16)


def _diag_mask(width):
    r = lax.broadcasted_iota(jnp.int32, (SUBLANES, width), 0)
    n = lax.broadcasted_iota(jnp.int32, (SUBLANES, width), 1)
    return (n % SUBLANES) == r


def _peer_u_kernel(eid_ref, xn_ref, g_ref, tab_ref, rept_ref, rep_ref, rep16_ref, wrep_ref, w16_ref, d_scr):
    mask = _diag_mask(CHUNK_ROWS)
    zeros = jnp.zeros((SUBLANES, LANES), BF16)

    def one_token(t):
        xhi, xlo = _split_bf16(xn_ref[t])
        lhs = jnp.concatenate([jnp.concatenate([xhi, zeros], axis=1),
                               jnp.concatenate([xlo, zeros], axis=1),
                               jnp.concatenate([zeros, xhi], axis=1),
                               jnp.concatenate([zeros, xlo], axis=1)], axis=0)
        for pr in range(N_CHUNKS // 2):
            s = jnp.concatenate([_gather_chunk(tab_ref, eid_ref, t, 2 * pr),
                                 _gather_chunk(tab_ref, eid_ref, t, 2 * pr + 1)], axis=1)
            z = lax.dot_general(lhs, s, (((1,), (1,)), ((), ())), preferred_element_type=F32)
            for half in range(2):
                z8 = z[16 * half:16 * half + 8] + z[16 * half + 8:16 * half + 16]
                zs = jnp.sum(jnp.where(mask, z8, 0.0), axis=0, keepdims=True)
                d_scr[pl.ds(t, 1), pl.ds((2 * pr + half) * CHUNK_ROWS, CHUNK_ROWS)] = zs

    def tok_body(i, carry):
        for u in range(TOKEN_UNROLL):
            one_token(i * TOKEN_UNROLL + u)
        return carry

    lax.fori_loop(0, PB // TOKEN_UNROLL, tok_body, 0)
    _gate_outputs(d_scr[...], rept_ref, g_ref, rep_ref, rep16_ref, wrep_ref, w16_ref)


def _gate_outputs(partial, sum_ref, g_ref, rep_ref, rep16_ref, wrep_ref, w16_ref):
    dhi, dlo = _split_bf16(partial)
    a = (jnp.dot(dhi, sum_ref[...], preferred_element_type=F32)
         + jnp.dot(dlo, sum_ref[...], preferred_element_type=F32))
    w = g_ref[...] * (0.5 * a * (1.0 + lax.erf(a * (2.0 ** -0.5))))
    whi, wlo = _split_bf16(w)
    wrep_ref[...] = (jnp.dot(whi, rep_ref[...], preferred_element_type=F32)
                     + jnp.dot(wlo, rep_ref[...], preferred_element_type=F32))
    w16_ref[...] = (jnp.dot(whi, rep16_ref[...], preferred_element_type=F32)
                    + jnp.dot(wlo, rep16_ref[...], preferred_element_type=F32))


def _gate_outputs_kernel(part_ref, g_ref, sum_ref, rep_ref, rep16_ref, wrep_ref, w16_ref):
    _gate_outputs(part_ref[...], sum_ref, g_ref, rep_ref, rep16_ref, wrep_ref, w16_ref)


def _gate_kernel(part_ref, g_ref, sum_ref, rep_ref, rep16_ref, wrep_in, w16_in, wrep_ref, w16_ref):
    del wrep_in, w16_in
    _gate_outputs(part_ref[...], sum_ref, g_ref, rep_ref, rep16_ref, wrep_ref, w16_ref)


def _peer_v_kernel(eid_ref, wrep_ref, x1_ref, tab_ref, o_ref):
    mask = _diag_mask(CHUNK_ROWS)

    def one_token(t):
        wr = wrep_ref[pl.ds(t, 1), :]
        acc = x1_ref[t]
        for pr in range(N_CHUNKS // 2):
            parts = []
            for half in range(2):
                seg = wr[:, (2 * pr + half) * CHUNK_ROWS:(2 * pr + half + 1) * CHUNK_ROWS]
                parts.extend(_split_bf16(jnp.where(mask, jnp.broadcast_to(seg, mask.shape), 0.0)))
            lhs = jnp.concatenate(parts, axis=0)
            s = jnp.concatenate([_gather_chunk(tab_ref, eid_ref, t, 2 * pr),
                                 _gather_chunk(tab_ref, eid_ref, t, 2 * pr + 1)], axis=1)
            z = jnp.dot(lhs, s, preferred_element_type=F32)
            acc = acc + ((z[0:8, :LANES] + z[8:16, :LANES]) + (z[16:24, LANES:] + z[24:32, LANES:]))
        return acc

    def tok_body(i, carry):
        accs = [one_token(i * TOKEN_UNROLL + u) for u in range(TOKEN_UNROLL)]
        base = pl.multiple_of(i * TOKEN_UNROLL, TOKEN_UNROLL)
        for r in range(SUBLANES):
            o_ref[pl.ds(base, TOKEN_UNROLL), r * LANES:(r + 1) * LANES] = jnp.concatenate(
                [acc[r:r + 1, :] for acc in accs], axis=0)
        return carry

    lax.fori_loop(0, PB // TOKEN_UNROLL, tok_body, 0)


def _sc_scratch(mid_shape, last_shape):
    half = HK // 2
    per_parity = [pltpu.VMEM((2, half), jnp.int32), pltpu.VMEM(mid_shape, F32), pltpu.VMEM(last_shape, F32)]
    return per_parity + per_parity + [
        pltpu.VMEM((half, ROW_WORDS), jnp.uint32), pltpu.VMEM((half, ROW_WORDS), jnp.uint32),
        pltpu.SemaphoreType.DMA, pltpu.SemaphoreType.DMA,
        pltpu.SemaphoreType.DMA, pltpu.SemaphoreType.DMA,
        pltpu.SemaphoreType.DMA, pltpu.SemaphoreType.DMA,
    ]


def _sc_u_phase(n_tok, tok0):
    per_w = n_tok // SC_WORKERS
    assert per_w * SC_WORKERS == n_tok and per_w % 2 == 0
    half = HK // 2
    mesh = plsc.VectorSubcoreMesh(core_axis_name="c", subcore_axis_name="s")

    @functools.partial(
        pl.kernel, mesh=mesh,
        out_type=jax.ShapeDtypeStruct((n_tok, HK * SC_LANES), F32),
        scratch_types=_sc_scratch((SUBLANES, LANES), (HK * SC_LANES,)),
        compiler_params=pltpu.CompilerParams(needs_layout_passes=False),
    )
    def sc_kernel(tab_hbm, eid_hbm, xn_hbm, out_hbm,
                  idx_a, x_a, a_a, idx_b, x_b, a_b, g0, g1, s_a, s_b, sg0, sg1, so_a, so_b):
        wid = lax.axis_index("s") * SC_CORES + lax.axis_index("c")
        first = wid * per_w
        par = ((idx_a, x_a, a_a, s_a, so_a), (idx_b, x_b, a_b, s_b, so_b))
        gbuf, gsem = (g0, g1), (sg0, sg1)

        def input_copies(t, p):
            idx_v, x_v, _, sem, _ = par[p]
            return (pltpu.make_async_copy(eid_hbm.at[tok0 + t], idx_v, sem),
                    pltpu.make_async_copy(xn_hbm.at[tok0 + t], x_v, sem))

        def gather(p, h):
            return pltpu.make_async_copy(tab_hbm.at[par[p][0].at[h]], gbuf[h], gsem[h])

        def out_copy(t, p):
            return pltpu.make_async_copy(par[p][2], out_hbm.at[t], par[p][4])

        def dot_half(p, h):
            _, x_v, a_v, _, _ = par[p]
            g = gbuf[h]
            groups = LANES // SC_LANES
            block = SUBLANES
            for rb in range(half // block):
                def col_body(k, accs):
                    c, kk = k // groups, k % groups
                    lane0 = pl.multiple_of(kk * SC_LANES, SC_LANES)
                    xl = x_v[2 * c, pl.ds(lane0, SC_LANES)]
                    xh = x_v[2 * c + 1, pl.ds(lane0, SC_LANES)]
                    out = []
                    for r in range(block):
                        words = g[rb * block + r, pl.ds(pl.multiple_of(k * SC_LANES, SC_LANES), SC_LANES)]
                        lov = plsc.bitcast(lax.shift_left(words, jnp.uint32(16)), F32)
                        hiv = plsc.bitcast(words & jnp.uint32(0xFFFF0000), F32)
                        out.append(accs[r] + lov * xl + hiv * xh)
                    return tuple(out)

                zero = jnp.zeros((SC_LANES,), F32)
                accs = lax.fori_loop(0, ROW_WORDS // SC_LANES, col_body, (zero,) * block)
                for r in range(block):
                    a_v[pl.ds((h * half + rb * block + r) * SC_LANES, SC_LANES)] = accs[r]

        def token(t, p):
            q = 1 - p
            has_next = t + 1 < first + per_w
            gather(p, 0).wait()
            gather(p, 1).start()

            @pl.when(has_next)
            def _():
                for cp in input_copies(t + 1, q):
                    cp.start()

            @pl.when(t >= first + 2)
            def _():
                out_copy(t - 2, p).wait()

            dot_half(p, 0)
            gather(p, 1).wait()

            @pl.when(has_next)
            def _():
                for cp in input_copies(t + 1, q):
                    cp.wait()
                gather(q, 0).start()

            dot_half(p, 1)
            out_copy(t, p).start()

        for cp in input_copies(first, 0):
            cp.start()
        for cp in input_copies(first, 0):
            cp.wait()
        gather(0, 0).start()

        def pair_body(i, carry):
            token(first + 2 * i, 0)
            token(first + 2 * i + 1, 1)
            return carry

        lax.fori_loop(0, per_w // 2, pair_body, 0)
        out_copy(first + per_w - 2, 0).wait()
        out_copy(first + per_w - 1, 1).wait()

    return sc_kernel


def _sc_v_phase(n_tok, tok0):
    per_w = n_tok // SC_WORKERS
    assert per_w * SC_WORKERS == n_tok and per_w % 2 == 0
    half = HK // 2
    mesh = plsc.VectorSubcoreMesh(core_axis_name="c", subcore_axis_name="s")

    @functools.partial(
        pl.kernel, mesh=mesh,
        out_type=jax.ShapeDtypeStruct((n_tok, SUBLANES, LANES), F32),
        scratch_types=_sc_scratch((HK * SC_LANES,), (SUBLANES, LANES)),
        compiler_params=pltpu.CompilerParams(needs_layout_passes=False),
    )
    def sc_kernel(tab_hbm, eid_hbm, w16_hbm, x1_hbm, out_hbm,
                  idx_a, w_a, x_a, idx_b, w_b, x_b, g0, g1, s_a, s_b, sg0, sg1, so_a, so_b):
        wid = lax.axis_index("s") * SC_CORES + lax.axis_index("c")
        first = wid * per_w
        par = ((idx_a, w_a, x_a, s_a, so_a), (idx_b, w_b, x_b, s_b, so_b))
        gbuf, gsem = (g0, g1), (sg0, sg1)

        def input_copies(t, p):
            idx_v, w_v, x_v, sem, _ = par[p]
            return (pltpu.make_async_copy(eid_hbm.at[tok0 + t], idx_v, sem),
                    pltpu.make_async_copy(w16_hbm.at[tok0 + t], w_v, sem),
                    pltpu.make_async_copy(x1_hbm.at[tok0 + t], x_v, sem))

        def gather(p, h):
            return pltpu.make_async_copy(tab_hbm.at[par[p][0].at[h]], gbuf[h], gsem[h])

        def out_copy(t, p):
            return pltpu.make_async_copy(par[p][2], out_hbm.at[t], par[p][4])

        def accumulate_half(p, h):
            _, w_v, x_v, _, _ = par[p]
            g = gbuf[h]
            groups = LANES // SC_LANES
            for c in range(ROW_SUB):
                lo0 = tuple(x_v[2 * c, pl.ds(SC_LANES * kk, SC_LANES)] for kk in range(groups))
                hi0 = tuple(x_v[2 * c + 1, pl.ds(SC_LANES * kk, SC_LANES)] for kk in range(groups))

                def row_body(j, acc):
                    lo, hi = acc
                    wj = w_v[pl.ds(pl.multiple_of((h * half + j) * SC_LANES, SC_LANES), SC_LANES)]
                    nlo, nhi = [], []
                    for kk in range(groups):
                        words = g[j, pl.ds(LANES * c + SC_LANES * kk, SC_LANES)]
                        lov = plsc.bitcast(lax.shift_left(words, jnp.uint32(16)), F32)
                        hiv = plsc.bitcast(words & jnp.uint32(0xFFFF0000), F32)
                        nlo.append(lo[kk] + wj * lov)
                        nhi.append(hi[kk] + wj * hiv)
                    return tuple(nlo), tuple(nhi)

                lo, hi = lax.fori_loop(0, half, row_body, (lo0, hi0))
                for kk in range(groups):
                    x_v[2 * c, pl.ds(SC_LANES * kk, SC_LANES)] = lo[kk]
                    x_v[2 * c + 1, pl.ds(SC_LANES * kk, SC_LANES)] = hi[kk]

        def token(t, p):
            q = 1 - p
            has_next = t + 1 < first + per_w
            gather(p, 0).wait()
            gather(p, 1).start()

            @pl.when(has_next)
            def _():
                @pl.when(t >= first + 1)
                def _():
                    out_copy(t - 1, q).wait()
                for cp in input_copies(t + 1, q):
                    cp.start()

            accumulate_half(p, 0)
            gather(p, 1).wait()

            @pl.when(has_next)
            def _():
                for cp in input_copies(t + 1, q):
                    cp.wait()
                gather(q, 0).start()

            accumulate_half(p, 1)
            out_copy(t, p).start()

        for cp in input_copies(first, 0):
            cp.start()
        for cp in input_copies(first, 0):
            cp.wait()
        gather(0, 0).start()

        def pair_body(i, carry):
            token(first + 2 * i, 0)
            token(first + 2 * i + 1, 1)
            return carry

        lax.fori_loop(0, per_w // 2, pair_body, 0)
        out_copy(first + per_w - 2, 0).wait()
        out_copy(first + per_w - 1, 1).wait()

    return sc_kernel


def _peer_experts(eid, g, xn3, x13, u_pack, v_pack, n_sc_u, n_sc_v):
    n = xn3.shape[0]
    n_tc_u, n_tc_v = n - n_sc_u, n - n_sc_v
    assert n_tc_u % PB == 0 and n_tc_v % PB == 0
    eid = eid.reshape(n // PB, PB, HK)
    smem_slot = pl.BlockSpec((1, PB, HK), lambda i: (i, 0, 0), memory_space=pltpu.SMEM)
    tok = pl.BlockSpec((PB, SUBLANES, LANES), lambda i: (i, 0, 0))
    flat = lambda w, off=0: pl.BlockSpec((PB, w), lambda i: (i + off, 0))
    table = pl.BlockSpec(u_pack.shape, lambda i: (0, 0), pipeline_mode=pl.Buffered(1))

    def repeat_matrix(times):
        m = (np.arange(HK)[:, None] == (np.arange(HK * times)[None, :] // times)).astype(np.float32)
        return jnp.asarray(m, BF16)

    rep, rep16 = repeat_matrix(SUBLANES), repeat_matrix(SC_LANES)
    gate_shapes = [jax.ShapeDtypeStruct((n, D_MODEL), F32), jax.ShapeDtypeStruct((n, HK * SC_LANES), F32)]
    expert = eid.reshape(n, 2, HK // 2) >> (ROW_SUB.bit_length() - 1)
    if n_sc_u:
        u_rows = u_pack.reshape(N_EXPERTS, ROW_WORDS)
        partial = _sc_u_phase(n_sc_u, n_tc_u)(u_rows, expert, xn3)
    if n_tc_u:
        wrep, w16 = pl.pallas_call(
            _peer_u_kernel,
            grid=(n_tc_u // PB,),
            in_specs=[smem_slot, tok, flat(HK), table, _const_spec(rep.T.shape), _const_spec(rep.shape),
                      _const_spec(rep16.shape)],
            out_specs=[flat(D_MODEL), flat(HK * SC_LANES)],
            out_shape=gate_shapes,
            scratch_shapes=[pltpu.VMEM((PB, D_MODEL), F32)],
            compiler_params=_cparams(("arbitrary",)),
        )(eid, xn3, g, u_pack, rep.T, rep, rep16)
    if n_sc_u:
        off = n_tc_u // PB
        gate_in = [flat(HK * SC_LANES), flat(HK, off), _const_spec(rep16.T.shape), _const_spec(rep.shape),
                   _const_spec(rep16.shape)]
        gate_args = (partial, g, rep16.T, rep, rep16)
        if n_tc_u:
            anyspec = pl.BlockSpec(memory_space=pl.ANY)
            gate_fn, gate_in, gate_args = _gate_kernel, gate_in + [anyspec, anyspec], gate_args + (wrep, w16)
            aliases = {5: 0, 6: 1}
        else:
            gate_fn, aliases = _gate_outputs_kernel, {}
        wrep, w16 = pl.pallas_call(
            gate_fn,
            grid=(n_sc_u // PB,),
            in_specs=gate_in,
            out_specs=[flat(D_MODEL, off), flat(HK * SC_LANES, off)],
            out_shape=gate_shapes,
            input_output_aliases=aliases,
            compiler_params=_cparams(("arbitrary",)),
        )(*gate_args)
    parts = []
    if n_tc_v:
        parts.append(pl.pallas_call(
            _peer_v_kernel,
            grid=(n_tc_v // PB,),
            in_specs=[smem_slot, flat(D_MODEL), tok, table],
            out_specs=flat(D_MODEL),
            out_shape=jax.ShapeDtypeStruct((n_tc_v, D_MODEL), F32),
            compiler_params=_cparams(("arbitrary",)),
        )(eid, wrep, x13, v_pack))
    if n_sc_v:
        v_rows = v_pack.reshape(N_EXPERTS, ROW_WORDS)
        parts.append(_sc_v_phase(n_sc_v, n_tc_v)(v_rows, expert, w16, x13).reshape(n_sc_v, D_MODEL))
    y = parts[0] if len(parts) == 1 else jnp.concatenate(parts, axis=0)
    return y, w16


def _encoder_layer(x, p, after):
    if after is not None:
        x, _ = lax.optimization_barrier((x, after))
    batch, seq, _ = x.shape
    n = batch * seq
    x2d = x.reshape(n, D_MODEL)
    naq, nak, nav, dq, dk, dv = _in_projection(x2d, seq, p["norm1_w"], p["w_in"], p["qk_w"], p["rope"])
    na = _neighbourhood_attention(naq, nak, nav, p["na_bias"], batch, seq)
    df = _diff_attention(dq, dk, dv, p["lam"], p["subln_w"], batch, seq)
    x1, xn, eid, g = _out_and_route(na, df, x2d, p["w_out"], p["norm2_w"], p["w_q"], p["sub_keys"])
    y, w16 = _peer_experts(eid, g, xn, x1, p["u_pack"], p["v_pack"], *SC_TOKENS.get(n, (0, 0)))
    return y.reshape(batch, seq, D_MODEL), w16


def kernel(x_prompt, x_sample, norm1_w, w_in, na_q_norm, na_k_norm, na_rpb, diff_q_norm, diff_k_norm,
           diff_lambda_q1, diff_lambda_k1, diff_lambda_q2, diff_lambda_k2, diff_subln_w, w_out, norm2_w,
           peer_w_q, peer_sub_keys, peer_u, peer_v):
    assert norm1_w.shape[0] == 1, "single-layer problem"
    reps = SECTION // HEAD_DIM
    params = {
        "norm1_w": norm1_w[0],
        "w_in": w_in[0].astype(BF16),
        "qk_w": jnp.stack([jnp.tile(w[0], reps) for w in (na_q_norm, na_k_norm, diff_q_norm, diff_k_norm)]),
        "lam": jnp.stack([diff_lambda_q1[0], diff_lambda_k1[0], diff_lambda_q2[0], diff_lambda_k2[0]]),
        "subln_w": diff_subln_w[0],
        "w_out": w_out[0].astype(BF16),
        "norm2_w": norm2_w[0],
        "w_q": peer_w_q[0].astype(BF16),
        "sub_keys": peer_sub_keys[0].reshape(2 * PEER_HEADS, N_KEYS, N_KEYS).astype(BF16),
        "na_bias": _na_bias(na_rpb[0]),
        "u_pack": _pack_table(peer_u[0]),
        "v_pack": _pack_table(peer_v[0]),
    }
    rope_by_seq = {}
    outs = [None, None]
    marker = None
    for slot, x in sorted(enumerate((x_prompt, x_sample)), key=lambda item: -item[1].shape[0] * item[1].shape[1]):
        seq = x.shape[1]
        if seq not in rope_by_seq:
            rope_by_seq[seq] = _rope_tables(seq)
        outs[slot], marker = _encoder_layer(x, dict(params, rope=rope_by_seq[seq]), marker)
    return tuple(outs)
```
